```python
import math
import jax, jax.numpy as jnp
from jax import lax
import numpy as np

D_MODEL = 1024
BATCH = 4
SEQ = 4096
DEPTH = 2

CHUNK = 64
N_EVEN = (DEPTH + 1) // 2
N_ODD = DEPTH // 2
EPS = 1e-6

ATT_HEADS = 8
ATT_HEAD_DIM = 64
ATT_WIDTH = ATT_HEADS * ATT_HEAD_DIM
ATT_LEFT_CHUNKS = 8
ATT_BAND = (ATT_LEFT_CHUNKS + 1) * CHUNK
ATT_MAX_REL = 128

POOL_WINDOWS = (2, 4, 8, 16)
POOL_GROUPS = 4
POOL_GROUP_DIM = 128
POOL_WIDTH = POOL_GROUPS * POOL_GROUP_DIM

SGU_BLOCK = 128
SGU_GROUPS = 4
SGU_GROUP_DIM = 128
SGU_WIDTH = SGU_GROUPS * SGU_GROUP_DIM

SB_HEADS = 8
SB_HEAD_DIM = 64
SB_WIDTH = SB_HEADS * SB_HEAD_DIM
SB_QBLOCK = 128

EVEN_IN = 3 * ATT_WIDTH + POOL_WIDTH
EVEN_MIX = ATT_WIDTH + POOL_WIDTH
ODD_IN = 2 * SGU_WIDTH + 3 * SB_WIDTH
ODD_MIX = SGU_WIDTH + SB_WIDTH

N_GROUPS = 4
N_EXP_PER_GROUP = 8
TOP_K = 2
EXPERT_FF = 256

kernel_name = "hybrid_chunk_stream_hmoe"


def rms_norm(x, g):
    xf = x.astype(jnp.float32)
    y = xf * lax.rsqrt(jnp.mean(xf * xf, axis=-1, keepdims=True) + EPS)
    return (y * g.astype(jnp.float32)).astype(x.dtype)


def chunk_band_attention(q, k, v, q_gain, k_gain, rel_bias):
    b, s, h, dh = q.shape
    nc = s // CHUNK
    L = ATT_LEFT_CHUNKS
    q = rms_norm(q, q_gain)
    k = rms_norm(k, k_gain)
    pad = ((0, 0), (L * CHUNK, 0), (0, 0), (0, 0))
    kp = jnp.pad(k, pad).reshape(b, nc + L, CHUNK, h, dh)
    vp = jnp.pad(v, pad).reshape(b, nc + L, CHUNK, h, dh)
    band_idx = jnp.arange(nc)[:, None] + jnp.arange(L + 1)[None, :]
    kb = kp[:, band_idx].reshape(b, nc, ATT_BAND, h, dh)
    vb = vp[:, band_idx].reshape(b, nc, ATT_BAND, h, dh)
    qc = q.reshape(b, nc, CHUNK, h, dh)
    scores = jnp.einsum('bcihd,bcjhd->bhcij', qc, kb,
                        preferred_element_type=jnp.float32) * (1.0 / math.sqrt(dh))
    dist = (jnp.arange(CHUNK)[:, None] + L * CHUNK) - jnp.arange(ATT_BAND)[None, :]
    ridx = jnp.clip(dist, -ATT_MAX_REL, ATT_MAX_REL) + ATT_MAX_REL
    bias = rel_bias[:, ridx].astype(jnp.float32)
    key_chunk = jnp.arange(nc)[:, None] - L + jnp.arange(ATT_BAND)[None, :] // CHUNK
    valid = key_chunk >= 0
    scores = jnp.where(valid[None, None, :, None, :], scores + bias[None, :, None], -1e30)
    probs = jax.nn.softmax(scores, axis=-1)
    out = jnp.einsum('bhcij,bcjhd->bcihd', probs.astype(v.dtype), vb)
    return out.reshape(b, s, h * dh)


def multiscale_pool(p, w_group, scale):
    b, s, _ = p.shape
    pg = p.reshape(b, s, POOL_GROUPS, POOL_GROUP_DIM).astype(jnp.float32)
    cs = jnp.pad(jnp.cumsum(pg, axis=1), ((0, 0), (1, 0), (0, 0), (0, 0)))
    t = jnp.arange(s)[:, None]
    win = jnp.array(POOL_WINDOWS, dtype=jnp.int32)[None, :]
    lo = jnp.maximum(t + 1 - win, 0)
    gidx = jnp.arange(POOL_GROUPS)[None, :]
    win_sum = cs[:, 1:] - cs[:, lo, gidx]
    count = jnp.minimum(t + 1, win).astype(jnp.float32)
    mixed = (win_sum / count[None, :, :, None] - pg).astype(p.dtype)
    y = jnp.einsum('bsgc,gcd->bsgd', mixed, w_group).reshape(b, s, POOL_WIDTH)
    return y * scale


def spatial_gating(u, vv, v_gain, w_s, b_s):
    b, s, _ = u.shape
    nb = s // SGU_BLOCK
    vv = rms_norm(vv, v_gain)
    vb = vv.reshape(b, nb, SGU_BLOCK, SGU_GROUPS, SGU_GROUP_DIM)
    tri = jnp.tril(jnp.ones((SGU_BLOCK, SGU_BLOCK), dtype=bool))
    ws = jnp.where(tri[None], w_s, jnp.zeros_like(w_s))
    mixed = jnp.einsum('gts,bnsgc->bntgc', ws, vb) + b_s.T[:, :, None]
    return u * mixed.reshape(b, s, SGU_WIDTH)


def stick_breaking_attention(q, k, v):
    b, s, h, dh = q.shape
    nq = s // SB_QBLOCK
    qb = q.reshape(b, nq, SB_QBLOCK, h, dh).transpose(1, 0, 2, 3, 4)
    kpos = jnp.arange(s)
    inv = 1.0 / math.sqrt(dh)

    def block(args):
        qblk, i = args
        z = jnp.einsum('bthd,bshd->bhts', qblk, k,
                       preferred_element_type=jnp.float32) * inv
        qpos = i * SB_QBLOCK + jnp.arange(SB_QBLOCK)
        causal = kpos[None, :] < qpos[:, None]
        log_beta = jax.nn.log_sigmoid(z)
        log_rest = jnp.where(causal, jax.nn.log_sigmoid(-z), 0.0)
        tail = lax.cumsum(log_rest, axis=3, reverse=True) - log_rest
        a = jnp.where(causal, jnp.exp(log_beta + tail), 0.0)
        return jnp.einsum('bhts,bshd->bthd', a.astype(v.dtype), v)

    out = lax.map(block, (qb, jnp.arange(nq)))
    return out.transpose(1, 0, 2, 3, 4).reshape(b, s, h * dh)


def hierarchical_moe(x, w_rg, w_re, w_gate, w_up, w_down):
    b, s, d = x.shape
    xt = x.reshape(-1, d)
    n = xt.shape[0]
    g_logits = jnp.einsum('nd,dg->ng', xt, w_rg, preferred_element_type=jnp.float32)
    g_prob = jax.nn.softmax(g_logits, axis=-1)
    g_sel = jnp.argmax(g_logits, axis=-1)
    g_weight = jnp.take_along_axis(g_prob, g_sel[:, None], axis=-1)
    e_logits = jnp.einsum('nd,de->ne', xt, w_re,
                          preferred_element_type=jnp.float32).reshape(n, N_GROUPS, N_EXP_PER_GROUP)
    e_in_group = jnp.take_along_axis(e_logits, g_sel[:, None, None], axis=1)[:, 0]
    top_val, top_idx = lax.top_k(e_in_group, TOP_K)
    top_w = jax.nn.softmax(top_val, axis=-1) * g_weight
    flat_idx = g_sel[:, None] * N_EXP_PER_GROUP + top_idx
    combine = jnp.einsum('nk,nke->ne', top_w,
                         jax.nn.one_hot(flat_idx, N_GROUPS * N_EXP_PER_GROUP, dtype=jnp.float32))
    combine = combine.reshape(n, N_GROUPS, N_EXP_PER_GROUP)
    out = jnp.zeros((n, d), jnp.float32)
    for g in range(N_GROUPS):
        hid = jax.nn.silu(jnp.einsum('nd,edf->nef', xt, w_gate[g])) * \
            jnp.einsum('nd,edf->nef', xt, w_up[g])
        hid = hid * combine[:, g, :, None].astype(hid.dtype)
        out = out + jnp.einsum('nef,efd->nd', hid, w_down[g],
                               preferred_element_type=jnp.float32)
    return out.astype(x.dtype).reshape(b, s, d)


def setup_inputs(seed: int = 0) -> dict:
    key = jax.random.key(seed)
    ks = jax.random.split(key, 24)
    f32 = jnp.float32

    def nrm(k, shape, scale):
        return jax.random.normal(k, shape, f32) * scale

    def gain(k, shape):
        return 1.0 + 0.1 * jax.random.normal(k, shape, f32)

    G, E, F = N_GROUPS, N_EXP_PER_GROUP, EXPERT_FF
    return {
        "x": jax.random.normal(ks[0], (BATCH, SEQ, D_MODEL), f32),
        "mix_norm_even": gain(ks[1], (N_EVEN, D_MODEL)),
        "w_in_even": nrm(ks[2], (N_EVEN, D_MODEL, EVEN_IN), D_MODEL ** -0.5),
        "att_q_norm": gain(ks[3], (N_EVEN, ATT_HEAD_DIM)),
        "att_k_norm": gain(ks[4], (N_EVEN, ATT_HEAD_DIM)),
        "att_rel_bias": nrm(ks[5], (N_EVEN, ATT_HEADS, 2 * ATT_MAX_REL + 1), 0.5),
        "pool_w": nrm(ks[6], (N_EVEN, POOL_GROUPS, POOL_GROUP_DIM, POOL_GROUP_DIM), POOL_GROUP_DIM ** -0.5),
        "pool_scale": gain(ks[7], (N_EVEN, POOL_WIDTH)),
        "w_out_even": nrm(ks[8], (N_EVEN, EVEN_MIX, D_MODEL), EVEN_MIX ** -0.5),
        "mix_norm_odd": gain(ks[9], (N_ODD, D_MODEL)),
        "w_in_odd": nrm(ks[10], (N_ODD, D_MODEL, ODD_IN), D_MODEL ** -0.5),
        "sgu_v_norm": gain(ks[11], (N_ODD, SGU_WIDTH)),
        "sgu_w": nrm(ks[12], (N_ODD, SGU_GROUPS, SGU_BLOCK, SGU_BLOCK), SGU_BLOCK ** -0.5),
        "sgu_b": gain(ks[13], (N_ODD, SGU_GROUPS, SGU_BLOCK)),
        "w_out_odd": nrm(ks[14], (N_ODD, ODD_MIX, D_MODEL), ODD_MIX ** -0.5),
        "ffn_norm": gain(ks[15], (DEPTH, D_MODEL)),
        "w_router_group": nrm(ks[16], (DEPTH, D_MODEL, G), D_MODEL ** -0.5),
        "w_router_expert": nrm(ks[17], (DEPTH, D_MODEL, G * E), D_MODEL ** -0.5),
        "w_exp_gate": nrm(ks[18], (DEPTH, G, E, D_MODEL, F), D_MODEL ** -0.5),
        "w_exp_up": nrm(ks[19], (DEPTH, G, E, D_MODEL, F), D_MODEL ** -0.5),
        "w_exp_down": nrm(ks[20], (DEPTH, G, E, F, D_MODEL), F ** -0.5),
    }


def reference(x, mix_norm_even, w_in_even, att_q_norm, att_k_norm, att_rel_bias,
              pool_w, pool_scale, w_out_even, mix_norm_odd, w_in_odd, sgu_v_norm,
              sgu_w, sgu_b, w_out_odd, ffn_norm, w_router_group, w_router_expert,
              w_exp_gate, w_exp_up, w_exp_down):
    b, s, _ = x.shape
    h = x
    for layer in range(DEPTH):
        i = layer // 2
        if layer % 2 == 0:
            hn = rms_norm(h, mix_norm_even[i])
            proj = jnp.einsum('bsd,de->bse', hn, w_in_even[i])
            q, k, v, p = jnp.split(proj, [ATT_WIDTH, 2 * ATT_WIDTH, 3 * ATT_WIDTH], axis=-1)
            hs = (b, s, ATT_HEADS, ATT_HEAD_DIM)
            ya = chunk_band_attention(q.reshape(hs), k.reshape(hs), v.reshape(hs),
                                      att_q_norm[i], att_k_norm[i], att_rel_bias[i])
            yb = multiscale_pool(p, pool_w[i], pool_scale[i])
            mixed = jnp.concatenate([ya, yb], axis=-1)
            h = h + jnp.einsum('bse,ed->bsd', mixed, w_out_even[i])
        else:
            hn = rms_norm(h, mix_norm_odd[i])
            proj = jnp.einsum('bsd,de->bse', hn, w_in_odd[i])
            u, vv, q, k, v = jnp.split(
                proj, [SGU_WIDTH, 2 * SGU_WIDTH, 2 * SGU_WIDTH + SB_WIDTH,
                       2 * SGU_WIDTH + 2 * SB_WIDTH], axis=-1)
            yc = spatial_gating(jax.nn.gelu(u, approximate=False),
                                jax.nn.gelu(vv, approximate=False),
                                sgu_v_norm[i], sgu_w[i], sgu_b[i])
            hs = (b, s, SB_HEADS, SB_HEAD_DIM)
            yd = stick_breaking_attention(q.reshape(hs), k.reshape(hs), v.reshape(hs))
            mixed = jnp.concatenate([yc, yd], axis=-1)
            h = h + jnp.einsum('bse,ed->bsd', mixed, w_out_odd[i])
        h = h + hierarchical_moe(rms_norm(h, ffn_norm[layer]), w_router_group[layer],
                                 w_router_expert[layer], w_exp_gate[layer],
                                 w_exp_up[layer], w_exp_down[layer])
    return h
```

```python
import functools
import math

import jax
import jax.numpy as jnp
from jax import lax
from jax.experimental import pallas as pl
from jax.experimental.pallas import tpu as pltpu

F32 = jnp.float32
BF16 = jnp.bfloat16

D_MODEL = 1024
CHUNK = 64
EPS = 1e-6
HEAD_DIM = 64
MIX_WIDTH = 512
LANES = 128
ATT_LEFT = 8 * CHUNK
ATT_MAX_REL = 128
POOL_WINDOWS = (2, 4, 8, 16)
POOL_HALO = 16
SGU_BLOCK = 128
N_GROUPS = 4
N_EXP_PER_GROUP = 8
N_EXPERTS = N_GROUPS * N_EXP_PER_GROUP
EXPERT_FF = 256
ROUTER_LANES = 128
NEG_BIG = -1e30
VMEM_LIMIT = 56 * 1024 * 1024

NT_DIMS = (((1,), (1,)), ((), ()))


def _cparams(sem):
    return pltpu.CompilerParams(dimension_semantics=sem, vmem_limit_bytes=VMEM_LIMIT)


def _rms(x, gain):
    return x * lax.rsqrt(jnp.mean(x * x, axis=-1, keepdims=True) + EPS) * gain


def _split_dot(x, m):
    hi = x.astype(BF16)
    lo = (x - hi.astype(F32)).astype(BF16)
    return (jnp.dot(hi, m, preferred_element_type=F32)
            + jnp.dot(lo, m, preferred_element_type=F32))


def _head_rms(t, gain):
    n = t.shape[-1]
    r = lax.broadcasted_iota(jnp.int32, (n, n), 0) // HEAD_DIM
    c = lax.broadcasted_iota(jnp.int32, (n, n), 1) // HEAD_DIM
    bd = jnp.where(r == c, 1.0, 0.0).astype(BF16)
    ms = _split_dot(t * t, bd) * (1.0 / HEAD_DIM)
    return t * lax.rsqrt(ms + EPS) * gain


def _in_even_body(h_ref, g_ref, w_ref, qg_ref, kg_ref, q_ref, k_ref, v_ref, p_ref):
    j = pl.program_id(1)

    @pl.when(j == 0)
    def _():
        k_ref[...] = jnp.zeros_like(k_ref)
        v_ref[...] = jnp.zeros_like(v_ref)

    @pl.when(j > 0)
    def _():
        xn = _rms(h_ref[0], g_ref[...])
        proj = jnp.dot(xn.astype(BF16), w_ref[...], preferred_element_type=F32)
        w = MIX_WIDTH
        q_ref[0] = (_head_rms(proj[:, :w], qg_ref[...]) * (1.0 / math.sqrt(HEAD_DIM))).astype(BF16)
        k_ref[0] = _head_rms(proj[:, w:2 * w], kg_ref[...]).astype(BF16)
        v_ref[0] = proj[:, 2 * w:3 * w].astype(BF16)
        p_ref[0] = proj[:, 3 * w:].astype(BF16)


def _in_even(h, gain, w, q_gain, k_gain, tm):
    b, s, d = h.shape
    assert tm == ATT_LEFT and s % tm == 0
    nt = s // tm
    cur = lambda bi, j: (bi, jnp.maximum(j - 1, 0), 0)
    const = lambda bi, j: (0, 0)
    out_sds = lambda rows: jax.ShapeDtypeStruct((b, rows, MIX_WIDTH), BF16)
    return pl.pallas_call(
        _in_even_body,
        grid=(b, nt + 1),
        in_specs=[
            pl.BlockSpec((1, tm, d), cur),
            pl.BlockSpec((1, d), const),
            pl.BlockSpec(w.shape, const),
            pl.BlockSpec((1, MIX_WIDTH), const),
            pl.BlockSpec((1, MIX_WIDTH), const),
        ],
        out_specs=[
            pl.BlockSpec((1, tm, MIX_WIDTH), cur),
            pl.BlockSpec((1, tm, MIX_WIDTH), lambda bi, j: (bi, j, 0)),
            pl.BlockSpec((1, tm, MIX_WIDTH), lambda bi, j: (bi, j, 0)),
            pl.BlockSpec((1, tm, MIX_WIDTH), cur),
        ],
        out_shape=[out_sds(s), out_sds(s + ATT_LEFT), out_sds(s + ATT_LEFT), out_sds(s)],
        compiler_params=_cparams(("arbitrary", "arbitrary")),
        name="in_even",
    )(h, gain, w, q_gain, k_gain)


BAND_TQ = 2 * CHUNK
BAND_TK = BAND_TQ + ATT_LEFT


def _band_body(q_ref, k_ref, v_ref, bias_ref, o_ref):
    i = pl.program_id(1)
    start = pl.multiple_of(i * BAND_TQ, BAND_TQ)
    lane = lax.broadcasted_iota(jnp.int32, (BAND_TQ, LANES), 1)
    col = lax.broadcasted_iota(jnp.int32, (BAND_TQ, BAND_TK), 1)
    is_pad = (col + start) < ATT_LEFT
    for hp in range(MIX_WIDTH // LANES):
        ls = slice(hp * LANES, (hp + 1) * LANES)
        q = q_ref[0, :, ls]
        kb = k_ref[0, pl.ds(start, BAND_TK), ls]
        vb = v_ref[0, pl.ds(start, BAND_TK), ls]
        outs = []
        for a in range(2):
            qa = jnp.where((lane < HEAD_DIM) == (a == 0), q, jnp.zeros_like(q))
            s = lax.dot_general(qa, kb, NT_DIMS, preferred_element_type=F32)
            s = jnp.where(is_pad, NEG_BIG, s + bias_ref[2 * hp + a])
            m = jnp.max(s, axis=-1, keepdims=True)
            p = jnp.exp(s - m)
            l = jnp.sum(p, axis=-1, keepdims=True)
            o = jnp.dot(p.astype(BF16), vb, preferred_element_type=F32)
            outs.append(o / l)
        o_ref[0, :, ls] = jnp.where(lane < HEAD_DIM, outs[0], outs[1]).astype(BF16)


def _band_bias(rel_bias):
    r = jnp.arange(BAND_TQ)[:, None]
    j = jnp.arange(BAND_TK)[None, :]
    jb = j - CHUNK * (r // CHUNK)
    in_band = (jb >= 0) & (jb < ATT_LEFT + CHUNK)
    dist = (r % CHUNK + ATT_LEFT) - jb
    ridx = jnp.clip(dist, -ATT_MAX_REL, ATT_MAX_REL) + ATT_MAX_REL
    return jnp.where(in_band[None], rel_bias[:, ridx].astype(F32), NEG_BIG)


def _band_attention(q, k_pad, v_pad, bias):
    b, s, w = q.shape
    sp = k_pad.shape[1]
    return pl.pallas_call(
        _band_body,
        grid=(b, s // BAND_TQ),
        in_specs=[
            pl.BlockSpec((1, BAND_TQ, w), lambda bi, i: (bi, i, 0)),
            pl.BlockSpec((1, sp, w), lambda bi, i: (bi, 0, 0)),
            pl.BlockSpec((1, sp, w), lambda bi, i: (bi, 0, 0)),
            pl.BlockSpec(bias.shape, lambda bi, i: (0, 0, 0)),
        ],
        out_specs=pl.BlockSpec((1, BAND_TQ, w), lambda bi, i: (bi, i, 0)),
        out_shape=jax.ShapeDtypeStruct((b, s, w), BF16),
        compiler_params=_cparams(("arbitrary", "arbitrary")),
        name="band_attention",
    )(q, k_pad, v_pad, bias)


def _out_even_body(tiles_per_seq, ya_ref, p_ref, halo_ref, h_ref, pw_ref, ps_ref, wo_ref,
                   o_ref, p_scr, yb_scr):
    tm = p_ref.shape[0]
    it = pl.program_id(0) % tiles_per_seq
    halo = halo_ref[...].astype(F32)
    p_scr[0:POOL_HALO, :] = jnp.where(it == 0, jnp.zeros_like(halo), halo)
    p_scr[POOL_HALO:, :] = p_ref[...].astype(F32)
    t = it * tm + lax.broadcasted_iota(jnp.int32, (tm, 1), 0)
    for g, win in enumerate(POOL_WINDOWS):
        ls = slice(g * LANES, (g + 1) * LANES)
        cur = p_scr[POOL_HALO:POOL_HALO + tm, ls]
        acc = cur
        for dlt in range(1, win):
            acc = acc + p_scr[POOL_HALO - dlt:POOL_HALO - dlt + tm, ls]
        cnt = jnp.minimum(t + 1, win).astype(F32)
        mixed = acc / cnt - cur
        yb = jnp.dot(mixed.astype(BF16), pw_ref[g], preferred_element_type=F32) * ps_ref[:, ls]
        yb_scr[:, ls] = yb.astype(BF16)
    w = MIX_WIDTH
    o_ref[...] = (h_ref[...]
                  + jnp.dot(ya_ref[...], wo_ref[0:w, :], preferred_element_type=F32)
                  + jnp.dot(yb_scr[...], wo_ref[w:, :], preferred_element_type=F32))


def _out_even(ya, p, h, pool_w, pool_scale, w_out, seq, tm):
    n, d = h.shape
    w = MIX_WIDTH
    row = lambda i: (i, 0)
    const2 = lambda i: (0, 0)
    halo_blocks = tm // POOL_HALO
    return pl.pallas_call(
        functools.partial(_out_even_body, seq // tm),
        grid=(n // tm,),
        in_specs=[
            pl.BlockSpec((tm, w), row),
            pl.BlockSpec((tm, w), row),
            pl.BlockSpec((POOL_HALO, w), lambda i: (jnp.maximum(i * halo_blocks - 1, 0), 0)),
            pl.BlockSpec((tm, d), row),
            pl.BlockSpec(pool_w.shape, lambda i: (0, 0, 0)),
            pl.BlockSpec((1, w), const2),
            pl.BlockSpec(w_out.shape, const2),
        ],
        out_specs=pl.BlockSpec((tm, d), row),
        out_shape=jax.ShapeDtypeStruct((n, d), F32),
        scratch_shapes=[pltpu.VMEM((tm + POOL_HALO, w), F32), pltpu.VMEM((tm, w), BF16)],
        compiler_params=_cparams(("arbitrary",)),
        name="out_even",
    )(ya, p, p, h, pool_w, pool_scale, w_out)


def _router_body(h_ref, g_ref, wr_ref, xn_ref, comb_ref):
    xn = _rms(h_ref[...], g_ref[...])
    xn_ref[...] = xn.astype(BF16)
    logits = jnp.dot(xn, wr_ref[...], preferred_element_type=F32, precision=lax.Precision.HIGHEST)
    lane = lax.broadcasted_iota(jnp.int32, logits.shape, 1).astype(F32)
    ninf = -jnp.inf

    def top(vals):
        m = jnp.max(vals, axis=-1, keepdims=True)
        idx = jnp.min(jnp.where(vals == m, lane, float(ROUTER_LANES)), axis=-1, keepdims=True)
        return m, idx

    is_group = lane < N_GROUPS
    g_max, g_sel = top(jnp.where(is_group, logits, ninf))
    g_den = jnp.sum(jnp.where(is_group, jnp.exp(logits - g_max), 0.0), axis=-1, keepdims=True)
    g_weight = 1.0 / g_den
    lo = N_GROUPS + N_EXP_PER_GROUP * g_sel
    e_logits = jnp.where((lane >= lo) & (lane < lo + N_EXP_PER_GROUP), logits, ninf)
    e1, i1 = top(e_logits)
    e2, i2 = top(jnp.where(lane == i1, ninf, e_logits))
    t = jnp.exp(e2 - e1)
    w1 = g_weight / (1.0 + t)
    w2 = g_weight * t / (1.0 + t)
    comb_ref[...] = jnp.where(lane == i1, w1, jnp.where(lane == i2, w2, 0.0))


def _router(h, gain, w_router, tm):
    n, d = h.shape
    row = lambda i: (i, 0)
    const2 = lambda i: (0, 0)
    return pl.pallas_call(
        _router_body,
        grid=(n // tm,),
        in_specs=[pl.BlockSpec((tm, d), row), pl.BlockSpec((1, d), const2),
                  pl.BlockSpec(w_router.shape, const2)],
        out_specs=[pl.BlockSpec((tm, d), row), pl.BlockSpec((tm, ROUTER_LANES), row)],
        out_shape=[jax.ShapeDtypeStruct((n, d), BF16),
                   jax.ShapeDtypeStruct((n, ROUTER_LANES), F32)],
        compiler_params=_cparams(("arbitrary",)),
        name="moe_router",
    )(h, gain, w_router)


def _moe_body(x_ref, comb_ref, h_ref, wg_ref, wu_ref, wd_ref, o_ref, acc_ref):
    e = pl.program_id(1)

    @pl.when(e == 0)
    def _():
        acc_ref[...] = h_ref[...]

    x = x_ref[...]
    comb = comb_ref[...]
    lane = lax.broadcasted_iota(jnp.int32, comb.shape, 1)
    c = jnp.sum(jnp.where(lane == e + N_GROUPS, comb, 0.0), axis=-1, keepdims=True)
    gate = jnp.dot(x, wg_ref[0], preferred_element_type=F32)
    up = jnp.dot(x, wu_ref[0], preferred_element_type=F32)
    hid = gate * jax.nn.sigmoid(gate) * up * c
    acc_ref[...] += jnp.dot(hid.astype(BF16), wd_ref[0], preferred_element_type=F32)

    @pl.when(e == N_EXPERTS - 1)
    def _():
        o_ref[...] = acc_ref[...]


def _moe(xn, comb, h, w_gate, w_up, w_down, tm):
    n, d = h.shape
    f = EXPERT_FF
    row = lambda i, e: (i, 0)
    return pl.pallas_call(
        _moe_body,
        grid=(n // tm, N_EXPERTS),
        in_specs=[
            pl.BlockSpec((tm, d), row),
            pl.BlockSpec((tm, ROUTER_LANES), row),
            pl.BlockSpec((tm, d), row),
            pl.BlockSpec((1, d, f), lambda i, e: (e, 0, 0)),
            pl.BlockSpec((1, d, f), lambda i, e: (e, 0, 0)),
            pl.BlockSpec((1, f, d), lambda i, e: (e, 0, 0)),
        ],
        out_specs=pl.BlockSpec((tm, d), row),
        out_shape=jax.ShapeDtypeStruct((n, d), F32),
        scratch_shapes=[pltpu.VMEM((tm, d), F32)],
        compiler_params=_cparams(("arbitrary", "arbitrary")),
        name="moe_experts",
    )(xn, comb, h, w_gate, w_up, w_down)


def _gelu(x):
    return 0.5 * x * (1.0 + lax.erf(x * (1.0 / math.sqrt(2.0))))


def _in_odd_body(h_ref, g_ref, w_ref, vg_ref, u_ref, vn_ref, q_ref, k_ref, v_ref):
    xn = _rms(h_ref[...], g_ref[...])
    proj = jnp.dot(xn.astype(BF16), w_ref[...], preferred_element_type=F32)
    w = MIX_WIDTH
    u_ref[...] = _gelu(proj[:, :w]).astype(BF16)
    vn_ref[...] = _rms(_gelu(proj[:, w:2 * w]), vg_ref[...]).astype(BF16)
    q_ref[...] = (proj[:, 2 * w:3 * w] * (1.0 / math.sqrt(HEAD_DIM))).astype(BF16)
    k_ref[...] = proj[:, 3 * w:4 * w].astype(BF16)
    v_ref[...] = proj[:, 4 * w:].astype(BF16)


def _in_odd(h, gain, w, v_gain, tm):
    n, d = h.shape
    row = lambda i: (i, 0)
    const2 = lambda i: (0, 0)
    sds = jax.ShapeDtypeStruct((n, MIX_WIDTH), BF16)
    return pl.pallas_call(
        _in_odd_body,
        grid=(n // tm,),
        in_specs=[pl.BlockSpec((tm, d), row), pl.BlockSpec((1, d), const2),
                  pl.BlockSpec(w.shape, const2), pl.BlockSpec((1, MIX_WIDTH), const2)],
        out_specs=[pl.BlockSpec((tm, MIX_WIDTH), row)] * 5,
        out_shape=[sds] * 5,
        compiler_params=_cparams(("arbitrary",)),
        name="in_odd",
    )(h, gain, w, v_gain)


SB_T = 128


def _sb_body(q_ref, k_ref, v_ref, o_ref):
    i = pl.program_id(2)
    q = q_ref[0]
    lane = lax.broadcasted_iota(jnp.int32, (SB_T, LANES), 1)
    zero = jnp.zeros_like(q)
    q_heads = (jnp.where(lane < HEAD_DIM, q, zero), jnp.where(lane < HEAD_DIM, zero, q))
    row = lax.broadcasted_iota(jnp.int32, (SB_T, SB_T), 0)
    col = lax.broadcasted_iota(jnp.int32, (SB_T, SB_T), 1)
    rr = lax.broadcasted_iota(jnp.int32, (SB_T, 2 * SB_T), 0)
    cc = lax.broadcasted_iota(jnp.int32, (SB_T, 2 * SB_T), 1)
    suffix = jnp.where((cc >= SB_T) | (rr > cc), 1.0, 0.0).astype(BF16)

    def body(step, carry):
        accs, runs = carry
        kb = i - step
        ks = pl.multiple_of(kb * SB_T, SB_T)
        kblk = k_ref[0, pl.ds(ks, SB_T), :]
        vblk = v_ref[0, pl.ds(ks, SB_T), :]
        causal = (col + ks) < (row + i * SB_T)
        new_accs, new_runs = [], []
        for a in range(2):
            z = lax.dot_general(q_heads[a], kblk, NT_DIMS, preferred_element_type=F32)
            sp = jnp.log(1.0 + jnp.exp(-jnp.abs(z)))
            log_beta = jnp.minimum(z, 0.0) - sp
            log_rest = jnp.where(causal, log_beta - z, 0.0)
            tt = _split_dot(log_rest, suffix)
            att = jnp.where(causal, jnp.exp(log_beta + tt[:, :SB_T] + runs[a]), 0.0)
            new_accs.append(accs[a] + jnp.dot(att.astype(BF16), vblk, preferred_element_type=F32))
            new_runs.append(runs[a] + tt[:, SB_T:])
        return tuple(new_accs), tuple(new_runs)

    z0 = jnp.zeros((SB_T, LANES), F32)
    accs, _ = lax.fori_loop(0, i + 1, body, ((z0, z0), (z0, z0)))
    o_ref[0] = jnp.where(lane < HEAD_DIM, accs[0], accs[1]).astype(BF16)


def _sb_attention(q, k, v):
    b, s, w = q.shape
    qspec = pl.BlockSpec((1, SB_T, LANES), lambda bi, hp, i: (bi, i, hp))
    kvspec = pl.BlockSpec((1, s, LANES), lambda bi, hp, i: (bi, 0, hp))
    return pl.pallas_call(
        _sb_body,
        grid=(b, w // LANES, s // SB_T),
        in_specs=[qspec, kvspec, kvspec],
        out_specs=qspec,
        out_shape=jax.ShapeDtypeStruct((b, s, w), BF16),
        compiler_params=_cparams(("arbitrary", "arbitrary", "arbitrary")),
        name="sb_attention",
    )(q, k, v)


def _out_odd_body(u_ref, vn_ref, yd_ref, h_ref, ws_ref, bs_ref, wo_ref, o_ref, yc_scr):
    tm = u_ref.shape[0]
    r = lax.broadcasted_iota(jnp.int32, (SGU_BLOCK, SGU_BLOCK), 0)
    c = lax.broadcasted_iota(jnp.int32, (SGU_BLOCK, SGU_BLOCK), 1)
    for g in range(MIX_WIDTH // LANES):
        ls = slice(g * LANES, (g + 1) * LANES)
        ws = jnp.where(c <= r, ws_ref[g], jnp.zeros_like(ws_ref[g]))
        for blk in range(tm // SGU_BLOCK):
            rs = slice(blk * SGU_BLOCK, (blk + 1) * SGU_BLOCK)
            mixed = jnp.dot(ws, vn_ref[rs, ls], preferred_element_type=F32) + bs_ref[g]
            yc_scr[rs, ls] = (u_ref[rs, ls].astype(F32) * mixed).astype(BF16)
    w = MIX_WIDTH
    o_ref[...] = (h_ref[...]
                  + jnp.dot(yc_scr[...], wo_ref[0:w, :], preferred_element_type=F32)
                  + jnp.dot(yd_ref[...], wo_ref[w:, :], preferred_element_type=F32))


def _out_odd(u, vn, yd, h, sgu_w, sgu_b, w_out, tm):
    n, d = h.shape
    w = MIX_WIDTH
    row = lambda i: (i, 0)
    const2 = lambda i: (0, 0)
    const3 = lambda i: (0, 0, 0)
    return pl.pallas_call(
        _out_odd_body,
        grid=(n // tm,),
        in_specs=[pl.BlockSpec((tm, w), row), pl.BlockSpec((tm, w), row), pl.BlockSpec((tm, w), row),
                  pl.BlockSpec((tm, d), row), pl.BlockSpec(sgu_w.shape, const3),
                  pl.BlockSpec(sgu_b.shape, const3), pl.BlockSpec(w_out.shape, const2)],
        out_specs=pl.BlockSpec((tm, d), row),
        out_shape=jax.ShapeDtypeStruct((n, d), F32),
        scratch_shapes=[pltpu.VMEM((tm, w), BF16)],
        compiler_params=_cparams(("arbitrary",)),
        name="out_odd",
    )(u, vn, yd, h, sgu_w, sgu_b, w_out)


def _moe_layer(h, gain, w_rg, w_re, w_gate, w_up, w_down):
    d = h.shape[1]
    pad = jnp.zeros((d, ROUTER_LANES - N_GROUPS - N_EXPERTS), F32)
    w_router = jnp.concatenate([w_rg, w_re, pad], axis=1)
    xn, comb = _router(h, gain[None, :], w_router, tm=512)
    wg = w_gate.reshape(N_EXPERTS, d, EXPERT_FF).astype(BF16)
    wu = w_up.reshape(N_EXPERTS, d, EXPERT_FF).astype(BF16)
    wd = w_down.reshape(N_EXPERTS, EXPERT_FF, d).astype(BF16)
    return _moe(xn, comb, h, wg, wu, wd, tm=1024)


def kernel(x, mix_norm_even, w_in_even, att_q_norm, att_k_norm, att_rel_bias, pool_w, pool_scale,
           w_out_even, mix_norm_odd, w_in_odd, sgu_v_norm, sgu_w, sgu_b, w_out_odd, ffn_norm,
           w_router_group, w_router_expert, w_exp_gate, w_exp_up, w_exp_down):
    b, s, d = x.shape
    n = b * s
    depth = ffn_norm.shape[0]
    heads = MIX_WIDTH // HEAD_DIM
    h = x.reshape(n, d)
    for layer in range(depth):
        i = layer // 2
        if layer % 2 == 0:
            q, k_pad, v_pad, p = _in_even(
                h.reshape(b, s, d), mix_norm_even[i][None, :], w_in_even[i].astype(BF16),
                jnp.tile(att_q_norm[i], heads)[None, :], jnp.tile(att_k_norm[i], heads)[None, :],
                tm=ATT_LEFT)
            ya = _band_attention(q, k_pad, v_pad, _band_bias(att_rel_bias[i]))
            h = _out_even(ya.reshape(n, MIX_WIDTH), p.reshape(n, MIX_WIDTH), h,
                          pool_w[i].astype(BF16), pool_scale[i][None, :],
                          w_out_even[i].astype(BF16), seq=s, tm=512)
        else:
            u, vn, q, k, v = _in_odd(h, mix_norm_odd[i][None, :], w_in_odd[i].astype(BF16),
                                     sgu_v_norm[i][None, :], tm=512)
            to3 = lambda t: t.reshape(b, s, MIX_WIDTH)
            yd = _sb_attention(to3(q), to3(k), to3(v))
            bias = jnp.broadcast_to(sgu_b[i][:, :, None], (N_GROUPS, SGU_BLOCK, LANES))
            h = _out_odd(u, vn, yd.reshape(n, MIX_WIDTH), h, sgu_w[i].astype(BF16), bias,
                         w_out_odd[i].astype(BF16), tm=512)
        h = _moe_layer(h, ffn_norm[layer], w_router_group[layer], w_router_expert[layer],
                       w_exp_gate[layer], w_exp_up[layer], w_exp_down[layer])
    return h.reshape(b, s, d)
```

```python
import functools
import math

import jax
import jax.numpy as jnp
from jax import lax
from jax.experimental import pallas as pl
from jax.experimental.pallas import tpu as pltpu

F32 = jnp.float32
BF16 = jnp.bfloat16

D_MODEL = 1024
CHUNK = 64
EPS = 1e-6
HEAD_DIM = 64
MIX_WIDTH = 512
LANES = 128
ATT_LEFT = 8 * CHUNK
ATT_MAX_REL = 128
POOL_WINDOWS = (2, 4, 8, 16)
POOL_HALO = 16
SGU_BLOCK = 128
N_GROUPS = 4
N_EXP_PER_GROUP = 8
N_EXPERTS = N_GROUPS * N_EXP_PER_GROUP
EXPERT_FF = 256
ROUTER_LANES = 128
NEG_BIG = -1e30
VMEM_LIMIT = 56 * 1024 * 1024

NT_DIMS = (((1,), (1,)), ((), ()))


def _cparams(sem):
    return pltpu.CompilerParams(dimension_semantics=sem, vmem_limit_bytes=VMEM_LIMIT)


def _rms(x, gain):
    return x * lax.rsqrt(jnp.mean(x * x, axis=-1, keepdims=True) + EPS) * gain


def _split_dot(x, m):
    hi = x.astype(BF16)
    lo = (x - hi.astype(F32)).astype(BF16)
    return (jnp.dot(hi, m, preferred_element_type=F32)
            + jnp.dot(lo, m, preferred_element_type=F32))


def _head_rms(t, gain):
    n = t.shape[-1]
    r = lax.broadcasted_iota(jnp.int32, (n, n), 0) // HEAD_DIM
    c = lax.broadcasted_iota(jnp.int32, (n, n), 1) // HEAD_DIM
    bd = jnp.where(r == c, 1.0, 0.0).astype(BF16)
    ms = _split_dot(t * t, bd) * (1.0 / HEAD_DIM)
    return t * lax.rsqrt(ms + EPS) * gain


def _in_even_body(h_ref, g_ref, w_ref, qg_ref, kg_ref, q_ref, k_ref, v_ref, p_ref):
    j = pl.program_id(1)

    @pl.when(j == 0)
    def _():
        k_ref[...] = jnp.zeros_like(k_ref)
        v_ref[...] = jnp.zeros_like(v_ref)

    @pl.when(j > 0)
    def _():
        xn = _rms(h_ref[0], g_ref[...])
        proj = jnp.dot(xn.astype(BF16), w_ref[...], preferred_element_type=F32)
        w = MIX_WIDTH
        q_ref[0] = (_head_rms(proj[:, :w], qg_ref[...]) * (1.0 / math.sqrt(HEAD_DIM))).astype(BF16)
        k_ref[0] = _head_rms(proj[:, w:2 * w], kg_ref[...]).astype(BF16)
        v_ref[0] = proj[:, 2 * w:3 * w].astype(BF16)
        p_ref[0] = proj[:, 3 * w:].astype(BF16)


def _in_even(h, gain, w, q_gain, k_gain, tm):
    b, s, d = h.shape
    assert tm == ATT_LEFT and s % tm == 0
    nt = s // tm
    cur = lambda bi, j: (bi, jnp.maximum(j - 1, 0), 0)
    const = lambda bi, j: (0, 0)
    out_sds = lambda rows: jax.ShapeDtypeStruct((b, rows, MIX_WIDTH), BF16)
    return pl.pallas_call(
        _in_even_body,
        grid=(b, nt + 1),
        in_specs=[
            pl.BlockSpec((1, tm, d), cur),
            pl.BlockSpec((1, d), const),
            pl.BlockSpec(w.shape, const),
            pl.BlockSpec((1, MIX_WIDTH), const),
            pl.BlockSpec((1, MIX_WIDTH), const),
        ],
        out_specs=[
            pl.BlockSpec((1, tm, MIX_WIDTH), cur),
            pl.BlockSpec((1, tm, MIX_WIDTH), lambda bi, j: (bi, j, 0)),
            pl.BlockSpec((1, tm, MIX_WIDTH), lambda bi, j: (bi, j, 0)),
            pl.BlockSpec((1, tm, MIX_WIDTH), cur),
        ],
        out_shape=[out_sds(s), out_sds(s + ATT_LEFT), out_sds(s + ATT_LEFT), out_sds(s)],
        compiler_params=_cparams(("arbitrary", "arbitrary")),
        name="in_even",
    )(h, gain, w, q_gain, k_gain)


BAND_TQ = 2 * CHUNK
BAND_TK = BAND_TQ + ATT_LEFT


def _band_body(q_ref, k_ref, v_ref, bias_ref, o_ref):
    i = pl.program_id(1)
    start = pl.multiple_of(i * BAND_TQ, BAND_TQ)
    lane = lax.broadcasted_iota(jnp.int32, (BAND_TQ, LANES), 1)
    col = lax.broadcasted_iota(jnp.int32, (BAND_TQ, BAND_TK), 1)
    is_pad = (col + start) < ATT_LEFT
    for hp in range(MIX_WIDTH // LANES):
        ls = slice(hp * LANES, (hp + 1) * LANES)
        q = q_ref[0, :, ls]
        kb = k_ref[0, pl.ds(start, BAND_TK), ls]
        vb = v_ref[0, pl.ds(start, BAND_TK), ls]
        outs = []
        for a in range(2):
            qa = jnp.where((lane < HEAD_DIM) == (a == 0), q, jnp.zeros_like(q))
            s = lax.dot_general(qa, kb, NT_DIMS, preferred_element_type=F32)
            s = jnp.where(is_pad, NEG_BIG, s + bias_ref[2 * hp + a])
            m = jnp.max(s, axis=-1, keepdims=True)
            p = jnp.exp(s - m)
            l = jnp.sum(p, axis=-1, keepdims=True)
            o = jnp.dot(p.astype(BF16), vb, preferred_element_type=F32)
            outs.append(o / l)
        o_ref[0, :, ls] = jnp.where(lane < HEAD_DIM, outs[0], outs[1]).astype(BF16)


def _band_bias(rel_bias):
    heads = rel_bias.shape[0]
    r = jnp.arange(BAND_TQ)[:, None]
    j = jnp.arange(BAND_TK)[None, :]
    jb = j - CHUNK * (r // CHUNK)
    in_band = (jb >= 0) & (jb < ATT_LEFT + CHUNK)
    period = BAND_TK + BAND_TQ
    far = jnp.broadcast_to(rel_bias[:, 2 * ATT_MAX_REL:], (heads, ATT_LEFT - ATT_MAX_REL + 1))
    near = rel_bias[:, 2 * ATT_MAX_REL - 1:0:-1]
    wrap = jnp.broadcast_to(rel_bias[:, 2 * ATT_MAX_REL:], (heads, period - BAND_TK))
    g = jnp.concatenate([far, near, wrap], axis=1).astype(F32)
    assert g.shape[1] == period
    toep = jnp.tile(g, (1, BAND_TQ))[:, :BAND_TQ * (period - 1)].reshape(heads, BAND_TQ, period - 1)
    return jnp.where(in_band[None], toep[:, :, :BAND_TK], NEG_BIG)


def _band_attention(q, k_pad, v_pad, bias):
    b, s, w = q.shape
    sp = k_pad.shape[1]
    return pl.pallas_call(
        _band_body,
        grid=(b, s // BAND_TQ),
        in_specs=[
            pl.BlockSpec((1, BAND_TQ, w), lambda bi, i: (bi, i, 0)),
            pl.BlockSpec((1, sp, w), lambda bi, i: (bi, 0, 0)),
            pl.BlockSpec((1, sp, w), lambda bi, i: (bi, 0, 0)),
            pl.BlockSpec(bias.shape, lambda bi, i: (0, 0, 0)),
        ],
        out_specs=pl.BlockSpec((1, BAND_TQ, w), lambda bi, i: (bi, i, 0)),
        out_shape=jax.ShapeDtypeStruct((b, s, w), BF16),
        compiler_params=_cparams(("arbitrary", "arbitrary")),
        name="band_attention",
    )(q, k_pad, v_pad, bias)


def _out_even_body(tiles_per_seq, ya_ref, p_ref, halo_ref, h_ref, pw_ref, ps_ref, wo_ref,
                   o_ref, p_scr, yb_scr):
    tm = p_ref.shape[0]
    it = pl.program_id(0) % tiles_per_seq
    halo = halo_ref[...].astype(F32)
    p_scr[0:POOL_HALO, :] = jnp.where(it == 0, jnp.zeros_like(halo), halo)
    p_scr[POOL_HALO:, :] = p_ref[...].astype(F32)
    t = it * tm + lax.broadcasted_iota(jnp.int32, (tm, 1), 0)
    for g, win in enumerate(POOL_WINDOWS):
        ls = slice(g * LANES, (g + 1) * LANES)
        cur = p_scr[POOL_HALO:POOL_HALO + tm, ls]
        acc = cur
        for dlt in range(1, win):
            acc = acc + p_scr[POOL_HALO - dlt:POOL_HALO - dlt + tm, ls]
        cnt = jnp.minimum(t + 1, win).astype(F32)
        mixed = acc / cnt - cur
        yb = jnp.dot(mixed.astype(BF16), pw_ref[g], preferred_element_type=F32) * ps_ref[:, ls]
        yb_scr[:, ls] = yb.astype(BF16)
    w = MIX_WIDTH
    o_ref[...] = (h_ref[...]
                  + jnp.dot(ya_ref[...], wo_ref[0:w, :], preferred_element_type=F32)
                  + jnp.dot(yb_scr[...], wo_ref[w:, :], preferred_element_type=F32))


def _out_even(ya, p, h, pool_w, pool_scale, w_out, seq, tm):
    n, d = h.shape
    w = MIX_WIDTH
    row = lambda i: (i, 0)
    const2 = lambda i: (0, 0)
    halo_blocks = tm // POOL_HALO
    return pl.pallas_call(
        functools.partial(_out_even_body, seq // tm),
        grid=(n // tm,),
        in_specs=[
            pl.BlockSpec((tm, w), row),
            pl.BlockSpec((tm, w), row),
            pl.BlockSpec((POOL_HALO, w), lambda i: (jnp.maximum(i * halo_blocks - 1, 0), 0)),
            pl.BlockSpec((tm, d), row),
            pl.BlockSpec(pool_w.shape, lambda i: (0, 0, 0)),
            pl.BlockSpec((1, w), const2),
            pl.BlockSpec(w_out.shape, const2),
        ],
        out_specs=pl.BlockSpec((tm, d), row),
        out_shape=jax.ShapeDtypeStruct((n, d), F32),
        scratch_shapes=[pltpu.VMEM((tm + POOL_HALO, w), F32), pltpu.VMEM((tm, w), BF16)],
        compiler_params=_cparams(("arbitrary",)),
        name="out_even",
    )(ya, p, p, h, pool_w, pool_scale, w_out)


def _router_body(h_ref, g_ref, wr_ref, xn_ref, comb_ref):
    xn = _rms(h_ref[...], g_ref[...])
    xn_ref[...] = xn.astype(BF16)
    logits = jnp.dot(xn, wr_ref[...], preferred_element_type=F32, precision=lax.Precision.HIGHEST)
    lane = lax.broadcasted_iota(jnp.int32, logits.shape, 1).astype(F32)
    ninf = -jnp.inf

    def top(vals):
        m = jnp.max(vals, axis=-1, keepdims=True)
        idx = jnp.min(jnp.where(vals == m, lane, float(ROUTER_LANES)), axis=-1, keepdims=True)
        return m, idx

    is_group = lane < N_GROUPS
    g_max, g_sel = top(jnp.where(is_group, logits, ninf))
    g_den = jnp.sum(jnp.where(is_group, jnp.exp(logits - g_max), 0.0), axis=-1, keepdims=True)
    g_weight = 1.0 / g_den
    lo = N_GROUPS + N_EXP_PER_GROUP * g_sel
    e_logits = jnp.where((lane >= lo) & (lane < lo + N_EXP_PER_GROUP), logits, ninf)
    e1, i1 = top(e_logits)
    e2, i2 = top(jnp.where(lane == i1, ninf, e_logits))
    t = jnp.exp(e2 - e1)
    w1 = g_weight / (1.0 + t)
    w2 = g_weight * t / (1.0 + t)
    comb_ref[...] = jnp.where(lane == i1, w1, jnp.where(lane == i2, w2, 0.0))


def _router(h, gain, w_router, tm):
    n, d = h.shape
    row = lambda i: (i, 0)
    const2 = lambda i: (0, 0)
    return pl.pallas_call(
        _router_body,
        grid=(n // tm,),
        in_specs=[pl.BlockSpec((tm, d), row), pl.BlockSpec((1, d), const2),
                  pl.BlockSpec(w_router.shape, const2)],
        out_specs=[pl.BlockSpec((tm, d), row), pl.BlockSpec((tm, ROUTER_LANES), row)],
        out_shape=[jax.ShapeDtypeStruct((n, d), BF16),
                   jax.ShapeDtypeStruct((n, ROUTER_LANES), F32)],
        compiler_params=_cparams(("arbitrary",)),
        name="moe_router",
    )(h, gain, w_router)


def _moe_body(x_ref, comb_ref, h_ref, wg_ref, wu_ref, wd_ref, o_ref, acc_ref):
    e = pl.program_id(1)

    @pl.when(e == 0)
    def _():
        acc_ref[...] = h_ref[...]

    x = x_ref[...]
    comb = comb_ref[...]
    lane = lax.broadcasted_iota(jnp.int32, comb.shape, 1)
    c = jnp.sum(jnp.where(lane == e + N_GROUPS, comb, 0.0), axis=-1, keepdims=True)
    gate = jnp.dot(x, wg_ref[0], preferred_element_type=F32)
    up = jnp.dot(x, wu_ref[0], preferred_element_type=F32)
    hid = gate * jax.nn.sigmoid(gate) * up * c
    acc_ref[...] += jnp.dot(hid.astype(BF16), wd_ref[0], preferred_element_type=F32)

    @pl.when(e == N_EXPERTS - 1)
    def _():
        o_ref[...] = acc_ref[...]


def _moe(xn, comb, h, w_gate, w_up, w_down, tm):
    n, d = h.shape
    f = EXPERT_FF
    row = lambda i, e: (i, 0)
    return pl.pallas_call(
        _moe_body,
        grid=(n // tm, N_EXPERTS),
        in_specs=[
            pl.BlockSpec((tm, d), row),
            pl.BlockSpec((tm, ROUTER_LANES), row),
            pl.BlockSpec((tm, d), row),
            pl.BlockSpec((1, d, f), lambda i, e: (e, 0, 0)),
            pl.BlockSpec((1, d, f), lambda i, e: (e, 0, 0)),
            pl.BlockSpec((1, f, d), lambda i, e: (e, 0, 0)),
        ],
        out_specs=pl.BlockSpec((tm, d), row),
        out_shape=jax.ShapeDtypeStruct((n, d), F32),
        scratch_shapes=[pltpu.VMEM((tm, d), F32)],
        compiler_params=_cparams(("arbitrary", "arbitrary")),
        name="moe_experts",
    )(xn, comb, h, w_gate, w_up, w_down)


def _gelu(x):
    return 0.5 * x * (1.0 + lax.erf(x * (1.0 / math.sqrt(2.0))))


def _in_odd_body(h_ref, g_ref, w_ref, vg_ref, u_ref, vn_ref, q_ref, k_ref, v_ref):
    xn = _rms(h_ref[...], g_ref[...])
    proj = jnp.dot(xn.astype(BF16), w_ref[...], preferred_element_type=F32)
    w = MIX_WIDTH
    u_ref[...] = _gelu(proj[:, :w]).astype(BF16)
    vn_ref[...] = _rms(_gelu(proj[:, w:2 * w]), vg_ref[...]).astype(BF16)
    q_ref[...] = (proj[:, 2 * w:3 * w] * (1.0 / math.sqrt(HEAD_DIM))).astype(BF16)
    k_ref[...] = proj[:, 3 * w:4 * w].astype(BF16)
    v_ref[...] = proj[:, 4 * w:].astype(BF16)


def _in_odd(h, gain, w, v_gain, tm):
    n, d = h.shape
    row = lambda i: (i, 0)
    const2 = lambda i: (0, 0)
    sds = jax.ShapeDtypeStruct((n, MIX_WIDTH), BF16)
    return pl.pallas_call(
        _in_odd_body,
        grid=(n // tm,),
        in_specs=[pl.BlockSpec((tm, d), row), pl.BlockSpec((1, d), const2),
                  pl.BlockSpec(w.shape, const2), pl.BlockSpec((1, MIX_WIDTH), const2)],
        out_specs=[pl.BlockSpec((tm, MIX_WIDTH), row)] * 5,
        out_shape=[sds] * 5,
        compiler_params=_cparams(("arbitrary",)),
        name="in_odd",
    )(h, gain, w, v_gain)


SB_T = 128
SB_NB = 3
SB_TK = SB_NB * SB_T
SB_QBLK = 512
SB_UNDERFLOW = -104.0


def _sb_body(q_ref, k_ref, v_ref, o_ref, acc_scr, run_scr):
    qi = pl.program_id(2)
    lane = lax.broadcasted_iota(jnp.int32, (SB_T, LANES), 1)
    row = lax.broadcasted_iota(jnp.int32, (SB_T, 1), 0)
    col = lax.broadcasted_iota(jnp.int32, (SB_T, SB_TK), 1)
    rr = lax.broadcasted_iota(jnp.int32, (SB_T, 2 * SB_T), 0)
    cc = lax.broadcasted_iota(jnp.int32, (SB_T, 2 * SB_T), 1)
    suffix = jnp.where((cc >= SB_T) | (rr > cc), 1.0, 0.0).astype(BF16)

    def subtile(sub, _):
        q_lo = pl.multiple_of(sub * SB_T, SB_T)
        qs = qi * SB_QBLK + q_lo
        q = q_ref[0, pl.ds(q_lo, SB_T), :]
        zero = jnp.zeros_like(q)
        q_heads = (jnp.where(lane < HEAD_DIM, q, zero), jnp.where(lane < HEAD_DIM, zero, q))
        acc_scr[...] = jnp.zeros_like(acc_scr)
        run_scr[...] = jnp.zeros_like(run_scr)

        def cond(carry):
            hi, done = carry
            return (hi > 0) & (done == 0)

        def body(carry):
            hi, _ = carry
            ks = pl.multiple_of(jnp.maximum(hi - SB_TK, 0), SB_T)
            kt = k_ref[0, pl.ds(ks, SB_TK), :]
            vt = v_ref[0, pl.ds(ks, SB_TK), :]
            valid = col < (jnp.minimum(row + qs, hi) - ks)
            worst = None
            for a in range(2):
                z = lax.dot_general(q_heads[a], kt, NT_DIMS, preferred_element_type=F32)
                sp = jnp.log(1.0 + jnp.exp(-jnp.abs(z)))
                log_beta = jnp.minimum(z, 0.0) - sp
                log_rest = jnp.where(valid, log_beta - z, 0.0)
                run = run_scr[a]
                pieces = [None] * SB_NB
                for blk in reversed(range(SB_NB)):
                    bs = slice(blk * SB_T, (blk + 1) * SB_T)
                    tt = _split_dot(log_rest[:, bs], suffix)
                    pieces[blk] = jnp.exp(log_beta[:, bs] + tt[:, :SB_T] + run)
                    run = run + tt[:, SB_T:]
                att = jnp.where(valid, jnp.concatenate(pieces, axis=1), 0.0)
                acc_scr[a] += jnp.dot(att.astype(BF16), vt, preferred_element_type=F32)
                run_scr[a] = run
                top = jnp.max(run)
                worst = top if worst is None else jnp.maximum(worst, top)
            return ks, (worst <= SB_UNDERFLOW).astype(jnp.int32)

        lax.while_loop(cond, body, (qs + SB_T, jnp.int32(0)))
        o_ref[0, pl.ds(q_lo, SB_T), :] = jnp.where(lane < HEAD_DIM, acc_scr[0], acc_scr[1]).astype(BF16)
        return 0

    lax.fori_loop(0, SB_QBLK // SB_T, subtile, 0)


def _sb_attention(q, k, v):
    b, s, w = q.shape
    assert s % SB_QBLK == 0 and s >= SB_TK
    qspec = pl.BlockSpec((1, SB_QBLK, LANES), lambda bi, hp, i: (bi, i, hp))
    kvspec = pl.BlockSpec((1, s, LANES), lambda bi, hp, i: (bi, 0, hp))
    return pl.pallas_call(
        _sb_body,
        grid=(b, w // LANES, s // SB_QBLK),
        in_specs=[qspec, kvspec, kvspec],
        out_specs=qspec,
        out_shape=jax.ShapeDtypeStruct((b, s, w), BF16),
        scratch_shapes=[pltpu.VMEM((2, SB_T, LANES), F32), pltpu.VMEM((2, SB_T, LANES), F32)],
        compiler_params=_cparams(("arbitrary", "arbitrary", "arbitrary")),
        name="sb_attention",
    )(q, k, v)


def _out_odd_body(u_ref, vn_ref, yd_ref, h_ref, ws_ref, bs_ref, wo_ref, o_ref, yc_scr):
    tm = u_ref.shape[0]
    r = lax.broadcasted_iota(jnp.int32, (SGU_BLOCK, SGU_BLOCK), 0)
    c = lax.broadcasted_iota(jnp.int32, (SGU_BLOCK, SGU_BLOCK), 1)
    for g in range(MIX_WIDTH // LANES):
        ls = slice(g * LANES, (g + 1) * LANES)
        ws = jnp.where(c <= r, ws_ref[g], jnp.zeros_like(ws_ref[g]))
        for blk in range(tm // SGU_BLOCK):
            rs = slice(blk * SGU_BLOCK, (blk + 1) * SGU_BLOCK)
            mixed = jnp.dot(ws, vn_ref[rs, ls], preferred_element_type=F32) + bs_ref[g]
            yc_scr[rs, ls] = (u_ref[rs, ls].astype(F32) * mixed).astype(BF16)
    w = MIX_WIDTH
    o_ref[...] = (h_ref[...]
                  + jnp.dot(yc_scr[...], wo_ref[0:w, :], preferred_element_type=F32)
                  + jnp.dot(yd_ref[...], wo_ref[w:, :], preferred_element_type=F32))


def _out_odd(u, vn, yd, h, sgu_w, sgu_b, w_out, tm):
    n, d = h.shape
    w = MIX_WIDTH
    row = lambda i: (i, 0)
    const2 = lambda i: (0, 0)
    const3 = lambda i: (0, 0, 0)
    return pl.pallas_call(
        _out_odd_body,
        grid=(n // tm,),
        in_specs=[pl.BlockSpec((tm, w), row), pl.BlockSpec((tm, w), row), pl.BlockSpec((tm, w), row),
                  pl.BlockSpec((tm, d), row), pl.BlockSpec(sgu_w.shape, const3),
                  pl.BlockSpec(sgu_b.shape, const3), pl.BlockSpec(w_out.shape, const2)],
        out_specs=pl.BlockSpec((tm, d), row),
        out_shape=jax.ShapeDtypeStruct((n, d), F32),
        scratch_shapes=[pltpu.VMEM((tm, w), BF16)],
        compiler_params=_cparams(("arbitrary",)),
        name="out_odd",
    )(u, vn, yd, h, sgu_w, sgu_b, w_out)


def _moe_layer(h, gain, w_rg, w_re, w_gate, w_up, w_down):
    d = h.shape[1]
    pad = jnp.zeros((d, ROUTER_LANES - N_GROUPS - N_EXPERTS), F32)
    w_router = jnp.concatenate([w_rg, w_re, pad], axis=1)
    xn, comb = _router(h, gain[None, :], w_router, tm=512)
    wg = w_gate.reshape(N_EXPERTS, d, EXPERT_FF).astype(BF16)
    wu = w_up.reshape(N_EXPERTS, d, EXPERT_FF).astype(BF16)
    wd = w_down.reshape(N_EXPERTS, EXPERT_FF, d).astype(BF16)
    return _moe(xn, comb, h, wg, wu, wd, tm=1024)


def kernel(x, mix_norm_even, w_in_even, att_q_norm, att_k_norm, att_rel_bias, pool_w, pool_scale,
           w_out_even, mix_norm_odd, w_in_odd, sgu_v_norm, sgu_w, sgu_b, w_out_odd, ffn_norm,
           w_router_group, w_router_expert, w_exp_gate, w_exp_up, w_exp_down):
    b, s, d = x.shape
    n = b * s
    depth = ffn_norm.shape[0]
    heads = MIX_WIDTH // HEAD_DIM
    h = x.reshape(n, d)
    for layer in range(depth):
        i = layer // 2
        if layer % 2 == 0:
            q, k_pad, v_pad, p = _in_even(
                h.reshape(b, s, d), mix_norm_even[i][None, :], w_in_even[i].astype(BF16),
                jnp.tile(att_q_norm[i], heads)[None, :], jnp.tile(att_k_norm[i], heads)[None, :],
                tm=ATT_LEFT)
            ya = _band_attention(q, k_pad, v_pad, _band_bias(att_rel_bias[i]))
            h = _out_even(ya.reshape(n, MIX_WIDTH), p.reshape(n, MIX_WIDTH), h,
                          pool_w[i].astype(BF16), pool_scale[i][None, :],
                          w_out_even[i].astype(BF16), seq=s, tm=512)
        else:
            u, vn, q, k, v = _in_odd(h, mix_norm_odd[i][None, :], w_in_odd[i].astype(BF16),
                                     sgu_v_norm[i][None, :], tm=512)
            to3 = lambda t: t.reshape(b, s, MIX_WIDTH)
            yd = _sb_attention(to3(q), to3(k), to3(v))
            bias = jnp.broadcast_to(sgu_b[i][:, :, None], (N_GROUPS, SGU_BLOCK, LANES))
            h = _out_odd(u, vn, yd.reshape(n, MIX_WIDTH), h, sgu_w[i].astype(BF16), bias,
                         w_out_odd[i].astype(BF16), tm=512)
        h = _moe_layer(h, ffn_norm[layer], w_router_group[layer], w_router_expert[layer],
                       w_exp_gate[layer], w_exp_up[layer], w_exp_down[layer])
    return h.reshape(b, s, d)
```

```python
import functools
import math

import jax
import jax.numpy as jnp
from jax import lax
from jax.experimental import pallas as pl
from jax.experimental.pallas import tpu as pltpu

F32 = jnp.float32
BF16 = jnp.bfloat16

D_MODEL = 1024
CHUNK = 64
EPS = 1e-6
HEAD_DIM = 64
MIX_WIDTH = 512
LANES = 128
ATT_LEFT = 8 * CHUNK
ATT_MAX_REL = 128
POOL_WINDOWS = (2, 4, 8, 16)
POOL_HALO = 16
SGU_BLOCK = 128
N_GROUPS = 4
N_EXP_PER_GROUP = 8
N_EXPERTS = N_GROUPS * N_EXP_PER_GROUP
EXPERT_FF = 256
ROUTER_LANES = 128
NEG_BIG = -1e30
VMEM_LIMIT = 56 * 1024 * 1024

NT_DIMS = (((1,), (1,)), ((), ()))


def _cparams(sem):
    return pltpu.CompilerParams(dimension_semantics=sem, vmem_limit_bytes=VMEM_LIMIT)


def _rms(x, gain):
    return x * lax.rsqrt(jnp.mean(x * x, axis=-1, keepdims=True) + EPS) * gain


def _split_dot(x, m):
    hi = x.astype(BF16)
    lo = (x - hi.astype(F32)).astype(BF16)
    return (jnp.dot(hi, m, preferred_element_type=F32)
            + jnp.dot(lo, m, preferred_element_type=F32))


def _head_rms(t, gain):
    n = t.shape[-1]
    r = lax.broadcasted_iota(jnp.int32, (n, n), 0) // HEAD_DIM
    c = lax.broadcasted_iota(jnp.int32, (n, n), 1) // HEAD_DIM
    bd = jnp.where(r == c, 1.0, 0.0).astype(BF16)
    ms = _split_dot(t * t, bd) * (1.0 / HEAD_DIM)
    return t * lax.rsqrt(ms + EPS) * gain


def _in_even_body(h_ref, g_ref, w_ref, qg_ref, kg_ref, q_ref, k_ref, v_ref, p_ref):
    j = pl.program_id(1)

    @pl.when(j == 0)
    def _():
        k_ref[...] = jnp.zeros_like(k_ref)
        v_ref[...] = jnp.zeros_like(v_ref)

    @pl.when(j > 0)
    def _():
        xn = _rms(h_ref[0], g_ref[...])
        proj = jnp.dot(xn.astype(BF16), w_ref[...], preferred_element_type=F32)
        w = MIX_WIDTH
        q_ref[0] = (_head_rms(proj[:, :w], qg_ref[...]) * (1.0 / math.sqrt(HEAD_DIM))).astype(BF16)
        k_ref[0] = _head_rms(proj[:, w:2 * w], kg_ref[...]).astype(BF16)
        v_ref[0] = proj[:, 2 * w:3 * w].astype(BF16)
        p_ref[0] = proj[:, 3 * w:].astype(BF16)


def _in_even(h, gain, w, q_gain, k_gain, tm):
    b, s, d = h.shape
    assert tm == ATT_LEFT and s % tm == 0
    nt = s // tm
    cur = lambda bi, j: (bi, jnp.maximum(j - 1, 0), 0)
    const = lambda bi, j: (0, 0)
    out_sds = lambda rows: jax.ShapeDtypeStruct((b, rows, MIX_WIDTH), BF16)
    return pl.pallas_call(
        _in_even_body,
        grid=(b, nt + 1),
        in_specs=[
            pl.BlockSpec((1, tm, d), cur),
            pl.BlockSpec((1, d), const),
            pl.BlockSpec(w.shape, const),
            pl.BlockSpec((1, MIX_WIDTH), const),
            pl.BlockSpec((1, MIX_WIDTH), const),
        ],
        out_specs=[
            pl.BlockSpec((1, tm, MIX_WIDTH), cur),
            pl.BlockSpec((1, tm, MIX_WIDTH), lambda bi, j: (bi, j, 0)),
            pl.BlockSpec((1, tm, MIX_WIDTH), lambda bi, j: (bi, j, 0)),
            pl.BlockSpec((1, tm, MIX_WIDTH), cur),
        ],
        out_shape=[out_sds(s), out_sds(s + ATT_LEFT), out_sds(s + ATT_LEFT), out_sds(s)],
        compiler_params=_cparams(("arbitrary", "arbitrary")),
        name="in_even",
    )(h, gain, w, q_gain, k_gain)


BAND_TQ = 2 * CHUNK
BAND_TK = BAND_TQ + ATT_LEFT


def _band_body(q_ref, k_ref, v_ref, bias_ref, o_ref):
    i = pl.program_id(1)
    start = pl.multiple_of(i * BAND_TQ, BAND_TQ)
    lane = lax.broadcasted_iota(jnp.int32, (BAND_TQ, LANES), 1)
    col = lax.broadcasted_iota(jnp.int32, (BAND_TQ, BAND_TK), 1)
    is_pad = (col + start) < ATT_LEFT
    for hp in range(MIX_WIDTH // LANES):
        ls = slice(hp * LANES, (hp + 1) * LANES)
        q = q_ref[0, :, ls]
        kb = k_ref[0, pl.ds(start, BAND_TK), ls]
        vb = v_ref[0, pl.ds(start, BAND_TK), ls]
        outs = []
        for a in range(2):
            qa = jnp.where((lane < HEAD_DIM) == (a == 0), q, jnp.zeros_like(q))
            s = lax.dot_general(qa, kb, NT_DIMS, preferred_element_type=F32)
            s = jnp.where(is_pad, NEG_BIG, s + bias_ref[2 * hp + a])
            m = jnp.max(s, axis=-1, keepdims=True)
            p = jnp.exp(s - m)
            l = jnp.sum(p, axis=-1, keepdims=True)
            o = jnp.dot(p.astype(BF16), vb, preferred_element_type=F32)
            outs.append(o / l)
        o_ref[0, :, ls] = jnp.where(lane < HEAD_DIM, outs[0], outs[1]).astype(BF16)


def _band_bias(rel_bias):
    heads = rel_bias.shape[0]
    r = jnp.arange(BAND_TQ)[:, None]
    j = jnp.arange(BAND_TK)[None, :]
    jb = j - CHUNK * (r // CHUNK)
    in_band = (jb >= 0) & (jb < ATT_LEFT + CHUNK)
    period = BAND_TK + BAND_TQ
    far = jnp.broadcast_to(rel_bias[:, 2 * ATT_MAX_REL:], (heads, ATT_LEFT - ATT_MAX_REL + 1))
    near = rel_bias[:, 2 * ATT_MAX_REL - 1:0:-1]
    wrap = jnp.broadcast_to(rel_bias[:, 2 * ATT_MAX_REL:], (heads, period - BAND_TK))
    g = jnp.concatenate([far, near, wrap], axis=1).astype(F32)
    assert g.shape[1] == period
    toep = jnp.tile(g, (1, BAND_TQ))[:, :BAND_TQ * (period - 1)].reshape(heads, BAND_TQ, period - 1)
    return jnp.where(in_band[None], toep[:, :, :BAND_TK], NEG_BIG)


def _band_attention(q, k_pad, v_pad, bias):
    b, s, w = q.shape
    sp = k_pad.shape[1]
    return pl.pallas_call(
        _band_body,
        grid=(b, s // BAND_TQ),
        in_specs=[
            pl.BlockSpec((1, BAND_TQ, w), lambda bi, i: (bi, i, 0)),
            pl.BlockSpec((1, sp, w), lambda bi, i: (bi, 0, 0)),
            pl.BlockSpec((1, sp, w), lambda bi, i: (bi, 0, 0)),
            pl.BlockSpec(bias.shape, lambda bi, i: (0, 0, 0)),
        ],
        out_specs=pl.BlockSpec((1, BAND_TQ, w), lambda bi, i: (bi, i, 0)),
        out_shape=jax.ShapeDtypeStruct((b, s, w), BF16),
        compiler_params=_cparams(("arbitrary", "arbitrary")),
        name="band_attention",
    )(q, k_pad, v_pad, bias)


def _out_even_body(tiles_per_seq, ya_ref, p_ref, halo_ref, h_ref, pw_ref, ps_ref, wo_ref,
                   o_ref, p_scr, yb_scr):
    tm = p_ref.shape[0]
    it = pl.program_id(0) % tiles_per_seq
    halo = halo_ref[...].astype(F32)
    p_scr[0:POOL_HALO, :] = jnp.where(it == 0, jnp.zeros_like(halo), halo)
    p_scr[POOL_HALO:, :] = p_ref[...].astype(F32)
    t = it * tm + lax.broadcasted_iota(jnp.int32, (tm, 1), 0)
    for g, win in enumerate(POOL_WINDOWS):
        ls = slice(g * LANES, (g + 1) * LANES)
        cur = p_scr[POOL_HALO:POOL_HALO + tm, ls]
        acc = cur
        for dlt in range(1, win):
            acc = acc + p_scr[POOL_HALO - dlt:POOL_HALO - dlt + tm, ls]
        cnt = jnp.minimum(t + 1, win).astype(F32)
        mixed = acc / cnt - cur
        yb = jnp.dot(mixed.astype(BF16), pw_ref[g], preferred_element_type=F32) * ps_ref[:, ls]
        yb_scr[:, ls] = yb.astype(BF16)
    w = MIX_WIDTH
    o_ref[...] = (h_ref[...]
                  + jnp.dot(ya_ref[...], wo_ref[0:w, :], preferred_element_type=F32)
                  + jnp.dot(yb_scr[...], wo_ref[w:, :], preferred_element_type=F32))


def _out_even(ya, p, h, pool_w, pool_scale, w_out, seq, tm):
    n, d = h.shape
    w = MIX_WIDTH
    row = lambda i: (i, 0)
    const2 = lambda i: (0, 0)
    halo_blocks = tm // POOL_HALO
    return pl.pallas_call(
        functools.partial(_out_even_body, seq // tm),
        grid=(n // tm,),
        in_specs=[
            pl.BlockSpec((tm, w), row),
            pl.BlockSpec((tm, w), row),
            pl.BlockSpec((POOL_HALO, w), lambda i: (jnp.maximum(i * halo_blocks - 1, 0), 0)),
            pl.BlockSpec((tm, d), row),
            pl.BlockSpec(pool_w.shape, lambda i: (0, 0, 0)),
            pl.BlockSpec((1, w), const2),
            pl.BlockSpec(w_out.shape, const2),
        ],
        out_specs=pl.BlockSpec((tm, d), row),
        out_shape=jax.ShapeDtypeStruct((n, d), F32),
        scratch_shapes=[pltpu.VMEM((tm + POOL_HALO, w), F32), pltpu.VMEM((tm, w), BF16)],
        compiler_params=_cparams(("arbitrary",)),
        name="out_even",
    )(ya, p, p, h, pool_w, pool_scale, w_out)


def _router_body(h_ref, g_ref, wr_ref, xn_ref, route_ref):
    xn = _rms(h_ref[...], g_ref[...])
    xn_ref[...] = xn
    logits = jnp.dot(xn, wr_ref[...], preferred_element_type=F32, precision=lax.Precision.HIGHEST)
    lane = lax.broadcasted_iota(jnp.int32, logits.shape, 1).astype(F32)
    ninf = -jnp.inf

    def top(vals):
        m = jnp.max(vals, axis=-1, keepdims=True)
        idx = jnp.min(jnp.where(vals == m, lane, float(ROUTER_LANES)), axis=-1, keepdims=True)
        return m, idx

    is_group = lane < N_GROUPS
    g_max, g_sel = top(jnp.where(is_group, logits, ninf))
    g_den = jnp.sum(jnp.where(is_group, jnp.exp(logits - g_max), 0.0), axis=-1, keepdims=True)
    g_weight = 1.0 / g_den
    lo = N_GROUPS + N_EXP_PER_GROUP * g_sel
    e_logits = jnp.where((lane >= lo) & (lane < lo + N_EXP_PER_GROUP), logits, ninf)
    e1, i1 = top(e_logits)
    e2, i2 = top(jnp.where(lane == i1, ninf, e_logits))
    t = jnp.exp(e2 - e1)
    w1 = g_weight / (1.0 + t)
    w2 = g_weight * t / (1.0 + t)
    route = jnp.where(lane == 0.0, i1 - N_GROUPS, jnp.where(lane == 1.0, i2 - N_GROUPS, 0.0))
    route_ref[...] = jnp.where(lane == 2.0, w1, jnp.where(lane == 3.0, w2, route))


def _router(h, gain, w_router, tm):
    n, d = h.shape
    row = lambda i: (i, 0)
    const2 = lambda i: (0, 0)
    return pl.pallas_call(
        _router_body,
        grid=(n // tm,),
        in_specs=[pl.BlockSpec((tm, d), row), pl.BlockSpec((1, d), const2),
                  pl.BlockSpec(w_router.shape, const2)],
        out_specs=[pl.BlockSpec((tm, d), row), pl.BlockSpec((tm, ROUTER_LANES), row)],
        out_shape=[jax.ShapeDtypeStruct((n, d), F32),
                   jax.ShapeDtypeStruct((n, ROUTER_LANES), F32)],
        compiler_params=_cparams(("arbitrary",)),
        name="moe_router",
    )(h, gain, w_router)


MOE_TM = 256
MOE_TILE_LANES = 256
SUBLANES = 8


def _moe_tiles(n):
    return (2 * n) // MOE_TM + N_EXPERTS


def _exact_dot_nt(ones, x):
    out = None
    for _ in range(3):
        part = x.astype(BF16)
        x = x - part.astype(F32)
        term = lax.dot_general(ones, part, NT_DIMS, preferred_element_type=F32)
        out = term if out is None else out + term
    return out


def _slots_body(n_tiles, route_ref, pos1_ref, pos2_ref, tile_ref, run_scr, start_scr):
    phase = pl.program_id(0)
    i = pl.program_id(1)
    tm = route_ref.shape[0]
    route = route_ref[...]
    lane = lax.broadcasted_iota(jnp.int32, (tm, ROUTER_LANES), 1).astype(F32)
    pick1 = jnp.where(lane == route[:, 0:1], 1.0, 0.0)
    pick2 = jnp.where(lane == route[:, 1:2], 1.0, 0.0)
    occ = (pick1 + pick2).astype(BF16)
    ones_rows = jnp.ones((SUBLANES, tm), BF16)
    ones_lanes = jnp.ones((SUBLANES, ROUTER_LANES), BF16)

    @pl.when(i == 0)
    def _():
        run_scr[...] = jnp.zeros_like(run_scr)

    @pl.when(phase == 0)
    def _():
        run_scr[...] += jnp.dot(ones_rows, occ, preferred_element_type=F32)

        @pl.when(i == pl.num_programs(1) - 1)
        def _():
            padded = jnp.floor((run_scr[...] + (MOE_TM - 1)) * (1.0 / MOE_TM)) * MOE_TM
            r = lax.broadcasted_iota(jnp.int32, (ROUTER_LANES, ROUTER_LANES), 0)
            c = lax.broadcasted_iota(jnp.int32, (ROUTER_LANES, ROUTER_LANES), 1)
            before = jnp.where(r < c, 1.0, 0.0).astype(BF16)
            hi = padded.astype(BF16)
            mid = (padded - hi.astype(F32)).astype(BF16)
            low = (padded - hi.astype(F32) - mid.astype(F32)).astype(BF16)
            start = (jnp.dot(hi, before, preferred_element_type=F32)
                     + jnp.dot(mid, before, preferred_element_type=F32)
                     + jnp.dot(low, before, preferred_element_type=F32))
            start_scr[...] = start
            seg_end = start[0:1, :] + padded[0:1, :]
            tile_lo = (lax.broadcasted_iota(jnp.int32, (MOE_TILE_LANES, ROUTER_LANES), 0) * MOE_TM).astype(F32)
            e_lane = lax.broadcasted_iota(jnp.int32, (MOE_TILE_LANES, ROUTER_LANES), 1)
            ended = jnp.where((seg_end <= tile_lo) & (e_lane < N_EXPERTS), 1.0, 0.0).astype(BF16)
            tile_ref[...] = lax.dot_general(ones_lanes, ended, NT_DIMS, preferred_element_type=F32)

    @pl.when(phase == 1)
    def _():
        r = lax.broadcasted_iota(jnp.int32, (tm, tm), 0)
        c = lax.broadcasted_iota(jnp.int32, (tm, tm), 1)
        earlier = jnp.where(c < r, 1.0, 0.0).astype(BF16)
        base = (jnp.dot(earlier, occ, preferred_element_type=F32)
                + run_scr[0:1, :] + start_scr[0:1, :])
        pos1_ref[...] = _exact_dot_nt(ones_lanes, pick1 * base)
        pos2_ref[...] = _exact_dot_nt(ones_lanes, pick2 * base)
        run_scr[...] += jnp.dot(ones_rows, occ, preferred_element_type=F32)


def _slots(route, tm):
    n = route.shape[0]
    n_tiles = _moe_tiles(n)
    assert n_tiles <= MOE_TILE_LANES and 2 * n + N_EXPERTS * MOE_TM < 2 ** 24
    row_out = pl.BlockSpec((SUBLANES, tm), lambda ph, i: (0, i * ph))
    sds = jax.ShapeDtypeStruct((SUBLANES, n), F32)
    return pl.pallas_call(
        functools.partial(_slots_body, n_tiles),
        grid=(2, n // tm),
        in_specs=[pl.BlockSpec((tm, ROUTER_LANES), lambda ph, i: (i, 0))],
        out_specs=[row_out, row_out, pl.BlockSpec((SUBLANES, MOE_TILE_LANES), lambda ph, i: (0, 0))],
        out_shape=[sds, sds, jax.ShapeDtypeStruct((SUBLANES, MOE_TILE_LANES), F32)],
        scratch_shapes=[pltpu.VMEM((SUBLANES, ROUTER_LANES), F32), pltpu.VMEM((SUBLANES, ROUTER_LANES), F32)],
        compiler_params=_cparams(("arbitrary", "arbitrary")),
        name="moe_slots",
    )(route)


MOE_DMA_BLOCK = 256


def _dispatch_body(n_tiles, pos_ref, tile_ref, xn_hbm, xs_hbm, zero_scr, zero_sem, row_sem):
    n = xn_hbm.shape[0]
    zero_scr[...] = jnp.zeros_like(zero_scr)

    def fill_copy(t):
        return pltpu.make_async_copy(zero_scr, xs_hbm.at[pl.ds(t * MOE_TM, MOE_TM), :], zero_sem)

    def has_padding(t):
        return (tile_ref[t] >= N_EXPERTS) | (tile_ref[t] != tile_ref[t + 1])

    @pl.loop(0, n_tiles)
    def _(t):
        @pl.when(has_padding(t))
        def _():
            fill_copy(t).start()

    @pl.loop(0, n_tiles)
    def _(t):
        @pl.when(has_padding(t))
        def _():
            fill_copy(t).wait()

    def wait_block():
        rows = pl.ds(0, 2 * MOE_DMA_BLOCK)
        pltpu.make_async_copy(xn_hbm.at[rows, :], xs_hbm.at[rows, :], row_sem).wait()

    @pl.loop(0, n // MOE_DMA_BLOCK)
    def _(blk):
        def issue(j, carry):
            t = blk * MOE_DMA_BLOCK + j
            src = xn_hbm.at[pl.ds(t, 1), :]
            pltpu.make_async_copy(src, xs_hbm.at[pl.ds(pos_ref[t], 1), :], row_sem).start()
            pltpu.make_async_copy(src, xs_hbm.at[pl.ds(pos_ref[n + t], 1), :], row_sem).start()
            return carry

        lax.fori_loop(0, MOE_DMA_BLOCK, issue, 0, unroll=8)

        @pl.when(blk > 0)
        def _():
            wait_block()

    wait_block()


def _dispatch(pos, tile_map, xn):
    n, d = xn.shape
    n_tiles = _moe_tiles(n)
    return pl.pallas_call(
        functools.partial(_dispatch_body, n_tiles),
        grid_spec=pltpu.PrefetchScalarGridSpec(
            num_scalar_prefetch=2,
            grid=(1,),
            in_specs=[pl.BlockSpec(memory_space=pl.ANY)],
            out_specs=pl.BlockSpec(memory_space=pl.ANY),
            scratch_shapes=[pltpu.VMEM((MOE_TM, d), F32), pltpu.SemaphoreType.DMA(()),
                            pltpu.SemaphoreType.DMA(())],
        ),
        out_shape=jax.ShapeDtypeStruct((n_tiles * MOE_TM, d), F32),
        compiler_params=_cparams(("arbitrary",)),
        name="moe_dispatch",
    )(pos, tile_map, xn)


def _experts_body(tile_ref, xs_ref, wg_ref, wu_ref, wd_ref, ys_ref):
    used = tile_ref[pl.program_id(0)] < N_EXPERTS

    @pl.when(used)
    def _():
        x = xs_ref[...].astype(BF16)
        gate = jnp.dot(x, wg_ref[0].astype(BF16), preferred_element_type=F32)
        up = jnp.dot(x, wu_ref[0].astype(BF16), preferred_element_type=F32)
        hid = gate * jax.nn.sigmoid(gate) * up
        ys_ref[...] = jnp.dot(hid.astype(BF16), wd_ref[0].astype(BF16), preferred_element_type=F32)

    @pl.when(jnp.logical_not(used))
    def _():
        ys_ref[...] = jnp.zeros_like(ys_ref)


def _experts(tile_map, xs, w_gate, w_up, w_down):
    rows, d = xs.shape
    f = EXPERT_FF
    n_tiles = rows // MOE_TM
    x_map = lambda i, tm_ref: (jnp.where(tm_ref[i] < N_EXPERTS, i, 0), 0)
    y_map = lambda i, tm_ref: (i, 0)
    w_map = lambda i, tm_ref: (jnp.minimum(tm_ref[i], N_EXPERTS - 1), 0, 0)
    return pl.pallas_call(
        _experts_body,
        grid_spec=pltpu.PrefetchScalarGridSpec(
            num_scalar_prefetch=1,
            grid=(n_tiles,),
            in_specs=[pl.BlockSpec((MOE_TM, d), x_map), pl.BlockSpec((1, d, f), w_map),
                      pl.BlockSpec((1, d, f), w_map), pl.BlockSpec((1, f, d), w_map)],
            out_specs=pl.BlockSpec((MOE_TM, d), y_map),
        ),
        out_shape=jax.ShapeDtypeStruct((rows, d), F32),
        compiler_params=_cparams(("arbitrary",)),
        name="moe_experts",
    )(tile_map, xs, w_gate, w_up, w_down)


def _combine_body(pos_ref, h_ref, route_ref, ys_hbm, o_ref, buf, sem):
    tm = h_ref.shape[0]
    n = pl.num_programs(0) * tm
    base = pl.program_id(0) * tm

    def issue(j, carry):
        t = base + j
        pltpu.make_async_copy(ys_hbm.at[pl.ds(pos_ref[t], 1), :], buf.at[0, pl.ds(j, 1), :], sem).start()
        pltpu.make_async_copy(ys_hbm.at[pl.ds(pos_ref[n + t], 1), :], buf.at[1, pl.ds(j, 1), :], sem).start()
        return carry

    lax.fori_loop(0, tm, issue, 0, unroll=8)
    pltpu.make_async_copy(buf, buf, sem).wait()
    route = route_ref[...]
    o_ref[...] = h_ref[...] + route[:, 2:3] * buf[0] + route[:, 3:4] * buf[1]


def _combine(pos, h, route, ys, tm):
    n, d = h.shape
    row = lambda i, p: (i, 0)
    return pl.pallas_call(
        _combine_body,
        grid_spec=pltpu.PrefetchScalarGridSpec(
            num_scalar_prefetch=1,
            grid=(n // tm,),
            in_specs=[pl.BlockSpec((tm, d), row), pl.BlockSpec((tm, ROUTER_LANES), row),
                      pl.BlockSpec(memory_space=pl.ANY)],
            out_specs=pl.BlockSpec((tm, d), row),
            scratch_shapes=[pltpu.VMEM((2, tm, d), F32), pltpu.SemaphoreType.DMA(())],
        ),
        out_shape=jax.ShapeDtypeStruct((n, d), F32),
        compiler_params=_cparams(("arbitrary",)),
        name="moe_combine",
    )(pos, h, route, ys)


def _gelu(x):
    return 0.5 * x * (1.0 + lax.erf(x * (1.0 / math.sqrt(2.0))))


def _in_odd_body(h_ref, g_ref, w_ref, vg_ref, u_ref, vn_ref, q_ref, k_ref, v_ref):
    xn = _rms(h_ref[...], g_ref[...])
    proj = jnp.dot(xn.astype(BF16), w_ref[...], preferred_element_type=F32)
    w = MIX_WIDTH
    u_ref[...] = _gelu(proj[:, :w]).astype(BF16)
    vn_ref[...] = _rms(_gelu(proj[:, w:2 * w]), vg_ref[...]).astype(BF16)
    q_ref[...] = (proj[:, 2 * w:3 * w] * (1.0 / math.sqrt(HEAD_DIM))).astype(BF16)
    k_ref[...] = proj[:, 3 * w:4 * w].astype(BF16)
    v_ref[...] = proj[:, 4 * w:].astype(BF16)


def _in_odd(h, gain, w, v_gain, tm):
    n, d = h.shape
    row = lambda i: (i, 0)
    const2 = lambda i: (0, 0)
    sds = jax.ShapeDtypeStruct((n, MIX_WIDTH), BF16)
    return pl.pallas_call(
        _in_odd_body,
        grid=(n // tm,),
        in_specs=[pl.BlockSpec((tm, d), row), pl.BlockSpec((1, d), const2),
                  pl.BlockSpec(w.shape, const2), pl.BlockSpec((1, MIX_WIDTH), const2)],
        out_specs=[pl.BlockSpec((tm, MIX_WIDTH), row)] * 5,
        out_shape=[sds] * 5,
        compiler_params=_cparams(("arbitrary",)),
        name="in_odd",
    )(h, gain, w, v_gain)


SB_T = 128
SB_NB = 3
SB_TK = SB_NB * SB_T
SB_QBLK = 512
SB_UNDERFLOW = -104.0


def _sb_body(q_ref, k_ref, v_ref, o_ref, acc_scr, run_scr):
    qi = pl.program_id(2)
    lane = lax.broadcasted_iota(jnp.int32, (SB_T, LANES), 1)
    row = lax.broadcasted_iota(jnp.int32, (SB_T, 1), 0)
    col = lax.broadcasted_iota(jnp.int32, (SB_T, SB_TK), 1)
    rr = lax.broadcasted_iota(jnp.int32, (SB_T, 2 * SB_T), 0)
    cc = lax.broadcasted_iota(jnp.int32, (SB_T, 2 * SB_T), 1)
    suffix = jnp.where((cc >= SB_T) | (rr > cc), 1.0, 0.0).astype(BF16)

    def subtile(sub, _):
        q_lo = pl.multiple_of(sub * SB_T, SB_T)
        qs = qi * SB_QBLK + q_lo
        q = q_ref[0, pl.ds(q_lo, SB_T), :]
        zero = jnp.zeros_like(q)
        q_heads = (jnp.where(lane < HEAD_DIM, q, zero), jnp.where(lane < HEAD_DIM, zero, q))
        acc_scr[...] = jnp.zeros_like(acc_scr)
        run_scr[...] = jnp.zeros_like(run_scr)

        def cond(carry):
            hi, done = carry
            return (hi > 0) & (done == 0)

        def body(carry):
            hi, _ = carry
            ks = pl.multiple_of(jnp.maximum(hi - SB_TK, 0), SB_T)
            kt = k_ref[0, pl.ds(ks, SB_TK), :]
            vt = v_ref[0, pl.ds(ks, SB_TK), :]
            valid = col < (jnp.minimum(row + qs, hi) - ks)
            worst = None
            for a in range(2):
                z = lax.dot_general(q_heads[a], kt, NT_DIMS, preferred_element_type=F32)
                sp = jnp.log(1.0 + jnp.exp(-jnp.abs(z)))
                log_beta = jnp.minimum(z, 0.0) - sp
                log_rest = jnp.where(valid, log_beta - z, 0.0)
                run = run_scr[a]
                pieces = [None] * SB_NB
                for blk in reversed(range(SB_NB)):
                    bs = slice(blk * SB_T, (blk + 1) * SB_T)
                    tt = _split_dot(log_rest[:, bs], suffix)
                    pieces[blk] = jnp.exp(log_beta[:, bs] + tt[:, :SB_T] + run)
                    run = run + tt[:, SB_T:]
                att = jnp.where(valid, jnp.concatenate(pieces, axis=1), 0.0)
                acc_scr[a] += jnp.dot(att.astype(BF16), vt, preferred_element_type=F32)
                run_scr[a] = run
                top = jnp.max(run)
                worst = top if worst is None else jnp.maximum(worst, top)
            return ks, (worst <= SB_UNDERFLOW).astype(jnp.int32)

        lax.while_loop(cond, body, (qs + SB_T, jnp.int32(0)))
        o_ref[0, pl.ds(q_lo, SB_T), :] = jnp.where(lane < HEAD_DIM, acc_scr[0], acc_scr[1]).astype(BF16)
        return 0

    lax.fori_loop(0, SB_QBLK // SB_T, subtile, 0)


def _sb_attention(q, k, v):
    b, s, w = q.shape
    assert s % SB_QBLK == 0 and s >= SB_TK
    qspec = pl.BlockSpec((1, SB_QBLK, LANES), lambda bi, hp, i: (bi, i, hp))
    kvspec = pl.BlockSpec((1, s, LANES), lambda bi, hp, i: (bi, 0, hp))
    return pl.pallas_call(
        _sb_body,
        grid=(b, w // LANES, s // SB_QBLK),
        in_specs=[qspec, kvspec, kvspec],
        out_specs=qspec,
        out_shape=jax.ShapeDtypeStruct((b, s, w), BF16),
        scratch_shapes=[pltpu.VMEM((2, SB_T, LANES), F32), pltpu.VMEM((2, SB_T, LANES), F32)],
        compiler_params=_cparams(("arbitrary", "arbitrary", "arbitrary")),
        name="sb_attention",
    )(q, k, v)


def _out_odd_body(u_ref, vn_ref, yd_ref, h_ref, ws_ref, bs_ref, wo_ref, o_ref, yc_scr):
    tm = u_ref.shape[0]
    r = lax.broadcasted_iota(jnp.int32, (SGU_BLOCK, SGU_BLOCK), 0)
    c = lax.broadcasted_iota(jnp.int32, (SGU_BLOCK, SGU_BLOCK), 1)
    for g in range(MIX_WIDTH // LANES):
        ls = slice(g * LANES, (g + 1) * LANES)
        ws = jnp.where(c <= r, ws_ref[g], jnp.zeros_like(ws_ref[g]))
        for blk in range(tm // SGU_BLOCK):
            rs = slice(blk * SGU_BLOCK, (blk + 1) * SGU_BLOCK)
            mixed = jnp.dot(ws, vn_ref[rs, ls], preferred_element_type=F32) + bs_ref[g]
            yc_scr[rs, ls] = (u_ref[rs, ls].astype(F32) * mixed).astype(BF16)
    w = MIX_WIDTH
    o_ref[...] = (h_ref[...]
                  + jnp.dot(yc_scr[...], wo_ref[0:w, :], preferred_element_type=F32)
                  + jnp.dot(yd_ref[...], wo_ref[w:, :], preferred_element_type=F32))


def _out_odd(u, vn, yd, h, sgu_w, sgu_b, w_out, tm):
    n, d = h.shape
    w = MIX_WIDTH
    row = lambda i: (i, 0)
    const2 = lambda i: (0, 0)
    const3 = lambda i: (0, 0, 0)
    return pl.pallas_call(
        _out_odd_body,
        grid=(n // tm,),
        in_specs=[pl.BlockSpec((tm, w), row), pl.BlockSpec((tm, w), row), pl.BlockSpec((tm, w), row),
                  pl.BlockSpec((tm, d), row), pl.BlockSpec(sgu_w.shape, const3),
                  pl.BlockSpec(sgu_b.shape, const3), pl.BlockSpec(w_out.shape, const2)],
        out_specs=pl.BlockSpec((tm, d), row),
        out_shape=jax.ShapeDtypeStruct((n, d), F32),
        scratch_shapes=[pltpu.VMEM((tm, w), BF16)],
        compiler_params=_cparams(("arbitrary",)),
        name="out_odd",
    )(u, vn, yd, h, sgu_w, sgu_b, w_out)


def _moe_layer(h, gain, w_rg, w_re, w_gate, w_up, w_down):
    d = h.shape[1]
    pad = jnp.zeros((d, ROUTER_LANES - N_GROUPS - N_EXPERTS), F32)
    w_router = jnp.concatenate([w_rg, w_re, pad], axis=1)
    xn, route = _router(h, gain[None, :], w_router, tm=512)
    pos1, pos2, tile_map = _slots(route, tm=512)
    pos = jnp.concatenate([pos1[0], pos2[0]]).astype(jnp.int32)
    tile_map = tile_map[0].astype(jnp.int32)
    xs = _dispatch(pos, tile_map, xn)
    ys = _experts(tile_map, xs, w_gate.reshape(N_EXPERTS, d, EXPERT_FF),
                  w_up.reshape(N_EXPERTS, d, EXPERT_FF), w_down.reshape(N_EXPERTS, EXPERT_FF, d))
    return _combine(pos, h, route, ys, tm=256)


def kernel(x, mix_norm_even, w_in_even, att_q_norm, att_k_norm, att_rel_bias, pool_w, pool_scale,
           w_out_even, mix_norm_odd, w_in_odd, sgu_v_norm, sgu_w, sgu_b, w_out_odd, ffn_norm,
           w_router_group, w_router_expert, w_exp_gate, w_exp_up, w_exp_down):
    b, s, d = x.shape
    n = b * s
    depth = ffn_norm.shape[0]
    heads = MIX_WIDTH // HEAD_DIM
    h = x.reshape(n, d)
    for layer in range(depth):
        i = layer // 2
        if layer % 2 == 0:
            q, k_pad, v_pad, p = _in_even(
                h.reshape(b, s, d), mix_norm_even[i][None, :], w_in_even[i].astype(BF16),
                jnp.tile(att_q_norm[i], heads)[None, :], jnp.tile(att_k_norm[i], heads)[None, :],
                tm=ATT_LEFT)
            ya = _band_attention(q, k_pad, v_pad, _band_bias(att_rel_bias[i]))
            h = _out_even(ya.reshape(n, MIX_WIDTH), p.reshape(n, MIX_WIDTH), h,
                          pool_w[i].astype(BF16), pool_scale[i][None, :],
                          w_out_even[i].astype(BF16), seq=s, tm=512)
        else:
            u, vn, q, k, v = _in_odd(h, mix_norm_odd[i][None, :], w_in_odd[i].astype(BF16),
                                     sgu_v_norm[i][None, :], tm=512)
            to3 = lambda t: t.reshape(b, s, MIX_WIDTH)
            yd = _sb_attention(to3(q), to3(k), to3(v))
            bias = jnp.broadcast_to(sgu_b[i][:, :, None], (N_GROUPS, SGU_BLOCK, LANES))
            h = _out_odd(u, vn, yd.reshape(n, MIX_WIDTH), h, sgu_w[i].astype(BF16), bias,
                         w_out_odd[i].astype(BF16), tm=512)
        h = _moe_layer(h, ffn_norm[layer], w_router_group[layer], w_router_expert[layer],
                       w_exp_gate[layer], w_exp_up[layer], w_exp_down[layer])
    return h.reshape(b, s, d)
```

```python
import functools
import math

import jax
import jax.numpy as jnp
from jax import lax
from jax.experimental import pallas as pl
from jax.experimental.pallas import tpu as pltpu

F32 = jnp.float32
BF16 = jnp.bfloat16

D_MODEL = 1024
CHUNK = 64
EPS = 1e-6
HEAD_DIM = 64
MIX_WIDTH = 512
LANES = 128
ATT_LEFT = 8 * CHUNK
ATT_MAX_REL = 128
POOL_WINDOWS = (2, 4, 8, 16)
POOL_HALO = 16
SGU_BLOCK = 128
N_GROUPS = 4
N_EXP_PER_GROUP = 8
N_EXPERTS = N_GROUPS * N_EXP_PER_GROUP
EXPERT_FF = 256
ROUTER_LANES = 128
NEG_BIG = -1e30
VMEM_LIMIT = 56 * 1024 * 1024

NT_DIMS = (((1,), (1,)), ((), ()))


def _cparams(sem):
    return pltpu.CompilerParams(dimension_semantics=sem, vmem_limit_bytes=VMEM_LIMIT)


def _rms(x, gain):
    return x * lax.rsqrt(jnp.mean(x * x, axis=-1, keepdims=True) + EPS) * gain


def _split_dot(x, m):
    hi = x.astype(BF16)
    lo = (x - hi.astype(F32)).astype(BF16)
    return (jnp.dot(hi, m, preferred_element_type=F32)
            + jnp.dot(lo, m, preferred_element_type=F32))


def _head_rms(t, gain):
    n = t.shape[-1]
    r = lax.broadcasted_iota(jnp.int32, (n, n), 0) // HEAD_DIM
    c = lax.broadcasted_iota(jnp.int32, (n, n), 1) // HEAD_DIM
    bd = jnp.where(r == c, 1.0, 0.0).astype(BF16)
    ms = _split_dot(t * t, bd) * (1.0 / HEAD_DIM)
    return t * lax.rsqrt(ms + EPS) * gain


def _in_even_body(h_ref, g_ref, w_ref, qg_ref, kg_ref, q_ref, k_ref, v_ref, p_ref):
    j = pl.program_id(1)

    @pl.when(j == 0)
    def _():
        k_ref[...] = jnp.zeros_like(k_ref)
        v_ref[...] = jnp.zeros_like(v_ref)

    @pl.when(j > 0)
    def _():
        xn = _rms(h_ref[0], g_ref[...])
        proj = jnp.dot(xn.astype(BF16), w_ref[...], preferred_element_type=F32)
        w = MIX_WIDTH
        q_ref[0] = (_head_rms(proj[:, :w], qg_ref[...]) * (1.0 / math.sqrt(HEAD_DIM))).astype(BF16)
        k_ref[0] = _head_rms(proj[:, w:2 * w], kg_ref[...]).astype(BF16)
        v_ref[0] = proj[:, 2 * w:3 * w].astype(BF16)
        p_ref[0] = proj[:, 3 * w:].astype(BF16)


def _in_even(h, gain, w, q_gain, k_gain, tm):
    b, s, d = h.shape
    assert tm == ATT_LEFT and s % tm == 0
    nt = s // tm
    cur = lambda bi, j: (bi, jnp.maximum(j - 1, 0), 0)
    const = lambda bi, j: (0, 0)
    out_sds = lambda rows: jax.ShapeDtypeStruct((b, rows, MIX_WIDTH), BF16)
    return pl.pallas_call(
        _in_even_body,
        grid=(b, nt + 1),
        in_specs=[
            pl.BlockSpec((1, tm, d), cur),
            pl.BlockSpec((1, d), const),
            pl.BlockSpec(w.shape, const),
            pl.BlockSpec((1, MIX_WIDTH), const),
            pl.BlockSpec((1, MIX_WIDTH), const),
        ],
        out_specs=[
            pl.BlockSpec((1, tm, MIX_WIDTH), cur),
            pl.BlockSpec((1, tm, MIX_WIDTH), lambda bi, j: (bi, j, 0)),
            pl.BlockSpec((1, tm, MIX_WIDTH), lambda bi, j: (bi, j, 0)),
            pl.BlockSpec((1, tm, MIX_WIDTH), cur),
        ],
        out_shape=[out_sds(s), out_sds(s + ATT_LEFT), out_sds(s + ATT_LEFT), out_sds(s)],
        compiler_params=_cparams(("arbitrary", "arbitrary")),
        name="in_even",
    )(h, gain, w, q_gain, k_gain)


BAND_TQ = 2 * CHUNK
BAND_TK = BAND_TQ + ATT_LEFT


def _band_body(q_ref, k_ref, v_ref, bias_ref, o_ref):
    i = pl.program_id(1)
    start = pl.multiple_of(i * BAND_TQ, BAND_TQ)
    lane = lax.broadcasted_iota(jnp.int32, (BAND_TQ, LANES), 1)
    col = lax.broadcasted_iota(jnp.int32, (BAND_TQ, BAND_TK), 1)
    is_pad = (col + start) < ATT_LEFT
    for hp in range(MIX_WIDTH // LANES):
        ls = slice(hp * LANES, (hp + 1) * LANES)
        q = q_ref[0, :, ls]
        kb = k_ref[0, pl.ds(start, BAND_TK), ls]
        vb = v_ref[0, pl.ds(start, BAND_TK), ls]
        outs = []
        for a in range(2):
            qa = jnp.where((lane < HEAD_DIM) == (a == 0), q, jnp.zeros_like(q))
            s = lax.dot_general(qa, kb, NT_DIMS, preferred_element_type=F32)
            s = jnp.where(is_pad, NEG_BIG, s + bias_ref[2 * hp + a])
            m = jnp.max(s, axis=-1, keepdims=True)
            p = jnp.exp(s - m)
            l = jnp.sum(p, axis=-1, keepdims=True)
            o = jnp.dot(p.astype(BF16), vb, preferred_element_type=F32)
            outs.append(o / l)
        o_ref[0, :, ls] = jnp.where(lane < HEAD_DIM, outs[0], outs[1]).astype(BF16)


def _band_bias(rel_bias):
    heads = rel_bias.shape[0]
    r = jnp.arange(BAND_TQ)[:, None]
    j = jnp.arange(BAND_TK)[None, :]
    jb = j - CHUNK * (r // CHUNK)
    in_band = (jb >= 0) & (jb < ATT_LEFT + CHUNK)
    period = BAND_TK + BAND_TQ
    far = jnp.broadcast_to(rel_bias[:, 2 * ATT_MAX_REL:], (heads, ATT_LEFT - ATT_MAX_REL + 1))
    near = rel_bias[:, 2 * ATT_MAX_REL - 1:0:-1]
    wrap = jnp.broadcast_to(rel_bias[:, 2 * ATT_MAX_REL:], (heads, period - BAND_TK))
    g = jnp.concatenate([far, near, wrap], axis=1).astype(F32)
    assert g.shape[1] == period
    toep = jnp.tile(g, (1, BAND_TQ))[:, :BAND_TQ * (period - 1)].reshape(heads, BAND_TQ, period - 1)
    return jnp.where(in_band[None], toep[:, :, :BAND_TK], NEG_BIG)


def _band_attention(q, k_pad, v_pad, bias):
    b, s, w = q.shape
    sp = k_pad.shape[1]
    return pl.pallas_call(
        _band_body,
        grid=(b, s // BAND_TQ),
        in_specs=[
            pl.BlockSpec((1, BAND_TQ, w), lambda bi, i: (bi, i, 0)),
            pl.BlockSpec((1, sp, w), lambda bi, i: (bi, 0, 0)),
            pl.BlockSpec((1, sp, w), lambda bi, i: (bi, 0, 0)),
            pl.BlockSpec(bias.shape, lambda bi, i: (0, 0, 0)),
        ],
        out_specs=pl.BlockSpec((1, BAND_TQ, w), lambda bi, i: (bi, i, 0)),
        out_shape=jax.ShapeDtypeStruct((b, s, w), BF16),
        compiler_params=_cparams(("arbitrary", "arbitrary")),
        name="band_attention",
    )(q, k_pad, v_pad, bias)


def _out_even_body(tiles_per_seq, ya_ref, p_ref, halo_ref, h_ref, pw_ref, ps_ref, wo_ref,
                   o_ref, p_scr, yb_scr):
    tm = p_ref.shape[0]
    it = pl.program_id(0) % tiles_per_seq
    halo = halo_ref[...].astype(F32)
    p_scr[0:POOL_HALO, :] = jnp.where(it == 0, jnp.zeros_like(halo), halo)
    p_scr[POOL_HALO:, :] = p_ref[...].astype(F32)
    t = it * tm + lax.broadcasted_iota(jnp.int32, (tm, 1), 0)
    for g, win in enumerate(POOL_WINDOWS):
        ls = slice(g * LANES, (g + 1) * LANES)
        cur = p_scr[POOL_HALO:POOL_HALO + tm, ls]
        acc = cur
        for dlt in range(1, win):
            acc = acc + p_scr[POOL_HALO - dlt:POOL_HALO - dlt + tm, ls]
        cnt = jnp.minimum(t + 1, win).astype(F32)
        mixed = acc / cnt - cur
        yb = jnp.dot(mixed.astype(BF16), pw_ref[g], preferred_element_type=F32) * ps_ref[:, ls]
        yb_scr[:, ls] = yb.astype(BF16)
    w = MIX_WIDTH
    o_ref[...] = (h_ref[...]
                  + jnp.dot(ya_ref[...], wo_ref[0:w, :], preferred_element_type=F32)
                  + jnp.dot(yb_scr[...], wo_ref[w:, :], preferred_element_type=F32))


def _out_even(ya, p, h, pool_w, pool_scale, w_out, seq, tm):
    n, d = h.shape
    w = MIX_WIDTH
    row = lambda i: (i, 0)
    const2 = lambda i: (0, 0)
    halo_blocks = tm // POOL_HALO
    return pl.pallas_call(
        functools.partial(_out_even_body, seq // tm),
        grid=(n // tm,),
        in_specs=[
            pl.BlockSpec((tm, w), row),
            pl.BlockSpec((tm, w), row),
            pl.BlockSpec((POOL_HALO, w), lambda i: (jnp.maximum(i * halo_blocks - 1, 0), 0)),
            pl.BlockSpec((tm, d), row),
            pl.BlockSpec(pool_w.shape, lambda i: (0, 0, 0)),
            pl.BlockSpec((1, w), const2),
            pl.BlockSpec(w_out.shape, const2),
        ],
        out_specs=pl.BlockSpec((tm, d), row),
        out_shape=jax.ShapeDtypeStruct((n, d), F32),
        scratch_shapes=[pltpu.VMEM((tm + POOL_HALO, w), F32), pltpu.VMEM((tm, w), BF16)],
        compiler_params=_cparams(("arbitrary",)),
        name="out_even",
    )(ya, p, p, h, pool_w, pool_scale, w_out)


def _router_body(h_ref, g_ref, wr_ref, xn_ref, route_ref):
    xn = _rms(h_ref[...], g_ref[...])
    xn_ref[...] = xn
    logits = jnp.dot(xn, wr_ref[...], preferred_element_type=F32, precision=lax.Precision.HIGHEST)
    lane = lax.broadcasted_iota(jnp.int32, logits.shape, 1).astype(F32)
    ninf = -jnp.inf

    def top(vals):
        m = jnp.max(vals, axis=-1, keepdims=True)
        idx = jnp.min(jnp.where(vals == m, lane, float(ROUTER_LANES)), axis=-1, keepdims=True)
        return m, idx

    is_group = lane < N_GROUPS
    g_max, g_sel = top(jnp.where(is_group, logits, ninf))
    g_den = jnp.sum(jnp.where(is_group, jnp.exp(logits - g_max), 0.0), axis=-1, keepdims=True)
    g_weight = 1.0 / g_den
    lo = N_GROUPS + N_EXP_PER_GROUP * g_sel
    e_logits = jnp.where((lane >= lo) & (lane < lo + N_EXP_PER_GROUP), logits, ninf)
    e1, i1 = top(e_logits)
    e2, i2 = top(jnp.where(lane == i1, ninf, e_logits))
    t = jnp.exp(e2 - e1)
    w1 = g_weight / (1.0 + t)
    w2 = g_weight * t / (1.0 + t)
    route = jnp.where(lane == 0.0, i1 - N_GROUPS, jnp.where(lane == 1.0, i2 - N_GROUPS, 0.0))
    route_ref[...] = jnp.where(lane == 2.0, w1, jnp.where(lane == 3.0, w2, route))


def _router(h, gain, w_router, tm):
    n, d = h.shape
    row = lambda i: (i, 0)
    const2 = lambda i: (0, 0)
    return pl.pallas_call(
        _router_body,
        grid=(n // tm,),
        in_specs=[pl.BlockSpec((tm, d), row), pl.BlockSpec((1, d), const2),
                  pl.BlockSpec(w_router.shape, const2)],
        out_specs=[pl.BlockSpec((tm, d), row), pl.BlockSpec((tm, ROUTER_LANES), row)],
        out_shape=[jax.ShapeDtypeStruct((n, d), F32),
                   jax.ShapeDtypeStruct((n, ROUTER_LANES), F32)],
        compiler_params=_cparams(("arbitrary",)),
        name="moe_router",
    )(h, gain, w_router)


MOE_TM = 256
MOE_TILE_LANES = 256
SUBLANES = 8


def _moe_tiles(n):
    return (2 * n) // MOE_TM + N_EXPERTS


def _exact_dot_nt(ones, x):
    out = None
    for _ in range(3):
        part = x.astype(BF16)
        x = x - part.astype(F32)
        term = lax.dot_general(ones, part, NT_DIMS, preferred_element_type=F32)
        out = term if out is None else out + term
    return out


def _slots_body(n_tiles, route_ref, pos1_ref, pos2_ref, tile_ref, run_scr, start_scr):
    phase = pl.program_id(0)
    i = pl.program_id(1)
    tm = route_ref.shape[0]
    route = route_ref[...]
    lane = lax.broadcasted_iota(jnp.int32, (tm, ROUTER_LANES), 1).astype(F32)
    pick1 = jnp.where(lane == route[:, 0:1], 1.0, 0.0)
    pick2 = jnp.where(lane == route[:, 1:2], 1.0, 0.0)
    occ = (pick1 + pick2).astype(BF16)
    ones_rows = jnp.ones((SUBLANES, tm), BF16)
    ones_lanes = jnp.ones((SUBLANES, ROUTER_LANES), BF16)

    @pl.when(i == 0)
    def _():
        run_scr[...] = jnp.zeros_like(run_scr)

    @pl.when(phase == 0)
    def _():
        run_scr[...] += jnp.dot(ones_rows, occ, preferred_element_type=F32)

        @pl.when(i == pl.num_programs(1) - 1)
        def _():
            padded = jnp.floor((run_scr[...] + (MOE_TM - 1)) * (1.0 / MOE_TM)) * MOE_TM
            r = lax.broadcasted_iota(jnp.int32, (ROUTER_LANES, ROUTER_LANES), 0)
            c = lax.broadcasted_iota(jnp.int32, (ROUTER_LANES, ROUTER_LANES), 1)
            before = jnp.where(r < c, 1.0, 0.0).astype(BF16)
            hi = padded.astype(BF16)
            mid = (padded - hi.astype(F32)).astype(BF16)
            low = (padded - hi.astype(F32) - mid.astype(F32)).astype(BF16)
            start = (jnp.dot(hi, before, preferred_element_type=F32)
                     + jnp.dot(mid, before, preferred_element_type=F32)
                     + jnp.dot(low, before, preferred_element_type=F32))
            start_scr[...] = start
            seg_end = start[0:1, :] + padded[0:1, :]
            tile_lo = (lax.broadcasted_iota(jnp.int32, (MOE_TILE_LANES, ROUTER_LANES), 0) * MOE_TM).astype(F32)
            e_lane = lax.broadcasted_iota(jnp.int32, (MOE_TILE_LANES, ROUTER_LANES), 1)
            ended = jnp.where((seg_end <= tile_lo) & (e_lane < N_EXPERTS), 1.0, 0.0).astype(BF16)
            tile_ref[...] = lax.dot_general(ones_lanes, ended, NT_DIMS, preferred_element_type=F32)

    @pl.when(phase == 1)
    def _():
        r = lax.broadcasted_iota(jnp.int32, (tm, tm), 0)
        c = lax.broadcasted_iota(jnp.int32, (tm, tm), 1)
        earlier = jnp.where(c < r, 1.0, 0.0).astype(BF16)
        base = (jnp.dot(earlier, occ, preferred_element_type=F32)
                + run_scr[0:1, :] + start_scr[0:1, :])
        pos1_ref[...] = _exact_dot_nt(ones_lanes, pick1 * base)
        pos2_ref[...] = _exact_dot_nt(ones_lanes, pick2 * base)
        run_scr[...] += jnp.dot(ones_rows, occ, preferred_element_type=F32)


def _slots(route, tm):
    n = route.shape[0]
    n_tiles = _moe_tiles(n)
    assert n_tiles <= MOE_TILE_LANES and 2 * n + N_EXPERTS * MOE_TM < 2 ** 24
    row_out = pl.BlockSpec((SUBLANES, tm), lambda ph, i: (0, i * ph))
    sds = jax.ShapeDtypeStruct((SUBLANES, n), F32)
    return pl.pallas_call(
        functools.partial(_slots_body, n_tiles),
        grid=(2, n // tm),
        in_specs=[pl.BlockSpec((tm, ROUTER_LANES), lambda ph, i: (i, 0))],
        out_specs=[row_out, row_out, pl.BlockSpec((SUBLANES, MOE_TILE_LANES), lambda ph, i: (0, 0))],
        out_shape=[sds, sds, jax.ShapeDtypeStruct((SUBLANES, MOE_TILE_LANES), F32)],
        scratch_shapes=[pltpu.VMEM((SUBLANES, ROUTER_LANES), F32), pltpu.VMEM((SUBLANES, ROUTER_LANES), F32)],
        compiler_params=_cparams(("arbitrary", "arbitrary")),
        name="moe_slots",
    )(route)


def _dispatch_body(n_tiles, pos_ref, tile_ref, xn_ref, xs_hbm, zero_scr, zero_sem, row_sem):
    tm = xn_ref.shape[0]
    n = pl.num_programs(0) * tm
    base = pl.program_id(0) * tm

    @pl.when(pl.program_id(0) == 0)
    def _():
        zero_scr[...] = jnp.zeros_like(zero_scr)

        def fill_copy(t):
            return pltpu.make_async_copy(zero_scr, xs_hbm.at[pl.ds(t * MOE_TM, MOE_TM), :], zero_sem)

        def has_padding(t):
            return (tile_ref[t] >= N_EXPERTS) | (tile_ref[t] != tile_ref[t + 1])

        @pl.loop(0, n_tiles)
        def _(t):
            @pl.when(has_padding(t))
            def _():
                fill_copy(t).start()

        @pl.loop(0, n_tiles)
        def _(t):
            @pl.when(has_padding(t))
            def _():
                fill_copy(t).wait()

    def issue(j, carry):
        src = xn_ref.at[pl.ds(j, 1), :]
        pltpu.make_async_copy(src, xs_hbm.at[pl.ds(pos_ref[base + j], 1), :], row_sem).start()
        pltpu.make_async_copy(src, xs_hbm.at[pl.ds(pos_ref[n + base + j], 1), :], row_sem).start()
        return carry

    lax.fori_loop(0, tm, issue, 0, unroll=8)
    for _ in range(2):
        pltpu.make_async_copy(xn_ref, xs_hbm.at[pl.ds(0, tm), :], row_sem).wait()


def _dispatch(pos, tile_map, xn, tm):
    n, d = xn.shape
    n_tiles = _moe_tiles(n)
    return pl.pallas_call(
        functools.partial(_dispatch_body, n_tiles),
        grid_spec=pltpu.PrefetchScalarGridSpec(
            num_scalar_prefetch=2,
            grid=(n // tm,),
            in_specs=[pl.BlockSpec((tm, d), lambda i, p, t: (i, 0))],
            out_specs=pl.BlockSpec(memory_space=pl.ANY),
            scratch_shapes=[pltpu.VMEM((MOE_TM, d), F32), pltpu.SemaphoreType.DMA(()),
                            pltpu.SemaphoreType.DMA(())],
        ),
        out_shape=jax.ShapeDtypeStruct((n_tiles * MOE_TM, d), F32),
        compiler_params=_cparams(("arbitrary",)),
        name="moe_dispatch",
    )(pos, tile_map, xn)


def _experts_body(tile_ref, xs_ref, wg_ref, wu_ref, wd_ref, ys_ref):
    used = tile_ref[pl.program_id(0)] < N_EXPERTS

    @pl.when(used)
    def _():
        x = xs_ref[...].astype(BF16)
        gate = jnp.dot(x, wg_ref[0].astype(BF16), preferred_element_type=F32)
        up = jnp.dot(x, wu_ref[0].astype(BF16), preferred_element_type=F32)
        hid = gate * jax.nn.sigmoid(gate) * up
        ys_ref[...] = jnp.dot(hid.astype(BF16), wd_ref[0].astype(BF16), preferred_element_type=F32)

    @pl.when(jnp.logical_not(used))
    def _():
        ys_ref[...] = jnp.zeros_like(ys_ref)


def _experts(tile_map, xs, w_gate, w_up, w_down):
    rows, d = xs.shape
    f = EXPERT_FF
    n_tiles = rows // MOE_TM
    x_map = lambda i, tm_ref: (jnp.where(tm_ref[i] < N_EXPERTS, i, 0), 0)
    y_map = lambda i, tm_ref: (i, 0)
    w_map = lambda i, tm_ref: (jnp.minimum(tm_ref[i], N_EXPERTS - 1), 0, 0)
    return pl.pallas_call(
        _experts_body,
        grid_spec=pltpu.PrefetchScalarGridSpec(
            num_scalar_prefetch=1,
            grid=(n_tiles,),
            in_specs=[pl.BlockSpec((MOE_TM, d), x_map), pl.BlockSpec((1, d, f), w_map),
                      pl.BlockSpec((1, d, f), w_map), pl.BlockSpec((1, f, d), w_map)],
            out_specs=pl.BlockSpec((MOE_TM, d), y_map),
        ),
        out_shape=jax.ShapeDtypeStruct((rows, d), F32),
        compiler_params=_cparams(("arbitrary",)),
        name="moe_experts",
    )(tile_map, xs, w_gate, w_up, w_down)


def _combine_body(pos_ref, h_ref, route_ref, ys_hbm, o_ref, buf, sem):
    tm = h_ref.shape[0]
    n = pl.num_programs(0) * tm
    base = pl.program_id(0) * tm

    def issue(j, carry):
        t = base + j
        pltpu.make_async_copy(ys_hbm.at[pl.ds(pos_ref[t], 1), :], buf.at[0, pl.ds(j, 1), :], sem).start()
        pltpu.make_async_copy(ys_hbm.at[pl.ds(pos_ref[n + t], 1), :], buf.at[1, pl.ds(j, 1), :], sem).start()
        return carry

    lax.fori_loop(0, tm, issue, 0, unroll=8)
    pltpu.make_async_copy(buf, buf, sem).wait()
    route = route_ref[...]
    o_ref[...] = h_ref[...] + route[:, 2:3] * buf[0] + route[:, 3:4] * buf[1]


def _combine(pos, h, route, ys, tm):
    n, d = h.shape
    row = lambda i, p: (i, 0)
    return pl.pallas_call(
        _combine_body,
        grid_spec=pltpu.PrefetchScalarGridSpec(
            num_scalar_prefetch=1,
            grid=(n // tm,),
            in_specs=[pl.BlockSpec((tm, d), row), pl.BlockSpec((tm, ROUTER_LANES), row),
                      pl.BlockSpec(memory_space=pl.ANY)],
            out_specs=pl.BlockSpec((tm, d), row),
            scratch_shapes=[pltpu.VMEM((2, tm, d), F32), pltpu.SemaphoreType.DMA(())],
        ),
        out_shape=jax.ShapeDtypeStruct((n, d), F32),
        compiler_params=_cparams(("arbitrary",)),
        name="moe_combine",
    )(pos, h, route, ys)


def _gelu(x):
    return 0.5 * x * (1.0 + lax.erf(x * (1.0 / math.sqrt(2.0))))


def _in_odd_body(h_ref, g_ref, w_ref, vg_ref, u_ref, vn_ref, q_ref, k_ref, v_ref):
    xn = _rms(h_ref[...], g_ref[...])
    proj = jnp.dot(xn.astype(BF16), w_ref[...], preferred_element_type=F32)
    w = MIX_WIDTH
    u_ref[...] = _gelu(proj[:, :w]).astype(BF16)
    vn_ref[...] = _rms(_gelu(proj[:, w:2 * w]), vg_ref[...]).astype(BF16)
    q_ref[...] = (proj[:, 2 * w:3 * w] * (1.0 / math.sqrt(HEAD_DIM))).astype(BF16)
    k_ref[...] = proj[:, 3 * w:4 * w].astype(BF16)
    v_ref[...] = proj[:, 4 * w:].astype(BF16)


def _in_odd(h, gain, w, v_gain, tm):
    n, d = h.shape
    row = lambda i: (i, 0)
    const2 = lambda i: (0, 0)
    sds = jax.ShapeDtypeStruct((n, MIX_WIDTH), BF16)
    return pl.pallas_call(
        _in_odd_body,
        grid=(n // tm,),
        in_specs=[pl.BlockSpec((tm, d), row), pl.BlockSpec((1, d), const2),
                  pl.BlockSpec(w.shape, const2), pl.BlockSpec((1, MIX_WIDTH), const2)],
        out_specs=[pl.BlockSpec((tm, MIX_WIDTH), row)] * 5,
        out_shape=[sds] * 5,
        compiler_params=_cparams(("arbitrary",)),
        name="in_odd",
    )(h, gain, w, v_gain)


SB_T = 128
SB_NB = 3
SB_TK = SB_NB * SB_T
SB_QBLK = 512
SB_UNDERFLOW = -104.0


def _sb_body(q_ref, k_ref, v_ref, o_ref, acc_scr, run_scr):
    qi = pl.program_id(2)
    lane = lax.broadcasted_iota(jnp.int32, (SB_T, LANES), 1)
    row = lax.broadcasted_iota(jnp.int32, (SB_T, 1), 0)
    col = lax.broadcasted_iota(jnp.int32, (SB_T, SB_TK), 1)
    rr = lax.broadcasted_iota(jnp.int32, (SB_T, 2 * SB_T), 0)
    cc = lax.broadcasted_iota(jnp.int32, (SB_T, 2 * SB_T), 1)
    suffix = jnp.where((cc >= SB_T) | (rr > cc), 1.0, 0.0).astype(BF16)

    def subtile(sub, _):
        q_lo = pl.multiple_of(sub * SB_T, SB_T)
        qs = qi * SB_QBLK + q_lo
        q = q_ref[0, pl.ds(q_lo, SB_T), :]
        zero = jnp.zeros_like(q)
        q_heads = (jnp.where(lane < HEAD_DIM, q, zero), jnp.where(lane < HEAD_DIM, zero, q))
        acc_scr[...] = jnp.zeros_like(acc_scr)
        run_scr[...] = jnp.zeros_like(run_scr)

        def cond(carry):
            hi, done = carry
            return (hi > 0) & (done == 0)

        def body(carry):
            hi, _ = carry
            ks = pl.multiple_of(jnp.maximum(hi - SB_TK, 0), SB_T)
            kt = k_ref[0, pl.ds(ks, SB_TK), :]
            vt = v_ref[0, pl.ds(ks, SB_TK), :]
            valid = col < (jnp.minimum(row + qs, hi) - ks)
            worst = None
            for a in range(2):
                z = lax.dot_general(q_heads[a], kt, NT_DIMS, preferred_element_type=F32)
                sp = jnp.log(1.0 + jnp.exp(-jnp.abs(z)))
                log_beta = jnp.minimum(z, 0.0) - sp
                log_rest = jnp.where(valid, log_beta - z, 0.0)
                run = run_scr[a]
                pieces = [None] * SB_NB
                for blk in reversed(range(SB_NB)):
                    bs = slice(blk * SB_T, (blk + 1) * SB_T)
                    tt = _split_dot(log_rest[:, bs], suffix)
                    pieces[blk] = jnp.exp(log_beta[:, bs] + tt[:, :SB_T] + run)
                    run = run + tt[:, SB_T:]
                att = jnp.where(valid, jnp.concatenate(pieces, axis=1), 0.0)
                acc_scr[a] += jnp.dot(att.astype(BF16), vt, preferred_element_type=F32)
                run_scr[a] = run
                top = jnp.max(run)
                worst = top if worst is None else jnp.maximum(worst, top)
            return ks, (worst <= SB_UNDERFLOW).astype(jnp.int32)

        lax.while_loop(cond, body, (qs + SB_T, jnp.int32(0)))
        o_ref[0, pl.ds(q_lo, SB_T), :] = jnp.where(lane < HEAD_DIM, acc_scr[0], acc_scr[1]).astype(BF16)
        return 0

    lax.fori_loop(0, SB_QBLK // SB_T, subtile, 0)


def _sb_attention(q, k, v):
    b, s, w = q.shape
    assert s % SB_QBLK == 0 and s >= SB_TK
    qspec = pl.BlockSpec((1, SB_QBLK, LANES), lambda bi, hp, i: (bi, i, hp))
    kvspec = pl.BlockSpec((1, s, LANES), lambda bi, hp, i: (bi, 0, hp))
    return pl.pallas_call(
        _sb_body,
        grid=(b, w // LANES, s // SB_QBLK),
        in_specs=[qspec, kvspec, kvspec],
        out_specs=qspec,
        out_shape=jax.ShapeDtypeStruct((b, s, w), BF16),
        scratch_shapes=[pltpu.VMEM((2, SB_T, LANES), F32), pltpu.VMEM((2, SB_T, LANES), F32)],
        compiler_params=_cparams(("arbitrary", "arbitrary", "arbitrary")),
        name="sb_attention",
    )(q, k, v)


def _out_odd_body(u_ref, vn_ref, yd_ref, h_ref, ws_ref, bs_ref, wo_ref, o_ref, yc_scr):
    tm = u_ref.shape[0]
    r = lax.broadcasted_iota(jnp.int32, (SGU_BLOCK, SGU_BLOCK), 0)
    c = lax.broadcasted_iota(jnp.int32, (SGU_BLOCK, SGU_BLOCK), 1)
    for g in range(MIX_WIDTH // LANES):
        ls = slice(g * LANES, (g + 1) * LANES)
        ws = jnp.where(c <= r, ws_ref[g], jnp.zeros_like(ws_ref[g]))
        for blk in range(tm // SGU_BLOCK):
            rs = slice(blk * SGU_BLOCK, (blk + 1) * SGU_BLOCK)
            mixed = jnp.dot(ws, vn_ref[rs, ls], preferred_element_type=F32) + bs_ref[g]
            yc_scr[rs, ls] = (u_ref[rs, ls].astype(F32) * mixed).astype(BF16)
    w = MIX_WIDTH
    o_ref[...] = (h_ref[...]
                  + jnp.dot(yc_scr[...], wo_ref[0:w, :], preferred_element_type=F32)
                  + jnp.dot(yd_ref[...], wo_ref[w:, :], preferred_element_type=F32))


def _out_odd(u, vn, yd, h, sgu_w, sgu_b, w_out, tm):
    n, d = h.shape
    w = MIX_WIDTH
    row = lambda i: (i, 0)
    const2 = lambda i: (0, 0)
    const3 = lambda i: (0, 0, 0)
    return pl.pallas_call(
        _out_odd_body,
        grid=(n // tm,),
        in_specs=[pl.BlockSpec((tm, w), row), pl.BlockSpec((tm, w), row), pl.BlockSpec((tm, w), row),
                  pl.BlockSpec((tm, d), row), pl.BlockSpec(sgu_w.shape, const3),
                  pl.BlockSpec(sgu_b.shape, const3), pl.BlockSpec(w_out.shape, const2)],
        out_specs=pl.BlockSpec((tm, d), row),
        out_shape=jax.ShapeDtypeStruct((n, d), F32),
        scratch_shapes=[pltpu.VMEM((tm, w), BF16)],
        compiler_params=_cparams(("arbitrary",)),
        name="out_odd",
    )(u, vn, yd, h, sgu_w, sgu_b, w_out)


def _moe_layer(h, gain, w_rg, w_re, w_gate, w_up, w_down):
    d = h.shape[1]
    pad = jnp.zeros((d, ROUTER_LANES - N_GROUPS - N_EXPERTS), F32)
    w_router = jnp.concatenate([w_rg, w_re, pad], axis=1)
    xn, route = _router(h, gain[None, :], w_router, tm=512)
    pos1, pos2, tile_map = _slots(route, tm=512)
    pos = jnp.concatenate([pos1[0], pos2[0]]).astype(jnp.int32)
    tile_map = tile_map[0].astype(jnp.int32)
    xs = _dispatch(pos, tile_map, xn, tm=256)
    ys = _experts(tile_map, xs, w_gate.reshape(N_EXPERTS, d, EXPERT_FF),
                  w_up.reshape(N_EXPERTS, d, EXPERT_FF), w_down.reshape(N_EXPERTS, EXPERT_FF, d))
    return _combine(pos, h, route, ys, tm=256)


def kernel(x, mix_norm_even, w_in_even, att_q_norm, att_k_norm, att_rel_bias, pool_w, pool_scale,
           w_out_even, mix_norm_odd, w_in_odd, sgu_v_norm, sgu_w, sgu_b, w_out_odd, ffn_norm,
           w_router_group, w_router_expert, w_exp_gate, w_exp_up, w_exp_down):
    b, s, d = x.shape
    n = b * s
    depth = ffn_norm.shape[0]
    heads = MIX_WIDTH // HEAD_DIM
    h = x.reshape(n, d)
    for layer in range(depth):
        i = layer // 2
        if layer % 2 == 0:
            q, k_pad, v_pad, p = _in_even(
                h.reshape(b, s, d), mix_norm_even[i][None, :], w_in_even[i].astype(BF16),
                jnp.tile(att_q_norm[i], heads)[None, :], jnp.tile(att_k_norm[i], heads)[None, :],
                tm=ATT_LEFT)
            ya = _band_attention(q, k_pad, v_pad, _band_bias(att_rel_bias[i]))
            h = _out_even(ya.reshape(n, MIX_WIDTH), p.reshape(n, MIX_WIDTH), h,
                          pool_w[i].astype(BF16), pool_scale[i][None, :],
                          w_out_even[i].astype(BF16), seq=s, tm=512)
        else:
            u, vn, q, k, v = _in_odd(h, mix_norm_odd[i][None, :], w_in_odd[i].astype(BF16),
                                     sgu_v_norm[i][None, :], tm=512)
            to3 = lambda t: t.reshape(b, s, MIX_WIDTH)
            yd = _sb_attention(to3(q), to3(k), to3(v))
            bias = jnp.broadcast_to(sgu_b[i][:, :, None], (N_GROUPS, SGU_BLOCK, LANES))
            h = _out_odd(u, vn, yd.reshape(n, MIX_WIDTH), h, sgu_w[i].astype(BF16), bias,
                         w_out_odd[i].astype(BF16), tm=512)
        h = _moe_layer(h, ffn_norm[layer], w_router_group[layer], w_router_expert[layer],
                       w_exp_gate[layer], w_exp_up[layer], w_exp_down[layer])
    return h.reshape(b, s, d)
```

```python
import functools
import math

import jax
import jax.numpy as jnp
from jax import lax
from jax.experimental import pallas as pl
from jax.experimental.pallas import tpu as pltpu

F32 = jnp.float32
BF16 = jnp.bfloat16

D_MODEL = 1024
CHUNK = 64
EPS = 1e-6
HEAD_DIM = 64
MIX_WIDTH = 512
LANES = 128
ATT_LEFT = 8 * CHUNK
ATT_MAX_REL = 128
POOL_WINDOWS = (2, 4, 8, 16)
POOL_HALO = 16
SGU_BLOCK = 128
N_GROUPS = 4
N_EXP_PER_GROUP = 8
N_EXPERTS = N_GROUPS * N_EXP_PER_GROUP
EXPERT_FF = 256
ROUTER_LANES = 128
NEG_BIG = -1e30
VMEM_LIMIT = 56 * 1024 * 1024

NT_DIMS = (((1,), (1,)), ((), ()))


def _cparams(sem):
    return pltpu.CompilerParams(dimension_semantics=sem, vmem_limit_bytes=VMEM_LIMIT)


def _rms(x, gain):
    return x * lax.rsqrt(jnp.mean(x * x, axis=-1, keepdims=True) + EPS) * gain


def _split_dot(x, m):
    hi = x.astype(BF16)
    lo = (x - hi.astype(F32)).astype(BF16)
    return (jnp.dot(hi, m, preferred_element_type=F32)
            + jnp.dot(lo, m, preferred_element_type=F32))


def _head_rms(t, gain):
    n = t.shape[-1]
    r = lax.broadcasted_iota(jnp.int32, (n, n), 0) // HEAD_DIM
    c = lax.broadcasted_iota(jnp.int32, (n, n), 1) // HEAD_DIM
    bd = jnp.where(r == c, 1.0, 0.0).astype(BF16)
    ms = _split_dot(t * t, bd) * (1.0 / HEAD_DIM)
    return t * lax.rsqrt(ms + EPS) * gain


def _in_even_body(h_ref, g_ref, w_ref, qg_ref, kg_ref, q_ref, k_ref, v_ref, p_ref):
    j = pl.program_id(1)

    @pl.when(j == 0)
    def _():
        k_ref[...] = jnp.zeros_like(k_ref)
        v_ref[...] = jnp.zeros_like(v_ref)

    @pl.when(j > 0)
    def _():
        xn = _rms(h_ref[0], g_ref[...])
        proj = jnp.dot(xn.astype(BF16), w_ref[...], preferred_element_type=F32)
        w = MIX_WIDTH
        q_ref[0] = (_head_rms(proj[:, :w], qg_ref[...]) * (1.0 / math.sqrt(HEAD_DIM))).astype(BF16)
        k_ref[0] = _head_rms(proj[:, w:2 * w], kg_ref[...]).astype(BF16)
        v_ref[0] = proj[:, 2 * w:3 * w].astype(BF16)
        p_ref[0] = proj[:, 3 * w:].astype(BF16)


def _in_even(h, gain, w, q_gain, k_gain, tm):
    b, s, d = h.shape
    assert tm == ATT_LEFT and s % tm == 0
    nt = s // tm
    cur = lambda bi, j: (bi, jnp.maximum(j - 1, 0), 0)
    const = lambda bi, j: (0, 0)
    out_sds = lambda rows: jax.ShapeDtypeStruct((b, rows, MIX_WIDTH), BF16)
    return pl.pallas_call(
        _in_even_body,
        grid=(b, nt + 1),
        in_specs=[
            pl.BlockSpec((1, tm, d), cur),
            pl.BlockSpec((1, d), const),
            pl.BlockSpec(w.shape, const),
            pl.BlockSpec((1, MIX_WIDTH), const),
            pl.BlockSpec((1, MIX_WIDTH), const),
        ],
        out_specs=[
            pl.BlockSpec((1, tm, MIX_WIDTH), cur),
            pl.BlockSpec((1, tm, MIX_WIDTH), lambda bi, j: (bi, j, 0)),
            pl.BlockSpec((1, tm, MIX_WIDTH), lambda bi, j: (bi, j, 0)),
            pl.BlockSpec((1, tm, MIX_WIDTH), cur),
        ],
        out_shape=[out_sds(s), out_sds(s + ATT_LEFT), out_sds(s + ATT_LEFT), out_sds(s)],
        compiler_params=_cparams(("arbitrary", "arbitrary")),
        name="in_even",
    )(h, gain, w, q_gain, k_gain)


BAND_TQ = 2 * CHUNK
BAND_TK = BAND_TQ + ATT_LEFT


def _band_body(q_ref, k_ref, v_ref, bias_ref, o_ref):
    i = pl.program_id(1)
    start = pl.multiple_of(i * BAND_TQ, BAND_TQ)
    lane = lax.broadcasted_iota(jnp.int32, (BAND_TQ, LANES), 1)
    col = lax.broadcasted_iota(jnp.int32, (BAND_TQ, BAND_TK), 1)
    is_pad = (col + start) < ATT_LEFT
    for hp in range(MIX_WIDTH // LANES):
        ls = slice(hp * LANES, (hp + 1) * LANES)
        q = q_ref[0, :, ls]
        kb = k_ref[0, pl.ds(start, BAND_TK), ls]
        vb = v_ref[0, pl.ds(start, BAND_TK), ls]
        outs = []
        for a in range(2):
            qa = jnp.where((lane < HEAD_DIM) == (a == 0), q, jnp.zeros_like(q))
            s = lax.dot_general(qa, kb, NT_DIMS, preferred_element_type=F32)
            s = jnp.where(is_pad, NEG_BIG, s + bias_ref[2 * hp + a])
            m = jnp.max(s, axis=-1, keepdims=True)
            p = jnp.exp(s - m)
            l = jnp.sum(p, axis=-1, keepdims=True)
            o = jnp.dot(p.astype(BF16), vb, preferred_element_type=F32)
            outs.append(o / l)
        o_ref[0, :, ls] = jnp.where(lane < HEAD_DIM, outs[0], outs[1]).astype(BF16)


def _band_bias(rel_bias):
    heads = rel_bias.shape[0]
    r = jnp.arange(BAND_TQ)[:, None]
    j = jnp.arange(BAND_TK)[None, :]
    jb = j - CHUNK * (r // CHUNK)
    in_band = (jb >= 0) & (jb < ATT_LEFT + CHUNK)
    period = BAND_TK + BAND_TQ
    far = jnp.broadcast_to(rel_bias[:, 2 * ATT_MAX_REL:], (heads, ATT_LEFT - ATT_MAX_REL + 1))
    near = rel_bias[:, 2 * ATT_MAX_REL - 1:0:-1]
    wrap = jnp.broadcast_to(rel_bias[:, 2 * ATT_MAX_REL:], (heads, period - BAND_TK))
    g = jnp.concatenate([far, near, wrap], axis=1).astype(F32)
    assert g.shape[1] == period
    toep = jnp.tile(g, (1, BAND_TQ))[:, :BAND_TQ * (period - 1)].reshape(heads, BAND_TQ, period - 1)
    return jnp.where(in_band[None], toep[:, :, :BAND_TK], NEG_BIG)


def _band_attention(q, k_pad, v_pad, bias):
    b, s, w = q.shape
    sp = k_pad.shape[1]
    return pl.pallas_call(
        _band_body,
        grid=(b, s // BAND_TQ),
        in_specs=[
            pl.BlockSpec((1, BAND_TQ, w), lambda bi, i: (bi, i, 0)),
            pl.BlockSpec((1, sp, w), lambda bi, i: (bi, 0, 0)),
            pl.BlockSpec((1, sp, w), lambda bi, i: (bi, 0, 0)),
            pl.BlockSpec(bias.shape, lambda bi, i: (0, 0, 0)),
        ],
        out_specs=pl.BlockSpec((1, BAND_TQ, w), lambda bi, i: (bi, i, 0)),
        out_shape=jax.ShapeDtypeStruct((b, s, w), BF16),
        compiler_params=_cparams(("arbitrary", "arbitrary")),
        name="band_attention",
    )(q, k_pad, v_pad, bias)


def _route_tokens(h, gain, w_router):
    xn = _rms(h, gain)
    logits = jnp.dot(xn, w_router, preferred_element_type=F32, precision=lax.Precision.HIGHEST)
    lane = lax.broadcasted_iota(jnp.int32, logits.shape, 1).astype(F32)
    ninf = -jnp.inf

    def top(vals):
        m = jnp.max(vals, axis=-1, keepdims=True)
        idx = jnp.min(jnp.where(vals == m, lane, float(ROUTER_LANES)), axis=-1, keepdims=True)
        return m, idx

    is_group = lane < N_GROUPS
    g_max, g_sel = top(jnp.where(is_group, logits, ninf))
    g_den = jnp.sum(jnp.where(is_group, jnp.exp(logits - g_max), 0.0), axis=-1, keepdims=True)
    g_weight = 1.0 / g_den
    lo = N_GROUPS + N_EXP_PER_GROUP * g_sel
    e_logits = jnp.where((lane >= lo) & (lane < lo + N_EXP_PER_GROUP), logits, ninf)
    e1, i1 = top(e_logits)
    e2, i2 = top(jnp.where(lane == i1, ninf, e_logits))
    t = jnp.exp(e2 - e1)
    w1 = g_weight / (1.0 + t)
    w2 = g_weight * t / (1.0 + t)
    route = jnp.where(lane == 0.0, i1 - N_GROUPS, jnp.where(lane == 1.0, i2 - N_GROUPS, 0.0))
    return xn, jnp.where(lane == 2.0, w1, jnp.where(lane == 3.0, w2, route))


def _router_weights(w_rg, w_re):
    pad = jnp.zeros((w_rg.shape[0], ROUTER_LANES - N_GROUPS - N_EXPERTS), F32)
    return jnp.concatenate([w_rg, w_re, pad], axis=1)


def _out_even_body(tiles_per_seq, ya_ref, p_ref, halo_ref, h_ref, pw_ref, ps_ref, wo_ref, fg_ref, wr_ref,
                   o_ref, xn_ref, route_ref, p_scr, yb_scr):
    tm = p_ref.shape[0]
    it = pl.program_id(0) % tiles_per_seq
    halo = halo_ref[...].astype(F32)
    p_scr[0:POOL_HALO, :] = jnp.where(it == 0, jnp.zeros_like(halo), halo)
    p_scr[POOL_HALO:, :] = p_ref[...].astype(F32)
    t = it * tm + lax.broadcasted_iota(jnp.int32, (tm, 1), 0)
    for g, win in enumerate(POOL_WINDOWS):
        ls = slice(g * LANES, (g + 1) * LANES)
        cur = p_scr[POOL_HALO:POOL_HALO + tm, ls]
        acc = cur
        for dlt in range(1, win):
            acc = acc + p_scr[POOL_HALO - dlt:POOL_HALO - dlt + tm, ls]
        cnt = jnp.minimum(t + 1, win).astype(F32)
        mixed = acc / cnt - cur
        yb = jnp.dot(mixed.astype(BF16), pw_ref[g], preferred_element_type=F32) * ps_ref[:, ls]
        yb_scr[:, ls] = yb.astype(BF16)
    w = MIX_WIDTH
    h_new = (h_ref[...]
             + jnp.dot(ya_ref[...], wo_ref[0:w, :], preferred_element_type=F32)
             + jnp.dot(yb_scr[...], wo_ref[w:, :], preferred_element_type=F32))
    o_ref[...] = h_new
    xn_ref[...], route_ref[...] = _route_tokens(h_new, fg_ref[...], wr_ref[...])


def _out_even(ya, p, h, pool_w, pool_scale, w_out, ffn_gain, w_router, seq, tm):
    n, d = h.shape
    w = MIX_WIDTH
    row = lambda i: (i, 0)
    const2 = lambda i: (0, 0)
    halo_blocks = tm // POOL_HALO
    return pl.pallas_call(
        functools.partial(_out_even_body, seq // tm),
        grid=(n // tm,),
        in_specs=[
            pl.BlockSpec((tm, w), row),
            pl.BlockSpec((tm, w), row),
            pl.BlockSpec((POOL_HALO, w), lambda i: (jnp.maximum(i * halo_blocks - 1, 0), 0)),
            pl.BlockSpec((tm, d), row),
            pl.BlockSpec(pool_w.shape, lambda i: (0, 0, 0)),
            pl.BlockSpec((1, w), const2),
            pl.BlockSpec(w_out.shape, const2),
            pl.BlockSpec((1, d), const2),
            pl.BlockSpec(w_router.shape, const2),
        ],
        out_specs=[pl.BlockSpec((tm, d), row), pl.BlockSpec((tm, d), row),
                   pl.BlockSpec((tm, ROUTER_LANES), row)],
        out_shape=[jax.ShapeDtypeStruct((n, d), F32), jax.ShapeDtypeStruct((n, d), F32),
                   jax.ShapeDtypeStruct((n, ROUTER_LANES), F32)],
        scratch_shapes=[pltpu.VMEM((tm + POOL_HALO, w), F32), pltpu.VMEM((tm, w), BF16)],
        compiler_params=_cparams(("arbitrary",)),
        name="out_even",
    )(ya, p, p, h, pool_w, pool_scale, w_out, ffn_gain, w_router)


MOE_TM = 256
MOE_TILE_LANES = 256
SUBLANES = 8


def _moe_tiles(n):
    return (2 * n) // MOE_TM + N_EXPERTS


def _exact_dot_nt(ones, x):
    out = None
    for _ in range(3):
        part = x.astype(BF16)
        x = x - part.astype(F32)
        term = lax.dot_general(ones, part, NT_DIMS, preferred_element_type=F32)
        out = term if out is None else out + term
    return out


def _slots_body(n_tiles, route_ref, pos1_ref, pos2_ref, tile_ref, run_scr, start_scr):
    phase = pl.program_id(0)
    i = pl.program_id(1)
    tm = route_ref.shape[0]
    route = route_ref[...]
    lane = lax.broadcasted_iota(jnp.int32, (tm, ROUTER_LANES), 1).astype(F32)
    pick1 = jnp.where(lane == route[:, 0:1], 1.0, 0.0)
    pick2 = jnp.where(lane == route[:, 1:2], 1.0, 0.0)
    occ = (pick1 + pick2).astype(BF16)
    ones_rows = jnp.ones((SUBLANES, tm), BF16)
    ones_lanes = jnp.ones((SUBLANES, ROUTER_LANES), BF16)

    @pl.when(i == 0)
    def _():
        run_scr[...] = jnp.zeros_like(run_scr)

    @pl.when(phase == 0)
    def _():
        run_scr[...] += jnp.dot(ones_rows, occ, preferred_element_type=F32)

        @pl.when(i == pl.num_programs(1) - 1)
        def _():
            padded = jnp.floor((run_scr[...] + (MOE_TM - 1)) * (1.0 / MOE_TM)) * MOE_TM
            r = lax.broadcasted_iota(jnp.int32, (ROUTER_LANES, ROUTER_LANES), 0)
            c = lax.broadcasted_iota(jnp.int32, (ROUTER_LANES, ROUTER_LANES), 1)
            before = jnp.where(r < c, 1.0, 0.0).astype(BF16)
            hi = padded.astype(BF16)
            mid = (padded - hi.astype(F32)).astype(BF16)
            low = (padded - hi.astype(F32) - mid.astype(F32)).astype(BF16)
            start = (jnp.dot(hi, before, preferred_element_type=F32)
                     + jnp.dot(mid, before, preferred_element_type=F32)
                     + jnp.dot(low, before, preferred_element_type=F32))
            start_scr[...] = start
            seg_end = start[0:1, :] + padded[0:1, :]
            tile_lo = (lax.broadcasted_iota(jnp.int32, (MOE_TILE_LANES, ROUTER_LANES), 0) * MOE_TM).astype(F32)
            e_lane = lax.broadcasted_iota(jnp.int32, (MOE_TILE_LANES, ROUTER_LANES), 1)
            ended = jnp.where((seg_end <= tile_lo) & (e_lane < N_EXPERTS), 1.0, 0.0).astype(BF16)
            tile_ref[...] = lax.dot_general(ones_lanes, ended, NT_DIMS, preferred_element_type=F32)

    @pl.when(phase == 1)
    def _():
        r = lax.broadcasted_iota(jnp.int32, (tm, tm), 0)
        c = lax.broadcasted_iota(jnp.int32, (tm, tm), 1)
        earlier = jnp.where(c < r, 1.0, 0.0).astype(BF16)
        base = (jnp.dot(earlier, occ, preferred_element_type=F32)
                + run_scr[0:1, :] + start_scr[0:1, :])
        pos1_ref[...] = _exact_dot_nt(ones_lanes, pick1 * base)
        pos2_ref[...] = _exact_dot_nt(ones_lanes, pick2 * base)
        run_scr[...] += jnp.dot(ones_rows, occ, preferred_element_type=F32)


def _slots(route, tm):
    n = route.shape[0]
    n_tiles = _moe_tiles(n)
    assert n_tiles <= MOE_TILE_LANES and 2 * n + N_EXPERTS * MOE_TM < 2 ** 24
    row_out = pl.BlockSpec((SUBLANES, tm), lambda ph, i: (0, i * ph))
    sds = jax.ShapeDtypeStruct((SUBLANES, n), F32)
    return pl.pallas_call(
        functools.partial(_slots_body, n_tiles),
        grid=(2, n // tm),
        in_specs=[pl.BlockSpec((tm, ROUTER_LANES), lambda ph, i: (i, 0))],
        out_specs=[row_out, row_out, pl.BlockSpec((SUBLANES, MOE_TILE_LANES), lambda ph, i: (0, 0))],
        out_shape=[sds, sds, jax.ShapeDtypeStruct((SUBLANES, MOE_TILE_LANES), F32)],
        scratch_shapes=[pltpu.VMEM((SUBLANES, ROUTER_LANES), F32), pltpu.VMEM((SUBLANES, ROUTER_LANES), F32)],
        compiler_params=_cparams(("arbitrary", "arbitrary")),
        name="moe_slots",
    )(route)


def _dispatch_body(n_tiles, pos_ref, tile_ref, xn_ref, xs_hbm, zero_scr, zero_sem, row_sem):
    tm = xn_ref.shape[0]
    n = pl.num_programs(0) * tm
    base = pl.program_id(0) * tm

    @pl.when(pl.program_id(0) == 0)
    def _():
        zero_scr[...] = jnp.zeros_like(zero_scr)

        def fill_copy(t):
            return pltpu.make_async_copy(zero_scr, xs_hbm.at[pl.ds(t * MOE_TM, MOE_TM), :], zero_sem)

        def has_padding(t):
            return (tile_ref[t] >= N_EXPERTS) | (tile_ref[t] != tile_ref[t + 1])

        @pl.loop(0, n_tiles)
        def _(t):
            @pl.when(has_padding(t))
            def _():
                fill_copy(t).start()

        @pl.loop(0, n_tiles)
        def _(t):
            @pl.when(has_padding(t))
            def _():
                fill_copy(t).wait()

    def issue(j, carry):
        src = xn_ref.at[pl.ds(j, 1), :]
        pltpu.make_async_copy(src, xs_hbm.at[pl.ds(pos_ref[base + j], 1), :], row_sem).start()
        pltpu.make_async_copy(src, xs_hbm.at[pl.ds(pos_ref[n + base + j], 1), :], row_sem).start()
        return carry

    lax.fori_loop(0, tm, issue, 0, unroll=8)
    for _ in range(2):
        pltpu.make_async_copy(xn_ref, xs_hbm.at[pl.ds(0, tm), :], row_sem).wait()


def _dispatch(pos, tile_map, xn, tm):
    n, d = xn.shape
    n_tiles = _moe_tiles(n)
    return pl.pallas_call(
        functools.partial(_dispatch_body, n_tiles),
        grid_spec=pltpu.PrefetchScalarGridSpec(
            num_scalar_prefetch=2,
            grid=(n // tm,),
            in_specs=[pl.BlockSpec((tm, d), lambda i, p, t: (i, 0))],
            out_specs=pl.BlockSpec(memory_space=pl.ANY),
            scratch_shapes=[pltpu.VMEM((MOE_TM, d), F32), pltpu.SemaphoreType.DMA(()),
                            pltpu.SemaphoreType.DMA(())],
        ),
        out_shape=jax.ShapeDtypeStruct((n_tiles * MOE_TM, d), F32),
        compiler_params=_cparams(("arbitrary",)),
        name="moe_dispatch",
    )(pos, tile_map, xn)


def _experts_body(tile_ref, xs_ref, wg_ref, wu_ref, wd_ref, ys_ref):
    used = tile_ref[pl.program_id(0)] < N_EXPERTS

    @pl.when(used)
    def _():
        x = xs_ref[...].astype(BF16)
        gate = jnp.dot(x, wg_ref[0].astype(BF16), preferred_element_type=F32)
        up = jnp.dot(x, wu_ref[0].astype(BF16), preferred_element_type=F32)
        hid = gate * jax.nn.sigmoid(gate) * up
        ys_ref[...] = jnp.dot(hid.astype(BF16), wd_ref[0].astype(BF16), preferred_element_type=F32)

    @pl.when(jnp.logical_not(used))
    def _():
        ys_ref[...] = jnp.zeros_like(ys_ref)


def _experts(tile_map, xs, w_gate, w_up, w_down, layer):
    rows, d = xs.shape
    f = EXPERT_FF
    n_tiles = rows // MOE_TM
    x_map = lambda i, tm_ref: (jnp.where(tm_ref[i] < N_EXPERTS, i, 0), 0)
    y_map = lambda i, tm_ref: (i, 0)
    w_map = lambda i, tm_ref: (layer * N_EXPERTS + jnp.minimum(tm_ref[i], N_EXPERTS - 1), 0, 0)
    return pl.pallas_call(
        _experts_body,
        grid_spec=pltpu.PrefetchScalarGridSpec(
            num_scalar_prefetch=1,
            grid=(n_tiles,),
            in_specs=[pl.BlockSpec((MOE_TM, d), x_map), pl.BlockSpec((1, d, f), w_map),
                      pl.BlockSpec((1, d, f), w_map), pl.BlockSpec((1, f, d), w_map)],
            out_specs=pl.BlockSpec((MOE_TM, d), y_map),
        ),
        out_shape=jax.ShapeDtypeStruct((rows, d), F32),
        compiler_params=_cparams(("arbitrary",)),
        name="moe_experts",
    )(tile_map, xs, w_gate, w_up, w_down)


def _combine_body(pos_ref, h_ref, route_ref, ys_hbm, o_ref, buf, sems):
    tm = h_ref.shape[0]
    steps = pl.num_programs(0)
    n = steps * tm
    i = pl.program_id(0)

    def start_gather(step, slot):
        base = step * tm

        def issue(j, carry):
            for pick in range(2):
                pltpu.make_async_copy(ys_hbm.at[pl.ds(pos_ref[pick * n + base + j], 1), :],
                                      buf.at[slot, pick, pl.ds(j, 1), :], sems.at[slot]).start()
            return carry

        lax.fori_loop(0, tm, issue, 0, unroll=8)

    @pl.when(i == 0)
    def _():
        start_gather(0, 0)

    @pl.when(i + 1 < steps)
    def _():
        start_gather(i + 1, (i + 1) % 2)

    slot = i % 2
    pltpu.make_async_copy(buf.at[slot], buf.at[slot], sems.at[slot]).wait()
    route = route_ref[...]
    o_ref[...] = h_ref[...] + route[:, 2:3] * buf[slot, 0] + route[:, 3:4] * buf[slot, 1]


def _combine(pos, h, route, ys, tm):
    n, d = h.shape
    row = lambda i, p: (i, 0)
    return pl.pallas_call(
        _combine_body,
        grid_spec=pltpu.PrefetchScalarGridSpec(
            num_scalar_prefetch=1,
            grid=(n // tm,),
            in_specs=[pl.BlockSpec((tm, d), row), pl.BlockSpec((tm, ROUTER_LANES), row),
                      pl.BlockSpec(memory_space=pl.ANY)],
            out_specs=pl.BlockSpec((tm, d), row),
            scratch_shapes=[pltpu.VMEM((2, 2, tm, d), F32), pltpu.SemaphoreType.DMA((2,))],
        ),
        out_shape=jax.ShapeDtypeStruct((n, d), F32),
        compiler_params=_cparams(("arbitrary",)),
        name="moe_combine",
    )(pos, h, route, ys)


def _gelu(x):
    return 0.5 * x * (1.0 + lax.erf(x * (1.0 / math.sqrt(2.0))))


def _in_odd_body(h_ref, g_ref, w_ref, vg_ref, u_ref, vn_ref, q_ref, k_ref, v_ref):
    xn = _rms(h_ref[...], g_ref[...])
    proj = jnp.dot(xn.astype(BF16), w_ref[...], preferred_element_type=F32)
    w = MIX_WIDTH
    u_ref[...] = _gelu(proj[:, :w]).astype(BF16)
    vn_ref[...] = _rms(_gelu(proj[:, w:2 * w]), vg_ref[...]).astype(BF16)
    q_ref[...] = (proj[:, 2 * w:3 * w] * (1.0 / math.sqrt(HEAD_DIM))).astype(BF16)
    k_ref[...] = proj[:, 3 * w:4 * w].astype(BF16)
    v_ref[...] = proj[:, 4 * w:].astype(BF16)


def _in_odd(h, gain, w, v_gain, tm):
    n, d = h.shape
    row = lambda i: (i, 0)
    const2 = lambda i: (0, 0)
    sds = jax.ShapeDtypeStruct((n, MIX_WIDTH), BF16)
    return pl.pallas_call(
        _in_odd_body,
        grid=(n // tm,),
        in_specs=[pl.BlockSpec((tm, d), row), pl.BlockSpec((1, d), const2),
                  pl.BlockSpec(w.shape, const2), pl.BlockSpec((1, MIX_WIDTH), const2)],
        out_specs=[pl.BlockSpec((tm, MIX_WIDTH), row)] * 5,
        out_shape=[sds] * 5,
        compiler_params=_cparams(("arbitrary",)),
        name="in_odd",
    )(h, gain, w, v_gain)


SB_T = 128
SB_NB = 3
SB_TK = SB_NB * SB_T
SB_QBLK = 512
SB_GROUP = 2
SB_UNDERFLOW = -104.0


def _sb_body(q_ref, k_ref, v_ref, o_ref, acc_scr, run_scr):
    qi = pl.program_id(2)
    lane = lax.broadcasted_iota(jnp.int32, (SB_T, LANES), 1)
    row = lax.broadcasted_iota(jnp.int32, (SB_T, 1), 0)
    col = lax.broadcasted_iota(jnp.int32, (SB_T, SB_TK), 1)
    rr = lax.broadcasted_iota(jnp.int32, (SB_T, 2 * SB_T), 0)
    cc = lax.broadcasted_iota(jnp.int32, (SB_T, 2 * SB_T), 1)
    suffix = jnp.where((cc >= SB_T) | (rr > cc), 1.0, 0.0).astype(BF16)

    def subtile_group(grp, _):
        q_los = [pl.multiple_of((grp * SB_GROUP + s) * SB_T, SB_T) for s in range(SB_GROUP)]
        q_starts = [qi * SB_QBLK + q_lo for q_lo in q_los]
        q_heads = []
        for q_lo in q_los:
            q = q_ref[0, pl.ds(q_lo, SB_T), :]
            zero = jnp.zeros_like(q)
            q_heads.append((jnp.where(lane < HEAD_DIM, q, zero), jnp.where(lane < HEAD_DIM, zero, q)))
        acc_scr[...] = jnp.zeros_like(acc_scr)
        run_scr[...] = jnp.zeros_like(run_scr)

        def cond(carry):
            his, dones = carry
            active = [(hi > 0) & (done == 0) for hi, done in zip(his, dones)]
            return functools.reduce(jnp.logical_or, active)

        def body(carry):
            his, _ = carry
            new_his, new_dones = [], []
            for s in range(SB_GROUP):
                hi = his[s]
                ks = pl.multiple_of(jnp.maximum(hi - SB_TK, 0), SB_T)
                kt = k_ref[0, pl.ds(ks, SB_TK), :]
                vt = v_ref[0, pl.ds(ks, SB_TK), :]
                valid = col < (jnp.minimum(row + q_starts[s], hi) - ks)
                worst = None
                for a in range(2):
                    z = lax.dot_general(q_heads[s][a], kt, NT_DIMS, preferred_element_type=F32)
                    sp = jnp.log(1.0 + jnp.exp(-jnp.abs(z)))
                    log_beta = jnp.minimum(z, 0.0) - sp
                    log_rest = jnp.where(valid, log_beta - z, 0.0)
                    run = run_scr[s, a]
                    pieces = [None] * SB_NB
                    for blk in reversed(range(SB_NB)):
                        bs = slice(blk * SB_T, (blk + 1) * SB_T)
                        tt = _split_dot(log_rest[:, bs], suffix)
                        pieces[blk] = jnp.exp(log_beta[:, bs] + tt[:, :SB_T] + run)
                        run = run + tt[:, SB_T:]
                    att = jnp.where(valid, jnp.concatenate(pieces, axis=1), 0.0)
                    acc_scr[s, a] += jnp.dot(att.astype(BF16), vt, preferred_element_type=F32)
                    run_scr[s, a] = run
                    top = jnp.max(run)
                    worst = top if worst is None else jnp.maximum(worst, top)
                new_his.append(ks)
                new_dones.append((worst <= SB_UNDERFLOW).astype(jnp.int32))
            return tuple(new_his), tuple(new_dones)

        lax.while_loop(cond, body, (tuple(qs + SB_T for qs in q_starts),
                                    tuple(jnp.int32(0) for _ in range(SB_GROUP))))
        for s in range(SB_GROUP):
            o_ref[0, pl.ds(q_los[s], SB_T), :] = jnp.where(
                lane < HEAD_DIM, acc_scr[s, 0], acc_scr[s, 1]).astype(BF16)
        return 0

    lax.fori_loop(0, SB_QBLK // (SB_T * SB_GROUP), subtile_group, 0)


def _sb_attention(q, k, v):
    b, s, w = q.shape
    assert s % SB_QBLK == 0 and s >= SB_TK
    qspec = pl.BlockSpec((1, SB_QBLK, LANES), lambda bi, hp, i: (bi, i, hp))
    kvspec = pl.BlockSpec((1, s, LANES), lambda bi, hp, i: (bi, 0, hp))
    return pl.pallas_call(
        _sb_body,
        grid=(b, w // LANES, s // SB_QBLK),
        in_specs=[qspec, kvspec, kvspec],
        out_specs=qspec,
        out_shape=jax.ShapeDtypeStruct((b, s, w), BF16),
        scratch_shapes=[pltpu.VMEM((SB_GROUP, 2, SB_T, LANES), F32),
                        pltpu.VMEM((SB_GROUP, 2, SB_T, LANES), F32)],
        compiler_params=_cparams(("arbitrary", "arbitrary", "arbitrary")),
        name="sb_attention",
    )(q, k, v)


def _out_odd_body(u_ref, vn_ref, yd_ref, h_ref, ws_ref, bs_ref, wo_ref, fg_ref, wr_ref,
                  o_ref, xn_ref, route_ref, yc_scr):
    tm = u_ref.shape[0]
    r = lax.broadcasted_iota(jnp.int32, (SGU_BLOCK, SGU_BLOCK), 0)
    c = lax.broadcasted_iota(jnp.int32, (SGU_BLOCK, SGU_BLOCK), 1)
    for g in range(MIX_WIDTH // LANES):
        ls = slice(g * LANES, (g + 1) * LANES)
        ws = jnp.where(c <= r, ws_ref[g], jnp.zeros_like(ws_ref[g]))
        for blk in range(tm // SGU_BLOCK):
            rs = slice(blk * SGU_BLOCK, (blk + 1) * SGU_BLOCK)
            mixed = jnp.dot(ws, vn_ref[rs, ls], preferred_element_type=F32) + bs_ref[g]
            yc_scr[rs, ls] = (u_ref[rs, ls].astype(F32) * mixed).astype(BF16)
    w = MIX_WIDTH
    h_new = (h_ref[...]
             + jnp.dot(yc_scr[...], wo_ref[0:w, :], preferred_element_type=F32)
             + jnp.dot(yd_ref[...], wo_ref[w:, :], preferred_element_type=F32))
    o_ref[...] = h_new
    xn_ref[...], route_ref[...] = _route_tokens(h_new, fg_ref[...], wr_ref[...])


def _out_odd(u, vn, yd, h, sgu_w, sgu_b, w_out, ffn_gain, w_router, tm):
    n, d = h.shape
    w = MIX_WIDTH
    row = lambda i: (i, 0)
    const2 = lambda i: (0, 0)
    const3 = lambda i: (0, 0, 0)
    return pl.pallas_call(
        _out_odd_body,
        grid=(n // tm,),
        in_specs=[pl.BlockSpec((tm, w), row), pl.BlockSpec((tm, w), row), pl.BlockSpec((tm, w), row),
                  pl.BlockSpec((tm, d), row), pl.BlockSpec(sgu_w.shape, const3),
                  pl.BlockSpec(sgu_b.shape, const3), pl.BlockSpec(w_out.shape, const2),
                  pl.BlockSpec((1, d), const2), pl.BlockSpec(w_router.shape, const2)],
        out_specs=[pl.BlockSpec((tm, d), row), pl.BlockSpec((tm, d), row),
                   pl.BlockSpec((tm, ROUTER_LANES), row)],
        out_shape=[jax.ShapeDtypeStruct((n, d), F32), jax.ShapeDtypeStruct((n, d), F32),
                   jax.ShapeDtypeStruct((n, ROUTER_LANES), F32)],
        scratch_shapes=[pltpu.VMEM((tm, w), BF16)],
        compiler_params=_cparams(("arbitrary",)),
        name="out_odd",
    )(u, vn, yd, h, sgu_w, sgu_b, w_out, ffn_gain, w_router)


def _moe_layer(h, xn, route, w_gate, w_up, w_down, layer):
    pos1, pos2, tile_map = _slots(route, tm=512)
    pos = jnp.concatenate([pos1[0], pos2[0]]).astype(jnp.int32)
    tile_map = tile_map[0].astype(jnp.int32)
    xs = _dispatch(pos, tile_map, xn, tm=256)
    ys = _experts(tile_map, xs, w_gate, w_up, w_down, layer)
    return _combine(pos, h, route, ys, tm=256)


def kernel(x, mix_norm_even, w_in_even, att_q_norm, att_k_norm, att_rel_bias, pool_w, pool_scale,
           w_out_even, mix_norm_odd, w_in_odd, sgu_v_norm, sgu_w, sgu_b, w_out_odd, ffn_norm,
           w_router_group, w_router_expert, w_exp_gate, w_exp_up, w_exp_down):
    b, s, d = x.shape
    n = b * s
    depth = ffn_norm.shape[0]
    heads = MIX_WIDTH // HEAD_DIM
    h = x.reshape(n, d)
    w_gate = w_exp_gate.reshape(depth * N_EXPERTS, d, EXPERT_FF)
    w_up = w_exp_up.reshape(depth * N_EXPERTS, d, EXPERT_FF)
    w_down = w_exp_down.reshape(depth * N_EXPERTS, EXPERT_FF, d)
    for layer in range(depth):
        i = layer // 2
        ffn_gain = ffn_norm[layer][None, :]
        w_router = _router_weights(w_router_group[layer], w_router_expert[layer])
        if layer % 2 == 0:
            q, k_pad, v_pad, p = _in_even(
                h.reshape(b, s, d), mix_norm_even[i][None, :], w_in_even[i].astype(BF16),
                jnp.tile(att_q_norm[i], heads)[None, :], jnp.tile(att_k_norm[i], heads)[None, :],
                tm=ATT_LEFT)
            ya = _band_attention(q, k_pad, v_pad, _band_bias(att_rel_bias[i]))
            h, xn, route = _out_even(ya.reshape(n, MIX_WIDTH), p.reshape(n, MIX_WIDTH), h,
                                     pool_w[i].astype(BF16), pool_scale[i][None, :],
                                     w_out_even[i].astype(BF16), ffn_gain, w_router, seq=s, tm=512)
        else:
            u, vn, q, k, v = _in_odd(h, mix_norm_odd[i][None, :], w_in_odd[i].astype(BF16),
                                     sgu_v_norm[i][None, :], tm=512)
            to3 = lambda t: t.reshape(b, s, MIX_WIDTH)
            yd = _sb_attention(to3(q), to3(k), to3(v))
            bias = jnp.broadcast_to(sgu_b[i][:, :, None], (N_GROUPS, SGU_BLOCK, LANES))
            h, xn, route = _out_odd(u, vn, yd.reshape(n, MIX_WIDTH), h, sgu_w[i].astype(BF16), bias,
                                    w_out_odd[i].astype(BF16), ffn_gain, w_router, tm=512)
        h = _moe_layer(h, xn, route, w_gate, w_up, w_down, layer)
    return h.reshape(b, s, d)
```

```python
import functools
import math

import jax
import jax.numpy as jnp
from jax import lax
from jax.experimental import pallas as pl
from jax.experimental.pallas import tpu as pltpu

F32 = jnp.float32
BF16 = jnp.bfloat16

D_MODEL = 1024
CHUNK = 64
EPS = 1e-6
HEAD_DIM = 64
MIX_WIDTH = 512
LANES = 128
ATT_LEFT = 8 * CHUNK
ATT_MAX_REL = 128
POOL_WINDOWS = (2, 4, 8, 16)
POOL_HALO = 16
SGU_BLOCK = 128
N_GROUPS = 4
N_EXP_PER_GROUP = 8
N_EXPERTS = N_GROUPS * N_EXP_PER_GROUP
EXPERT_FF = 256
ROUTER_LANES = 128
ROUTER_ROWS = 40
NEG_BIG = -1e30
VMEM_LIMIT = 56 * 1024 * 1024

NT_DIMS = (((1,), (1,)), ((), ()))


def _cparams(sem):
    return pltpu.CompilerParams(dimension_semantics=sem, vmem_limit_bytes=VMEM_LIMIT)


def _rms(x, gain):
    return x * lax.rsqrt(jnp.mean(x * x, axis=-1, keepdims=True) + EPS) * gain


def _split_dot(x, m):
    hi = x.astype(BF16)
    lo = (x - hi.astype(F32)).astype(BF16)
    return (jnp.dot(hi, m, preferred_element_type=F32)
            + jnp.dot(lo, m, preferred_element_type=F32))


def _head_rms(t, gain):
    n = t.shape[-1]
    r = lax.broadcasted_iota(jnp.int32, (n, n), 0) // HEAD_DIM
    c = lax.broadcasted_iota(jnp.int32, (n, n), 1) // HEAD_DIM
    bd = jnp.where(r == c, 1.0, 0.0).astype(BF16)
    ms = _split_dot(t * t, bd) * (1.0 / HEAD_DIM)
    return t * lax.rsqrt(ms + EPS) * gain


def _in_even_body(h_ref, g_ref, w_ref, qg_ref, kg_ref, q_ref, k_ref, v_ref, p_ref):
    j = pl.program_id(1)

    @pl.when(j == 0)
    def _():
        k_ref[...] = jnp.zeros_like(k_ref)
        v_ref[...] = jnp.zeros_like(v_ref)

    @pl.when(j > 0)
    def _():
        xn = _rms(h_ref[0], g_ref[...])
        proj = jnp.dot(xn.astype(BF16), w_ref[...], preferred_element_type=F32)
        w = MIX_WIDTH
        q_ref[0] = (_head_rms(proj[:, :w], qg_ref[...]) * (1.0 / math.sqrt(HEAD_DIM))).astype(BF16)
        k_ref[0] = _head_rms(proj[:, w:2 * w], kg_ref[...]).astype(BF16)
        v_ref[0] = proj[:, 2 * w:3 * w].astype(BF16)
        p_ref[0] = proj[:, 3 * w:].astype(BF16)


def _in_even(h, gain, w, q_gain, k_gain, tm):
    b, s, d = h.shape
    assert tm == ATT_LEFT and s % tm == 0
    nt = s // tm
    cur = lambda bi, j: (bi, jnp.maximum(j - 1, 0), 0)
    const = lambda bi, j: (0, 0)
    out_sds = lambda rows: jax.ShapeDtypeStruct((b, rows, MIX_WIDTH), BF16)
    return pl.pallas_call(
        _in_even_body,
        grid=(b, nt + 1),
        in_specs=[
            pl.BlockSpec((1, tm, d), cur),
            pl.BlockSpec((1, d), const),
            pl.BlockSpec(w.shape, const),
            pl.BlockSpec((1, MIX_WIDTH), const),
            pl.BlockSpec((1, MIX_WIDTH), const),
        ],
        out_specs=[
            pl.BlockSpec((1, tm, MIX_WIDTH), cur),
            pl.BlockSpec((1, tm, MIX_WIDTH), lambda bi, j: (bi, j, 0)),
            pl.BlockSpec((1, tm, MIX_WIDTH), lambda bi, j: (bi, j, 0)),
            pl.BlockSpec((1, tm, MIX_WIDTH), cur),
        ],
        out_shape=[out_sds(s), out_sds(s + ATT_LEFT), out_sds(s + ATT_LEFT), out_sds(s)],
        compiler_params=_cparams(("arbitrary", "arbitrary")),
        name="in_even",
    )(h, gain, w, q_gain, k_gain)


BAND_TQ = 2 * CHUNK
BAND_TK = BAND_TQ + ATT_LEFT


def _band_body(q_ref, k_ref, v_ref, bias_ref, o_ref):
    i = pl.program_id(1)
    start = pl.multiple_of(i * BAND_TQ, BAND_TQ)
    lane = lax.broadcasted_iota(jnp.int32, (BAND_TQ, LANES), 1)
    col = lax.broadcasted_iota(jnp.int32, (BAND_TQ, BAND_TK), 1)
    is_pad = (col + start) < ATT_LEFT
    for hp in range(MIX_WIDTH // LANES):
        ls = slice(hp * LANES, (hp + 1) * LANES)
        q = q_ref[0, :, ls]
        kb = k_ref[0, pl.ds(start, BAND_TK), ls]
        vb = v_ref[0, pl.ds(start, BAND_TK), ls]
        outs = []
        for a in range(2):
            qa = jnp.where((lane < HEAD_DIM) == (a == 0), q, jnp.zeros_like(q))
            s = lax.dot_general(qa, kb, NT_DIMS, preferred_element_type=F32)
            s = jnp.where(is_pad, NEG_BIG, s + bias_ref[2 * hp + a])
            m = jnp.max(s, axis=-1, keepdims=True)
            p = jnp.exp(s - m)
            l = jnp.sum(p, axis=-1, keepdims=True)
            o = jnp.dot(p.astype(BF16), vb, preferred_element_type=F32)
            outs.append(o / l)
        o_ref[0, :, ls] = jnp.where(lane < HEAD_DIM, outs[0], outs[1]).astype(BF16)


def _band_bias(rel_bias):
    heads = rel_bias.shape[0]
    r = jnp.arange(BAND_TQ)[:, None]
    j = jnp.arange(BAND_TK)[None, :]
    jb = j - CHUNK * (r // CHUNK)
    in_band = (jb >= 0) & (jb < ATT_LEFT + CHUNK)
    period = BAND_TK + BAND_TQ
    far = jnp.broadcast_to(rel_bias[:, 2 * ATT_MAX_REL:], (heads, ATT_LEFT - ATT_MAX_REL + 1))
    near = rel_bias[:, 2 * ATT_MAX_REL - 1:0:-1]
    wrap = jnp.broadcast_to(rel_bias[:, 2 * ATT_MAX_REL:], (heads, period - BAND_TK))
    g = jnp.concatenate([far, near, wrap], axis=1).astype(F32)
    assert g.shape[1] == period
    toep = jnp.tile(g, (1, BAND_TQ))[:, :BAND_TQ * (period - 1)].reshape(heads, BAND_TQ, period - 1)
    return jnp.where(in_band[None], toep[:, :, :BAND_TK], NEG_BIG)


def _band_attention(q, k_pad, v_pad, bias):
    b, s, w = q.shape
    sp = k_pad.shape[1]
    return pl.pallas_call(
        _band_body,
        grid=(b, s // BAND_TQ),
        in_specs=[
            pl.BlockSpec((1, BAND_TQ, w), lambda bi, i: (bi, i, 0)),
            pl.BlockSpec((1, sp, w), lambda bi, i: (bi, 0, 0)),
            pl.BlockSpec((1, sp, w), lambda bi, i: (bi, 0, 0)),
            pl.BlockSpec(bias.shape, lambda bi, i: (0, 0, 0)),
        ],
        out_specs=pl.BlockSpec((1, BAND_TQ, w), lambda bi, i: (bi, i, 0)),
        out_shape=jax.ShapeDtypeStruct((b, s, w), BF16),
        compiler_params=_cparams(("arbitrary", "arbitrary")),
        name="band_attention",
    )(q, k_pad, v_pad, bias)


def _route_tokens(h, gain, w_router):
    xn = _rms(h, gain)
    x_hi = xn.astype(BF16)
    x_lo = (xn - x_hi.astype(F32)).astype(BF16)
    w_hi = w_router.astype(BF16)
    w_lo = (w_router - w_hi.astype(F32)).astype(BF16)
    logits = (jnp.dot(x_hi, w_hi, preferred_element_type=F32)
              + jnp.dot(x_lo, w_hi, preferred_element_type=F32)
              + jnp.dot(x_hi, w_lo, preferred_element_type=F32))
    lt = logits.T[:ROUTER_ROWS]
    sub = lax.broadcasted_iota(jnp.int32, lt.shape, 0).astype(F32)
    ninf = -jnp.inf

    def top(vals):
        m = jnp.max(vals, axis=0, keepdims=True)
        idx = jnp.min(jnp.where(vals == m, sub, float(ROUTER_LANES)), axis=0, keepdims=True)
        return m, idx

    is_group = sub < N_GROUPS
    g_max, g_sel = top(jnp.where(is_group, lt, ninf))
    g_den = jnp.sum(jnp.where(is_group, jnp.exp(lt - g_max), 0.0), axis=0, keepdims=True)
    g_weight = 1.0 / g_den
    lo = N_GROUPS + N_EXP_PER_GROUP * g_sel
    e_logits = jnp.where((sub >= lo) & (sub < lo + N_EXP_PER_GROUP), lt, ninf)
    e1, i1 = top(e_logits)
    e2, i2 = top(jnp.where(sub == i1, ninf, e_logits))
    t = jnp.exp(e2 - e1)
    w1 = g_weight / (1.0 + t)
    w2 = g_weight * t / (1.0 + t)
    rows = lax.broadcasted_iota(jnp.int32, (ROUTER_LANES, lt.shape[1]), 0)
    route_t = jnp.where(rows == 0, i1 - N_GROUPS, jnp.where(rows == 1, i2 - N_GROUPS, 0.0))
    route_t = jnp.where(rows == 2, w1, jnp.where(rows == 3, w2, route_t))
    return xn, route_t.T


def _router_weights(w_rg, w_re):
    pad = jnp.zeros((w_rg.shape[0], ROUTER_LANES - N_GROUPS - N_EXPERTS), F32)
    return jnp.concatenate([w_rg, w_re, pad], axis=1)


def _out_even_body(tiles_per_seq, ya_ref, p_ref, halo_ref, h_ref, pw_ref, ps_ref, wo_ref, fg_ref, wr_ref,
                   o_ref, xn_ref, route_ref, p_scr, yb_scr):
    tm = p_ref.shape[0]
    it = pl.program_id(0) % tiles_per_seq
    halo = halo_ref[...].astype(F32)
    p_scr[0:POOL_HALO, :] = jnp.where(it == 0, jnp.zeros_like(halo), halo)
    p_scr[POOL_HALO:, :] = p_ref[...].astype(F32)
    t = it * tm + lax.broadcasted_iota(jnp.int32, (tm, 1), 0)
    for g, win in enumerate(POOL_WINDOWS):
        ls = slice(g * LANES, (g + 1) * LANES)
        cur = p_scr[POOL_HALO:POOL_HALO + tm, ls]
        acc = cur
        for dlt in range(1, win):
            acc = acc + p_scr[POOL_HALO - dlt:POOL_HALO - dlt + tm, ls]
        cnt = jnp.minimum(t + 1, win).astype(F32)
        mixed = acc / cnt - cur
        yb = jnp.dot(mixed.astype(BF16), pw_ref[g], preferred_element_type=F32) * ps_ref[:, ls]
        yb_scr[:, ls] = yb.astype(BF16)
    w = MIX_WIDTH
    h_new = (h_ref[...]
             + jnp.dot(ya_ref[...], wo_ref[0:w, :], preferred_element_type=F32)
             + jnp.dot(yb_scr[...], wo_ref[w:, :], preferred_element_type=F32))
    o_ref[...] = h_new
    xn_ref[...], route_ref[...] = _route_tokens(h_new, fg_ref[...], wr_ref[...])


def _out_even(ya, p, h, pool_w, pool_scale, w_out, ffn_gain, w_router, seq, tm):
    n, d = h.shape
    w = MIX_WIDTH
    row = lambda i: (i, 0)
    const2 = lambda i: (0, 0)
    halo_blocks = tm // POOL_HALO
    return pl.pallas_call(
        functools.partial(_out_even_body, seq // tm),
        grid=(n // tm,),
        in_specs=[
            pl.BlockSpec((tm, w), row),
            pl.BlockSpec((tm, w), row),
            pl.BlockSpec((POOL_HALO, w), lambda i: (jnp.maximum(i * halo_blocks - 1, 0), 0)),
            pl.BlockSpec((tm, d), row),
            pl.BlockSpec(pool_w.shape, lambda i: (0, 0, 0)),
            pl.BlockSpec((1, w), const2),
            pl.BlockSpec(w_out.shape, const2),
            pl.BlockSpec((1, d), const2),
            pl.BlockSpec(w_router.shape, const2),
        ],
        out_specs=[pl.BlockSpec((tm, d), row), pl.BlockSpec((tm, d), row),
                   pl.BlockSpec((tm, ROUTER_LANES), row)],
        out_shape=[jax.ShapeDtypeStruct((n, d), F32), jax.ShapeDtypeStruct((n, d), F32),
                   jax.ShapeDtypeStruct((n, ROUTER_LANES), F32)],
        scratch_shapes=[pltpu.VMEM((tm + POOL_HALO, w), F32), pltpu.VMEM((tm, w), BF16)],
        compiler_params=_cparams(("arbitrary",)),
        name="out_even",
    )(ya, p, p, h, pool_w, pool_scale, w_out, ffn_gain, w_router)


MOE_TM = 256
MOE_TILE_LANES = 256
SUBLANES = 8


def _moe_tiles(n):
    return (2 * n) // MOE_TM + N_EXPERTS


def _exact_dot_nt(ones, x):
    out = None
    for _ in range(3):
        part = x.astype(BF16)
        x = x - part.astype(F32)
        term = lax.dot_general(ones, part, NT_DIMS, preferred_element_type=F32)
        out = term if out is None else out + term
    return out


def _slots_body(n_tiles, route_ref, pos1_ref, pos2_ref, tile_ref, run_scr, start_scr):
    phase = pl.program_id(0)
    i = pl.program_id(1)
    tm = route_ref.shape[0]
    route = route_ref[...]
    lane = lax.broadcasted_iota(jnp.int32, (tm, ROUTER_LANES), 1).astype(F32)
    pick1 = jnp.where(lane == route[:, 0:1], 1.0, 0.0)
    pick2 = jnp.where(lane == route[:, 1:2], 1.0, 0.0)
    occ = (pick1 + pick2).astype(BF16)
    ones_rows = jnp.ones((SUBLANES, tm), BF16)
    ones_lanes = jnp.ones((SUBLANES, ROUTER_LANES), BF16)

    @pl.when(i == 0)
    def _():
        run_scr[...] = jnp.zeros_like(run_scr)

    @pl.when(phase == 0)
    def _():
        run_scr[...] += jnp.dot(ones_rows, occ, preferred_element_type=F32)

        @pl.when(i == pl.num_programs(1) - 1)
        def _():
            padded = jnp.floor((run_scr[...] + (MOE_TM - 1)) * (1.0 / MOE_TM)) * MOE_TM
            r = lax.broadcasted_iota(jnp.int32, (ROUTER_LANES, ROUTER_LANES), 0)
            c = lax.broadcasted_iota(jnp.int32, (ROUTER_LANES, ROUTER_LANES), 1)
            before = jnp.where(r < c, 1.0, 0.0).astype(BF16)
            hi = padded.astype(BF16)
            mid = (padded - hi.astype(F32)).astype(BF16)
            low = (padded - hi.astype(F32) - mid.astype(F32)).astype(BF16)
            start = (jnp.dot(hi, before, preferred_element_type=F32)
                     + jnp.dot(mid, before, preferred_element_type=F32)
                     + jnp.dot(low, before, preferred_element_type=F32))
            start_scr[...] = start
            seg_end = start[0:1, :] + padded[0:1, :]
            tile_lo = (lax.broadcasted_iota(jnp.int32, (MOE_TILE_LANES, ROUTER_LANES), 0) * MOE_TM).astype(F32)
            e_lane = lax.broadcasted_iota(jnp.int32, (MOE_TILE_LANES, ROUTER_LANES), 1)
            ended = jnp.where((seg_end <= tile_lo) & (e_lane < N_EXPERTS), 1.0, 0.0).astype(BF16)
            tile_ref[...] = lax.dot_general(ones_lanes, ended, NT_DIMS, preferred_element_type=F32)

    @pl.when(phase == 1)
    def _():
        r = lax.broadcasted_iota(jnp.int32, (tm, tm), 0)
        c = lax.broadcasted_iota(jnp.int32, (tm, tm), 1)
        earlier = jnp.where(c < r, 1.0, 0.0).astype(BF16)
        base = (jnp.dot(earlier, occ, preferred_element_type=F32)
                + run_scr[0:1, :] + start_scr[0:1, :])
        pos1_ref[...] = _exact_dot_nt(ones_lanes, pick1 * base)
        pos2_ref[...] = _exact_dot_nt(ones_lanes, pick2 * base)
        run_scr[...] += jnp.dot(ones_rows, occ, preferred_element_type=F32)


def _slots(route, tm):
    n = route.shape[0]
    n_tiles = _moe_tiles(n)
    assert n_tiles <= MOE_TILE_LANES and 2 * n + N_EXPERTS * MOE_TM < 2 ** 24
    row_out = pl.BlockSpec((SUBLANES, tm), lambda ph, i: (0, i * ph))
    sds = jax.ShapeDtypeStruct((SUBLANES, n), F32)
    return pl.pallas_call(
        functools.partial(_slots_body, n_tiles),
        grid=(2, n // tm),
        in_specs=[pl.BlockSpec((tm, ROUTER_LANES), lambda ph, i: (i, 0))],
        out_specs=[row_out, row_out, pl.BlockSpec((SUBLANES, MOE_TILE_LANES), lambda ph, i: (0, 0))],
        out_shape=[sds, sds, jax.ShapeDtypeStruct((SUBLANES, MOE_TILE_LANES), F32)],
        scratch_shapes=[pltpu.VMEM((SUBLANES, ROUTER_LANES), F32), pltpu.VMEM((SUBLANES, ROUTER_LANES), F32)],
        compiler_params=_cparams(("arbitrary", "arbitrary")),
        name="moe_slots",
    )(route)


def _dispatch_body(n_tiles, pos_ref, tile_ref, xn_ref, xs_hbm, zero_scr, zero_sem, row_sem):
    tm = xn_ref.shape[0]
    n = pl.num_programs(0) * tm
    base = pl.program_id(0) * tm

    @pl.when(pl.program_id(0) == 0)
    def _():
        zero_scr[...] = jnp.zeros_like(zero_scr)

        def fill_copy(t):
            return pltpu.make_async_copy(zero_scr, xs_hbm.at[pl.ds(t * MOE_TM, MOE_TM), :], zero_sem)

        def has_padding(t):
            return (tile_ref[t] >= N_EXPERTS) | (tile_ref[t] != tile_ref[t + 1])

        @pl.loop(0, n_tiles)
        def _(t):
            @pl.when(has_padding(t))
            def _():
                fill_copy(t).start()

        @pl.loop(0, n_tiles)
        def _(t):
            @pl.when(has_padding(t))
            def _():
                fill_copy(t).wait()

    def issue(j, carry):
        src = xn_ref.at[pl.ds(j, 1), :]
        pltpu.make_async_copy(src, xs_hbm.at[pl.ds(pos_ref[base + j], 1), :], row_sem).start()
        pltpu.make_async_copy(src, xs_hbm.at[pl.ds(pos_ref[n + base + j], 1), :], row_sem).start()
        return carry

    lax.fori_loop(0, tm, issue, 0, unroll=8)
    for _ in range(2):
        pltpu.make_async_copy(xn_ref, xs_hbm.at[pl.ds(0, tm), :], row_sem).wait()


def _dispatch(pos, tile_map, xn, tm):
    n, d = xn.shape
    n_tiles = _moe_tiles(n)
    return pl.pallas_call(
        functools.partial(_dispatch_body, n_tiles),
        grid_spec=pltpu.PrefetchScalarGridSpec(
            num_scalar_prefetch=2,
            grid=(n // tm,),
            in_specs=[pl.BlockSpec((tm, d), lambda i, p, t: (i, 0))],
            out_specs=pl.BlockSpec(memory_space=pl.ANY),
            scratch_shapes=[pltpu.VMEM((MOE_TM, d), F32), pltpu.SemaphoreType.DMA(()),
                            pltpu.SemaphoreType.DMA(())],
        ),
        out_shape=jax.ShapeDtypeStruct((n_tiles * MOE_TM, d), F32),
        compiler_params=_cparams(("arbitrary",)),
        name="moe_dispatch",
    )(pos, tile_map, xn)


def _experts_body(tile_ref, xs_ref, wg_ref, wu_ref, wd_ref, ys_ref):
    used = tile_ref[pl.program_id(0)] < N_EXPERTS

    @pl.when(used)
    def _():
        x = xs_ref[...].astype(BF16)
        gate = jnp.dot(x, wg_ref[0].astype(BF16), preferred_element_type=F32)
        up = jnp.dot(x, wu_ref[0].astype(BF16), preferred_element_type=F32)
        hid = gate * jax.nn.sigmoid(gate) * up
        ys_ref[...] = jnp.dot(hid.astype(BF16), wd_ref[0].astype(BF16), preferred_element_type=F32)

    @pl.when(jnp.logical_not(used))
    def _():
        ys_ref[...] = jnp.zeros_like(ys_ref)


def _experts(tile_map, xs, w_gate, w_up, w_down, layer):
    rows, d = xs.shape
    f = EXPERT_FF
    n_tiles = rows // MOE_TM
    x_map = lambda i, tm_ref: (jnp.where(tm_ref[i] < N_EXPERTS, i, 0), 0)
    y_map = lambda i, tm_ref: (i, 0)
    w_map = lambda i, tm_ref: (layer * N_EXPERTS + jnp.minimum(tm_ref[i], N_EXPERTS - 1), 0, 0)
    return pl.pallas_call(
        _experts_body,
        grid_spec=pltpu.PrefetchScalarGridSpec(
            num_scalar_prefetch=1,
            grid=(n_tiles,),
            in_specs=[pl.BlockSpec((MOE_TM, d), x_map), pl.BlockSpec((1, d, f), w_map),
                      pl.BlockSpec((1, d, f), w_map), pl.BlockSpec((1, f, d), w_map)],
            out_specs=pl.BlockSpec((MOE_TM, d), y_map),
        ),
        out_shape=jax.ShapeDtypeStruct((rows, d), F32),
        compiler_params=_cparams(("arbitrary",)),
        name="moe_experts",
    )(tile_map, xs, w_gate, w_up, w_down)


def _combine_body(pos_ref, h_ref, route_ref, ys_hbm, o_ref, buf, sems):
    tm = h_ref.shape[0]
    steps = pl.num_programs(0)
    n = steps * tm
    i = pl.program_id(0)

    def start_gather(step, slot):
        base = step * tm

        def issue(j, carry):
            for pick in range(2):
                pltpu.make_async_copy(ys_hbm.at[pl.ds(pos_ref[pick * n + base + j], 1), :],
                                      buf.at[slot, pick, pl.ds(j, 1), :], sems.at[slot]).start()
            return carry

        lax.fori_loop(0, tm, issue, 0, unroll=8)

    @pl.when(i == 0)
    def _():
        start_gather(0, 0)

    @pl.when(i + 1 < steps)
    def _():
        start_gather(i + 1, (i + 1) % 2)

    slot = i % 2
    pltpu.make_async_copy(buf.at[slot], buf.at[slot], sems.at[slot]).wait()
    route = route_ref[...]
    o_ref[...] = h_ref[...] + route[:, 2:3] * buf[slot, 0] + route[:, 3:4] * buf[slot, 1]


def _combine(pos, h, route, ys, tm):
    n, d = h.shape
    row = lambda i, p: (i, 0)
    return pl.pallas_call(
        _combine_body,
        grid_spec=pltpu.PrefetchScalarGridSpec(
            num_scalar_prefetch=1,
            grid=(n // tm,),
            in_specs=[pl.BlockSpec((tm, d), row), pl.BlockSpec((tm, ROUTER_LANES), row),
                      pl.BlockSpec(memory_space=pl.ANY)],
            out_specs=pl.BlockSpec((tm, d), row),
            scratch_shapes=[pltpu.VMEM((2, 2, tm, d), F32), pltpu.SemaphoreType.DMA((2,))],
        ),
        out_shape=jax.ShapeDtypeStruct((n, d), F32),
        compiler_params=_cparams(("arbitrary",)),
        name="moe_combine",
    )(pos, h, route, ys)


def _gelu(x):
    return 0.5 * x * (1.0 + lax.erf(x * (1.0 / math.sqrt(2.0))))


def _in_odd_body(h_ref, g_ref, w_ref, vg_ref, u_ref, vn_ref, q_ref, k_ref, v_ref):
    xn = _rms(h_ref[...], g_ref[...])
    proj = jnp.dot(xn.astype(BF16), w_ref[...], preferred_element_type=F32)
    w = MIX_WIDTH
    u_ref[...] = _gelu(proj[:, :w]).astype(BF16)
    vn_ref[...] = _rms(_gelu(proj[:, w:2 * w]), vg_ref[...]).astype(BF16)
    q_ref[...] = (proj[:, 2 * w:3 * w] * (1.0 / math.sqrt(HEAD_DIM))).astype(BF16)
    k_ref[...] = proj[:, 3 * w:4 * w].astype(BF16)
    v_ref[...] = proj[:, 4 * w:].astype(BF16)


def _in_odd(h, gain, w, v_gain, tm):
    n, d = h.shape
    row = lambda i: (i, 0)
    const2 = lambda i: (0, 0)
    sds = jax.ShapeDtypeStruct((n, MIX_WIDTH), BF16)
    return pl.pallas_call(
        _in_odd_body,
        grid=(n // tm,),
        in_specs=[pl.BlockSpec((tm, d), row), pl.BlockSpec((1, d), const2),
                  pl.BlockSpec(w.shape, const2), pl.BlockSpec((1, MIX_WIDTH), const2)],
        out_specs=[pl.BlockSpec((tm, MIX_WIDTH), row)] * 5,
        out_shape=[sds] * 5,
        compiler_params=_cparams(("arbitrary",)),
        name="in_odd",
    )(h, gain, w, v_gain)


SB_T = 128
SB_NB = 3
SB_TK = SB_NB * SB_T
SB_QBLK = 512
SB_GROUP = 2
SB_UNDERFLOW = -104.0


def _sb_body(q_ref, k_ref, v_ref, o_ref, acc_scr, run_scr):
    qi = pl.program_id(2)
    lane = lax.broadcasted_iota(jnp.int32, (SB_T, LANES), 1)
    row = lax.broadcasted_iota(jnp.int32, (2 * SB_T, 1), 0) % SB_T
    col = lax.broadcasted_iota(jnp.int32, (2 * SB_T, SB_TK), 1)
    rr = lax.broadcasted_iota(jnp.int32, (SB_T, 2 * SB_T), 0)
    cc = lax.broadcasted_iota(jnp.int32, (SB_T, 2 * SB_T), 1)
    suffix = jnp.where((cc >= SB_T) | (rr > cc), 1.0, 0.0).astype(BF16)

    def subtile_group(grp, _):
        q_los = [pl.multiple_of((grp * SB_GROUP + s) * SB_T, SB_T) for s in range(SB_GROUP)]
        q_starts = [qi * SB_QBLK + q_lo for q_lo in q_los]
        q_heads = []
        for q_lo in q_los:
            q = q_ref[0, pl.ds(q_lo, SB_T), :]
            zero = jnp.zeros_like(q)
            q_heads.append(jnp.concatenate(
                [jnp.where(lane < HEAD_DIM, q, zero), jnp.where(lane < HEAD_DIM, zero, q)], axis=0))
        acc_scr[...] = jnp.zeros_like(acc_scr)
        run_scr[...] = jnp.zeros_like(run_scr)

        def cond(carry):
            his, dones = carry
            active = [(hi > 0) & (done == 0) for hi, done in zip(his, dones)]
            return functools.reduce(jnp.logical_or, active)

        def body(carry):
            his, _ = carry
            new_his, new_dones = [], []
            for s in range(SB_GROUP):
                hi = his[s]
                ks = pl.multiple_of(jnp.maximum(hi - SB_TK, 0), SB_T)
                kt = k_ref[0, pl.ds(ks, SB_TK), :]
                vt = v_ref[0, pl.ds(ks, SB_TK), :]
                valid = col < (jnp.minimum(row + q_starts[s], hi) - ks)
                z = lax.dot_general(q_heads[s], kt, NT_DIMS, preferred_element_type=F32)
                sp = jnp.log(1.0 + jnp.exp(-jnp.abs(z)))
                log_beta = jnp.minimum(z, 0.0) - sp
                log_rest = jnp.where(valid, log_beta - z, 0.0)
                run = run_scr[s]
                pieces = [None] * SB_NB
                for blk in reversed(range(SB_NB)):
                    bs = slice(blk * SB_T, (blk + 1) * SB_T)
                    tt = _split_dot(log_rest[:, bs], suffix)
                    pieces[blk] = jnp.exp(log_beta[:, bs] + tt[:, :SB_T] + run)
                    run = run + tt[:, SB_T:]
                att = jnp.where(valid, jnp.concatenate(pieces, axis=1), 0.0)
                acc_scr[s] += jnp.dot(att.astype(BF16), vt, preferred_element_type=F32)
                run_scr[s] = run
                new_his.append(ks)
                new_dones.append((jnp.max(run) <= SB_UNDERFLOW).astype(jnp.int32))
            return tuple(new_his), tuple(new_dones)

        lax.while_loop(cond, body, (tuple(qs + SB_T for qs in q_starts),
                                    tuple(jnp.int32(0) for _ in range(SB_GROUP))))
        for s in range(SB_GROUP):
            o_ref[0, pl.ds(q_los[s], SB_T), :] = jnp.where(
                lane < HEAD_DIM, acc_scr[s, :SB_T], acc_scr[s, SB_T:]).astype(BF16)
        return 0

    lax.fori_loop(0, SB_QBLK // (SB_T * SB_GROUP), subtile_group, 0)


def _sb_attention(q, k, v):
    b, s, w = q.shape
    assert s % SB_QBLK == 0 and s >= SB_TK
    qspec = pl.BlockSpec((1, SB_QBLK, LANES), lambda bi, hp, i: (bi, i, hp))
    kvspec = pl.BlockSpec((1, s, LANES), lambda bi, hp, i: (bi, 0, hp))
    return pl.pallas_call(
        _sb_body,
        grid=(b, w // LANES, s // SB_QBLK),
        in_specs=[qspec, kvspec, kvspec],
        out_specs=qspec,
        out_shape=jax.ShapeDtypeStruct((b, s, w), BF16),
        scratch_shapes=[pltpu.VMEM((SB_GROUP, 2 * SB_T, LANES), F32),
                        pltpu.VMEM((SB_GROUP, 2 * SB_T, LANES), F32)],
        compiler_params=_cparams(("arbitrary", "arbitrary", "arbitrary")),
        name="sb_attention",
    )(q, k, v)


def _out_odd_body(u_ref, vn_ref, yd_ref, h_ref, ws_ref, bs_ref, wo_ref, fg_ref, wr_ref,
                  o_ref, xn_ref, route_ref, yc_scr):
    tm = u_ref.shape[0]
    r = lax.broadcasted_iota(jnp.int32, (SGU_BLOCK, SGU_BLOCK), 0)
    c = lax.broadcasted_iota(jnp.int32, (SGU_BLOCK, SGU_BLOCK), 1)
    for g in range(MIX_WIDTH // LANES):
        ls = slice(g * LANES, (g + 1) * LANES)
        ws = jnp.where(c <= r, ws_ref[g], jnp.zeros_like(ws_ref[g]))
        for blk in range(tm // SGU_BLOCK):
            rs = slice(blk * SGU_BLOCK, (blk + 1) * SGU_BLOCK)
            mixed = jnp.dot(ws, vn_ref[rs, ls], preferred_element_type=F32) + bs_ref[g]
            yc_scr[rs, ls] = (u_ref[rs, ls].astype(F32) * mixed).astype(BF16)
    w = MIX_WIDTH
    h_new = (h_ref[...]
             + jnp.dot(yc_scr[...], wo_ref[0:w, :], preferred_element_type=F32)
             + jnp.dot(yd_ref[...], wo_ref[w:, :], preferred_element_type=F32))
    o_ref[...] = h_new
    xn_ref[...], route_ref[...] = _route_tokens(h_new, fg_ref[...], wr_ref[...])


def _out_odd(u, vn, yd, h, sgu_w, sgu_b, w_out, ffn_gain, w_router, tm):
    n, d = h.shape
    w = MIX_WIDTH
    row = lambda i: (i, 0)
    const2 = lambda i: (0, 0)
    const3 = lambda i: (0, 0, 0)
    return pl.pallas_call(
        _out_odd_body,
        grid=(n // tm,),
        in_specs=[pl.BlockSpec((tm, w), row), pl.BlockSpec((tm, w), row), pl.BlockSpec((tm, w), row),
                  pl.BlockSpec((tm, d), row), pl.BlockSpec(sgu_w.shape, const3),
                  pl.BlockSpec(sgu_b.shape, const3), pl.BlockSpec(w_out.shape, const2),
                  pl.BlockSpec((1, d), const2), pl.BlockSpec(w_router.shape, const2)],
        out_specs=[pl.BlockSpec((tm, d), row), pl.BlockSpec((tm, d), row),
                   pl.BlockSpec((tm, ROUTER_LANES), row)],
        out_shape=[jax.ShapeDtypeStruct((n, d), F32), jax.ShapeDtypeStruct((n, d), F32),
                   jax.ShapeDtypeStruct((n, ROUTER_LANES), F32)],
        scratch_shapes=[pltpu.VMEM((tm, w), BF16)],
        compiler_params=_cparams(("arbitrary",)),
        name="out_odd",
    )(u, vn, yd, h, sgu_w, sgu_b, w_out, ffn_gain, w_router)


def _moe_layer(h, xn, route, w_gate, w_up, w_down, layer):
    pos1, pos2, tile_map = _slots(route, tm=512)
    pos = jnp.concatenate([pos1[0], pos2[0]]).astype(jnp.int32)
    tile_map = tile_map[0].astype(jnp.int32)
    xs = _dispatch(pos, tile_map, xn, tm=256)
    ys = _experts(tile_map, xs, w_gate, w_up, w_down, layer)
    return _combine(pos, h, route, ys, tm=256)


def kernel(x, mix_norm_even, w_in_even, att_q_norm, att_k_norm, att_rel_bias, pool_w, pool_scale,
           w_out_even, mix_norm_odd, w_in_odd, sgu_v_norm, sgu_w, sgu_b, w_out_odd, ffn_norm,
           w_router_group, w_router_expert, w_exp_gate, w_exp_up, w_exp_down):
    b, s, d = x.shape
    n = b * s
    depth = ffn_norm.shape[0]
    heads = MIX_WIDTH // HEAD_DIM
    h = x.reshape(n, d)
    w_gate = w_exp_gate.reshape(depth * N_EXPERTS, d, EXPERT_FF)
    w_up = w_exp_up.reshape(depth * N_EXPERTS, d, EXPERT_FF)
    w_down = w_exp_down.reshape(depth * N_EXPERTS, EXPERT_FF, d)
    for layer in range(depth):
        i = layer // 2
        ffn_gain = ffn_norm[layer][None, :]
        w_router = _router_weights(w_router_group[layer], w_router_expert[layer])
        if layer % 2 == 0:
            q, k_pad, v_pad, p = _in_even(
                h.reshape(b, s, d), mix_norm_even[i][None, :], w_in_even[i].astype(BF16),
                jnp.tile(att_q_norm[i], heads)[None, :], jnp.tile(att_k_norm[i], heads)[None, :],
                tm=ATT_LEFT)
            ya = _band_attention(q, k_pad, v_pad, _band_bias(att_rel_bias[i]))
            h, xn, route = _out_even(ya.reshape(n, MIX_WIDTH), p.reshape(n, MIX_WIDTH), h,
                                     pool_w[i].astype(BF16), pool_scale[i][None, :],
                                     w_out_even[i].astype(BF16), ffn_gain, w_router, seq=s, tm=512)
        else:
            u, vn, q, k, v = _in_odd(h, mix_norm_odd[i][None, :], w_in_odd[i].astype(BF16),
                                     sgu_v_norm[i][None, :], tm=512)
            to3 = lambda t: t.reshape(b, s, MIX_WIDTH)
            yd = _sb_attention(to3(q), to3(k), to3(v))
            bias = jnp.broadcast_to(sgu_b[i][:, :, None], (N_GROUPS, SGU_BLOCK, LANES))
            h, xn, route = _out_odd(u, vn, yd.reshape(n, MIX_WIDTH), h, sgu_w[i].astype(BF16), bias,
                                    w_out_odd[i].astype(BF16), ffn_gain, w_router, tm=512)
        h = _moe_layer(h, xn, route, w_gate, w_up, w_down, layer)
    return h.reshape(b, s, d)
```

```python
import functools
import math

import jax
import jax.numpy as jnp
from jax import lax
from jax.experimental import pallas as pl
from jax.experimental.pallas import tpu as pltpu

F32 = jnp.float32
BF16 = jnp.bfloat16

D_MODEL = 1024
CHUNK = 64
EPS = 1e-6
HEAD_DIM = 64
MIX_WIDTH = 512
LANES = 128
ATT_LEFT = 8 * CHUNK
ATT_MAX_REL = 128
POOL_WINDOWS = (2, 4, 8, 16)
POOL_HALO = 16
SGU_BLOCK = 128
N_GROUPS = 4
N_EXP_PER_GROUP = 8
N_EXPERTS = N_GROUPS * N_EXP_PER_GROUP
EXPERT_FF = 256
ROUTER_LANES = 128
ROUTER_ROWS = 40
NEG_BIG = -1e30
VMEM_LIMIT = 56 * 1024 * 1024

NT_DIMS = (((1,), (1,)), ((), ()))


def _cparams(sem):
    return pltpu.CompilerParams(dimension_semantics=sem, vmem_limit_bytes=VMEM_LIMIT)


def _rms(x, gain):
    return x * lax.rsqrt(jnp.mean(x * x, axis=-1, keepdims=True) + EPS) * gain


def _split_dot(x, m):
    hi = x.astype(BF16)
    lo = (x - hi.astype(F32)).astype(BF16)
    return (jnp.dot(hi, m, preferred_element_type=F32)
            + jnp.dot(lo, m, preferred_element_type=F32))


def _head_rms(t, gain):
    n = t.shape[-1]
    r = lax.broadcasted_iota(jnp.int32, (n, n), 0) // HEAD_DIM
    c = lax.broadcasted_iota(jnp.int32, (n, n), 1) // HEAD_DIM
    bd = jnp.where(r == c, 1.0, 0.0).astype(BF16)
    ms = _split_dot(t * t, bd) * (1.0 / HEAD_DIM)
    return t * lax.rsqrt(ms + EPS) * gain


def _in_even_body(h_ref, g_ref, w_ref, qg_ref, kg_ref, q_ref, k_ref, v_ref, p_ref):
    j = pl.program_id(1)

    @pl.when(j == 0)
    def _():
        k_ref[...] = jnp.zeros_like(k_ref)
        v_ref[...] = jnp.zeros_like(v_ref)

    @pl.when(j > 0)
    def _():
        xn = _rms(h_ref[0], g_ref[...])
        proj = jnp.dot(xn.astype(BF16), w_ref[...], preferred_element_type=F32)
        w = MIX_WIDTH
        q_ref[0] = (_head_rms(proj[:, :w], qg_ref[...]) * (1.0 / math.sqrt(HEAD_DIM))).astype(BF16)
        k_ref[0] = _head_rms(proj[:, w:2 * w], kg_ref[...]).astype(BF16)
        v_ref[0] = proj[:, 2 * w:3 * w].astype(BF16)
        p_ref[0] = proj[:, 3 * w:].astype(BF16)


def _in_even(h, gain, w, q_gain, k_gain, tm):
    b, s, d = h.shape
    assert tm == ATT_LEFT and s % tm == 0
    nt = s // tm
    cur = lambda bi, j: (bi, jnp.maximum(j - 1, 0), 0)
    const = lambda bi, j: (0, 0)
    out_sds = lambda rows: jax.ShapeDtypeStruct((b, rows, MIX_WIDTH), BF16)
    return pl.pallas_call(
        _in_even_body,
        grid=(b, nt + 1),
        in_specs=[
            pl.BlockSpec((1, tm, d), cur),
            pl.BlockSpec((1, d), const),
            pl.BlockSpec(w.shape, const),
            pl.BlockSpec((1, MIX_WIDTH), const),
            pl.BlockSpec((1, MIX_WIDTH), const),
        ],
        out_specs=[
            pl.BlockSpec((1, tm, MIX_WIDTH), cur),
            pl.BlockSpec((1, tm, MIX_WIDTH), lambda bi, j: (bi, j, 0)),
            pl.BlockSpec((1, tm, MIX_WIDTH), lambda bi, j: (bi, j, 0)),
            pl.BlockSpec((1, tm, MIX_WIDTH), cur),
        ],
        out_shape=[out_sds(s), out_sds(s + ATT_LEFT), out_sds(s + ATT_LEFT), out_sds(s)],
        compiler_params=_cparams(("arbitrary", "arbitrary")),
        name="in_even",
    )(h, gain, w, q_gain, k_gain)


BAND_TQ = 2 * CHUNK
BAND_TK = BAND_TQ + ATT_LEFT


def _band_body(q_ref, k_ref, v_ref, bias_ref, o_ref):
    i = pl.program_id(1)
    start = pl.multiple_of(i * BAND_TQ, BAND_TQ)
    lane = lax.broadcasted_iota(jnp.int32, (BAND_TQ, LANES), 1)
    col = lax.broadcasted_iota(jnp.int32, (2 * BAND_TQ, BAND_TK), 1)
    is_pad = (col + start) < ATT_LEFT
    pairs = range(MIX_WIDTH // LANES)
    lanes = [slice(hp * LANES, (hp + 1) * LANES) for hp in pairs]
    scores = []
    for hp in pairs:
        q = q_ref[0, :, lanes[hp]]
        kb = k_ref[0, pl.ds(start, BAND_TK), lanes[hp]]
        zero = jnp.zeros_like(q)
        q2 = jnp.concatenate([jnp.where(lane < HEAD_DIM, q, zero), jnp.where(lane < HEAD_DIM, zero, q)], axis=0)
        scores.append(lax.dot_general(q2, kb, NT_DIMS, preferred_element_type=F32))
    probs, denoms = [], []
    for hp in pairs:
        bias = bias_ref[2 * hp:2 * hp + 2].reshape(2 * BAND_TQ, BAND_TK)
        s = jnp.where(is_pad, NEG_BIG, scores[hp] + bias)
        p = jnp.exp(s - jnp.max(s, axis=-1, keepdims=True))
        denoms.append(jnp.sum(p, axis=-1, keepdims=True))
        probs.append(p.astype(BF16))
    for hp in pairs:
        vb = v_ref[0, pl.ds(start, BAND_TK), lanes[hp]]
        o = jnp.dot(probs[hp], vb, preferred_element_type=F32) / denoms[hp]
        o_ref[0, :, lanes[hp]] = jnp.where(lane < HEAD_DIM, o[:BAND_TQ], o[BAND_TQ:]).astype(BF16)


def _band_bias(rel_bias):
    heads = rel_bias.shape[0]
    r = jnp.arange(BAND_TQ)[:, None]
    j = jnp.arange(BAND_TK)[None, :]
    jb = j - CHUNK * (r // CHUNK)
    in_band = (jb >= 0) & (jb < ATT_LEFT + CHUNK)
    period = BAND_TK + BAND_TQ
    far = jnp.broadcast_to(rel_bias[:, 2 * ATT_MAX_REL:], (heads, ATT_LEFT - ATT_MAX_REL + 1))
    near = rel_bias[:, 2 * ATT_MAX_REL - 1:0:-1]
    wrap = jnp.broadcast_to(rel_bias[:, 2 * ATT_MAX_REL:], (heads, period - BAND_TK))
    g = jnp.concatenate([far, near, wrap], axis=1).astype(F32)
    assert g.shape[1] == period
    toep = jnp.tile(g, (1, BAND_TQ))[:, :BAND_TQ * (period - 1)].reshape(heads, BAND_TQ, period - 1)
    return jnp.where(in_band[None], toep[:, :, :BAND_TK], NEG_BIG)


def _band_attention(q, k_pad, v_pad, bias):
    b, s, w = q.shape
    sp = k_pad.shape[1]
    return pl.pallas_call(
        _band_body,
        grid=(b, s // BAND_TQ),
        in_specs=[
            pl.BlockSpec((1, BAND_TQ, w), lambda bi, i: (bi, i, 0)),
            pl.BlockSpec((1, sp, w), lambda bi, i: (bi, 0, 0)),
            pl.BlockSpec((1, sp, w), lambda bi, i: (bi, 0, 0)),
            pl.BlockSpec(bias.shape, lambda bi, i: (0, 0, 0)),
        ],
        out_specs=pl.BlockSpec((1, BAND_TQ, w), lambda bi, i: (bi, i, 0)),
        out_shape=jax.ShapeDtypeStruct((b, s, w), BF16),
        compiler_params=_cparams(("arbitrary", "arbitrary")),
        name="band_attention",
    )(q, k_pad, v_pad, bias)


def _route_tokens(h, gain, w_router):
    xn = _rms(h, gain)
    x_hi = xn.astype(BF16)
    x_lo = (xn - x_hi.astype(F32)).astype(BF16)
    w_hi = w_router.astype(BF16)
    w_lo = (w_router - w_hi.astype(F32)).astype(BF16)
    logits = (jnp.dot(x_hi, w_hi, preferred_element_type=F32)
              + jnp.dot(x_lo, w_hi, preferred_element_type=F32)
              + jnp.dot(x_hi, w_lo, preferred_element_type=F32))
    lt = logits.T[:ROUTER_ROWS]
    sub = lax.broadcasted_iota(jnp.int32, lt.shape, 0).astype(F32)
    ninf = -jnp.inf

    def top(vals):
        m = jnp.max(vals, axis=0, keepdims=True)
        idx = jnp.min(jnp.where(vals == m, sub, float(ROUTER_LANES)), axis=0, keepdims=True)
        return m, idx

    is_group = sub < N_GROUPS
    g_max, g_sel = top(jnp.where(is_group, lt, ninf))
    g_den = jnp.sum(jnp.where(is_group, jnp.exp(lt - g_max), 0.0), axis=0, keepdims=True)
    g_weight = 1.0 / g_den
    lo = N_GROUPS + N_EXP_PER_GROUP * g_sel
    e_logits = jnp.where((sub >= lo) & (sub < lo + N_EXP_PER_GROUP), lt, ninf)
    e1, i1 = top(e_logits)
    e2, i2 = top(jnp.where(sub == i1, ninf, e_logits))
    t = jnp.exp(e2 - e1)
    w1 = g_weight / (1.0 + t)
    w2 = g_weight * t / (1.0 + t)
    rows = lax.broadcasted_iota(jnp.int32, (ROUTER_LANES, lt.shape[1]), 0)
    route_t = jnp.where(rows == 0, i1 - N_GROUPS, jnp.where(rows == 1, i2 - N_GROUPS, 0.0))
    route_t = jnp.where(rows == 2, w1, jnp.where(rows == 3, w2, route_t))
    return xn, route_t.T


def _router_weights(w_rg, w_re):
    pad = jnp.zeros((w_rg.shape[0], ROUTER_LANES - N_GROUPS - N_EXPERTS), F32)
    return jnp.concatenate([w_rg, w_re, pad], axis=1)


def _out_even_body(tiles_per_seq, ya_ref, p_ref, halo_ref, h_ref, pw_ref, ps_ref, wo_ref, fg_ref, wr_ref,
                   o_ref, xn_ref, route_ref, p_scr, yb_scr):
    tm = p_ref.shape[0]
    it = pl.program_id(0) % tiles_per_seq
    halo = halo_ref[...].astype(F32)
    p_scr[0:POOL_HALO, :] = jnp.where(it == 0, jnp.zeros_like(halo), halo)
    p_scr[POOL_HALO:, :] = p_ref[...].astype(F32)
    t = it * tm + lax.broadcasted_iota(jnp.int32, (tm, 1), 0)
    for g, win in enumerate(POOL_WINDOWS):
        ls = slice(g * LANES, (g + 1) * LANES)
        cur = p_scr[POOL_HALO:POOL_HALO + tm, ls]
        acc = cur
        for dlt in range(1, win):
            acc = acc + p_scr[POOL_HALO - dlt:POOL_HALO - dlt + tm, ls]
        cnt = jnp.minimum(t + 1, win).astype(F32)
        mixed = acc / cnt - cur
        yb = jnp.dot(mixed.astype(BF16), pw_ref[g], preferred_element_type=F32) * ps_ref[:, ls]
        yb_scr[:, ls] = yb.astype(BF16)
    w = MIX_WIDTH
    h_new = (h_ref[...]
             + jnp.dot(ya_ref[...], wo_ref[0:w, :], preferred_element_type=F32)
             + jnp.dot(yb_scr[...], wo_ref[w:, :], preferred_element_type=F32))
    o_ref[...] = h_new
    xn_ref[...], route_ref[...] = _route_tokens(h_new, fg_ref[...], wr_ref[...])


def _out_even(ya, p, h, pool_w, pool_scale, w_out, ffn_gain, w_router, seq, tm):
    n, d = h.shape
    w = MIX_WIDTH
    row = lambda i: (i, 0)
    const2 = lambda i: (0, 0)
    halo_blocks = tm // POOL_HALO
    return pl.pallas_call(
        functools.partial(_out_even_body, seq // tm),
        grid=(n // tm,),
        in_specs=[
            pl.BlockSpec((tm, w), row),
            pl.BlockSpec((tm, w), row),
            pl.BlockSpec((POOL_HALO, w), lambda i: (jnp.maximum(i * halo_blocks - 1, 0), 0)),
            pl.BlockSpec((tm, d), row),
            pl.BlockSpec(pool_w.shape, lambda i: (0, 0, 0)),
            pl.BlockSpec((1, w), const2),
            pl.BlockSpec(w_out.shape, const2),
            pl.BlockSpec((1, d), const2),
            pl.BlockSpec(w_router.shape, const2),
        ],
        out_specs=[pl.BlockSpec((tm, d), row), pl.BlockSpec((tm, d), row),
                   pl.BlockSpec((tm, ROUTER_LANES), row)],
        out_shape=[jax.ShapeDtypeStruct((n, d), F32), jax.ShapeDtypeStruct((n, d), F32),
                   jax.ShapeDtypeStruct((n, ROUTER_LANES), F32)],
        scratch_shapes=[pltpu.VMEM((tm + POOL_HALO, w), F32), pltpu.VMEM((tm, w), BF16)],
        compiler_params=_cparams(("arbitrary",)),
        name="out_even",
    )(ya, p, p, h, pool_w, pool_scale, w_out, ffn_gain, w_router)


MOE_TM = 256
MOE_TILE_LANES = 256
SUBLANES = 8


def _moe_tiles(n):
    return (2 * n) // MOE_TM + N_EXPERTS


def _exact_dot_nt(ones, x):
    out = None
    for _ in range(3):
        part = x.astype(BF16)
        x = x - part.astype(F32)
        term = lax.dot_general(ones, part, NT_DIMS, preferred_element_type=F32)
        out = term if out is None else out + term
    return out


def _slots_body(n_tiles, route_ref, pos1_ref, pos2_ref, tile_ref, run_scr, start_scr):
    phase = pl.program_id(0)
    i = pl.program_id(1)
    tm = route_ref.shape[0]
    route = route_ref[...]
    lane = lax.broadcasted_iota(jnp.int32, (tm, ROUTER_LANES), 1).astype(F32)
    pick1 = jnp.where(lane == route[:, 0:1], 1.0, 0.0)
    pick2 = jnp.where(lane == route[:, 1:2], 1.0, 0.0)
    occ = (pick1 + pick2).astype(BF16)
    ones_rows = jnp.ones((SUBLANES, tm), BF16)
    ones_lanes = jnp.ones((SUBLANES, ROUTER_LANES), BF16)

    @pl.when(i == 0)
    def _():
        run_scr[...] = jnp.zeros_like(run_scr)

    @pl.when(phase == 0)
    def _():
        run_scr[...] += jnp.dot(ones_rows, occ, preferred_element_type=F32)

        @pl.when(i == pl.num_programs(1) - 1)
        def _():
            padded = jnp.floor((run_scr[...] + (MOE_TM - 1)) * (1.0 / MOE_TM)) * MOE_TM
            r = lax.broadcasted_iota(jnp.int32, (ROUTER_LANES, ROUTER_LANES), 0)
            c = lax.broadcasted_iota(jnp.int32, (ROUTER_LANES, ROUTER_LANES), 1)
            before = jnp.where(r < c, 1.0, 0.0).astype(BF16)
            hi = padded.astype(BF16)
            mid = (padded - hi.astype(F32)).astype(BF16)
            low = (padded - hi.astype(F32) - mid.astype(F32)).astype(BF16)
            start = (jnp.dot(hi, before, preferred_element_type=F32)
                     + jnp.dot(mid, before, preferred_element_type=F32)
                     + jnp.dot(low, before, preferred_element_type=F32))
            start_scr[...] = start
            seg_end = start[0:1, :] + padded[0:1, :]
            tile_lo = (lax.broadcasted_iota(jnp.int32, (MOE_TILE_LANES, ROUTER_LANES), 0) * MOE_TM).astype(F32)
            e_lane = lax.broadcasted_iota(jnp.int32, (MOE_TILE_LANES, ROUTER_LANES), 1)
            ended = jnp.where((seg_end <= tile_lo) & (e_lane < N_EXPERTS), 1.0, 0.0).astype(BF16)
            tile_ref[...] = lax.dot_general(ones_lanes, ended, NT_DIMS, preferred_element_type=F32)

    @pl.when(phase == 1)
    def _():
        r = lax.broadcasted_iota(jnp.int32, (tm, tm), 0)
        c = lax.broadcasted_iota(jnp.int32, (tm, tm), 1)
        earlier = jnp.where(c < r, 1.0, 0.0).astype(BF16)
        base = (jnp.dot(earlier, occ, preferred_element_type=F32)
                + run_scr[0:1, :] + start_scr[0:1, :])
        pos1_ref[...] = _exact_dot_nt(ones_lanes, pick1 * base)
        pos2_ref[...] = _exact_dot_nt(ones_lanes, pick2 * base)
        run_scr[...] += jnp.dot(ones_rows, occ, preferred_element_type=F32)


def _slots(route, tm):
    n = route.shape[0]
    n_tiles = _moe_tiles(n)
    assert n_tiles <= MOE_TILE_LANES and 2 * n + N_EXPERTS * MOE_TM < 2 ** 24
    row_out = pl.BlockSpec((SUBLANES, tm), lambda ph, i: (0, i * ph))
    sds = jax.ShapeDtypeStruct((SUBLANES, n), F32)
    return pl.pallas_call(
        functools.partial(_slots_body, n_tiles),
        grid=(2, n // tm),
        in_specs=[pl.BlockSpec((tm, ROUTER_LANES), lambda ph, i: (i, 0))],
        out_specs=[row_out, row_out, pl.BlockSpec((SUBLANES, MOE_TILE_LANES), lambda ph, i: (0, 0))],
        out_shape=[sds, sds, jax.ShapeDtypeStruct((SUBLANES, MOE_TILE_LANES), F32)],
        scratch_shapes=[pltpu.VMEM((SUBLANES, ROUTER_LANES), F32), pltpu.VMEM((SUBLANES, ROUTER_LANES), F32)],
        compiler_params=_cparams(("arbitrary", "arbitrary")),
        name="moe_slots",
    )(route)


def _dispatch_body(n_tiles, pos_ref, tile_ref, xn_ref, xs_hbm, zero_scr, zero_sem, row_sem):
    tm = xn_ref.shape[0]
    n = pl.num_programs(0) * tm
    base = pl.program_id(0) * tm

    @pl.when(pl.program_id(0) == 0)
    def _():
        zero_scr[...] = jnp.zeros_like(zero_scr)

        def fill_copy(t):
            return pltpu.make_async_copy(zero_scr, xs_hbm.at[pl.ds(t * MOE_TM, MOE_TM), :], zero_sem)

        def has_padding(t):
            return (tile_ref[t] >= N_EXPERTS) | (tile_ref[t] != tile_ref[t + 1])

        @pl.loop(0, n_tiles)
        def _(t):
            @pl.when(has_padding(t))
            def _():
                fill_copy(t).start()

        @pl.loop(0, n_tiles)
        def _(t):
            @pl.when(has_padding(t))
            def _():
                fill_copy(t).wait()

    def issue(j, carry):
        src = xn_ref.at[pl.ds(j, 1), :]
        pltpu.make_async_copy(src, xs_hbm.at[pl.ds(pos_ref[base + j], 1), :], row_sem).start()
        pltpu.make_async_copy(src, xs_hbm.at[pl.ds(pos_ref[n + base + j], 1), :], row_sem).start()
        return carry

    lax.fori_loop(0, tm, issue, 0, unroll=8)
    for _ in range(2):
        pltpu.make_async_copy(xn_ref, xs_hbm.at[pl.ds(0, tm), :], row_sem).wait()


def _dispatch(pos, tile_map, xn, tm):
    n, d = xn.shape
    n_tiles = _moe_tiles(n)
    return pl.pallas_call(
        functools.partial(_dispatch_body, n_tiles),
        grid_spec=pltpu.PrefetchScalarGridSpec(
            num_scalar_prefetch=2,
            grid=(n // tm,),
            in_specs=[pl.BlockSpec((tm, d), lambda i, p, t: (i, 0))],
            out_specs=pl.BlockSpec(memory_space=pl.ANY),
            scratch_shapes=[pltpu.VMEM((MOE_TM, d), F32), pltpu.SemaphoreType.DMA(()),
                            pltpu.SemaphoreType.DMA(())],
        ),
        out_shape=jax.ShapeDtypeStruct((n_tiles * MOE_TM, d), F32),
        compiler_params=_cparams(("arbitrary",)),
        name="moe_dispatch",
    )(pos, tile_map, xn)


def _experts_body(tile_ref, xs_ref, wg_ref, wu_ref, wd_ref, ys_ref):
    used = tile_ref[pl.program_id(0)] < N_EXPERTS

    @pl.when(used)
    def _():
        x = xs_ref[...].astype(BF16)
        gate = jnp.dot(x, wg_ref[0].astype(BF16), preferred_element_type=F32)
        up = jnp.dot(x, wu_ref[0].astype(BF16), preferred_element_type=F32)
        hid = gate * jax.nn.sigmoid(gate) * up
        ys_ref[...] = jnp.dot(hid.astype(BF16), wd_ref[0].astype(BF16), preferred_element_type=F32)

    @pl.when(jnp.logical_not(used))
    def _():
        ys_ref[...] = jnp.zeros_like(ys_ref)


def _experts(tile_map, xs, w_gate, w_up, w_down, layer):
    rows, d = xs.shape
    f = EXPERT_FF
    n_tiles = rows // MOE_TM
    x_map = lambda i, tm_ref: (jnp.where(tm_ref[i] < N_EXPERTS, i, 0), 0)
    y_map = lambda i, tm_ref: (i, 0)
    w_map = lambda i, tm_ref: (layer * N_EXPERTS + jnp.minimum(tm_ref[i], N_EXPERTS - 1), 0, 0)
    return pl.pallas_call(
        _experts_body,
        grid_spec=pltpu.PrefetchScalarGridSpec(
            num_scalar_prefetch=1,
            grid=(n_tiles,),
            in_specs=[pl.BlockSpec((MOE_TM, d), x_map), pl.BlockSpec((1, d, f), w_map),
                      pl.BlockSpec((1, d, f), w_map), pl.BlockSpec((1, f, d), w_map)],
            out_specs=pl.BlockSpec((MOE_TM, d), y_map),
        ),
        out_shape=jax.ShapeDtypeStruct((rows, d), F32),
        compiler_params=_cparams(("arbitrary",)),
        name="moe_experts",
    )(tile_map, xs, w_gate, w_up, w_down)


def _combine_body(pos_ref, h_ref, route_ref, ys_hbm, o_ref, buf, sems):
    tm = h_ref.shape[0]
    steps = pl.num_programs(0)
    n = steps * tm
    i = pl.program_id(0)

    def start_gather(step, slot):
        base = step * tm

        def issue(j, carry):
            for pick in range(2):
                pltpu.make_async_copy(ys_hbm.at[pl.ds(pos_ref[pick * n + base + j], 1), :],
                                      buf.at[slot, pick, pl.ds(j, 1), :], sems.at[slot]).start()
            return carry

        lax.fori_loop(0, tm, issue, 0, unroll=8)

    @pl.when(i == 0)
    def _():
        start_gather(0, 0)

    @pl.when(i + 1 < steps)
    def _():
        start_gather(i + 1, (i + 1) % 2)

    slot = i % 2
    pltpu.make_async_copy(buf.at[slot], buf.at[slot], sems.at[slot]).wait()
    route = route_ref[...]
    o_ref[...] = h_ref[...] + route[:, 2:3] * buf[slot, 0] + route[:, 3:4] * buf[slot, 1]


def _combine(pos, h, route, ys, tm):
    n, d = h.shape
    row = lambda i, p: (i, 0)
    return pl.pallas_call(
        _combine_body,
        grid_spec=pltpu.PrefetchScalarGridSpec(
            num_scalar_prefetch=1,
            grid=(n // tm,),
            in_specs=[pl.BlockSpec((tm, d), row), pl.BlockSpec((tm, ROUTER_LANES), row),
                      pl.BlockSpec(memory_space=pl.ANY)],
            out_specs=pl.BlockSpec((tm, d), row),
            scratch_shapes=[pltpu.VMEM((2, 2, tm, d), F32), pltpu.SemaphoreType.DMA((2,))],
        ),
        out_shape=jax.ShapeDtypeStruct((n, d), F32),
        compiler_params=_cparams(("arbitrary",)),
        name="moe_combine",
    )(pos, h, route, ys)


def _gelu(x):
    return 0.5 * x * (1.0 + lax.erf(x * (1.0 / math.sqrt(2.0))))


def _in_odd_body(h_ref, g_ref, w_ref, vg_ref, u_ref, vn_ref, q_ref, k_ref, v_ref):
    xn = _rms(h_ref[...], g_ref[...])
    proj = jnp.dot(xn.astype(BF16), w_ref[...], preferred_element_type=F32)
    w = MIX_WIDTH
    u_ref[...] = _gelu(proj[:, :w]).astype(BF16)
    vn_ref[...] = _rms(_gelu(proj[:, w:2 * w]), vg_ref[...]).astype(BF16)
    q_ref[...] = (proj[:, 2 * w:3 * w] * (1.0 / math.sqrt(HEAD_DIM))).astype(BF16)
    k_ref[...] = proj[:, 3 * w:4 * w].astype(BF16)
    v_ref[...] = proj[:, 4 * w:].astype(BF16)


def _in_odd(h, gain, w, v_gain, tm):
    n, d = h.shape
    row = lambda i: (i, 0)
    const2 = lambda i: (0, 0)
    sds = jax.ShapeDtypeStruct((n, MIX_WIDTH), BF16)
    return pl.pallas_call(
        _in_odd_body,
        grid=(n // tm,),
        in_specs=[pl.BlockSpec((tm, d), row), pl.BlockSpec((1, d), const2),
                  pl.BlockSpec(w.shape, const2), pl.BlockSpec((1, MIX_WIDTH), const2)],
        out_specs=[pl.BlockSpec((tm, MIX_WIDTH), row)] * 5,
        out_shape=[sds] * 5,
        compiler_params=_cparams(("arbitrary",)),
        name="in_odd",
    )(h, gain, w, v_gain)


SB_T = 128
SB_NB = 3
SB_TK = SB_NB * SB_T
SB_QBLK = 512
SB_GROUP = 2
SB_UNDERFLOW = -104.0


def _sb_body(q_ref, k_ref, v_ref, o_ref, acc_scr, run_scr):
    qi = pl.program_id(2)
    lane = lax.broadcasted_iota(jnp.int32, (SB_T, LANES), 1)
    row = lax.broadcasted_iota(jnp.int32, (2 * SB_T, 1), 0) % SB_T
    col = lax.broadcasted_iota(jnp.int32, (2 * SB_T, SB_TK), 1)
    rr = lax.broadcasted_iota(jnp.int32, (SB_T, 2 * SB_T), 0)
    cc = lax.broadcasted_iota(jnp.int32, (SB_T, 2 * SB_T), 1)
    suffix = jnp.where((cc >= SB_T) | (rr > cc), 1.0, 0.0).astype(BF16)

    def subtile_group(grp, _):
        q_los = [pl.multiple_of((grp * SB_GROUP + s) * SB_T, SB_T) for s in range(SB_GROUP)]
        q_starts = [qi * SB_QBLK + q_lo for q_lo in q_los]
        q_heads = []
        for q_lo in q_los:
            q = q_ref[0, pl.ds(q_lo, SB_T), :]
            zero = jnp.zeros_like(q)
            q_heads.append(jnp.concatenate(
                [jnp.where(lane < HEAD_DIM, q, zero), jnp.where(lane < HEAD_DIM, zero, q)], axis=0))
        acc_scr[...] = jnp.zeros_like(acc_scr)
        run_scr[...] = jnp.zeros_like(run_scr)

        def cond(carry):
            his, dones = carry
            active = [(hi > 0) & (done == 0) for hi, done in zip(his, dones)]
            return functools.reduce(jnp.logical_or, active)

        def body(carry):
            his, _ = carry
            group = range(SB_GROUP)
            kss = [pl.multiple_of(jnp.maximum(his[s] - SB_TK, 0), SB_T) for s in group]
            valid = [col < (jnp.minimum(row + q_starts[s], his[s]) - kss[s]) for s in group]
            zs = [lax.dot_general(q_heads[s], k_ref[0, pl.ds(kss[s], SB_TK), :], NT_DIMS,
                                  preferred_element_type=F32) for s in group]
            log_beta, log_rest = [], []
            for s in group:
                sp = jnp.log(1.0 + jnp.exp(-jnp.abs(zs[s])))
                log_beta.append(jnp.minimum(zs[s], 0.0) - sp)
                log_rest.append(jnp.where(valid[s], log_beta[s] - zs[s], 0.0))
            sums = [[_split_dot(log_rest[s][:, blk * SB_T:(blk + 1) * SB_T], suffix) for blk in range(SB_NB)]
                    for s in group]
            dones = []
            for s in group:
                run = run_scr[s]
                pieces = [None] * SB_NB
                for blk in reversed(range(SB_NB)):
                    tt = sums[s][blk]
                    pieces[blk] = jnp.exp(log_beta[s][:, blk * SB_T:(blk + 1) * SB_T] + tt[:, :SB_T] + run)
                    run = run + tt[:, SB_T:]
                att = jnp.where(valid[s], jnp.concatenate(pieces, axis=1), 0.0)
                acc_scr[s] += jnp.dot(att.astype(BF16), v_ref[0, pl.ds(kss[s], SB_TK), :],
                                      preferred_element_type=F32)
                run_scr[s] = run
                dones.append((jnp.max(run) <= SB_UNDERFLOW).astype(jnp.int32))
            return tuple(kss), tuple(dones)

        lax.while_loop(cond, body, (tuple(qs + SB_T for qs in q_starts),
                                    tuple(jnp.int32(0) for _ in range(SB_GROUP))))
        for s in range(SB_GROUP):
            o_ref[0, pl.ds(q_los[s], SB_T), :] = jnp.where(
                lane < HEAD_DIM, acc_scr[s, :SB_T], acc_scr[s, SB_T:]).astype(BF16)
        return 0

    lax.fori_loop(0, SB_QBLK // (SB_T * SB_GROUP), subtile_group, 0)


def _sb_attention(q, k, v):
    b, s, w = q.shape
    assert s % SB_QBLK == 0 and s >= SB_TK
    qspec = pl.BlockSpec((1, SB_QBLK, LANES), lambda bi, hp, i: (bi, i, hp))
    kvspec = pl.BlockSpec((1, s, LANES), lambda bi, hp, i: (bi, 0, hp))
    return pl.pallas_call(
        _sb_body,
        grid=(b, w // LANES, s // SB_QBLK),
        in_specs=[qspec, kvspec, kvspec],
        out_specs=qspec,
        out_shape=jax.ShapeDtypeStruct((b, s, w), BF16),
        scratch_shapes=[pltpu.VMEM((SB_GROUP, 2 * SB_T, LANES), F32),
                        pltpu.VMEM((SB_GROUP, 2 * SB_T, LANES), F32)],
        compiler_params=_cparams(("arbitrary", "arbitrary", "arbitrary")),
        name="sb_attention",
    )(q, k, v)


def _out_odd_body(u_ref, vn_ref, yd_ref, h_ref, ws_ref, bs_ref, wo_ref, fg_ref, wr_ref,
                  o_ref, xn_ref, route_ref, yc_scr):
    tm = u_ref.shape[0]
    r = lax.broadcasted_iota(jnp.int32, (SGU_BLOCK, SGU_BLOCK), 0)
    c = lax.broadcasted_iota(jnp.int32, (SGU_BLOCK, SGU_BLOCK), 1)
    for g in range(MIX_WIDTH // LANES):
        ls = slice(g * LANES, (g + 1) * LANES)
        ws = jnp.where(c <= r, ws_ref[g], jnp.zeros_like(ws_ref[g]))
        for blk in range(tm // SGU_BLOCK):
            rs = slice(blk * SGU_BLOCK, (blk + 1) * SGU_BLOCK)
            mixed = jnp.dot(ws, vn_ref[rs, ls], preferred_element_type=F32) + bs_ref[g]
            yc_scr[rs, ls] = (u_ref[rs, ls].astype(F32) * mixed).astype(BF16)
    w = MIX_WIDTH
    h_new = (h_ref[...]
             + jnp.dot(yc_scr[...], wo_ref[0:w, :], preferred_element_type=F32)
             + jnp.dot(yd_ref[...], wo_ref[w:, :], preferred_element_type=F32))
    o_ref[...] = h_new
    xn_ref[...], route_ref[...] = _route_tokens(h_new, fg_ref[...], wr_ref[...])


def _out_odd(u, vn, yd, h, sgu_w, sgu_b, w_out, ffn_gain, w_router, tm):
    n, d = h.shape
    w = MIX_WIDTH
    row = lambda i: (i, 0)
    const2 = lambda i: (0, 0)
    const3 = lambda i: (0, 0, 0)
    return pl.pallas_call(
        _out_odd_body,
        grid=(n // tm,),
        in_specs=[pl.BlockSpec((tm, w), row), pl.BlockSpec((tm, w), row), pl.BlockSpec((tm, w), row),
                  pl.BlockSpec((tm, d), row), pl.BlockSpec(sgu_w.shape, const3),
                  pl.BlockSpec(sgu_b.shape, const3), pl.BlockSpec(w_out.shape, const2),
                  pl.BlockSpec((1, d), const2), pl.BlockSpec(w_router.shape, const2)],
        out_specs=[pl.BlockSpec((tm, d), row), pl.BlockSpec((tm, d), row),
                   pl.BlockSpec((tm, ROUTER_LANES), row)],
        out_shape=[jax.ShapeDtypeStruct((n, d), F32), jax.ShapeDtypeStruct((n, d), F32),
                   jax.ShapeDtypeStruct((n, ROUTER_LANES), F32)],
        scratch_shapes=[pltpu.VMEM((tm, w), BF16)],
        compiler_params=_cparams(("arbitrary",)),
        name="out_odd",
    )(u, vn, yd, h, sgu_w, sgu_b, w_out, ffn_gain, w_router)


def _moe_layer(h, xn, route, w_gate, w_up, w_down, layer):
    pos1, pos2, tile_map = _slots(route, tm=512)
    pos = jnp.concatenate([pos1[0], pos2[0]]).astype(jnp.int32)
    tile_map = tile_map[0].astype(jnp.int32)
    xs = _dispatch(pos, tile_map, xn, tm=256)
    ys = _experts(tile_map, xs, w_gate, w_up, w_down, layer)
    return _combine(pos, h, route, ys, tm=256)


def kernel(x, mix_norm_even, w_in_even, att_q_norm, att_k_norm, att_rel_bias, pool_w, pool_scale,
           w_out_even, mix_norm_odd, w_in_odd, sgu_v_norm, sgu_w, sgu_b, w_out_odd, ffn_norm,
           w_router_group, w_router_expert, w_exp_gate, w_exp_up, w_exp_down):
    b, s, d = x.shape
    n = b * s
    depth = ffn_norm.shape[0]
    heads = MIX_WIDTH // HEAD_DIM
    h = x.reshape(n, d)
    w_gate = w_exp_gate.reshape(depth * N_EXPERTS, d, EXPERT_FF)
    w_up = w_exp_up.reshape(depth * N_EXPERTS, d, EXPERT_FF)
    w_down = w_exp_down.reshape(depth * N_EXPERTS, EXPERT_FF, d)
    for layer in range(depth):
        i = layer // 2
        ffn_gain = ffn_norm[layer][None, :]
        w_router = _router_weights(w_router_group[layer], w_router_expert[layer])
        if layer % 2 == 0:
            q, k_pad, v_pad, p = _in_even(
                h.reshape(b, s, d), mix_norm_even[i][None, :], w_in_even[i].astype(BF16),
                jnp.tile(att_q_norm[i], heads)[None, :], jnp.tile(att_k_norm[i], heads)[None, :],
                tm=ATT_LEFT)
            ya = _band_attention(q, k_pad, v_pad, _band_bias(att_rel_bias[i]))
            h, xn, route = _out_even(ya.reshape(n, MIX_WIDTH), p.reshape(n, MIX_WIDTH), h,
                                     pool_w[i].astype(BF16), pool_scale[i][None, :],
                                     w_out_even[i].astype(BF16), ffn_gain, w_router, seq=s, tm=512)
        else:
            u, vn, q, k, v = _in_odd(h, mix_norm_odd[i][None, :], w_in_odd[i].astype(BF16),
                                     sgu_v_norm[i][None, :], tm=512)
            to3 = lambda t: t.reshape(b, s, MIX_WIDTH)
            yd = _sb_attention(to3(q), to3(k), to3(v))
            bias = jnp.broadcast_to(sgu_b[i][:, :, None], (N_GROUPS, SGU_BLOCK, LANES))
            h, xn, route = _out_odd(u, vn, yd.reshape(n, MIX_WIDTH), h, sgu_w[i].astype(BF16), bias,
                                    w_out_odd[i].astype(BF16), ffn_gain, w_router, tm=512)
        h = _moe_layer(h, xn, route, w_gate, w_up, w_down, layer)
    return h.reshape(b, s, d)
```

```python
import functools
import math

import jax
import jax.numpy as jnp
from jax import lax
from jax.experimental import pallas as pl
from jax.experimental.pallas import tpu as pltpu

F32 = jnp.float32
BF16 = jnp.bfloat16

D_MODEL = 1024
CHUNK = 64
EPS = 1e-6
HEAD_DIM = 64
MIX_WIDTH = 512
LANES = 128
ATT_LEFT = 8 * CHUNK
ATT_MAX_REL = 128
POOL_WINDOWS = (2, 4, 8, 16)
POOL_HALO = 16
SGU_BLOCK = 128
N_GROUPS = 4
N_EXP_PER_GROUP = 8
N_EXPERTS = N_GROUPS * N_EXP_PER_GROUP
EXPERT_FF = 256
ROUTER_LANES = 128
ROUTER_ROWS = 40
NEG_BIG = -1e30
VMEM_LIMIT = 56 * 1024 * 1024

NT_DIMS = (((1,), (1,)), ((), ()))
LOG2E = math.log2(math.e)
Q_SCALE = LOG2E / math.sqrt(HEAD_DIM)


def _cparams(sem):
    return pltpu.CompilerParams(dimension_semantics=sem, vmem_limit_bytes=VMEM_LIMIT)


def _rms(x, gain):
    return x * lax.rsqrt(jnp.mean(x * x, axis=-1, keepdims=True) + EPS) * gain


def _split_dot(x, m):
    hi = x.astype(BF16)
    lo = (x - hi.astype(F32)).astype(BF16)
    return (jnp.dot(hi, m, preferred_element_type=F32)
            + jnp.dot(lo, m, preferred_element_type=F32))


def _head_rms(t, gain):
    n = t.shape[-1]
    r = lax.broadcasted_iota(jnp.int32, (n, n), 0) // HEAD_DIM
    c = lax.broadcasted_iota(jnp.int32, (n, n), 1) // HEAD_DIM
    bd = jnp.where(r == c, 1.0, 0.0).astype(BF16)
    ms = _split_dot(t * t, bd) * (1.0 / HEAD_DIM)
    return t * lax.rsqrt(ms + EPS) * gain


def _in_even_body(h_ref, g_ref, w_ref, qg_ref, kg_ref, q_ref, k_ref, v_ref, p_ref):
    j = pl.program_id(1)

    @pl.when(j == 0)
    def _():
        k_ref[...] = jnp.zeros_like(k_ref)
        v_ref[...] = jnp.zeros_like(v_ref)

    @pl.when(j > 0)
    def _():
        xn = _rms(h_ref[0], g_ref[...])
        proj = jnp.dot(xn.astype(BF16), w_ref[...], preferred_element_type=F32)
        w = MIX_WIDTH
        q_ref[0] = (_head_rms(proj[:, :w], qg_ref[...]) * Q_SCALE).astype(BF16)
        k_ref[0] = _head_rms(proj[:, w:2 * w], kg_ref[...]).astype(BF16)
        v_ref[0] = proj[:, 2 * w:3 * w].astype(BF16)
        p_ref[0] = proj[:, 3 * w:].astype(BF16)


def _in_even(h, gain, w, q_gain, k_gain, tm):
    b, s, d = h.shape
    assert tm == ATT_LEFT and s % tm == 0
    nt = s // tm
    cur = lambda bi, j: (bi, jnp.maximum(j - 1, 0), 0)
    const = lambda bi, j: (0, 0)
    out_sds = lambda rows: jax.ShapeDtypeStruct((b, rows, MIX_WIDTH), BF16)
    return pl.pallas_call(
        _in_even_body,
        grid=(b, nt + 1),
        in_specs=[
            pl.BlockSpec((1, tm, d), cur),
            pl.BlockSpec((1, d), const),
            pl.BlockSpec(w.shape, const),
            pl.BlockSpec((1, MIX_WIDTH), const),
            pl.BlockSpec((1, MIX_WIDTH), const),
        ],
        out_specs=[
            pl.BlockSpec((1, tm, MIX_WIDTH), cur),
            pl.BlockSpec((1, tm, MIX_WIDTH), lambda bi, j: (bi, j, 0)),
            pl.BlockSpec((1, tm, MIX_WIDTH), lambda bi, j: (bi, j, 0)),
            pl.BlockSpec((1, tm, MIX_WIDTH), cur),
        ],
        out_shape=[out_sds(s), out_sds(s + ATT_LEFT), out_sds(s + ATT_LEFT), out_sds(s)],
        compiler_params=_cparams(("arbitrary", "arbitrary")),
        name="in_even",
    )(h, gain, w, q_gain, k_gain)


BAND_TQ = 2 * CHUNK
BAND_TK = BAND_TQ + ATT_LEFT


def _band_body(q_ref, k_ref, v_ref, bias_ref, o_ref):
    i = pl.program_id(1)
    start = pl.multiple_of(i * BAND_TQ, BAND_TQ)
    lane = lax.broadcasted_iota(jnp.int32, (BAND_TQ, LANES), 1)
    col = lax.broadcasted_iota(jnp.int32, (2 * BAND_TQ, BAND_TK), 1)
    is_pad = (col + start) < ATT_LEFT
    pairs = range(MIX_WIDTH // LANES)
    lanes = [slice(hp * LANES, (hp + 1) * LANES) for hp in pairs]
    scores = []
    for hp in pairs:
        q = q_ref[0, :, lanes[hp]]
        kb = k_ref[0, pl.ds(start, BAND_TK), lanes[hp]]
        zero = jnp.zeros_like(q)
        q2 = jnp.concatenate([jnp.where(lane < HEAD_DIM, q, zero), jnp.where(lane < HEAD_DIM, zero, q)], axis=0)
        scores.append(lax.dot_general(q2, kb, NT_DIMS, preferred_element_type=F32))
    probs, denoms = [], []
    for hp in pairs:
        bias = bias_ref[2 * hp:2 * hp + 2].reshape(2 * BAND_TQ, BAND_TK)
        s = jnp.where(is_pad, NEG_BIG, scores[hp] + bias)
        p = jnp.exp2(s - jnp.max(s, axis=-1, keepdims=True))
        denoms.append(jnp.sum(p, axis=-1, keepdims=True))
        probs.append(p.astype(BF16))
    for hp in pairs:
        vb = v_ref[0, pl.ds(start, BAND_TK), lanes[hp]]
        o = jnp.dot(probs[hp], vb, preferred_element_type=F32) / denoms[hp]
        o_ref[0, :, lanes[hp]] = jnp.where(lane < HEAD_DIM, o[:BAND_TQ], o[BAND_TQ:]).astype(BF16)


def _band_bias(rel_bias):
    heads = rel_bias.shape[0]
    r = jnp.arange(BAND_TQ)[:, None]
    j = jnp.arange(BAND_TK)[None, :]
    jb = j - CHUNK * (r // CHUNK)
    in_band = (jb >= 0) & (jb < ATT_LEFT + CHUNK)
    period = BAND_TK + BAND_TQ
    far = jnp.broadcast_to(rel_bias[:, 2 * ATT_MAX_REL:], (heads, ATT_LEFT - ATT_MAX_REL + 1))
    near = rel_bias[:, 2 * ATT_MAX_REL - 1:0:-1]
    wrap = jnp.broadcast_to(rel_bias[:, 2 * ATT_MAX_REL:], (heads, period - BAND_TK))
    g = jnp.concatenate([far, near, wrap], axis=1).astype(F32)
    assert g.shape[1] == period
    toep = jnp.tile(g, (1, BAND_TQ))[:, :BAND_TQ * (period - 1)].reshape(heads, BAND_TQ, period - 1)
    return jnp.where(in_band[None], toep[:, :, :BAND_TK] * LOG2E, NEG_BIG)


def _band_attention(q, k_pad, v_pad, bias):
    b, s, w = q.shape
    sp = k_pad.shape[1]
    return pl.pallas_call(
        _band_body,
        grid=(b, s // BAND_TQ),
        in_specs=[
            pl.BlockSpec((1, BAND_TQ, w), lambda bi, i: (bi, i, 0)),
            pl.BlockSpec((1, sp, w), lambda bi, i: (bi, 0, 0)),
            pl.BlockSpec((1, sp, w), lambda bi, i: (bi, 0, 0)),
            pl.BlockSpec(bias.shape, lambda bi, i: (0, 0, 0)),
        ],
        out_specs=pl.BlockSpec((1, BAND_TQ, w), lambda bi, i: (bi, i, 0)),
        out_shape=jax.ShapeDtypeStruct((b, s, w), BF16),
        compiler_params=_cparams(("arbitrary", "arbitrary")),
        name="band_attention",
    )(q, k_pad, v_pad, bias)


def _route_tokens(h, gain, w_router):
    xn = _rms(h, gain)
    x_hi = xn.astype(BF16)
    x_lo = (xn - x_hi.astype(F32)).astype(BF16)
    w_hi = w_router.astype(BF16)
    w_lo = (w_router - w_hi.astype(F32)).astype(BF16)
    logits = (jnp.dot(x_hi, w_hi, preferred_element_type=F32)
              + jnp.dot(x_lo, w_hi, preferred_element_type=F32)
              + jnp.dot(x_hi, w_lo, preferred_element_type=F32))
    lt = logits.T[:ROUTER_ROWS]
    sub = lax.broadcasted_iota(jnp.int32, lt.shape, 0).astype(F32)
    ninf = -jnp.inf

    def top(vals):
        m = jnp.max(vals, axis=0, keepdims=True)
        idx = jnp.min(jnp.where(vals == m, sub, float(ROUTER_LANES)), axis=0, keepdims=True)
        return m, idx

    is_group = sub < N_GROUPS
    g_max, g_sel = top(jnp.where(is_group, lt, ninf))
    g_den = jnp.sum(jnp.where(is_group, jnp.exp(lt - g_max), 0.0), axis=0, keepdims=True)
    g_weight = 1.0 / g_den
    lo = N_GROUPS + N_EXP_PER_GROUP * g_sel
    e_logits = jnp.where((sub >= lo) & (sub < lo + N_EXP_PER_GROUP), lt, ninf)
    e1, i1 = top(e_logits)
    e2, i2 = top(jnp.where(sub == i1, ninf, e_logits))
    t = jnp.exp(e2 - e1)
    w1 = g_weight / (1.0 + t)
    w2 = g_weight * t / (1.0 + t)
    rows = lax.broadcasted_iota(jnp.int32, (ROUTER_LANES, lt.shape[1]), 0)
    route_t = jnp.where(rows == 0, i1 - N_GROUPS, jnp.where(rows == 1, i2 - N_GROUPS, 0.0))
    route_t = jnp.where(rows == 2, w1, jnp.where(rows == 3, w2, route_t))
    return xn, route_t.T


def _router_weights(w_rg, w_re):
    pad = jnp.zeros((w_rg.shape[0], ROUTER_LANES - N_GROUPS - N_EXPERTS), F32)
    return jnp.concatenate([w_rg, w_re, pad], axis=1)


def _out_even_body(tiles_per_seq, ya_ref, p_ref, halo_ref, h_ref, pw_ref, ps_ref, wo_ref, fg_ref, wr_ref,
                   o_ref, xn_ref, route_ref, p_scr, yb_scr):
    tm = p_ref.shape[0]
    it = pl.program_id(0) % tiles_per_seq
    halo = halo_ref[...].astype(F32)
    p_scr[0:POOL_HALO, :] = jnp.where(it == 0, jnp.zeros_like(halo), halo)
    p_scr[POOL_HALO:, :] = p_ref[...].astype(F32)
    t = it * tm + lax.broadcasted_iota(jnp.int32, (tm, 1), 0)
    for g, win in enumerate(POOL_WINDOWS):
        ls = slice(g * LANES, (g + 1) * LANES)
        cur = p_scr[POOL_HALO:POOL_HALO + tm, ls]
        acc = cur
        for dlt in range(1, win):
            acc = acc + p_scr[POOL_HALO - dlt:POOL_HALO - dlt + tm, ls]
        cnt = jnp.minimum(t + 1, win).astype(F32)
        mixed = acc / cnt - cur
        yb = jnp.dot(mixed.astype(BF16), pw_ref[g], preferred_element_type=F32) * ps_ref[:, ls]
        yb_scr[:, ls] = yb.astype(BF16)
    w = MIX_WIDTH
    h_new = (h_ref[...]
             + jnp.dot(ya_ref[...], wo_ref[0:w, :], preferred_element_type=F32)
             + jnp.dot(yb_scr[...], wo_ref[w:, :], preferred_element_type=F32))
    o_ref[...] = h_new
    xn_ref[...], route_ref[...] = _route_tokens(h_new, fg_ref[...], wr_ref[...])


def _out_even(ya, p, h, pool_w, pool_scale, w_out, ffn_gain, w_router, seq, tm):
    n, d = h.shape
    w = MIX_WIDTH
    row = lambda i: (i, 0)
    const2 = lambda i: (0, 0)
    halo_blocks = tm // POOL_HALO
    return pl.pallas_call(
        functools.partial(_out_even_body, seq // tm),
        grid=(n // tm,),
        in_specs=[
            pl.BlockSpec((tm, w), row),
            pl.BlockSpec((tm, w), row),
            pl.BlockSpec((POOL_HALO, w), lambda i: (jnp.maximum(i * halo_blocks - 1, 0), 0)),
            pl.BlockSpec((tm, d), row),
            pl.BlockSpec(pool_w.shape, lambda i: (0, 0, 0)),
            pl.BlockSpec((1, w), const2),
            pl.BlockSpec(w_out.shape, const2),
            pl.BlockSpec((1, d), const2),
            pl.BlockSpec(w_router.shape, const2),
        ],
        out_specs=[pl.BlockSpec((tm, d), row), pl.BlockSpec((tm, d), row),
                   pl.BlockSpec((tm, ROUTER_LANES), row)],
        out_shape=[jax.ShapeDtypeStruct((n, d), F32), jax.ShapeDtypeStruct((n, d), F32),
                   jax.ShapeDtypeStruct((n, ROUTER_LANES), F32)],
        scratch_shapes=[pltpu.VMEM((tm + POOL_HALO, w), F32), pltpu.VMEM((tm, w), BF16)],
        compiler_params=_cparams(("arbitrary",)),
        name="out_even",
    )(ya, p, p, h, pool_w, pool_scale, w_out, ffn_gain, w_router)


MOE_TM = 256
MOE_TILE_LANES = 256
SUBLANES = 8


def _moe_tiles(n):
    return (2 * n) // MOE_TM + N_EXPERTS


def _exact_dot_nt(ones, x):
    out = None
    for _ in range(3):
        part = x.astype(BF16)
        x = x - part.astype(F32)
        term = lax.dot_general(ones, part, NT_DIMS, preferred_element_type=F32)
        out = term if out is None else out + term
    return out


def _slots_body(n_tiles, route_ref, pos1_ref, pos2_ref, tile_ref, run_scr, start_scr):
    phase = pl.program_id(0)
    i = pl.program_id(1)
    tm = route_ref.shape[0]
    route = route_ref[...]
    lane = lax.broadcasted_iota(jnp.int32, (tm, ROUTER_LANES), 1).astype(F32)
    pick1 = jnp.where(lane == route[:, 0:1], 1.0, 0.0)
    pick2 = jnp.where(lane == route[:, 1:2], 1.0, 0.0)
    occ = (pick1 + pick2).astype(BF16)
    ones_rows = jnp.ones((SUBLANES, tm), BF16)
    ones_lanes = jnp.ones((SUBLANES, ROUTER_LANES), BF16)

    @pl.when(i == 0)
    def _():
        run_scr[...] = jnp.zeros_like(run_scr)

    @pl.when(phase == 0)
    def _():
        run_scr[...] += jnp.dot(ones_rows, occ, preferred_element_type=F32)

        @pl.when(i == pl.num_programs(1) - 1)
        def _():
            padded = jnp.floor((run_scr[...] + (MOE_TM - 1)) * (1.0 / MOE_TM)) * MOE_TM
            r = lax.broadcasted_iota(jnp.int32, (ROUTER_LANES, ROUTER_LANES), 0)
            c = lax.broadcasted_iota(jnp.int32, (ROUTER_LANES, ROUTER_LANES), 1)
            before = jnp.where(r < c, 1.0, 0.0).astype(BF16)
            hi = padded.astype(BF16)
            mid = (padded - hi.astype(F32)).astype(BF16)
            low = (padded - hi.astype(F32) - mid.astype(F32)).astype(BF16)
            start = (jnp.dot(hi, before, preferred_element_type=F32)
                     + jnp.dot(mid, before, preferred_element_type=F32)
                     + jnp.dot(low, before, preferred_element_type=F32))
            start_scr[...] = start
            seg_end = start[0:1, :] + padded[0:1, :]
            tile_lo = (lax.broadcasted_iota(jnp.int32, (MOE_TILE_LANES, ROUTER_LANES), 0) * MOE_TM).astype(F32)
            e_lane = lax.broadcasted_iota(jnp.int32, (MOE_TILE_LANES, ROUTER_LANES), 1)
            ended = jnp.where((seg_end <= tile_lo) & (e_lane < N_EXPERTS), 1.0, 0.0).astype(BF16)
            tile_ref[...] = lax.dot_general(ones_lanes, ended, NT_DIMS, preferred_element_type=F32)

    @pl.when(phase == 1)
    def _():
        r = lax.broadcasted_iota(jnp.int32, (tm, tm), 0)
        c = lax.broadcasted_iota(jnp.int32, (tm, tm), 1)
        earlier = jnp.where(c < r, 1.0, 0.0).astype(BF16)
        base = (jnp.dot(earlier, occ, preferred_element_type=F32)
                + run_scr[0:1, :] + start_scr[0:1, :])
        pos1_ref[...] = _exact_dot_nt(ones_lanes, pick1 * base)
        pos2_ref[...] = _exact_dot_nt(ones_lanes, pick2 * base)
        run_scr[...] += jnp.dot(ones_rows, occ, preferred_element_type=F32)


def _slots(route, tm):
    n = route.shape[0]
    n_tiles = _moe_tiles(n)
    assert n_tiles <= MOE_TILE_LANES and 2 * n + N_EXPERTS * MOE_TM < 2 ** 24
    row_out = pl.BlockSpec((SUBLANES, tm), lambda ph, i: (0, i * ph))
    sds = jax.ShapeDtypeStruct((SUBLANES, n), F32)
    return pl.pallas_call(
        functools.partial(_slots_body, n_tiles),
        grid=(2, n // tm),
        in_specs=[pl.BlockSpec((tm, ROUTER_LANES), lambda ph, i: (i, 0))],
        out_specs=[row_out, row_out, pl.BlockSpec((SUBLANES, MOE_TILE_LANES), lambda ph, i: (0, 0))],
        out_shape=[sds, sds, jax.ShapeDtypeStruct((SUBLANES, MOE_TILE_LANES), F32)],
        scratch_shapes=[pltpu.VMEM((SUBLANES, ROUTER_LANES), F32), pltpu.VMEM((SUBLANES, ROUTER_LANES), F32)],
        compiler_params=_cparams(("arbitrary", "arbitrary")),
        name="moe_slots",
    )(route)


def _dispatch_body(n_tiles, pos_ref, tile_ref, xn_ref, xs_hbm, zero_scr, zero_sem, row_sem):
    tm = xn_ref.shape[0]
    n = pl.num_programs(0) * tm
    base = pl.program_id(0) * tm

    @pl.when(pl.program_id(0) == 0)
    def _():
        zero_scr[...] = jnp.zeros_like(zero_scr)

        def fill_copy(t):
            return pltpu.make_async_copy(zero_scr, xs_hbm.at[pl.ds(t * MOE_TM, MOE_TM), :], zero_sem)

        def has_padding(t):
            return (tile_ref[t] >= N_EXPERTS) | (tile_ref[t] != tile_ref[t + 1])

        @pl.loop(0, n_tiles)
        def _(t):
            @pl.when(has_padding(t))
            def _():
                fill_copy(t).start()

        @pl.loop(0, n_tiles)
        def _(t):
            @pl.when(has_padding(t))
            def _():
                fill_copy(t).wait()

    def issue(j, carry):
        src = xn_ref.at[pl.ds(j, 1), :]
        pltpu.make_async_copy(src, xs_hbm.at[pl.ds(pos_ref[base + j], 1), :], row_sem).start()
        pltpu.make_async_copy(src, xs_hbm.at[pl.ds(pos_ref[n + base + j], 1), :], row_sem).start()
        return carry

    lax.fori_loop(0, tm, issue, 0, unroll=8)
    for _ in range(2):
        pltpu.make_async_copy(xn_ref, xs_hbm.at[pl.ds(0, tm), :], row_sem).wait()


def _dispatch(pos, tile_map, xn, tm):
    n, d = xn.shape
    n_tiles = _moe_tiles(n)
    return pl.pallas_call(
        functools.partial(_dispatch_body, n_tiles),
        grid_spec=pltpu.PrefetchScalarGridSpec(
            num_scalar_prefetch=2,
            grid=(n // tm,),
            in_specs=[pl.BlockSpec((tm, d), lambda i, p, t: (i, 0))],
            out_specs=pl.BlockSpec(memory_space=pl.ANY),
            scratch_shapes=[pltpu.VMEM((MOE_TM, d), F32), pltpu.SemaphoreType.DMA(()),
                            pltpu.SemaphoreType.DMA(())],
        ),
        out_shape=jax.ShapeDtypeStruct((n_tiles * MOE_TM, d), F32),
        compiler_params=_cparams(("arbitrary",)),
        name="moe_dispatch",
    )(pos, tile_map, xn)


def _experts_body(tile_ref, xs_ref, wg_ref, wu_ref, wd_ref, ys_ref):
    used = tile_ref[pl.program_id(0)] < N_EXPERTS

    @pl.when(used)
    def _():
        x = xs_ref[...].astype(BF16)
        gate = jnp.dot(x, wg_ref[0].astype(BF16), preferred_element_type=F32)
        up = jnp.dot(x, wu_ref[0].astype(BF16), preferred_element_type=F32)
        hid = gate * jax.nn.sigmoid(gate) * up
        ys_ref[...] = jnp.dot(hid.astype(BF16), wd_ref[0].astype(BF16), preferred_element_type=F32)

    @pl.when(jnp.logical_not(used))
    def _():
        ys_ref[...] = jnp.zeros_like(ys_ref)


def _experts(tile_map, xs, w_gate, w_up, w_down, layer):
    rows, d = xs.shape
    f = EXPERT_FF
    n_tiles = rows // MOE_TM
    x_map = lambda i, tm_ref: (jnp.where(tm_ref[i] < N_EXPERTS, i, 0), 0)
    y_map = lambda i, tm_ref: (i, 0)
    w_map = lambda i, tm_ref: (layer * N_EXPERTS + jnp.minimum(tm_ref[i], N_EXPERTS - 1), 0, 0)
    return pl.pallas_call(
        _experts_body,
        grid_spec=pltpu.PrefetchScalarGridSpec(
            num_scalar_prefetch=1,
            grid=(n_tiles,),
            in_specs=[pl.BlockSpec((MOE_TM, d), x_map), pl.BlockSpec((1, d, f), w_map),
                      pl.BlockSpec((1, d, f), w_map), pl.BlockSpec((1, f, d), w_map)],
            out_specs=pl.BlockSpec((MOE_TM, d), y_map),
        ),
        out_shape=jax.ShapeDtypeStruct((rows, d), F32),
        compiler_params=_cparams(("arbitrary",)),
        name="moe_experts",
    )(tile_map, xs, w_gate, w_up, w_down)


def _combine_body(pos_ref, h_ref, route_ref, ys_hbm, o_ref, buf, sems):
    tm = h_ref.shape[0]
    steps = pl.num_programs(0)
    n = steps * tm
    i = pl.program_id(0)

    def start_gather(step, slot):
        base = step * tm

        def issue(j, carry):
            for pick in range(2):
                pltpu.make_async_copy(ys_hbm.at[pl.ds(pos_ref[pick * n + base + j], 1), :],
                                      buf.at[slot, pick, pl.ds(j, 1), :], sems.at[slot]).start()
            return carry

        lax.fori_loop(0, tm, issue, 0, unroll=8)

    @pl.when(i == 0)
    def _():
        start_gather(0, 0)

    @pl.when(i + 1 < steps)
    def _():
        start_gather(i + 1, (i + 1) % 2)

    slot = i % 2
    pltpu.make_async_copy(buf.at[slot], buf.at[slot], sems.at[slot]).wait()
    route = route_ref[...]
    o_ref[...] = h_ref[...] + route[:, 2:3] * buf[slot, 0] + route[:, 3:4] * buf[slot, 1]


def _combine(pos, h, route, ys, tm):
    n, d = h.shape
    row = lambda i, p: (i, 0)
    return pl.pallas_call(
        _combine_body,
        grid_spec=pltpu.PrefetchScalarGridSpec(
            num_scalar_prefetch=1,
            grid=(n // tm,),
            in_specs=[pl.BlockSpec((tm, d), row), pl.BlockSpec((tm, ROUTER_LANES), row),
                      pl.BlockSpec(memory_space=pl.ANY)],
            out_specs=pl.BlockSpec((tm, d), row),
            scratch_shapes=[pltpu.VMEM((2, 2, tm, d), F32), pltpu.SemaphoreType.DMA((2,))],
        ),
        out_shape=jax.ShapeDtypeStruct((n, d), F32),
        compiler_params=_cparams(("arbitrary",)),
        name="moe_combine",
    )(pos, h, route, ys)


def _gelu(x):
    return 0.5 * x * (1.0 + lax.erf(x * (1.0 / math.sqrt(2.0))))


def _in_odd_body(h_ref, g_ref, w_ref, vg_ref, u_ref, vn_ref, q_ref, k_ref, v_ref):
    xn = _rms(h_ref[...], g_ref[...])
    proj = jnp.dot(xn.astype(BF16), w_ref[...], preferred_element_type=F32)
    w = MIX_WIDTH
    u_ref[...] = _gelu(proj[:, :w]).astype(BF16)
    vn_ref[...] = _rms(_gelu(proj[:, w:2 * w]), vg_ref[...]).astype(BF16)
    q_ref[...] = (proj[:, 2 * w:3 * w] * Q_SCALE).astype(BF16)
    k_ref[...] = proj[:, 3 * w:4 * w].astype(BF16)
    v_ref[...] = proj[:, 4 * w:].astype(BF16)


def _in_odd(h, gain, w, v_gain, tm):
    n, d = h.shape
    row = lambda i: (i, 0)
    const2 = lambda i: (0, 0)
    sds = jax.ShapeDtypeStruct((n, MIX_WIDTH), BF16)
    return pl.pallas_call(
        _in_odd_body,
        grid=(n // tm,),
        in_specs=[pl.BlockSpec((tm, d), row), pl.BlockSpec((1, d), const2),
                  pl.BlockSpec(w.shape, const2), pl.BlockSpec((1, MIX_WIDTH), const2)],
        out_specs=[pl.BlockSpec((tm, MIX_WIDTH), row)] * 5,
        out_shape=[sds] * 5,
        compiler_params=_cparams(("arbitrary",)),
        name="in_odd",
    )(h, gain, w, v_gain)


SB_T = 128
SB_NB = 3
SB_TK = SB_NB * SB_T
SB_QBLK = 512
SB_GROUP = 2
SB_UNDERFLOW = -150.0


def _sb_body(q_ref, k_ref, v_ref, o_ref, acc_scr, run_scr):
    qi = pl.program_id(2)
    lane = lax.broadcasted_iota(jnp.int32, (SB_T, LANES), 1)
    row = lax.broadcasted_iota(jnp.int32, (2 * SB_T, 1), 0) % SB_T
    col = lax.broadcasted_iota(jnp.int32, (2 * SB_T, SB_TK), 1)
    rr = lax.broadcasted_iota(jnp.int32, (2 * SB_T, 2 * SB_T), 0) % SB_T
    cc = lax.broadcasted_iota(jnp.int32, (2 * SB_T, 2 * SB_T), 1)
    suffix = jnp.where((cc >= SB_T) | (rr > cc), 1.0, 0.0).astype(BF16)

    def suffix_sums(x):
        hi = x.astype(BF16)
        lo = (x - hi.astype(F32)).astype(BF16)
        return jnp.dot(jnp.concatenate([hi, lo], axis=1), suffix, preferred_element_type=F32)

    def subtile_group(grp, _):
        q_los = [pl.multiple_of((grp * SB_GROUP + s) * SB_T, SB_T) for s in range(SB_GROUP)]
        q_starts = [qi * SB_QBLK + q_lo for q_lo in q_los]
        q_heads = []
        for q_lo in q_los:
            q = q_ref[0, pl.ds(q_lo, SB_T), :]
            zero = jnp.zeros_like(q)
            q_heads.append(jnp.concatenate(
                [jnp.where(lane < HEAD_DIM, q, zero), jnp.where(lane < HEAD_DIM, zero, q)], axis=0))
        acc_scr[...] = jnp.zeros_like(acc_scr)
        run_scr[...] = jnp.zeros_like(run_scr)

        def cond(carry):
            his, dones = carry
            active = [(hi > 0) & (done == 0) for hi, done in zip(his, dones)]
            return functools.reduce(jnp.logical_or, active)

        def body(carry):
            his, _ = carry
            group = range(SB_GROUP)
            kss = [pl.multiple_of(jnp.maximum(his[s] - SB_TK, 0), SB_T) for s in group]
            valid = [col < (jnp.minimum(row + q_starts[s], his[s]) - kss[s]) for s in group]
            zs = [lax.dot_general(q_heads[s], k_ref[0, pl.ds(kss[s], SB_TK), :], NT_DIMS,
                                  preferred_element_type=F32) for s in group]
            log_beta, log_rest = [], []
            for s in group:
                z = jnp.where(valid[s], zs[s], NEG_BIG)
                sp = jnp.log2(1.0 + jnp.exp2(-jnp.abs(z)))
                log_beta.append(jnp.minimum(z, 0.0) - sp)
                log_rest.append(log_beta[s] - z)
            sums = [[suffix_sums(log_rest[s][:, blk * SB_T:(blk + 1) * SB_T]) for blk in range(SB_NB)]
                    for s in group]
            dones = []
            for s in group:
                run = run_scr[s]
                pieces = [None] * SB_NB
                for blk in reversed(range(SB_NB)):
                    tt = sums[s][blk]
                    pieces[blk] = jnp.exp2(log_beta[s][:, blk * SB_T:(blk + 1) * SB_T] + tt[:, :SB_T] + run)
                    run = run + tt[:, SB_T:]
                att = jnp.concatenate(pieces, axis=1)
                acc_scr[s] += jnp.dot(att.astype(BF16), v_ref[0, pl.ds(kss[s], SB_TK), :],
                                      preferred_element_type=F32)
                run_scr[s] = run
                dones.append((jnp.max(run) <= SB_UNDERFLOW).astype(jnp.int32))
            return tuple(kss), tuple(dones)

        lax.while_loop(cond, body, (tuple(qs + SB_T for qs in q_starts),
                                    tuple(jnp.int32(0) for _ in range(SB_GROUP))))
        for s in range(SB_GROUP):
            o_ref[0, pl.ds(q_los[s], SB_T), :] = jnp.where(
                lane < HEAD_DIM, acc_scr[s, :SB_T], acc_scr[s, SB_T:]).astype(BF16)
        return 0

    lax.fori_loop(0, SB_QBLK // (SB_T * SB_GROUP), subtile_group, 0)


def _sb_attention(q, k, v):
    b, s, w = q.shape
    assert s % SB_QBLK == 0 and s >= SB_TK
    qspec = pl.BlockSpec((1, SB_QBLK, LANES), lambda bi, hp, i: (bi, i, hp))
    kvspec = pl.BlockSpec((1, s, LANES), lambda bi, hp, i: (bi, 0, hp))
    return pl.pallas_call(
        _sb_body,
        grid=(b, w // LANES, s // SB_QBLK),
        in_specs=[qspec, kvspec, kvspec],
        out_specs=qspec,
        out_shape=jax.ShapeDtypeStruct((b, s, w), BF16),
        scratch_shapes=[pltpu.VMEM((SB_GROUP, 2 * SB_T, LANES), F32),
                        pltpu.VMEM((SB_GROUP, 2 * SB_T, LANES), F32)],
        compiler_params=_cparams(("arbitrary", "arbitrary", "arbitrary")),
        name="sb_attention",
    )(q, k, v)


def _out_odd_body(u_ref, vn_ref, yd_ref, h_ref, ws_ref, bs_ref, wo_ref, fg_ref, wr_ref,
                  o_ref, xn_ref, route_ref, yc_scr):
    tm = u_ref.shape[0]
    r = lax.broadcasted_iota(jnp.int32, (SGU_BLOCK, SGU_BLOCK), 0)
    c = lax.broadcasted_iota(jnp.int32, (SGU_BLOCK, SGU_BLOCK), 1)
    for g in range(MIX_WIDTH // LANES):
        ls = slice(g * LANES, (g + 1) * LANES)
        ws = jnp.where(c <= r, ws_ref[g], jnp.zeros_like(ws_ref[g]))
        for blk in range(tm // SGU_BLOCK):
            rs = slice(blk * SGU_BLOCK, (blk + 1) * SGU_BLOCK)
            mixed = jnp.dot(ws, vn_ref[rs, ls], preferred_element_type=F32) + bs_ref[g]
            yc_scr[rs, ls] = (u_ref[rs, ls].astype(F32) * mixed).astype(BF16)
    w = MIX_WIDTH
    h_new = (h_ref[...]
             + jnp.dot(yc_scr[...], wo_ref[0:w, :], preferred_element_type=F32)
             + jnp.dot(yd_ref[...], wo_ref[w:, :], preferred_element_type=F32))
    o_ref[...] = h_new
    xn_ref[...], route_ref[...] = _route_tokens(h_new, fg_ref[...], wr_ref[...])


def _out_odd(u, vn, yd, h, sgu_w, sgu_b, w_out, ffn_gain, w_router, tm):
    n, d = h.shape
    w = MIX_WIDTH
    row = lambda i: (i, 0)
    const2 = lambda i: (0, 0)
    const3 = lambda i: (0, 0, 0)
    return pl.pallas_call(
        _out_odd_body,
        grid=(n // tm,),
        in_specs=[pl.BlockSpec((tm, w), row), pl.BlockSpec((tm, w), row), pl.BlockSpec((tm, w), row),
                  pl.BlockSpec((tm, d), row), pl.BlockSpec(sgu_w.shape, const3),
                  pl.BlockSpec(sgu_b.shape, const3), pl.BlockSpec(w_out.shape, const2),
                  pl.BlockSpec((1, d), const2), pl.BlockSpec(w_router.shape, const2)],
        out_specs=[pl.BlockSpec((tm, d), row), pl.BlockSpec((tm, d), row),
                   pl.BlockSpec((tm, ROUTER_LANES), row)],
        out_shape=[jax.ShapeDtypeStruct((n, d), F32), jax.ShapeDtypeStruct((n, d), F32),
                   jax.ShapeDtypeStruct((n, ROUTER_LANES), F32)],
        scratch_shapes=[pltpu.VMEM((tm, w), BF16)],
        compiler_params=_cparams(("arbitrary",)),
        name="out_odd",
    )(u, vn, yd, h, sgu_w, sgu_b, w_out, ffn_gain, w_router)


def _moe_layer(h, xn, route, w_gate, w_up, w_down, layer):
    pos1, pos2, tile_map = _slots(route, tm=512)
    pos = jnp.concatenate([pos1[0], pos2[0]]).astype(jnp.int32)
    tile_map = tile_map[0].astype(jnp.int32)
    xs = _dispatch(pos, tile_map, xn, tm=256)
    ys = _experts(tile_map, xs, w_gate, w_up, w_down, layer)
    return _combine(pos, h, route, ys, tm=256)


def kernel(x, mix_norm_even, w_in_even, att_q_norm, att_k_norm, att_rel_bias, pool_w, pool_scale,
           w_out_even, mix_norm_odd, w_in_odd, sgu_v_norm, sgu_w, sgu_b, w_out_odd, ffn_norm,
           w_router_group, w_router_expert, w_exp_gate, w_exp_up, w_exp_down):
    b, s, d = x.shape
    n = b * s
    depth = ffn_norm.shape[0]
    heads = MIX_WIDTH // HEAD_DIM
    h = x.reshape(n, d)
    w_gate = w_exp_gate.reshape(depth * N_EXPERTS, d, EXPERT_FF)
    w_up = w_exp_up.reshape(depth * N_EXPERTS, d, EXPERT_FF)
    w_down = w_exp_down.reshape(depth * N_EXPERTS, EXPERT_FF, d)
    for layer in range(depth):
        i = layer // 2
        ffn_gain = ffn_norm[layer][None, :]
        w_router = _router_weights(w_router_group[layer], w_router_expert[layer])
        if layer % 2 == 0:
            q, k_pad, v_pad, p = _in_even(
                h.reshape(b, s, d), mix_norm_even[i][None, :], w_in_even[i].astype(BF16),
                jnp.tile(att_q_norm[i], heads)[None, :], jnp.tile(att_k_norm[i], heads)[None, :],
                tm=ATT_LEFT)
            ya = _band_attention(q, k_pad, v_pad, _band_bias(att_rel_bias[i]))
            h, xn, route = _out_even(ya.reshape(n, MIX_WIDTH), p.reshape(n, MIX_WIDTH), h,
                                     pool_w[i].astype(BF16), pool_scale[i][None, :],
                                     w_out_even[i].astype(BF16), ffn_gain, w_router, seq=s, tm=512)
        else:
            u, vn, q, k, v = _in_odd(h, mix_norm_odd[i][None, :], w_in_odd[i].astype(BF16),
                                     sgu_v_norm[i][None, :], tm=512)
            to3 = lambda t: t.reshape(b, s, MIX_WIDTH)
            yd = _sb_attention(to3(q), to3(k), to3(v))
            bias = jnp.broadcast_to(sgu_b[i][:, :, None], (N_GROUPS, SGU_BLOCK, LANES))
            h, xn, route = _out_odd(u, vn, yd.reshape(n, MIX_WIDTH), h, sgu_w[i].astype(BF16), bias,
                                    w_out_odd[i].astype(BF16), ffn_gain, w_router, tm=512)
        h = _moe_layer(h, xn, route, w_gate, w_up, w_down, layer)
    return h.reshape(b, s, d)
```

```python
import functools
import math

import jax
import jax.numpy as jnp
from jax import lax
from jax.experimental import pallas as pl
from jax.experimental.pallas import tpu as pltpu

F32 = jnp.float32
BF16 = jnp.bfloat16

D_MODEL = 1024
CHUNK = 64
EPS = 1e-6
HEAD_DIM = 64
MIX_WIDTH = 512
LANES = 128
ATT_LEFT = 8 * CHUNK
ATT_MAX_REL = 128
POOL_WINDOWS = (2, 4, 8, 16)
POOL_HALO = 16
SGU_BLOCK = 128
N_GROUPS = 4
N_EXP_PER_GROUP = 8
N_EXPERTS = N_GROUPS * N_EXP_PER_GROUP
EXPERT_FF = 256
ROUTER_LANES = 128
ROUTER_ROWS = 40
NEG_BIG = -1e30
VMEM_LIMIT = 56 * 1024 * 1024

NT_DIMS = (((1,), (1,)), ((), ()))
LOG2E = math.log2(math.e)
Q_SCALE = LOG2E / math.sqrt(HEAD_DIM)


def _cparams(sem):
    return pltpu.CompilerParams(dimension_semantics=sem, vmem_limit_bytes=VMEM_LIMIT)


def _rms(x, gain):
    return x * lax.rsqrt(jnp.mean(x * x, axis=-1, keepdims=True) + EPS) * gain


def _split_dot(x, m):
    hi = x.astype(BF16)
    lo = (x - hi.astype(F32)).astype(BF16)
    return (jnp.dot(hi, m, preferred_element_type=F32)
            + jnp.dot(lo, m, preferred_element_type=F32))


def _head_rms(t, gain):
    n = t.shape[-1]
    r = lax.broadcasted_iota(jnp.int32, (n, n), 0) // HEAD_DIM
    c = lax.broadcasted_iota(jnp.int32, (n, n), 1) // HEAD_DIM
    bd = jnp.where(r == c, 1.0, 0.0).astype(BF16)
    ms = _split_dot(t * t, bd) * (1.0 / HEAD_DIM)
    return t * lax.rsqrt(ms + EPS) * gain


def _in_even_body(h_ref, g_ref, w_ref, qg_ref, kg_ref, q_ref, k_ref, v_ref, p_ref):
    j = pl.program_id(1)

    @pl.when(j == 0)
    def _():
        k_ref[...] = jnp.zeros_like(k_ref)
        v_ref[...] = jnp.zeros_like(v_ref)

    @pl.when(j > 0)
    def _():
        xn = _rms(h_ref[0], g_ref[...])
        proj = jnp.dot(xn.astype(BF16), w_ref[...], preferred_element_type=F32)
        w = MIX_WIDTH
        q_ref[0] = (_head_rms(proj[:, :w], qg_ref[...]) * Q_SCALE).astype(BF16)
        k_ref[0] = _head_rms(proj[:, w:2 * w], kg_ref[...]).astype(BF16)
        v_ref[0] = proj[:, 2 * w:3 * w].astype(BF16)
        p_ref[0] = proj[:, 3 * w:].astype(BF16)


def _in_even(h, gain, w, q_gain, k_gain, tm):
    b, s, d = h.shape
    assert tm == ATT_LEFT and s % tm == 0
    nt = s // tm
    cur = lambda bi, j: (bi, jnp.maximum(j - 1, 0), 0)
    const = lambda bi, j: (0, 0)
    out_sds = lambda rows: jax.ShapeDtypeStruct((b, rows, MIX_WIDTH), BF16)
    return pl.pallas_call(
        _in_even_body,
        grid=(b, nt + 1),
        in_specs=[
            pl.BlockSpec((1, tm, d), cur),
            pl.BlockSpec((1, d), const),
            pl.BlockSpec(w.shape, const),
            pl.BlockSpec((1, MIX_WIDTH), const),
            pl.BlockSpec((1, MIX_WIDTH), const),
        ],
        out_specs=[
            pl.BlockSpec((1, tm, MIX_WIDTH), cur),
            pl.BlockSpec((1, tm, MIX_WIDTH), lambda bi, j: (bi, j, 0)),
            pl.BlockSpec((1, tm, MIX_WIDTH), lambda bi, j: (bi, j, 0)),
            pl.BlockSpec((1, tm, MIX_WIDTH), cur),
        ],
        out_shape=[out_sds(s), out_sds(s + ATT_LEFT), out_sds(s + ATT_LEFT), out_sds(s)],
        compiler_params=_cparams(("arbitrary", "arbitrary")),
        name="in_even",
    )(h, gain, w, q_gain, k_gain)


BAND_TQ = 2 * CHUNK
BAND_TK = BAND_TQ + ATT_LEFT


def _band_body(q_ref, k_ref, v_ref, bias_ref, o_ref):
    i = pl.program_id(1)
    start = pl.multiple_of(i * BAND_TQ, BAND_TQ)
    lane = lax.broadcasted_iota(jnp.int32, (BAND_TQ, LANES), 1)
    col = lax.broadcasted_iota(jnp.int32, (2 * BAND_TQ, BAND_TK), 1)
    is_pad = (col + start) < ATT_LEFT
    pairs = range(MIX_WIDTH // LANES)
    lanes = [slice(hp * LANES, (hp + 1) * LANES) for hp in pairs]
    scores = []
    for hp in pairs:
        q = q_ref[0, :, lanes[hp]]
        kb = k_ref[0, pl.ds(start, BAND_TK), lanes[hp]]
        zero = jnp.zeros_like(q)
        q2 = jnp.concatenate([jnp.where(lane < HEAD_DIM, q, zero), jnp.where(lane < HEAD_DIM, zero, q)], axis=0)
        scores.append(lax.dot_general(q2, kb, NT_DIMS, preferred_element_type=F32))
    probs, denoms = [], []
    for hp in pairs:
        bias = bias_ref[2 * hp:2 * hp + 2].reshape(2 * BAND_TQ, BAND_TK)
        s = jnp.where(is_pad, NEG_BIG, scores[hp] + bias)
        p = jnp.exp2(s - jnp.max(s, axis=-1, keepdims=True))
        denoms.append(jnp.sum(p, axis=-1, keepdims=True))
        probs.append(p.astype(BF16))
    for hp in pairs:
        vb = v_ref[0, pl.ds(start, BAND_TK), lanes[hp]]
        o = jnp.dot(probs[hp], vb, preferred_element_type=F32) / denoms[hp]
        o_ref[0, :, lanes[hp]] = jnp.where(lane < HEAD_DIM, o[:BAND_TQ], o[BAND_TQ:]).astype(BF16)


def _band_bias(rel_bias):
    heads = rel_bias.shape[0]
    r = jnp.arange(BAND_TQ)[:, None]
    j = jnp.arange(BAND_TK)[None, :]
    jb = j - CHUNK * (r // CHUNK)
    in_band = (jb >= 0) & (jb < ATT_LEFT + CHUNK)
    period = BAND_TK + BAND_TQ
    far = jnp.broadcast_to(rel_bias[:, 2 * ATT_MAX_REL:], (heads, ATT_LEFT - ATT_MAX_REL + 1))
    near = rel_bias[:, 2 * ATT_MAX_REL - 1:0:-1]
    wrap = jnp.broadcast_to(rel_bias[:, 2 * ATT_MAX_REL:], (heads, period - BAND_TK))
    g = jnp.concatenate([far, near, wrap], axis=1).astype(F32)
    assert g.shape[1] == period
    toep = jnp.tile(g, (1, BAND_TQ))[:, :BAND_TQ * (period - 1)].reshape(heads, BAND_TQ, period - 1)
    return jnp.where(in_band[None], toep[:, :, :BAND_TK] * LOG2E, NEG_BIG)


def _band_attention(q, k_pad, v_pad, bias):
    b, s, w = q.shape
    sp = k_pad.shape[1]
    return pl.pallas_call(
        _band_body,
        grid=(b, s // BAND_TQ),
        in_specs=[
            pl.BlockSpec((1, BAND_TQ, w), lambda bi, i: (bi, i, 0)),
            pl.BlockSpec((1, sp, w), lambda bi, i: (bi, 0, 0)),
            pl.BlockSpec((1, sp, w), lambda bi, i: (bi, 0, 0)),
            pl.BlockSpec(bias.shape, lambda bi, i: (0, 0, 0)),
        ],
        out_specs=pl.BlockSpec((1, BAND_TQ, w), lambda bi, i: (bi, i, 0)),
        out_shape=jax.ShapeDtypeStruct((b, s, w), BF16),
        compiler_params=_cparams(("arbitrary", "arbitrary")),
        name="band_attention",
    )(q, k_pad, v_pad, bias)


def _route_tokens(h, gain, w_router):
    xn = _rms(h, gain)
    x_hi = xn.astype(BF16)
    x_lo = (xn - x_hi.astype(F32)).astype(BF16)
    w_hi = w_router.astype(BF16)
    w_lo = (w_router - w_hi.astype(F32)).astype(BF16)
    logits = (jnp.dot(x_hi, w_hi, preferred_element_type=F32)
              + jnp.dot(x_lo, w_hi, preferred_element_type=F32)
              + jnp.dot(x_hi, w_lo, preferred_element_type=F32))
    lt = logits.T[:ROUTER_ROWS]
    sub = lax.broadcasted_iota(jnp.int32, lt.shape, 0).astype(F32)
    ninf = -jnp.inf

    def top(vals):
        m = jnp.max(vals, axis=0, keepdims=True)
        idx = jnp.min(jnp.where(vals == m, sub, float(ROUTER_LANES)), axis=0, keepdims=True)
        return m, idx

    is_group = sub < N_GROUPS
    g_max, g_sel = top(jnp.where(is_group, lt, ninf))
    g_den = jnp.sum(jnp.where(is_group, jnp.exp(lt - g_max), 0.0), axis=0, keepdims=True)
    g_weight = 1.0 / g_den
    lo = N_GROUPS + N_EXP_PER_GROUP * g_sel
    e_logits = jnp.where((sub >= lo) & (sub < lo + N_EXP_PER_GROUP), lt, ninf)
    e1, i1 = top(e_logits)
    e2, i2 = top(jnp.where(sub == i1, ninf, e_logits))
    t = jnp.exp(e2 - e1)
    w1 = g_weight / (1.0 + t)
    w2 = g_weight * t / (1.0 + t)
    rows = lax.broadcasted_iota(jnp.int32, (ROUTER_LANES, lt.shape[1]), 0)
    route_t = jnp.where(rows == 0, i1 - N_GROUPS, jnp.where(rows == 1, i2 - N_GROUPS, 0.0))
    route_t = jnp.where(rows == 2, w1, jnp.where(rows == 3, w2, route_t))
    return xn, route_t.T


def _router_weights(w_rg, w_re):
    pad = jnp.zeros((w_rg.shape[0], ROUTER_LANES - N_GROUPS - N_EXPERTS), F32)
    return jnp.concatenate([w_rg, w_re, pad], axis=1)


def _out_even_body(tiles_per_seq, ya_ref, p_ref, halo_ref, h_ref, pw_ref, ps_ref, wo_ref, fg_ref, wr_ref,
                   o_ref, xn_ref, route_ref, p_scr, yb_scr):
    tm = p_ref.shape[0]
    it = pl.program_id(0) % tiles_per_seq
    halo = halo_ref[...].astype(F32)
    p_scr[0:POOL_HALO, :] = jnp.where(it == 0, jnp.zeros_like(halo), halo)
    p_scr[POOL_HALO:, :] = p_ref[...].astype(F32)
    t = it * tm + lax.broadcasted_iota(jnp.int32, (tm, 1), 0)
    for g, win in enumerate(POOL_WINDOWS):
        ls = slice(g * LANES, (g + 1) * LANES)
        cur = p_scr[POOL_HALO:POOL_HALO + tm, ls]
        acc = cur
        for dlt in range(1, win):
            acc = acc + p_scr[POOL_HALO - dlt:POOL_HALO - dlt + tm, ls]
        cnt = jnp.minimum(t + 1, win).astype(F32)
        mixed = acc / cnt - cur
        yb = jnp.dot(mixed.astype(BF16), pw_ref[g], preferred_element_type=F32) * ps_ref[:, ls]
        yb_scr[:, ls] = yb.astype(BF16)
    w = MIX_WIDTH
    h_new = (h_ref[...]
             + jnp.dot(ya_ref[...], wo_ref[0:w, :], preferred_element_type=F32)
             + jnp.dot(yb_scr[...], wo_ref[w:, :], preferred_element_type=F32))
    o_ref[...] = h_new
    xn_ref[...], route_ref[...] = _route_tokens(h_new, fg_ref[...], wr_ref[...])


def _out_even(ya, p, h, pool_w, pool_scale, w_out, ffn_gain, w_router, seq, tm):
    n, d = h.shape
    w = MIX_WIDTH
    row = lambda i: (i, 0)
    const2 = lambda i: (0, 0)
    halo_blocks = tm // POOL_HALO
    return pl.pallas_call(
        functools.partial(_out_even_body, seq // tm),
        grid=(n // tm,),
        in_specs=[
            pl.BlockSpec((tm, w), row),
            pl.BlockSpec((tm, w), row),
            pl.BlockSpec((POOL_HALO, w), lambda i: (jnp.maximum(i * halo_blocks - 1, 0), 0)),
            pl.BlockSpec((tm, d), row),
            pl.BlockSpec(pool_w.shape, lambda i: (0, 0, 0)),
            pl.BlockSpec((1, w), const2),
            pl.BlockSpec(w_out.shape, const2),
            pl.BlockSpec((1, d), const2),
            pl.BlockSpec(w_router.shape, const2),
        ],
        out_specs=[pl.BlockSpec((tm, d), row), pl.BlockSpec((tm, d), row),
                   pl.BlockSpec((tm, ROUTER_LANES), row)],
        out_shape=[jax.ShapeDtypeStruct((n, d), F32), jax.ShapeDtypeStruct((n, d), F32),
                   jax.ShapeDtypeStruct((n, ROUTER_LANES), F32)],
        scratch_shapes=[pltpu.VMEM((tm + POOL_HALO, w), F32), pltpu.VMEM((tm, w), BF16)],
        compiler_params=_cparams(("arbitrary",)),
        name="out_even",
    )(ya, p, p, h, pool_w, pool_scale, w_out, ffn_gain, w_router)


MOE_TM = 256
MOE_TILE_LANES = 256
SUBLANES = 8


def _moe_tiles(n):
    return (2 * n) // MOE_TM + N_EXPERTS


def _exact_dot_nt(ones, x):
    out = None
    for _ in range(3):
        part = x.astype(BF16)
        x = x - part.astype(F32)
        term = lax.dot_general(ones, part, NT_DIMS, preferred_element_type=F32)
        out = term if out is None else out + term
    return out


def _slots_body(n_tiles, route_ref, pos1_ref, pos2_ref, tile_ref, run_scr, start_scr):
    phase = pl.program_id(0)
    i = pl.program_id(1)
    tm = route_ref.shape[0]
    route = route_ref[...]
    lane = lax.broadcasted_iota(jnp.int32, (tm, ROUTER_LANES), 1).astype(F32)
    pick1 = jnp.where(lane == route[:, 0:1], 1.0, 0.0)
    pick2 = jnp.where(lane == route[:, 1:2], 1.0, 0.0)
    occ = (pick1 + pick2).astype(BF16)
    ones_rows = jnp.ones((SUBLANES, tm), BF16)
    ones_lanes = jnp.ones((SUBLANES, ROUTER_LANES), BF16)

    @pl.when(i == 0)
    def _():
        run_scr[...] = jnp.zeros_like(run_scr)

    @pl.when(phase == 0)
    def _():
        run_scr[...] += jnp.dot(ones_rows, occ, preferred_element_type=F32)

        @pl.when(i == pl.num_programs(1) - 1)
        def _():
            padded = jnp.floor((run_scr[...] + (MOE_TM - 1)) * (1.0 / MOE_TM)) * MOE_TM
            r = lax.broadcasted_iota(jnp.int32, (ROUTER_LANES, ROUTER_LANES), 0)
            c = lax.broadcasted_iota(jnp.int32, (ROUTER_LANES, ROUTER_LANES), 1)
            before = jnp.where(r < c, 1.0, 0.0).astype(BF16)
            hi = padded.astype(BF16)
            mid = (padded - hi.astype(F32)).astype(BF16)
            low = (padded - hi.astype(F32) - mid.astype(F32)).astype(BF16)
            start = (jnp.dot(hi, before, preferred_element_type=F32)
                     + jnp.dot(mid, before, preferred_element_type=F32)
                     + jnp.dot(low, before, preferred_element_type=F32))
            start_scr[...] = start
            seg_end = start[0:1, :] + padded[0:1, :]
            tile_lo = (lax.broadcasted_iota(jnp.int32, (MOE_TILE_LANES, ROUTER_LANES), 0) * MOE_TM).astype(F32)
            e_lane = lax.broadcasted_iota(jnp.int32, (MOE_TILE_LANES, ROUTER_LANES), 1)
            ended = jnp.where((seg_end <= tile_lo) & (e_lane < N_EXPERTS), 1.0, 0.0).astype(BF16)
            tile_ref[...] = lax.dot_general(ones_lanes, ended, NT_DIMS, preferred_element_type=F32)

    @pl.when(phase == 1)
    def _():
        r = lax.broadcasted_iota(jnp.int32, (tm, tm), 0)
        c = lax.broadcasted_iota(jnp.int32, (tm, tm), 1)
        earlier = jnp.where(c < r, 1.0, 0.0).astype(BF16)
        base = (jnp.dot(earlier, occ, preferred_element_type=F32)
                + run_scr[0:1, :] + start_scr[0:1, :])
        pos1_ref[...] = _exact_dot_nt(ones_lanes, pick1 * base)
        pos2_ref[...] = _exact_dot_nt(ones_lanes, pick2 * base)
        run_scr[...] += jnp.dot(ones_rows, occ, preferred_element_type=F32)


def _slots(route, tm):
    n = route.shape[0]
    n_tiles = _moe_tiles(n)
    assert n_tiles <= MOE_TILE_LANES and 2 * n + N_EXPERTS * MOE_TM < 2 ** 24
    row_out = pl.BlockSpec((SUBLANES, tm), lambda ph, i: (0, i * ph))
    sds = jax.ShapeDtypeStruct((SUBLANES, n), F32)
    return pl.pallas_call(
        functools.partial(_slots_body, n_tiles),
        grid=(2, n // tm),
        in_specs=[pl.BlockSpec((tm, ROUTER_LANES), lambda ph, i: (i, 0))],
        out_specs=[row_out, row_out, pl.BlockSpec((SUBLANES, MOE_TILE_LANES), lambda ph, i: (0, 0))],
        out_shape=[sds, sds, jax.ShapeDtypeStruct((SUBLANES, MOE_TILE_LANES), F32)],
        scratch_shapes=[pltpu.VMEM((SUBLANES, ROUTER_LANES), F32), pltpu.VMEM((SUBLANES, ROUTER_LANES), F32)],
        compiler_params=_cparams(("arbitrary", "arbitrary")),
        name="moe_slots",
    )(route)


def _slot_tokens_body(pos_ref, zeros_hbm, tok_ref, sem):
    n = pos_ref.shape[0] // 2
    fill = pltpu.make_async_copy(zeros_hbm, tok_ref, sem)
    fill.start()
    fill.wait()

    @pl.loop(0, n // SUBLANES)
    def _(g):
        for u in range(SUBLANES):
            t = g * SUBLANES + u
            tok_ref[pos_ref[t]] = t
            tok_ref[pos_ref[n + t]] = t


def _slot_tokens(pos, n_slots):
    return pl.pallas_call(
        _slot_tokens_body,
        grid_spec=pltpu.PrefetchScalarGridSpec(
            num_scalar_prefetch=1,
            grid=(1,),
            in_specs=[pl.BlockSpec(memory_space=pl.ANY)],
            out_specs=pl.BlockSpec(memory_space=pltpu.SMEM),
            scratch_shapes=[pltpu.SemaphoreType.DMA(())],
        ),
        out_shape=jax.ShapeDtypeStruct((n_slots,), jnp.int32),
        compiler_params=_cparams(("arbitrary",)),
        name="moe_slot_tokens",
    )(pos, jnp.zeros((n_slots,), jnp.int32))


def _experts_body(tile_ref, tok_ref, xn_hbm, wg_ref, wu_ref, wd_ref, ys_ref, xbuf, sems):
    i = pl.program_id(0)
    groups = MOE_TM // SUBLANES

    def used(t):
        return tile_ref[t] < N_EXPERTS

    def start_gather(t, slot):
        base = t * MOE_TM

        def issue(g, carry):
            for u in range(SUBLANES):
                tok = tok_ref[base + g * SUBLANES + u]
                pltpu.make_async_copy(xn_hbm.at[pl.ds(tok, 1), :], xbuf.at[slot, g, pl.ds(u, 1), :],
                                      sems.at[slot]).start()
            return carry

        lax.fori_loop(0, groups, issue, 0)

    @pl.when((i == 0) & used(0))
    def _():
        start_gather(0, 0)

    @pl.when(used(i + 1) & (i + 1 < pl.num_programs(0)))
    def _():
        start_gather(i + 1, (i + 1) % 2)

    @pl.when(used(i))
    def _():
        slot = i % 2
        pltpu.make_async_copy(xbuf.at[slot], xbuf.at[slot], sems.at[slot]).wait()
        x = xbuf[slot].reshape(MOE_TM, xbuf.shape[-1]).astype(BF16)
        gate = jnp.dot(x, wg_ref[0].astype(BF16), preferred_element_type=F32)
        up = jnp.dot(x, wu_ref[0].astype(BF16), preferred_element_type=F32)
        hid = gate * jax.nn.sigmoid(gate) * up
        ys_ref[...] = jnp.dot(hid.astype(BF16), wd_ref[0].astype(BF16), preferred_element_type=F32)

    @pl.when(jnp.logical_not(used(i)))
    def _():
        ys_ref[...] = jnp.zeros_like(ys_ref)


def _experts(tile_map, slot_tokens, xn, w_gate, w_up, w_down, layer):
    n, d = xn.shape
    f = EXPERT_FF
    n_tiles = _moe_tiles(n)
    assert slot_tokens.shape[0] == n_tiles * MOE_TM
    w_map = lambda i, tm_ref, tk_ref: (layer * N_EXPERTS + jnp.minimum(tm_ref[i], N_EXPERTS - 1), 0, 0)
    return pl.pallas_call(
        _experts_body,
        grid_spec=pltpu.PrefetchScalarGridSpec(
            num_scalar_prefetch=2,
            grid=(n_tiles,),
            in_specs=[pl.BlockSpec(memory_space=pl.ANY), pl.BlockSpec((1, d, f), w_map),
                      pl.BlockSpec((1, d, f), w_map), pl.BlockSpec((1, f, d), w_map)],
            out_specs=pl.BlockSpec((MOE_TM, d), lambda i, tm_ref, tk_ref: (i, 0)),
            scratch_shapes=[pltpu.VMEM((2, MOE_TM // SUBLANES, SUBLANES, d), F32),
                            pltpu.SemaphoreType.DMA((2,))],
        ),
        out_shape=jax.ShapeDtypeStruct((n_tiles * MOE_TM, d), F32),
        compiler_params=_cparams(("arbitrary",)),
        name="moe_experts",
    )(tile_map, slot_tokens, xn, w_gate, w_up, w_down)


def _combine_body(pos_ref, h_ref, route_ref, ys_hbm, o_ref, buf, sems):
    tm = h_ref.shape[0]
    steps = pl.num_programs(0)
    n = steps * tm
    i = pl.program_id(0)

    def start_gather(step, slot):
        base = step * tm

        def issue(g, carry):
            for u in range(SUBLANES):
                for pick in range(2):
                    slot_row = pos_ref[pick * n + base + g * SUBLANES + u]
                    pltpu.make_async_copy(ys_hbm.at[pl.ds(slot_row, 1), :],
                                          buf.at[slot, pick, g, pl.ds(u, 1), :], sems.at[slot]).start()
            return carry

        lax.fori_loop(0, tm // SUBLANES, issue, 0)

    @pl.when(i == 0)
    def _():
        start_gather(0, 0)

    @pl.when(i + 1 < steps)
    def _():
        start_gather(i + 1, (i + 1) % 2)

    slot = i % 2
    pltpu.make_async_copy(buf.at[slot], buf.at[slot], sems.at[slot]).wait()
    route = route_ref[...]
    y1 = buf[slot, 0].reshape(tm, buf.shape[-1])
    y2 = buf[slot, 1].reshape(tm, buf.shape[-1])
    o_ref[...] = h_ref[...] + route[:, 2:3] * y1 + route[:, 3:4] * y2


def _combine(pos, h, route, ys, tm):
    n, d = h.shape
    row = lambda i, p: (i, 0)
    return pl.pallas_call(
        _combine_body,
        grid_spec=pltpu.PrefetchScalarGridSpec(
            num_scalar_prefetch=1,
            grid=(n // tm,),
            in_specs=[pl.BlockSpec((tm, d), row), pl.BlockSpec((tm, ROUTER_LANES), row),
                      pl.BlockSpec(memory_space=pl.ANY)],
            out_specs=pl.BlockSpec((tm, d), row),
            scratch_shapes=[pltpu.VMEM((2, 2, tm // SUBLANES, SUBLANES, d), F32),
                            pltpu.SemaphoreType.DMA((2,))],
        ),
        out_shape=jax.ShapeDtypeStruct((n, d), F32),
        compiler_params=_cparams(("arbitrary",)),
        name="moe_combine",
    )(pos, h, route, ys)


def _gelu(x):
    return 0.5 * x * (1.0 + lax.erf(x * (1.0 / math.sqrt(2.0))))


def _in_odd_body(h_ref, g_ref, w_ref, vg_ref, u_ref, vn_ref, q_ref, k_ref, v_ref):
    xn = _rms(h_ref[...], g_ref[...])
    proj = jnp.dot(xn.astype(BF16), w_ref[...], preferred_element_type=F32)
    w = MIX_WIDTH
    u_ref[...] = _gelu(proj[:, :w]).astype(BF16)
    vn_ref[...] = _rms(_gelu(proj[:, w:2 * w]), vg_ref[...]).astype(BF16)
    q_ref[...] = (proj[:, 2 * w:3 * w] * Q_SCALE).astype(BF16)
    k_ref[...] = proj[:, 3 * w:4 * w].astype(BF16)
    v_ref[...] = proj[:, 4 * w:].astype(BF16)


def _in_odd(h, gain, w, v_gain, tm):
    n, d = h.shape
    row = lambda i: (i, 0)
    const2 = lambda i: (0, 0)
    sds = jax.ShapeDtypeStruct((n, MIX_WIDTH), BF16)
    return pl.pallas_call(
        _in_odd_body,
        grid=(n // tm,),
        in_specs=[pl.BlockSpec((tm, d), row), pl.BlockSpec((1, d), const2),
                  pl.BlockSpec(w.shape, const2), pl.BlockSpec((1, MIX_WIDTH), const2)],
        out_specs=[pl.BlockSpec((tm, MIX_WIDTH), row)] * 5,
        out_shape=[sds] * 5,
        compiler_params=_cparams(("arbitrary",)),
        name="in_odd",
    )(h, gain, w, v_gain)


SB_T = 128
SB_NB = 3
SB_TK = SB_NB * SB_T
SB_QBLK = 512
SB_GROUP = 2
SB_UNDERFLOW = -150.0


def _sb_body(q_ref, k_ref, v_ref, o_ref, acc_scr, run_scr):
    qi = pl.program_id(2)
    lane = lax.broadcasted_iota(jnp.int32, (SB_T, LANES), 1)
    row = lax.broadcasted_iota(jnp.int32, (2 * SB_T, 1), 0) % SB_T
    col = lax.broadcasted_iota(jnp.int32, (2 * SB_T, SB_TK), 1)
    rr = lax.broadcasted_iota(jnp.int32, (2 * SB_T, 2 * SB_T), 0) % SB_T
    cc = lax.broadcasted_iota(jnp.int32, (2 * SB_T, 2 * SB_T), 1)
    suffix = jnp.where((cc >= SB_T) | (rr > cc), 1.0, 0.0).astype(BF16)

    def suffix_sums(x):
        hi = x.astype(BF16)
        lo = (x - hi.astype(F32)).astype(BF16)
        return jnp.dot(jnp.concatenate([hi, lo], axis=1), suffix, preferred_element_type=F32)

    def subtile_group(grp, _):
        q_los = [pl.multiple_of((grp * SB_GROUP + s) * SB_T, SB_T) for s in range(SB_GROUP)]
        q_starts = [qi * SB_QBLK + q_lo for q_lo in q_los]
        q_heads = []
        for q_lo in q_los:
            q = q_ref[0, pl.ds(q_lo, SB_T), :]
            zero = jnp.zeros_like(q)
            q_heads.append(jnp.concatenate(
                [jnp.where(lane < HEAD_DIM, q, zero), jnp.where(lane < HEAD_DIM, zero, q)], axis=0))
        acc_scr[...] = jnp.zeros_like(acc_scr)
        run_scr[...] = jnp.zeros_like(run_scr)

        def cond(carry):
            his, dones = carry
            active = [(hi > 0) & (done == 0) for hi, done in zip(his, dones)]
            return functools.reduce(jnp.logical_or, active)

        def body(carry):
            his, _ = carry
            group = range(SB_GROUP)
            kss = [pl.multiple_of(jnp.maximum(his[s] - SB_TK, 0), SB_T) for s in group]
            valid = [col < (jnp.minimum(row + q_starts[s], his[s]) - kss[s]) for s in group]
            zs = [lax.dot_general(q_heads[s], k_ref[0, pl.ds(kss[s], SB_TK), :], NT_DIMS,
                                  preferred_element_type=F32) for s in group]
            log_beta, log_rest = [], []
            for s in group:
                z = jnp.where(valid[s], zs[s], NEG_BIG)
                sp = jnp.log2(1.0 + jnp.exp2(-jnp.abs(z)))
                log_beta.append(jnp.minimum(z, 0.0) - sp)
                log_rest.append(log_beta[s] - z)
            sums = [[suffix_sums(log_rest[s][:, blk * SB_T:(blk + 1) * SB_T]) for blk in range(SB_NB)]
                    for s in group]
            dones = []
            for s in group:
                run = run_scr[s]
                pieces = [None] * SB_NB
                for blk in reversed(range(SB_NB)):
                    tt = sums[s][blk]
                    pieces[blk] = jnp.exp2(log_beta[s][:, blk * SB_T:(blk + 1) * SB_T] + tt[:, :SB_T] + run)
                    run = run + tt[:, SB_T:]
                att = jnp.concatenate(pieces, axis=1)
                acc_scr[s] += jnp.dot(att.astype(BF16), v_ref[0, pl.ds(kss[s], SB_TK), :],
                                      preferred_element_type=F32)
                run_scr[s] = run
                dones.append((jnp.max(run) <= SB_UNDERFLOW).astype(jnp.int32))
            return tuple(kss), tuple(dones)

        lax.while_loop(cond, body, (tuple(qs + SB_T for qs in q_starts),
                                    tuple(jnp.int32(0) for _ in range(SB_GROUP))))
        for s in range(SB_GROUP):
            o_ref[0, pl.ds(q_los[s], SB_T), :] = jnp.where(
                lane < HEAD_DIM, acc_scr[s, :SB_T], acc_scr[s, SB_T:]).astype(BF16)
        return 0

    lax.fori_loop(0, SB_QBLK // (SB_T * SB_GROUP), subtile_group, 0)


def _sb_attention(q, k, v):
    b, s, w = q.shape
    assert s % SB_QBLK == 0 and s >= SB_TK
    qspec = pl.BlockSpec((1, SB_QBLK, LANES), lambda bi, hp, i: (bi, i, hp))
    kvspec = pl.BlockSpec((1, s, LANES), lambda bi, hp, i: (bi, 0, hp))
    return pl.pallas_call(
        _sb_body,
        grid=(b, w // LANES, s // SB_QBLK),
        in_specs=[qspec, kvspec, kvspec],
        out_specs=qspec,
        out_shape=jax.ShapeDtypeStruct((b, s, w), BF16),
        scratch_shapes=[pltpu.VMEM((SB_GROUP, 2 * SB_T, LANES), F32),
                        pltpu.VMEM((SB_GROUP, 2 * SB_T, LANES), F32)],
        compiler_params=_cparams(("arbitrary", "arbitrary", "arbitrary")),
        name="sb_attention",
    )(q, k, v)


def _out_odd_body(u_ref, vn_ref, yd_ref, h_ref, ws_ref, bs_ref, wo_ref, fg_ref, wr_ref,
                  o_ref, xn_ref, route_ref, yc_scr):
    tm = u_ref.shape[0]
    r = lax.broadcasted_iota(jnp.int32, (SGU_BLOCK, SGU_BLOCK), 0)
    c = lax.broadcasted_iota(jnp.int32, (SGU_BLOCK, SGU_BLOCK), 1)
    for g in range(MIX_WIDTH // LANES):
        ls = slice(g * LANES, (g + 1) * LANES)
        ws = jnp.where(c <= r, ws_ref[g], jnp.zeros_like(ws_ref[g]))
        for blk in range(tm // SGU_BLOCK):
            rs = slice(blk * SGU_BLOCK, (blk + 1) * SGU_BLOCK)
            mixed = jnp.dot(ws, vn_ref[rs, ls], preferred_element_type=F32) + bs_ref[g]
            yc_scr[rs, ls] = (u_ref[rs, ls].astype(F32) * mixed).astype(BF16)
    w = MIX_WIDTH
    h_new = (h_ref[...]
             + jnp.dot(yc_scr[...], wo_ref[0:w, :], preferred_element_type=F32)
             + jnp.dot(yd_ref[...], wo_ref[w:, :], preferred_element_type=F32))
    o_ref[...] = h_new
    xn_ref[...], route_ref[...] = _route_tokens(h_new, fg_ref[...], wr_ref[...])


def _out_odd(u, vn, yd, h, sgu_w, sgu_b, w_out, ffn_gain, w_router, tm):
    n, d = h.shape
    w = MIX_WIDTH
    row = lambda i: (i, 0)
    const2 = lambda i: (0, 0)
    const3 = lambda i: (0, 0, 0)
    return pl.pallas_call(
        _out_odd_body,
        grid=(n // tm,),
        in_specs=[pl.BlockSpec((tm, w), row), pl.BlockSpec((tm, w), row), pl.BlockSpec((tm, w), row),
                  pl.BlockSpec((tm, d), row), pl.BlockSpec(sgu_w.shape, const3),
                  pl.BlockSpec(sgu_b.shape, const3), pl.BlockSpec(w_out.shape, const2),
                  pl.BlockSpec((1, d), const2), pl.BlockSpec(w_router.shape, const2)],
        out_specs=[pl.BlockSpec((tm, d), row), pl.BlockSpec((tm, d), row),
                   pl.BlockSpec((tm, ROUTER_LANES), row)],
        out_shape=[jax.ShapeDtypeStruct((n, d), F32), jax.ShapeDtypeStruct((n, d), F32),
                   jax.ShapeDtypeStruct((n, ROUTER_LANES), F32)],
        scratch_shapes=[pltpu.VMEM((tm, w), BF16)],
        compiler_params=_cparams(("arbitrary",)),
        name="out_odd",
    )(u, vn, yd, h, sgu_w, sgu_b, w_out, ffn_gain, w_router)


def _moe_layer(h, xn, route, w_gate, w_up, w_down, layer):
    pos1, pos2, tile_map = _slots(route, tm=512)
    pos = jnp.concatenate([pos1[0], pos2[0]]).astype(jnp.int32)
    tile_map = tile_map[0].astype(jnp.int32)
    slot_tokens = _slot_tokens(pos, _moe_tiles(h.shape[0]) * MOE_TM)
    ys = _experts(tile_map, slot_tokens, xn, w_gate, w_up, w_down, layer)
    return _combine(pos, h, route, ys, tm=256)


def kernel(x, mix_norm_even, w_in_even, att_q_norm, att_k_norm, att_rel_bias, pool_w, pool_scale,
           w_out_even, mix_norm_odd, w_in_odd, sgu_v_norm, sgu_w, sgu_b, w_out_odd, ffn_norm,
           w_router_group, w_router_expert, w_exp_gate, w_exp_up, w_exp_down):
    b, s, d = x.shape
    n = b * s
    depth = ffn_norm.shape[0]
    heads = MIX_WIDTH // HEAD_DIM
    h = x.reshape(n, d)
    w_gate = w_exp_gate.reshape(depth * N_EXPERTS, d, EXPERT_FF)
    w_up = w_exp_up.reshape(depth * N_EXPERTS, d, EXPERT_FF)
    w_down = w_exp_down.reshape(depth * N_EXPERTS, EXPERT_FF, d)
    for layer in range(depth):
        i = layer // 2
        ffn_gain = ffn_norm[layer][None, :]
        w_router = _router_weights(w_router_group[layer], w_router_expert[layer])
        if layer % 2 == 0:
            q, k_pad, v_pad, p = _in_even(
                h.reshape(b, s, d), mix_norm_even[i][None, :], w_in_even[i].astype(BF16),
                jnp.tile(att_q_norm[i], heads)[None, :], jnp.tile(att_k_norm[i], heads)[None, :],
                tm=ATT_LEFT)
            ya = _band_attention(q, k_pad, v_pad, _band_bias(att_rel_bias[i]))
            h, xn, route = _out_even(ya.reshape(n, MIX_WIDTH), p.reshape(n, MIX_WIDTH), h,
                                     pool_w[i].astype(BF16), pool_scale[i][None, :],
                                     w_out_even[i].astype(BF16), ffn_gain, w_router, seq=s, tm=512)
        else:
            u, vn, q, k, v = _in_odd(h, mix_norm_odd[i][None, :], w_in_odd[i].astype(BF16),
                                     sgu_v_norm[i][None, :], tm=512)
            to3 = lambda t: t.reshape(b, s, MIX_WIDTH)
            yd = _sb_attention(to3(q), to3(k), to3(v))
            bias = jnp.broadcast_to(sgu_b[i][:, :, None], (N_GROUPS, SGU_BLOCK, LANES))
            h, xn, route = _out_odd(u, vn, yd.reshape(n, MIX_WIDTH), h, sgu_w[i].astype(BF16), bias,
                                    w_out_odd[i].astype(BF16), ffn_gain, w_router, tm=512)
        h = _moe_layer(h, xn, route, w_gate, w_up, w_down, layer)
    return h.reshape(b, s, d)
```

```python
import functools
import math

import jax
import jax.numpy as jnp
from jax import lax
from jax.experimental import pallas as pl
from jax.experimental.pallas import tpu as pltpu

F32 = jnp.float32
BF16 = jnp.bfloat16

D_MODEL = 1024
CHUNK = 64
EPS = 1e-6
HEAD_DIM = 64
MIX_WIDTH = 512
LANES = 128
SUBLANES = 8
ATT_LEFT = 8 * CHUNK
ATT_MAX_REL = 128
POOL_WINDOWS = (2, 4, 8, 16)
POOL_HALO = 16
SGU_BLOCK = 128
N_GROUPS = 4
N_EXP_PER_GROUP = 8
N_EXPERTS = N_GROUPS * N_EXP_PER_GROUP
EXPERT_FF = 256
ROUTER_LANES = 128
ROUTER_ROWS = 40
NEG_BIG = -1e30
VMEM_LIMIT = 56 * 1024 * 1024

NT_DIMS = (((1,), (1,)), ((), ()))
LOG2E = math.log2(math.e)
Q_SCALE = LOG2E / math.sqrt(HEAD_DIM)


def _cparams(sem):
    return pltpu.CompilerParams(dimension_semantics=sem, vmem_limit_bytes=VMEM_LIMIT)


def _store_token_major(ref, x):
    rows = x.shape[0]
    for s in range(SUBLANES):
        ref[pl.ds(s, rows, stride=SUBLANES), :] = x[:, s * LANES:(s + 1) * LANES]


def _load_token_major(ref, rows, lead=()):
    return jnp.concatenate(
        [ref[lead + (pl.ds(s, rows, stride=SUBLANES), slice(None))] for s in range(SUBLANES)], axis=1)


def _rms(x, gain):
    return x * lax.rsqrt(jnp.mean(x * x, axis=-1, keepdims=True) + EPS) * gain


def _split_dot(x, m):
    hi = x.astype(BF16)
    lo = (x - hi.astype(F32)).astype(BF16)
    return (jnp.dot(hi, m, preferred_element_type=F32)
            + jnp.dot(lo, m, preferred_element_type=F32))


def _head_rms(t, gain):
    n = t.shape[-1]
    r = lax.broadcasted_iota(jnp.int32, (n, n), 0) // HEAD_DIM
    c = lax.broadcasted_iota(jnp.int32, (n, n), 1) // HEAD_DIM
    bd = jnp.where(r == c, 1.0, 0.0).astype(BF16)
    ms = _split_dot(t * t, bd) * (1.0 / HEAD_DIM)
    return t * lax.rsqrt(ms + EPS) * gain


def _in_even_body(h_ref, g_ref, w_ref, qg_ref, kg_ref, q_ref, k_ref, v_ref, p_ref):
    j = pl.program_id(1)

    @pl.when(j == 0)
    def _():
        k_ref[...] = jnp.zeros_like(k_ref)
        v_ref[...] = jnp.zeros_like(v_ref)

    @pl.when(j > 0)
    def _():
        xn = _rms(h_ref[0], g_ref[...])
        proj = jnp.dot(xn.astype(BF16), w_ref[...], preferred_element_type=F32)
        w = MIX_WIDTH
        q_ref[0] = (_head_rms(proj[:, :w], qg_ref[...]) * Q_SCALE).astype(BF16)
        k_ref[0] = _head_rms(proj[:, w:2 * w], kg_ref[...]).astype(BF16)
        v_ref[0] = proj[:, 2 * w:3 * w].astype(BF16)
        p_ref[0] = proj[:, 3 * w:].astype(BF16)


def _in_even(h, gain, w, q_gain, k_gain, tm):
    b, s, d = h.shape
    assert tm == ATT_LEFT and s % tm == 0
    nt = s // tm
    cur = lambda bi, j: (bi, jnp.maximum(j - 1, 0), 0)
    const = lambda bi, j: (0, 0)
    out_sds = lambda rows: jax.ShapeDtypeStruct((b, rows, MIX_WIDTH), BF16)
    return pl.pallas_call(
        _in_even_body,
        grid=(b, nt + 1),
        in_specs=[
            pl.BlockSpec((1, tm, d), cur),
            pl.BlockSpec((1, d), const),
            pl.BlockSpec(w.shape, const),
            pl.BlockSpec((1, MIX_WIDTH), const),
            pl.BlockSpec((1, MIX_WIDTH), const),
        ],
        out_specs=[
            pl.BlockSpec((1, tm, MIX_WIDTH), cur),
            pl.BlockSpec((1, tm, MIX_WIDTH), lambda bi, j: (bi, j, 0)),
            pl.BlockSpec((1, tm, MIX_WIDTH), lambda bi, j: (bi, j, 0)),
            pl.BlockSpec((1, tm, MIX_WIDTH), cur),
        ],
        out_shape=[out_sds(s), out_sds(s + ATT_LEFT), out_sds(s + ATT_LEFT), out_sds(s)],
        compiler_params=_cparams(("arbitrary", "arbitrary")),
        name="in_even",
    )(h, gain, w, q_gain, k_gain)


BAND_TQ = 2 * CHUNK
BAND_TK = BAND_TQ + ATT_LEFT


def _band_body(q_ref, k_ref, v_ref, bias_ref, o_ref):
    i = pl.program_id(1)
    start = pl.multiple_of(i * BAND_TQ, BAND_TQ)
    lane = lax.broadcasted_iota(jnp.int32, (BAND_TQ, LANES), 1)
    col = lax.broadcasted_iota(jnp.int32, (2 * BAND_TQ, BAND_TK), 1)
    is_pad = (col + start) < ATT_LEFT
    pairs = range(MIX_WIDTH // LANES)
    lanes = [slice(hp * LANES, (hp + 1) * LANES) for hp in pairs]
    scores = []
    for hp in pairs:
        q = q_ref[0, :, lanes[hp]]
        kb = k_ref[0, pl.ds(start, BAND_TK), lanes[hp]]
        zero = jnp.zeros_like(q)
        q2 = jnp.concatenate([jnp.where(lane < HEAD_DIM, q, zero), jnp.where(lane < HEAD_DIM, zero, q)], axis=0)
        scores.append(lax.dot_general(q2, kb, NT_DIMS, preferred_element_type=F32))
    probs, denoms = [], []
    for hp in pairs:
        bias = bias_ref[2 * hp:2 * hp + 2].reshape(2 * BAND_TQ, BAND_TK)
        s = jnp.where(is_pad, NEG_BIG, scores[hp] + bias)
        p = jnp.exp2(s - jnp.max(s, axis=-1, keepdims=True))
        denoms.append(jnp.sum(p, axis=-1, keepdims=True))
        probs.append(p.astype(BF16))
    for hp in pairs:
        vb = v_ref[0, pl.ds(start, BAND_TK), lanes[hp]]
        o = jnp.dot(probs[hp], vb, preferred_element_type=F32) / denoms[hp]
        o_ref[0, :, lanes[hp]] = jnp.where(lane < HEAD_DIM, o[:BAND_TQ], o[BAND_TQ:]).astype(BF16)


def _band_bias(rel_bias):
    heads = rel_bias.shape[0]
    r = jnp.arange(BAND_TQ)[:, None]
    j = jnp.arange(BAND_TK)[None, :]
    jb = j - CHUNK * (r // CHUNK)
    in_band = (jb >= 0) & (jb < ATT_LEFT + CHUNK)
    period = BAND_TK + BAND_TQ
    far = jnp.broadcast_to(rel_bias[:, 2 * ATT_MAX_REL:], (heads, ATT_LEFT - ATT_MAX_REL + 1))
    near = rel_bias[:, 2 * ATT_MAX_REL - 1:0:-1]
    wrap = jnp.broadcast_to(rel_bias[:, 2 * ATT_MAX_REL:], (heads, period - BAND_TK))
    g = jnp.concatenate([far, near, wrap], axis=1).astype(F32)
    assert g.shape[1] == period
    toep = jnp.tile(g, (1, BAND_TQ))[:, :BAND_TQ * (period - 1)].reshape(heads, BAND_TQ, period - 1)
    return jnp.where(in_band[None], toep[:, :, :BAND_TK] * LOG2E, NEG_BIG)


def _band_attention(q, k_pad, v_pad, bias):
    b, s, w = q.shape
    sp = k_pad.shape[1]
    return pl.pallas_call(
        _band_body,
        grid=(b, s // BAND_TQ),
        in_specs=[
            pl.BlockSpec((1, BAND_TQ, w), lambda bi, i: (bi, i, 0)),
            pl.BlockSpec((1, sp, w), lambda bi, i: (bi, 0, 0)),
            pl.BlockSpec((1, sp, w), lambda bi, i: (bi, 0, 0)),
            pl.BlockSpec(bias.shape, lambda bi, i: (0, 0, 0)),
        ],
        out_specs=pl.BlockSpec((1, BAND_TQ, w), lambda bi, i: (bi, i, 0)),
        out_shape=jax.ShapeDtypeStruct((b, s, w), BF16),
        compiler_params=_cparams(("arbitrary", "arbitrary")),
        name="band_attention",
    )(q, k_pad, v_pad, bias)


def _route_tokens(h, gain, w_router):
    xn = _rms(h, gain)
    x_hi = xn.astype(BF16)
    x_lo = (xn - x_hi.astype(F32)).astype(BF16)
    w_hi = w_router.astype(BF16)
    w_lo = (w_router - w_hi.astype(F32)).astype(BF16)
    logits = (jnp.dot(x_hi, w_hi, preferred_element_type=F32)
              + jnp.dot(x_lo, w_hi, preferred_element_type=F32)
              + jnp.dot(x_hi, w_lo, preferred_element_type=F32))
    lt = logits.T[:ROUTER_ROWS]
    sub = lax.broadcasted_iota(jnp.int32, lt.shape, 0).astype(F32)
    ninf = -jnp.inf

    def top(vals):
        m = jnp.max(vals, axis=0, keepdims=True)
        idx = jnp.min(jnp.where(vals == m, sub, float(ROUTER_LANES)), axis=0, keepdims=True)
        return m, idx

    is_group = sub < N_GROUPS
    g_max, g_sel = top(jnp.where(is_group, lt, ninf))
    g_den = jnp.sum(jnp.where(is_group, jnp.exp(lt - g_max), 0.0), axis=0, keepdims=True)
    g_weight = 1.0 / g_den
    lo = N_GROUPS + N_EXP_PER_GROUP * g_sel
    e_logits = jnp.where((sub >= lo) & (sub < lo + N_EXP_PER_GROUP), lt, ninf)
    e1, i1 = top(e_logits)
    e2, i2 = top(jnp.where(sub == i1, ninf, e_logits))
    t = jnp.exp(e2 - e1)
    w1 = g_weight / (1.0 + t)
    w2 = g_weight * t / (1.0 + t)
    rows = lax.broadcasted_iota(jnp.int32, (ROUTER_LANES, lt.shape[1]), 0)
    route_t = jnp.where(rows == 0, i1 - N_GROUPS, jnp.where(rows == 1, i2 - N_GROUPS, 0.0))
    route_t = jnp.where(rows == 2, w1, jnp.where(rows == 3, w2, route_t))
    return xn, route_t.T


def _router_weights(w_rg, w_re):
    pad = jnp.zeros((w_rg.shape[0], ROUTER_LANES - N_GROUPS - N_EXPERTS), F32)
    return jnp.concatenate([w_rg, w_re, pad], axis=1)


def _out_even_body(tiles_per_seq, ya_ref, p_ref, halo_ref, h_ref, pw_ref, ps_ref, wo_ref, fg_ref, wr_ref,
                   o_ref, xn_ref, route_ref, p_scr, yb_scr):
    tm = p_ref.shape[0]
    it = pl.program_id(0) % tiles_per_seq
    halo = halo_ref[...].astype(F32)
    p_scr[0:POOL_HALO, :] = jnp.where(it == 0, jnp.zeros_like(halo), halo)
    p_scr[POOL_HALO:, :] = p_ref[...].astype(F32)
    t = it * tm + lax.broadcasted_iota(jnp.int32, (tm, 1), 0)
    for g, win in enumerate(POOL_WINDOWS):
        ls = slice(g * LANES, (g + 1) * LANES)
        cur = p_scr[POOL_HALO:POOL_HALO + tm, ls]
        acc = cur
        for dlt in range(1, win):
            acc = acc + p_scr[POOL_HALO - dlt:POOL_HALO - dlt + tm, ls]
        cnt = jnp.minimum(t + 1, win).astype(F32)
        mixed = acc / cnt - cur
        yb = jnp.dot(mixed.astype(BF16), pw_ref[g], preferred_element_type=F32) * ps_ref[:, ls]
        yb_scr[:, ls] = yb.astype(BF16)
    w = MIX_WIDTH
    h_new = (h_ref[...]
             + jnp.dot(ya_ref[...], wo_ref[0:w, :], preferred_element_type=F32)
             + jnp.dot(yb_scr[...], wo_ref[w:, :], preferred_element_type=F32))
    o_ref[...] = h_new
    xn, route_ref[...] = _route_tokens(h_new, fg_ref[...], wr_ref[...])
    _store_token_major(xn_ref, xn)


def _out_even(ya, p, h, pool_w, pool_scale, w_out, ffn_gain, w_router, seq, tm):
    n, d = h.shape
    w = MIX_WIDTH
    row = lambda i: (i, 0)
    const2 = lambda i: (0, 0)
    halo_blocks = tm // POOL_HALO
    return pl.pallas_call(
        functools.partial(_out_even_body, seq // tm),
        grid=(n // tm,),
        in_specs=[
            pl.BlockSpec((tm, w), row),
            pl.BlockSpec((tm, w), row),
            pl.BlockSpec((POOL_HALO, w), lambda i: (jnp.maximum(i * halo_blocks - 1, 0), 0)),
            pl.BlockSpec((tm, d), row),
            pl.BlockSpec(pool_w.shape, lambda i: (0, 0, 0)),
            pl.BlockSpec((1, w), const2),
            pl.BlockSpec(w_out.shape, const2),
            pl.BlockSpec((1, d), const2),
            pl.BlockSpec(w_router.shape, const2),
        ],
        out_specs=[pl.BlockSpec((tm, d), row), pl.BlockSpec((tm * SUBLANES, LANES), row),
                   pl.BlockSpec((tm, ROUTER_LANES), row)],
        out_shape=[jax.ShapeDtypeStruct((n, d), F32), jax.ShapeDtypeStruct((n * SUBLANES, LANES), F32),
                   jax.ShapeDtypeStruct((n, ROUTER_LANES), F32)],
        scratch_shapes=[pltpu.VMEM((tm + POOL_HALO, w), F32), pltpu.VMEM((tm, w), BF16)],
        compiler_params=_cparams(("arbitrary",)),
        name="out_even",
    )(ya, p, p, h, pool_w, pool_scale, w_out, ffn_gain, w_router)


MOE_TM = 256
MOE_TILE_LANES = 256


def _moe_tiles(n):
    return (2 * n) // MOE_TM + N_EXPERTS


def _exact_dot_nt(ones, x):
    out = None
    for _ in range(3):
        part = x.astype(BF16)
        x = x - part.astype(F32)
        term = lax.dot_general(ones, part, NT_DIMS, preferred_element_type=F32)
        out = term if out is None else out + term
    return out


def _slots_body(n_tiles, route_ref, pos1_ref, pos2_ref, tile_ref, run_scr, start_scr):
    phase = pl.program_id(0)
    i = pl.program_id(1)
    tm = route_ref.shape[0]
    route = route_ref[...]
    lane = lax.broadcasted_iota(jnp.int32, (tm, ROUTER_LANES), 1).astype(F32)
    pick1 = jnp.where(lane == route[:, 0:1], 1.0, 0.0)
    pick2 = jnp.where(lane == route[:, 1:2], 1.0, 0.0)
    occ = (pick1 + pick2).astype(BF16)
    ones_rows = jnp.ones((SUBLANES, tm), BF16)
    ones_lanes = jnp.ones((SUBLANES, ROUTER_LANES), BF16)

    @pl.when(i == 0)
    def _():
        run_scr[...] = jnp.zeros_like(run_scr)

    @pl.when(phase == 0)
    def _():
        run_scr[...] += jnp.dot(ones_rows, occ, preferred_element_type=F32)

        @pl.when(i == pl.num_programs(1) - 1)
        def _():
            padded = jnp.floor((run_scr[...] + (MOE_TM - 1)) * (1.0 / MOE_TM)) * MOE_TM
            r = lax.broadcasted_iota(jnp.int32, (ROUTER_LANES, ROUTER_LANES), 0)
            c = lax.broadcasted_iota(jnp.int32, (ROUTER_LANES, ROUTER_LANES), 1)
            before = jnp.where(r < c, 1.0, 0.0).astype(BF16)
            hi = padded.astype(BF16)
            mid = (padded - hi.astype(F32)).astype(BF16)
            low = (padded - hi.astype(F32) - mid.astype(F32)).astype(BF16)
            start = (jnp.dot(hi, before, preferred_element_type=F32)
                     + jnp.dot(mid, before, preferred_element_type=F32)
                     + jnp.dot(low, before, preferred_element_type=F32))
            start_scr[...] = start
            seg_end = start[0:1, :] + padded[0:1, :]
            tile_lo = (lax.broadcasted_iota(jnp.int32, (MOE_TILE_LANES, ROUTER_LANES), 0) * MOE_TM).astype(F32)
            e_lane = lax.broadcasted_iota(jnp.int32, (MOE_TILE_LANES, ROUTER_LANES), 1)
            ended = jnp.where((seg_end <= tile_lo) & (e_lane < N_EXPERTS), 1.0, 0.0).astype(BF16)
            tile_ref[...] = lax.dot_general(ones_lanes, ended, NT_DIMS, preferred_element_type=F32)

    @pl.when(phase == 1)
    def _():
        r = lax.broadcasted_iota(jnp.int32, (tm, tm), 0)
        c = lax.broadcasted_iota(jnp.int32, (tm, tm), 1)
        earlier = jnp.where(c < r, 1.0, 0.0).astype(BF16)
        base = (jnp.dot(earlier, occ, preferred_element_type=F32)
                + run_scr[0:1, :] + start_scr[0:1, :])
        pos1_ref[...] = _exact_dot_nt(ones_lanes, pick1 * base)
        pos2_ref[...] = _exact_dot_nt(ones_lanes, pick2 * base)
        run_scr[...] += jnp.dot(ones_rows, occ, preferred_element_type=F32)


def _slots(route, tm):
    n = route.shape[0]
    n_tiles = _moe_tiles(n)
    assert n_tiles <= MOE_TILE_LANES and 2 * n + N_EXPERTS * MOE_TM < 2 ** 24
    row_out = pl.BlockSpec((SUBLANES, tm), lambda ph, i: (0, i * ph))
    sds = jax.ShapeDtypeStruct((SUBLANES, n), F32)
    return pl.pallas_call(
        functools.partial(_slots_body, n_tiles),
        grid=(2, n // tm),
        in_specs=[pl.BlockSpec((tm, ROUTER_LANES), lambda ph, i: (i, 0))],
        out_specs=[row_out, row_out, pl.BlockSpec((SUBLANES, MOE_TILE_LANES), lambda ph, i: (0, 0))],
        out_shape=[sds, sds, jax.ShapeDtypeStruct((SUBLANES, MOE_TILE_LANES), F32)],
        scratch_shapes=[pltpu.VMEM((SUBLANES, ROUTER_LANES), F32), pltpu.VMEM((SUBLANES, ROUTER_LANES), F32)],
        compiler_params=_cparams(("arbitrary", "arbitrary")),
        name="moe_slots",
    )(route)


def _dispatch_body(n_tiles, pos_ref, tile_ref, xn_ref, xs_hbm, zero_scr, zero_sem, row_sem):
    tm = xn_ref.shape[0] // SUBLANES
    n = pl.num_programs(0) * tm
    base = pl.program_id(0) * tm
    tile_rows = MOE_TM * SUBLANES

    @pl.when(pl.program_id(0) == 0)
    def _():
        zero_scr[...] = jnp.zeros_like(zero_scr)

        def fill_copy(t):
            return pltpu.make_async_copy(zero_scr, xs_hbm.at[pl.ds(t * tile_rows, tile_rows), :], zero_sem)

        def has_padding(t):
            return (tile_ref[t] >= N_EXPERTS) | (tile_ref[t] != tile_ref[t + 1])

        @pl.loop(0, n_tiles)
        def _(t):
            @pl.when(has_padding(t))
            def _():
                fill_copy(t).start()

        @pl.loop(0, n_tiles)
        def _(t):
            @pl.when(has_padding(t))
            def _():
                fill_copy(t).wait()

    def issue(j, carry):
        src = xn_ref.at[pl.ds(pl.multiple_of(j * SUBLANES, SUBLANES), SUBLANES), :]
        for pick in range(2):
            dst = pl.multiple_of(pos_ref[pick * n + base + j], SUBLANES)
            pltpu.make_async_copy(src, xs_hbm.at[pl.ds(dst, SUBLANES), :], row_sem).start()
        return carry

    lax.fori_loop(0, tm, issue, 0, unroll=8)
    for _ in range(2):
        pltpu.make_async_copy(xn_ref, xs_hbm.at[pl.ds(0, tm * SUBLANES), :], row_sem).wait()


def _dispatch(pos, tile_map, xn, tm):
    n = xn.shape[0] // SUBLANES
    n_tiles = _moe_tiles(n)
    return pl.pallas_call(
        functools.partial(_dispatch_body, n_tiles),
        grid_spec=pltpu.PrefetchScalarGridSpec(
            num_scalar_prefetch=2,
            grid=(n // tm,),
            in_specs=[pl.BlockSpec((tm * SUBLANES, LANES), lambda i, p, t: (i, 0))],
            out_specs=pl.BlockSpec(memory_space=pl.ANY),
            scratch_shapes=[pltpu.VMEM((MOE_TM * SUBLANES, LANES), F32), pltpu.SemaphoreType.DMA(()),
                            pltpu.SemaphoreType.DMA(())],
        ),
        out_shape=jax.ShapeDtypeStruct((n_tiles * MOE_TM * SUBLANES, LANES), F32),
        compiler_params=_cparams(("arbitrary",)),
        name="moe_dispatch",
    )(pos, tile_map, xn)


def _experts_body(tile_ref, xs_ref, wg_ref, wu_ref, wd_ref, ys_ref):
    used = tile_ref[pl.program_id(0)] < N_EXPERTS

    @pl.when(used)
    def _():
        x = _load_token_major(xs_ref, MOE_TM).astype(BF16)
        gate = jnp.dot(x, wg_ref[0].astype(BF16), preferred_element_type=F32)
        up = jnp.dot(x, wu_ref[0].astype(BF16), preferred_element_type=F32)
        hid = gate * jax.nn.sigmoid(gate) * up
        _store_token_major(ys_ref, jnp.dot(hid.astype(BF16), wd_ref[0].astype(BF16),
                                           preferred_element_type=F32))

    @pl.when(jnp.logical_not(used))
    def _():
        ys_ref[...] = jnp.zeros_like(ys_ref)


def _experts(tile_map, xs, w_gate, w_up, w_down, layer):
    d, f = w_gate.shape[1:]
    tile_rows = MOE_TM * SUBLANES
    n_tiles = xs.shape[0] // tile_rows
    x_map = lambda i, tm_ref: (jnp.where(tm_ref[i] < N_EXPERTS, i, 0), 0)
    y_map = lambda i, tm_ref: (i, 0)
    w_map = lambda i, tm_ref: (layer * N_EXPERTS + jnp.minimum(tm_ref[i], N_EXPERTS - 1), 0, 0)
    return pl.pallas_call(
        _experts_body,
        grid_spec=pltpu.PrefetchScalarGridSpec(
            num_scalar_prefetch=1,
            grid=(n_tiles,),
            in_specs=[pl.BlockSpec((tile_rows, LANES), x_map), pl.BlockSpec((1, d, f), w_map),
                      pl.BlockSpec((1, d, f), w_map), pl.BlockSpec((1, f, d), w_map)],
            out_specs=pl.BlockSpec((tile_rows, LANES), y_map),
        ),
        out_shape=jax.ShapeDtypeStruct(xs.shape, F32),
        compiler_params=_cparams(("arbitrary",)),
        name="moe_experts",
    )(tile_map, xs, w_gate, w_up, w_down)


def _combine_body(pos_ref, h_ref, route_ref, ys_hbm, o_ref, buf, sems):
    tm = h_ref.shape[0]
    steps = pl.num_programs(0)
    n = steps * tm
    i = pl.program_id(0)

    def start_gather(step, slot):
        base = step * tm

        def issue(g, carry):
            for u in range(SUBLANES):
                for pick in range(2):
                    src = pl.multiple_of(pos_ref[pick * n + base + g * SUBLANES + u], SUBLANES)
                    dst = pl.multiple_of(g * SUBLANES * SUBLANES, SUBLANES) + u * SUBLANES
                    pltpu.make_async_copy(ys_hbm.at[pl.ds(src, SUBLANES), :],
                                          buf.at[slot, pick, pl.ds(dst, SUBLANES), :], sems.at[slot]).start()
            return carry

        lax.fori_loop(0, tm // SUBLANES, issue, 0)

    @pl.when(i == 0)
    def _():
        start_gather(0, 0)

    @pl.when(i + 1 < steps)
    def _():
        start_gather(i + 1, (i + 1) % 2)

    slot = i % 2
    pltpu.make_async_copy(buf.at[slot], buf.at[slot], sems.at[slot]).wait()
    route = route_ref[...]
    y1 = _load_token_major(buf, tm, (slot, 0))
    y2 = _load_token_major(buf, tm, (slot, 1))
    o_ref[...] = h_ref[...] + route[:, 2:3] * y1 + route[:, 3:4] * y2


def _combine(pos, h, route, ys, tm):
    n, d = h.shape
    row = lambda i, p: (i, 0)
    return pl.pallas_call(
        _combine_body,
        grid_spec=pltpu.PrefetchScalarGridSpec(
            num_scalar_prefetch=1,
            grid=(n // tm,),
            in_specs=[pl.BlockSpec((tm, d), row), pl.BlockSpec((tm, ROUTER_LANES), row),
                      pl.BlockSpec(memory_space=pl.ANY)],
            out_specs=pl.BlockSpec((tm, d), row),
            scratch_shapes=[pltpu.VMEM((2, 2, tm * SUBLANES, LANES), F32), pltpu.SemaphoreType.DMA((2,))],
        ),
        out_shape=jax.ShapeDtypeStruct((n, d), F32),
        compiler_params=_cparams(("arbitrary",)),
        name="moe_combine",
    )(pos, h, route, ys)


def _gelu(x):
    return 0.5 * x * (1.0 + lax.erf(x * (1.0 / math.sqrt(2.0))))


def _in_odd_body(h_ref, g_ref, w_ref, vg_ref, u_ref, vn_ref, q_ref, k_ref, v_ref):
    xn = _rms(h_ref[...], g_ref[...])
    proj = jnp.dot(xn.astype(BF16), w_ref[...], preferred_element_type=F32)
    w = MIX_WIDTH
    u_ref[...] = _gelu(proj[:, :w]).astype(BF16)
    vn_ref[...] = _rms(_gelu(proj[:, w:2 * w]), vg_ref[...]).astype(BF16)
    q_ref[...] = (proj[:, 2 * w:3 * w] * Q_SCALE).astype(BF16)
    k_ref[...] = proj[:, 3 * w:4 * w].astype(BF16)
    v_ref[...] = proj[:, 4 * w:].astype(BF16)


def _in_odd(h, gain, w, v_gain, tm):
    n, d = h.shape
    row = lambda i: (i, 0)
    const2 = lambda i: (0, 0)
    sds = jax.ShapeDtypeStruct((n, MIX_WIDTH), BF16)
    return pl.pallas_call(
        _in_odd_body,
        grid=(n // tm,),
        in_specs=[pl.BlockSpec((tm, d), row), pl.BlockSpec((1, d), const2),
                  pl.BlockSpec(w.shape, const2), pl.BlockSpec((1, MIX_WIDTH), const2)],
        out_specs=[pl.BlockSpec((tm, MIX_WIDTH), row)] * 5,
        out_shape=[sds] * 5,
        compiler_params=_cparams(("arbitrary",)),
        name="in_odd",
    )(h, gain, w, v_gain)


SB_T = 128
SB_NB = 3
SB_TK = SB_NB * SB_T
SB_QBLK = 512
SB_GROUP = 2
SB_UNDERFLOW = -150.0


def _sb_body(q_ref, k_ref, v_ref, o_ref, acc_scr, run_scr):
    qi = pl.program_id(2)
    lane = lax.broadcasted_iota(jnp.int32, (SB_T, LANES), 1)
    row = lax.broadcasted_iota(jnp.int32, (2 * SB_T, 1), 0) % SB_T
    col = lax.broadcasted_iota(jnp.int32, (2 * SB_T, SB_TK), 1)
    rr = lax.broadcasted_iota(jnp.int32, (2 * SB_T, 2 * SB_T), 0) % SB_T
    cc = lax.broadcasted_iota(jnp.int32, (2 * SB_T, 2 * SB_T), 1)
    suffix = jnp.where((cc >= SB_T) | (rr > cc), 1.0, 0.0).astype(BF16)

    def suffix_sums(x):
        hi = x.astype(BF16)
        lo = (x - hi.astype(F32)).astype(BF16)
        return jnp.dot(jnp.concatenate([hi, lo], axis=1), suffix, preferred_element_type=F32)

    def subtile_group(grp, _):
        q_los = [pl.multiple_of((grp * SB_GROUP + s) * SB_T, SB_T) for s in range(SB_GROUP)]
        q_starts = [qi * SB_QBLK + q_lo for q_lo in q_los]
        q_heads = []
        for q_lo in q_los:
            q = q_ref[0, pl.ds(q_lo, SB_T), :]
            zero = jnp.zeros_like(q)
            q_heads.append(jnp.concatenate(
                [jnp.where(lane < HEAD_DIM, q, zero), jnp.where(lane < HEAD_DIM, zero, q)], axis=0))
        acc_scr[...] = jnp.zeros_like(acc_scr)
        run_scr[...] = jnp.zeros_like(run_scr)

        def cond(carry):
            his, dones = carry
            active = [(hi > 0) & (done == 0) for hi, done in zip(his, dones)]
            return functools.reduce(jnp.logical_or, active)

        def body(carry):
            his, _ = carry
            group = range(SB_GROUP)
            kss = [pl.multiple_of(jnp.maximum(his[s] - SB_TK, 0), SB_T) for s in group]
            valid = [col < (jnp.minimum(row + q_starts[s], his[s]) - kss[s]) for s in group]
            zs = [lax.dot_general(q_heads[s], k_ref[0, pl.ds(kss[s], SB_TK), :], NT_DIMS,
                                  preferred_element_type=F32) for s in group]
            log_beta, log_rest = [], []
            for s in group:
                z = jnp.where(valid[s], zs[s], NEG_BIG)
                sp = jnp.log2(1.0 + jnp.exp2(-jnp.abs(z)))
                log_beta.append(jnp.minimum(z, 0.0) - sp)
                log_rest.append(log_beta[s] - z)
            sums = [[suffix_sums(log_rest[s][:, blk * SB_T:(blk + 1) * SB_T]) for blk in range(SB_NB)]
                    for s in group]
            dones = []
            for s in group:
                run = run_scr[s]
                pieces = [None] * SB_NB
                for blk in reversed(range(SB_NB)):
                    tt = sums[s][blk]
                    pieces[blk] = jnp.exp2(log_beta[s][:, blk * SB_T:(blk + 1) * SB_T] + tt[:, :SB_T] + run)
                    run = run + tt[:, SB_T:]
                att = jnp.concatenate(pieces, axis=1)
                acc_scr[s] += jnp.dot(att.astype(BF16), v_ref[0, pl.ds(kss[s], SB_TK), :],
                                      preferred_element_type=F32)
                run_scr[s] = run
                dones.append((jnp.max(run) <= SB_UNDERFLOW).astype(jnp.int32))
            return tuple(kss), tuple(dones)

        lax.while_loop(cond, body, (tuple(qs + SB_T for qs in q_starts),
                                    tuple(jnp.int32(0) for _ in range(SB_GROUP))))
        for s in range(SB_GROUP):
            o_ref[0, pl.ds(q_los[s], SB_T), :] = jnp.where(
                lane < HEAD_DIM, acc_scr[s, :SB_T], acc_scr[s, SB_T:]).astype(BF16)
        return 0

    lax.fori_loop(0, SB_QBLK // (SB_T * SB_GROUP), subtile_group, 0)


def _sb_attention(q, k, v):
    b, s, w = q.shape
    assert s % SB_QBLK == 0 and s >= SB_TK
    qspec = pl.BlockSpec((1, SB_QBLK, LANES), lambda bi, hp, i: (bi, i, hp))
    kvspec = pl.BlockSpec((1, s, LANES), lambda bi, hp, i: (bi, 0, hp))
    return pl.pallas_call(
        _sb_body,
        grid=(b, w // LANES, s // SB_QBLK),
        in_specs=[qspec, kvspec, kvspec],
        out_specs=qspec,
        out_shape=jax.ShapeDtypeStruct((b, s, w), BF16),
        scratch_shapes=[pltpu.VMEM((SB_GROUP, 2 * SB_T, LANES), F32),
                        pltpu.VMEM((SB_GROUP, 2 * SB_T, LANES), F32)],
        compiler_params=_cparams(("arbitrary", "arbitrary", "arbitrary")),
        name="sb_attention",
    )(q, k, v)


def _out_odd_body(u_ref, vn_ref, yd_ref, h_ref, ws_ref, bs_ref, wo_ref, fg_ref, wr_ref,
                  o_ref, xn_ref, route_ref, yc_scr):
    tm = u_ref.shape[0]
    r = lax.broadcasted_iota(jnp.int32, (SGU_BLOCK, SGU_BLOCK), 0)
    c = lax.broadcasted_iota(jnp.int32, (SGU_BLOCK, SGU_BLOCK), 1)
    for g in range(MIX_WIDTH // LANES):
        ls = slice(g * LANES, (g + 1) * LANES)
        ws = jnp.where(c <= r, ws_ref[g], jnp.zeros_like(ws_ref[g]))
        for blk in range(tm // SGU_BLOCK):
            rs = slice(blk * SGU_BLOCK, (blk + 1) * SGU_BLOCK)
            mixed = jnp.dot(ws, vn_ref[rs, ls], preferred_element_type=F32) + bs_ref[g]
            yc_scr[rs, ls] = (u_ref[rs, ls].astype(F32) * mixed).astype(BF16)
    w = MIX_WIDTH
    h_new = (h_ref[...]
             + jnp.dot(yc_scr[...], wo_ref[0:w, :], preferred_element_type=F32)
             + jnp.dot(yd_ref[...], wo_ref[w:, :], preferred_element_type=F32))
    o_ref[...] = h_new
    xn, route_ref[...] = _route_tokens(h_new, fg_ref[...], wr_ref[...])
    _store_token_major(xn_ref, xn)


def _out_odd(u, vn, yd, h, sgu_w, sgu_b, w_out, ffn_gain, w_router, tm):
    n, d = h.shape
    w = MIX_WIDTH
    row = lambda i: (i, 0)
    const2 = lambda i: (0, 0)
    const3 = lambda i: (0, 0, 0)
    return pl.pallas_call(
        _out_odd_body,
        grid=(n // tm,),
        in_specs=[pl.BlockSpec((tm, w), row), pl.BlockSpec((tm, w), row), pl.BlockSpec((tm, w), row),
                  pl.BlockSpec((tm, d), row), pl.BlockSpec(sgu_w.shape, const3),
                  pl.BlockSpec(sgu_b.shape, const3), pl.BlockSpec(w_out.shape, const2),
                  pl.BlockSpec((1, d), const2), pl.BlockSpec(w_router.shape, const2)],
        out_specs=[pl.BlockSpec((tm, d), row), pl.BlockSpec((tm * SUBLANES, LANES), row),
                   pl.BlockSpec((tm, ROUTER_LANES), row)],
        out_shape=[jax.ShapeDtypeStruct((n, d), F32), jax.ShapeDtypeStruct((n * SUBLANES, LANES), F32),
                   jax.ShapeDtypeStruct((n, ROUTER_LANES), F32)],
        scratch_shapes=[pltpu.VMEM((tm, w), BF16)],
        compiler_params=_cparams(("arbitrary",)),
        name="out_odd",
    )(u, vn, yd, h, sgu_w, sgu_b, w_out, ffn_gain, w_router)


def _moe_layer(h, xn, route, w_gate, w_up, w_down, layer):
    pos1, pos2, tile_map = _slots(route, tm=512)
    pos = jnp.concatenate([pos1[0], pos2[0]]).astype(jnp.int32) * SUBLANES
    tile_map = tile_map[0].astype(jnp.int32)
    xs = _dispatch(pos, tile_map, xn, tm=256)
    ys = _experts(tile_map, xs, w_gate, w_up, w_down, layer)
    return _combine(pos, h, route, ys, tm=256)


def kernel(x, mix_norm_even, w_in_even, att_q_norm, att_k_norm, att_rel_bias, pool_w, pool_scale,
           w_out_even, mix_norm_odd, w_in_odd, sgu_v_norm, sgu_w, sgu_b, w_out_odd, ffn_norm,
           w_router_group, w_router_expert, w_exp_gate, w_exp_up, w_exp_down):
    b, s, d = x.shape
    n = b * s
    depth = ffn_norm.shape[0]
    heads = MIX_WIDTH // HEAD_DIM
    h = x.reshape(n, d)
    w_gate = w_exp_gate.reshape(depth * N_EXPERTS, d, EXPERT_FF)
    w_up = w_exp_up.reshape(depth * N_EXPERTS, d, EXPERT_FF)
    w_down = w_exp_down.reshape(depth * N_EXPERTS, EXPERT_FF, d)
    for layer in range(depth):
        i = layer // 2
        ffn_gain = ffn_norm[layer][None, :]
        w_router = _router_weights(w_router_group[layer], w_router_expert[layer])
        if layer % 2 == 0:
            q, k_pad, v_pad, p = _in_even(
                h.reshape(b, s, d), mix_norm_even[i][None, :], w_in_even[i].astype(BF16),
                jnp.tile(att_q_norm[i], heads)[None, :], jnp.tile(att_k_norm[i], heads)[None, :],
                tm=ATT_LEFT)
            ya = _band_attention(q, k_pad, v_pad, _band_bias(att_rel_bias[i]))
            h, xn, route = _out_even(ya.reshape(n, MIX_WIDTH), p.reshape(n, MIX_WIDTH), h,
                                     pool_w[i].astype(BF16), pool_scale[i][None, :],
                                     w_out_even[i].astype(BF16), ffn_gain, w_router, seq=s, tm=512)
        else:
            u, vn, q, k, v = _in_odd(h, mix_norm_odd[i][None, :], w_in_odd[i].astype(BF16),
                                     sgu_v_norm[i][None, :], tm=512)
            to3 = lambda t: t.reshape(b, s, MIX_WIDTH)
            yd = _sb_attention(to3(q), to3(k), to3(v))
            bias = jnp.broadcast_to(sgu_b[i][:, :, None], (N_GROUPS, SGU_BLOCK, LANES))
            h, xn, route = _out_odd(u, vn, yd.reshape(n, MIX_WIDTH), h, sgu_w[i].astype(BF16), bias,
                                    w_out_odd[i].astype(BF16), ffn_gain, w_router, tm=512)
        h = _moe_layer(h, xn, route, w_gate, w_up, w_down, layer)
    return h.reshape(b, s, d)
```

```python
import functools
import math

import jax
import jax.numpy as jnp
from jax import lax
from jax.experimental import pallas as pl
from jax.experimental.pallas import tpu as pltpu

F32 = jnp.float32
BF16 = jnp.bfloat16

D_MODEL = 1024
CHUNK = 64
EPS = 1e-6
HEAD_DIM = 64
MIX_WIDTH = 512
LANES = 128
SUBLANES = 8
ATT_LEFT = 8 * CHUNK
ATT_MAX_REL = 128
POOL_WINDOWS = (2, 4, 8, 16)
POOL_HALO = 16
SGU_BLOCK = 128
N_GROUPS = 4
N_EXP_PER_GROUP = 8
N_EXPERTS = N_GROUPS * N_EXP_PER_GROUP
EXPERT_FF = 256
ROUTER_LANES = 128
ROUTER_ROWS = 40
NEG_BIG = -1e30
VMEM_LIMIT = 56 * 1024 * 1024

NT_DIMS = (((1,), (1,)), ((), ()))
LOG2E = math.log2(math.e)
Q_SCALE = LOG2E / math.sqrt(HEAD_DIM)


def _cparams(sem):
    return pltpu.CompilerParams(dimension_semantics=sem, vmem_limit_bytes=VMEM_LIMIT)


def _store_token_major(ref, x):
    rows = x.shape[0]
    for s in range(SUBLANES):
        ref[pl.ds(s, rows, stride=SUBLANES), :] = x[:, s * LANES:(s + 1) * LANES]


def _load_token_major(ref, rows, lead=()):
    return jnp.concatenate(
        [ref[lead + (pl.ds(s, rows, stride=SUBLANES), slice(None))] for s in range(SUBLANES)], axis=1)


def _rms(x, gain):
    return x * lax.rsqrt(jnp.mean(x * x, axis=-1, keepdims=True) + EPS) * gain


def _split_dot(x, m):
    hi = x.astype(BF16)
    lo = (x - hi.astype(F32)).astype(BF16)
    return (jnp.dot(hi, m, preferred_element_type=F32)
            + jnp.dot(lo, m, preferred_element_type=F32))


def _head_rms(t, gain):
    n = t.shape[-1]
    r = lax.broadcasted_iota(jnp.int32, (n, n), 0) // HEAD_DIM
    c = lax.broadcasted_iota(jnp.int32, (n, n), 1) // HEAD_DIM
    bd = jnp.where(r == c, 1.0, 0.0).astype(BF16)
    ms = _split_dot(t * t, bd) * (1.0 / HEAD_DIM)
    return t * lax.rsqrt(ms + EPS) * gain


def _in_even_body(h_ref, g_ref, w_ref, qg_ref, kg_ref, q_ref, k_ref, v_ref, p_ref):
    j = pl.program_id(1)

    @pl.when(j == 0)
    def _():
        k_ref[...] = jnp.zeros_like(k_ref)
        v_ref[...] = jnp.zeros_like(v_ref)

    @pl.when(j > 0)
    def _():
        xn = _rms(h_ref[0], g_ref[...])
        proj = jnp.dot(xn.astype(BF16), w_ref[...], preferred_element_type=F32)
        w = MIX_WIDTH
        q_ref[0] = (_head_rms(proj[:, :w], qg_ref[...]) * Q_SCALE).astype(BF16)
        k_ref[0] = _head_rms(proj[:, w:2 * w], kg_ref[...]).astype(BF16)
        v_ref[0] = proj[:, 2 * w:3 * w].astype(BF16)
        p_ref[0] = proj[:, 3 * w:].astype(BF16)


def _in_even(h, gain, w, q_gain, k_gain, tm):
    b, s, d = h.shape
    assert tm == ATT_LEFT and s % tm == 0
    nt = s // tm
    cur = lambda bi, j: (bi, jnp.maximum(j - 1, 0), 0)
    const = lambda bi, j: (0, 0)
    out_sds = lambda rows: jax.ShapeDtypeStruct((b, rows, MIX_WIDTH), BF16)
    return pl.pallas_call(
        _in_even_body,
        grid=(b, nt + 1),
        in_specs=[
            pl.BlockSpec((1, tm, d), cur),
            pl.BlockSpec((1, d), const),
            pl.BlockSpec(w.shape, const),
            pl.BlockSpec((1, MIX_WIDTH), const),
            pl.BlockSpec((1, MIX_WIDTH), const),
        ],
        out_specs=[
            pl.BlockSpec((1, tm, MIX_WIDTH), cur),
            pl.BlockSpec((1, tm, MIX_WIDTH), lambda bi, j: (bi, j, 0)),
            pl.BlockSpec((1, tm, MIX_WIDTH), lambda bi, j: (bi, j, 0)),
            pl.BlockSpec((1, tm, MIX_WIDTH), cur),
        ],
        out_shape=[out_sds(s), out_sds(s + ATT_LEFT), out_sds(s + ATT_LEFT), out_sds(s)],
        compiler_params=_cparams(("arbitrary", "arbitrary")),
        name="in_even",
    )(h, gain, w, q_gain, k_gain)


BAND_TQ = 2 * CHUNK
BAND_TK = BAND_TQ + ATT_LEFT


def _band_body(q_ref, k_ref, v_ref, bias_ref, o_ref):
    i = pl.program_id(1)
    start = pl.multiple_of(i * BAND_TQ, BAND_TQ)
    lane = lax.broadcasted_iota(jnp.int32, (BAND_TQ, LANES), 1)
    col = lax.broadcasted_iota(jnp.int32, (2 * BAND_TQ, BAND_TK), 1)
    is_pad = (col + start) < ATT_LEFT
    pairs = range(MIX_WIDTH // LANES)
    lanes = [slice(hp * LANES, (hp + 1) * LANES) for hp in pairs]
    scores = []
    for hp in pairs:
        q = q_ref[0, :, lanes[hp]]
        kb = k_ref[0, pl.ds(start, BAND_TK), lanes[hp]]
        zero = jnp.zeros_like(q)
        q2 = jnp.concatenate([jnp.where(lane < HEAD_DIM, q, zero), jnp.where(lane < HEAD_DIM, zero, q)], axis=0)
        scores.append(lax.dot_general(q2, kb, NT_DIMS, preferred_element_type=F32))
    probs, denoms = [], []
    for hp in pairs:
        bias = bias_ref[2 * hp:2 * hp + 2].reshape(2 * BAND_TQ, BAND_TK)
        s = jnp.where(is_pad, NEG_BIG, scores[hp] + bias)
        p = jnp.exp2(s - jnp.max(s, axis=-1, keepdims=True))
        denoms.append(jnp.sum(p, axis=-1, keepdims=True))
        probs.append(p.astype(BF16))
    for hp in pairs:
        vb = v_ref[0, pl.ds(start, BAND_TK), lanes[hp]]
        o = jnp.dot(probs[hp], vb, preferred_element_type=F32) / denoms[hp]
        o_ref[0, :, lanes[hp]] = jnp.where(lane < HEAD_DIM, o[:BAND_TQ], o[BAND_TQ:]).astype(BF16)


def _band_bias(rel_bias):
    heads = rel_bias.shape[0]
    r = jnp.arange(BAND_TQ)[:, None]
    j = jnp.arange(BAND_TK)[None, :]
    jb = j - CHUNK * (r // CHUNK)
    in_band = (jb >= 0) & (jb < ATT_LEFT + CHUNK)
    period = BAND_TK + BAND_TQ
    far = jnp.broadcast_to(rel_bias[:, 2 * ATT_MAX_REL:], (heads, ATT_LEFT - ATT_MAX_REL + 1))
    near = rel_bias[:, 2 * ATT_MAX_REL - 1:0:-1]
    wrap = jnp.broadcast_to(rel_bias[:, 2 * ATT_MAX_REL:], (heads, period - BAND_TK))
    g = jnp.concatenate([far, near, wrap], axis=1).astype(F32)
    assert g.shape[1] == period
    toep = jnp.tile(g, (1, BAND_TQ))[:, :BAND_TQ * (period - 1)].reshape(heads, BAND_TQ, period - 1)
    return jnp.where(in_band[None], toep[:, :, :BAND_TK] * LOG2E, NEG_BIG)


def _band_attention(q, k_pad, v_pad, bias):
    b, s, w = q.shape
    sp = k_pad.shape[1]
    return pl.pallas_call(
        _band_body,
        grid=(b, s // BAND_TQ),
        in_specs=[
            pl.BlockSpec((1, BAND_TQ, w), lambda bi, i: (bi, i, 0)),
            pl.BlockSpec((1, sp, w), lambda bi, i: (bi, 0, 0)),
            pl.BlockSpec((1, sp, w), lambda bi, i: (bi, 0, 0)),
            pl.BlockSpec(bias.shape, lambda bi, i: (0, 0, 0)),
        ],
        out_specs=pl.BlockSpec((1, BAND_TQ, w), lambda bi, i: (bi, i, 0)),
        out_shape=jax.ShapeDtypeStruct((b, s, w), BF16),
        compiler_params=_cparams(("arbitrary", "arbitrary")),
        name="band_attention",
    )(q, k_pad, v_pad, bias)


def _route_tokens(h, gain, w_router):
    xn = _rms(h, gain)
    x_hi = xn.astype(BF16)
    x_lo = (xn - x_hi.astype(F32)).astype(BF16)
    w_hi = w_router.astype(BF16)
    w_lo = (w_router - w_hi.astype(F32)).astype(BF16)
    logits = (jnp.dot(x_hi, w_hi, preferred_element_type=F32)
              + jnp.dot(x_lo, w_hi, preferred_element_type=F32)
              + jnp.dot(x_hi, w_lo, preferred_element_type=F32))
    lt = logits.T[:ROUTER_ROWS]
    sub = lax.broadcasted_iota(jnp.int32, lt.shape, 0).astype(F32)
    ninf = -jnp.inf

    def top(vals):
        m = jnp.max(vals, axis=0, keepdims=True)
        idx = jnp.min(jnp.where(vals == m, sub, float(ROUTER_LANES)), axis=0, keepdims=True)
        return m, idx

    is_group = sub < N_GROUPS
    g_max, g_sel = top(jnp.where(is_group, lt, ninf))
    g_den = jnp.sum(jnp.where(is_group, jnp.exp(lt - g_max), 0.0), axis=0, keepdims=True)
    g_weight = 1.0 / g_den
    lo = N_GROUPS + N_EXP_PER_GROUP * g_sel
    e_logits = jnp.where((sub >= lo) & (sub < lo + N_EXP_PER_GROUP), lt, ninf)
    e1, i1 = top(e_logits)
    e2, i2 = top(jnp.where(sub == i1, ninf, e_logits))
    t = jnp.exp(e2 - e1)
    w1 = g_weight / (1.0 + t)
    w2 = g_weight * t / (1.0 + t)
    rows = lax.broadcasted_iota(jnp.int32, (ROUTER_LANES, lt.shape[1]), 0)
    route_t = jnp.where(rows == 0, i1 - N_GROUPS, jnp.where(rows == 1, i2 - N_GROUPS, 0.0))
    route_t = jnp.where(rows == 2, w1, jnp.where(rows == 3, w2, route_t))
    return xn, route_t.T


def _router_weights(w_rg, w_re):
    pad = jnp.zeros((w_rg.shape[0], ROUTER_LANES - N_GROUPS - N_EXPERTS), F32)
    return jnp.concatenate([w_rg, w_re, pad], axis=1)


def _out_even_body(tiles_per_seq, ya_ref, p_ref, halo_ref, h_ref, pw_ref, ps_ref, wo_ref, fg_ref, wr_ref,
                   o_ref, xn_ref, route_ref, p_scr, yb_scr):
    tm = p_ref.shape[0]
    it = pl.program_id(0) % tiles_per_seq
    halo = halo_ref[...].astype(F32)
    p_scr[0:POOL_HALO, :] = jnp.where(it == 0, jnp.zeros_like(halo), halo)
    p_scr[POOL_HALO:, :] = p_ref[...].astype(F32)
    t = it * tm + lax.broadcasted_iota(jnp.int32, (tm, 1), 0)
    for g, win in enumerate(POOL_WINDOWS):
        ls = slice(g * LANES, (g + 1) * LANES)
        cur = p_scr[POOL_HALO:POOL_HALO + tm, ls]
        acc = cur
        for dlt in range(1, win):
            acc = acc + p_scr[POOL_HALO - dlt:POOL_HALO - dlt + tm, ls]
        cnt = jnp.minimum(t + 1, win).astype(F32)
        mixed = acc / cnt - cur
        yb = jnp.dot(mixed.astype(BF16), pw_ref[g], preferred_element_type=F32) * ps_ref[:, ls]
        yb_scr[:, ls] = yb.astype(BF16)
    w = MIX_WIDTH
    h_new = (h_ref[...]
             + jnp.dot(ya_ref[...], wo_ref[0:w, :], preferred_element_type=F32)
             + jnp.dot(yb_scr[...], wo_ref[w:, :], preferred_element_type=F32))
    o_ref[...] = h_new
    xn, route_ref[...] = _route_tokens(h_new, fg_ref[...], wr_ref[...])
    _store_token_major(xn_ref, xn)


def _out_even(ya, p, h, pool_w, pool_scale, w_out, ffn_gain, w_router, seq, tm):
    n, d = h.shape
    w = MIX_WIDTH
    row = lambda i: (i, 0)
    const2 = lambda i: (0, 0)
    halo_blocks = tm // POOL_HALO
    return pl.pallas_call(
        functools.partial(_out_even_body, seq // tm),
        grid=(n // tm,),
        in_specs=[
            pl.BlockSpec((tm, w), row),
            pl.BlockSpec((tm, w), row),
            pl.BlockSpec((POOL_HALO, w), lambda i: (jnp.maximum(i * halo_blocks - 1, 0), 0)),
            pl.BlockSpec((tm, d), row),
            pl.BlockSpec(pool_w.shape, lambda i: (0, 0, 0)),
            pl.BlockSpec((1, w), const2),
            pl.BlockSpec(w_out.shape, const2),
            pl.BlockSpec((1, d), const2),
            pl.BlockSpec(w_router.shape, const2),
        ],
        out_specs=[pl.BlockSpec((tm, d), row), pl.BlockSpec((tm * SUBLANES, LANES), row),
                   pl.BlockSpec((tm, ROUTER_LANES), row)],
        out_shape=[jax.ShapeDtypeStruct((n, d), F32), jax.ShapeDtypeStruct((n * SUBLANES, LANES), F32),
                   jax.ShapeDtypeStruct((n, ROUTER_LANES), F32)],
        scratch_shapes=[pltpu.VMEM((tm + POOL_HALO, w), F32), pltpu.VMEM((tm, w), BF16)],
        compiler_params=_cparams(("arbitrary",)),
        name="out_even",
    )(ya, p, p, h, pool_w, pool_scale, w_out, ffn_gain, w_router)


MOE_TM = 256
MOE_TILE_LANES = 256


def _moe_tiles(n):
    return (2 * n) // MOE_TM + N_EXPERTS


def _exact_dot_nt(ones, x):
    out = None
    for _ in range(3):
        part = x.astype(BF16)
        x = x - part.astype(F32)
        term = lax.dot_general(ones, part, NT_DIMS, preferred_element_type=F32)
        out = term if out is None else out + term
    return out


def _slots_body(n_tiles, route_ref, pos1_ref, pos2_ref, tile_ref, run_scr, start_scr):
    phase = pl.program_id(0)
    i = pl.program_id(1)
    tm = route_ref.shape[0]
    route = route_ref[...]
    lane = lax.broadcasted_iota(jnp.int32, (tm, ROUTER_LANES), 1).astype(F32)
    pick1 = jnp.where(lane == route[:, 0:1], 1.0, 0.0)
    pick2 = jnp.where(lane == route[:, 1:2], 1.0, 0.0)
    occ = (pick1 + pick2).astype(BF16)
    ones_rows = jnp.ones((SUBLANES, tm), BF16)
    ones_lanes = jnp.ones((SUBLANES, ROUTER_LANES), BF16)

    @pl.when(i == 0)
    def _():
        run_scr[...] = jnp.zeros_like(run_scr)

    @pl.when(phase == 0)
    def _():
        run_scr[...] += jnp.dot(ones_rows, occ, preferred_element_type=F32)

        @pl.when(i == pl.num_programs(1) - 1)
        def _():
            padded = jnp.floor((run_scr[...] + (MOE_TM - 1)) * (1.0 / MOE_TM)) * MOE_TM
            r = lax.broadcasted_iota(jnp.int32, (ROUTER_LANES, ROUTER_LANES), 0)
            c = lax.broadcasted_iota(jnp.int32, (ROUTER_LANES, ROUTER_LANES), 1)
            before = jnp.where(r < c, 1.0, 0.0).astype(BF16)
            hi = padded.astype(BF16)
            mid = (padded - hi.astype(F32)).astype(BF16)
            low = (padded - hi.astype(F32) - mid.astype(F32)).astype(BF16)
            start = (jnp.dot(hi, before, preferred_element_type=F32)
                     + jnp.dot(mid, before, preferred_element_type=F32)
                     + jnp.dot(low, before, preferred_element_type=F32))
            start_scr[...] = start
            seg_end = start[0:1, :] + padded[0:1, :]
            tile_lo = (lax.broadcasted_iota(jnp.int32, (MOE_TILE_LANES, ROUTER_LANES), 0) * MOE_TM).astype(F32)
            e_lane = lax.broadcasted_iota(jnp.int32, (MOE_TILE_LANES, ROUTER_LANES), 1)
            ended = jnp.where((seg_end <= tile_lo) & (e_lane < N_EXPERTS), 1.0, 0.0).astype(BF16)
            tile_ref[...] = lax.dot_general(ones_lanes, ended, NT_DIMS, preferred_element_type=F32)

    @pl.when(phase == 1)
    def _():
        r = lax.broadcasted_iota(jnp.int32, (tm, tm), 0)
        c = lax.broadcasted_iota(jnp.int32, (tm, tm), 1)
        earlier = jnp.where(c < r, 1.0, 0.0).astype(BF16)
        base = (jnp.dot(earlier, occ, preferred_element_type=F32)
                + run_scr[0:1, :] + start_scr[0:1, :])
        pos1_ref[...] = _exact_dot_nt(ones_lanes, pick1 * base)
        pos2_ref[...] = _exact_dot_nt(ones_lanes, pick2 * base)
        run_scr[...] += jnp.dot(ones_rows, occ, preferred_element_type=F32)


def _slots(route, tm):
    n = route.shape[0]
    n_tiles = _moe_tiles(n)
    assert n_tiles <= MOE_TILE_LANES and 2 * n + N_EXPERTS * MOE_TM < 2 ** 24
    row_out = pl.BlockSpec((SUBLANES, tm), lambda ph, i: (0, i * ph))
    sds = jax.ShapeDtypeStruct((SUBLANES, n), F32)
    return pl.pallas_call(
        functools.partial(_slots_body, n_tiles),
        grid=(2, n // tm),
        in_specs=[pl.BlockSpec((tm, ROUTER_LANES), lambda ph, i: (i, 0))],
        out_specs=[row_out, row_out, pl.BlockSpec((SUBLANES, MOE_TILE_LANES), lambda ph, i: (0, 0))],
        out_shape=[sds, sds, jax.ShapeDtypeStruct((SUBLANES, MOE_TILE_LANES), F32)],
        scratch_shapes=[pltpu.VMEM((SUBLANES, ROUTER_LANES), F32), pltpu.VMEM((SUBLANES, ROUTER_LANES), F32)],
        compiler_params=_cparams(("arbitrary", "arbitrary")),
        name="moe_slots",
    )(route)


def _dispatch_body(n_tiles, pos_ref, tile_ref, xn_ref, xs_hbm, zero_scr, zero_sem, row_sem):
    tm = xn_ref.shape[0] // SUBLANES
    n = pl.num_programs(0) * tm
    base = pl.program_id(0) * tm
    tile_rows = MOE_TM * SUBLANES

    @pl.when(pl.program_id(0) == 0)
    def _():
        zero_scr[...] = jnp.zeros_like(zero_scr)

        def fill_copy(t):
            return pltpu.make_async_copy(zero_scr, xs_hbm.at[pl.ds(t * tile_rows, tile_rows), :], zero_sem)

        def has_padding(t):
            return (tile_ref[t] >= N_EXPERTS) | (tile_ref[t] != tile_ref[t + 1])

        @pl.loop(0, n_tiles)
        def _(t):
            @pl.when(has_padding(t))
            def _():
                fill_copy(t).start()

        @pl.loop(0, n_tiles)
        def _(t):
            @pl.when(has_padding(t))
            def _():
                fill_copy(t).wait()

    def issue(j, carry):
        src = xn_ref.at[pl.ds(pl.multiple_of(j * SUBLANES, SUBLANES), SUBLANES), :]
        for pick in range(2):
            dst = pl.multiple_of(pos_ref[pick * n + base + j], SUBLANES)
            pltpu.make_async_copy(src, xs_hbm.at[pl.ds(dst, SUBLANES), :], row_sem).start(priority=pick)
        return carry

    lax.fori_loop(0, tm, issue, 0, unroll=8)
    for _ in range(2):
        pltpu.make_async_copy(xn_ref, xs_hbm.at[pl.ds(0, tm * SUBLANES), :], row_sem).wait()


def _dispatch(pos, tile_map, xn, tm):
    n = xn.shape[0] // SUBLANES
    n_tiles = _moe_tiles(n)
    return pl.pallas_call(
        functools.partial(_dispatch_body, n_tiles),
        grid_spec=pltpu.PrefetchScalarGridSpec(
            num_scalar_prefetch=2,
            grid=(n // tm,),
            in_specs=[pl.BlockSpec((tm * SUBLANES, LANES), lambda i, p, t: (i, 0))],
            out_specs=pl.BlockSpec(memory_space=pl.ANY),
            scratch_shapes=[pltpu.VMEM((MOE_TM * SUBLANES, LANES), F32), pltpu.SemaphoreType.DMA(()),
                            pltpu.SemaphoreType.DMA(())],
        ),
        out_shape=jax.ShapeDtypeStruct((n_tiles * MOE_TM * SUBLANES, LANES), F32),
        compiler_params=_cparams(("arbitrary",)),
        name="moe_dispatch",
    )(pos, tile_map, xn)


EXPERT_SLOTS = 3


def _experts_body(tile_ref, xs_hbm, wg_ref, wu_ref, wd_ref, ys_ref, xbuf, sems):
    i = pl.program_id(0)
    tile_rows = MOE_TM * SUBLANES
    used = tile_ref[i] < N_EXPERTS

    def tile_copy(t, slot):
        rows = pl.ds(pl.multiple_of(t * tile_rows, tile_rows), tile_rows)
        return pltpu.make_async_copy(xs_hbm.at[rows, :], xbuf.at[slot], sems.at[slot])

    def prefetch(t):
        @pl.when(tile_ref[t] < N_EXPERTS)
        def _():
            tile_copy(t, t % EXPERT_SLOTS).start()

    @pl.when(i == 0)
    def _():
        prefetch(0)
        prefetch(1)

    prefetch(i + 2)

    @pl.when(used)
    def _():
        slot = i % EXPERT_SLOTS
        tile_copy(i, slot).wait()
        x = _load_token_major(xbuf, MOE_TM, (slot,)).astype(BF16)
        gate = jnp.dot(x, wg_ref[0].astype(BF16), preferred_element_type=F32)
        up = jnp.dot(x, wu_ref[0].astype(BF16), preferred_element_type=F32)
        hid = gate * jax.nn.sigmoid(gate) * up
        _store_token_major(ys_ref, jnp.dot(hid.astype(BF16), wd_ref[0].astype(BF16),
                                           preferred_element_type=F32))

    @pl.when(jnp.logical_not(used))
    def _():
        ys_ref[...] = jnp.zeros_like(ys_ref)


def _experts(tile_map, xs, w_gate, w_up, w_down, layer):
    d, f = w_gate.shape[1:]
    tile_rows = MOE_TM * SUBLANES
    n_tiles = xs.shape[0] // tile_rows
    y_map = lambda i, tm_ref: (i, 0)
    w_map = lambda i, tm_ref: (layer * N_EXPERTS + jnp.minimum(tm_ref[i], N_EXPERTS - 1), 0, 0)
    return pl.pallas_call(
        _experts_body,
        grid_spec=pltpu.PrefetchScalarGridSpec(
            num_scalar_prefetch=1,
            grid=(n_tiles,),
            in_specs=[pl.BlockSpec(memory_space=pl.ANY), pl.BlockSpec((1, d, f), w_map),
                      pl.BlockSpec((1, d, f), w_map), pl.BlockSpec((1, f, d), w_map)],
            out_specs=pl.BlockSpec((tile_rows, LANES), y_map),
            scratch_shapes=[pltpu.VMEM((EXPERT_SLOTS, tile_rows, LANES), F32),
                            pltpu.SemaphoreType.DMA((EXPERT_SLOTS,))],
        ),
        out_shape=jax.ShapeDtypeStruct(xs.shape, F32),
        compiler_params=_cparams(("arbitrary",)),
        name="moe_experts",
    )(tile_map, xs, w_gate, w_up, w_down)


def _combine_body(pos_ref, h_ref, route_ref, ys_hbm, o_ref, buf, sems):
    tm = h_ref.shape[0]
    steps = pl.num_programs(0)
    n = steps * tm
    i = pl.program_id(0)

    def start_gather(step, slot):
        base = step * tm

        def issue(g, carry):
            for u in range(SUBLANES):
                for pick in range(2):
                    src = pl.multiple_of(pos_ref[pick * n + base + g * SUBLANES + u], SUBLANES)
                    dst = pl.multiple_of(g * SUBLANES * SUBLANES, SUBLANES) + u * SUBLANES
                    pltpu.make_async_copy(ys_hbm.at[pl.ds(src, SUBLANES), :],
                                          buf.at[slot, pick, pl.ds(dst, SUBLANES), :],
                                          sems.at[slot]).start(priority=pick)
            return carry

        lax.fori_loop(0, tm // SUBLANES, issue, 0)

    @pl.when(i == 0)
    def _():
        start_gather(0, 0)

    @pl.when(i + 1 < steps)
    def _():
        start_gather(i + 1, (i + 1) % 2)

    slot = i % 2
    pltpu.make_async_copy(buf.at[slot], buf.at[slot], sems.at[slot]).wait()
    route = route_ref[...]
    y1 = _load_token_major(buf, tm, (slot, 0))
    y2 = _load_token_major(buf, tm, (slot, 1))
    o_ref[...] = h_ref[...] + route[:, 2:3] * y1 + route[:, 3:4] * y2


def _combine(pos, h, route, ys, tm):
    n, d = h.shape
    row = lambda i, p: (i, 0)
    return pl.pallas_call(
        _combine_body,
        grid_spec=pltpu.PrefetchScalarGridSpec(
            num_scalar_prefetch=1,
            grid=(n // tm,),
            in_specs=[pl.BlockSpec((tm, d), row), pl.BlockSpec((tm, ROUTER_LANES), row),
                      pl.BlockSpec(memory_space=pl.ANY)],
            out_specs=pl.BlockSpec((tm, d), row),
            scratch_shapes=[pltpu.VMEM((2, 2, tm * SUBLANES, LANES), F32), pltpu.SemaphoreType.DMA((2,))],
        ),
        out_shape=jax.ShapeDtypeStruct((n, d), F32),
        compiler_params=_cparams(("arbitrary",)),
        name="moe_combine",
    )(pos, h, route, ys)


def _gelu(x):
    return 0.5 * x * (1.0 + lax.erf(x * (1.0 / math.sqrt(2.0))))


def _in_odd_body(h_ref, g_ref, w_ref, vg_ref, u_ref, vn_ref, q_ref, k_ref, v_ref):
    xn = _rms(h_ref[...], g_ref[...])
    proj = jnp.dot(xn.astype(BF16), w_ref[...], preferred_element_type=F32)
    w = MIX_WIDTH
    u_ref[...] = _gelu(proj[:, :w]).astype(BF16)
    vn_ref[...] = _rms(_gelu(proj[:, w:2 * w]), vg_ref[...]).astype(BF16)
    q_ref[...] = (proj[:, 2 * w:3 * w] * Q_SCALE).astype(BF16)
    k_ref[...] = proj[:, 3 * w:4 * w].astype(BF16)
    v_ref[...] = proj[:, 4 * w:].astype(BF16)


def _in_odd(h, gain, w, v_gain, tm):
    n, d = h.shape
    row = lambda i: (i, 0)
    const2 = lambda i: (0, 0)
    sds = jax.ShapeDtypeStruct((n, MIX_WIDTH), BF16)
    return pl.pallas_call(
        _in_odd_body,
        grid=(n // tm,),
        in_specs=[pl.BlockSpec((tm, d), row), pl.BlockSpec((1, d), const2),
                  pl.BlockSpec(w.shape, const2), pl.BlockSpec((1, MIX_WIDTH), const2)],
        out_specs=[pl.BlockSpec((tm, MIX_WIDTH), row)] * 5,
        out_shape=[sds] * 5,
        compiler_params=_cparams(("arbitrary",)),
        name="in_odd",
    )(h, gain, w, v_gain)


SB_T = 128
SB_NB = 3
SB_TK = SB_NB * SB_T
SB_QBLK = 512
SB_GROUP = 2
SB_UNDERFLOW = -150.0


def _sb_body(q_ref, k_ref, v_ref, o_ref, acc_scr, run_scr):
    qi = pl.program_id(2)
    lane = lax.broadcasted_iota(jnp.int32, (SB_T, LANES), 1)
    row = lax.broadcasted_iota(jnp.int32, (2 * SB_T, 1), 0) % SB_T
    col = lax.broadcasted_iota(jnp.int32, (2 * SB_T, SB_TK), 1)
    rr = lax.broadcasted_iota(jnp.int32, (2 * SB_T, 2 * SB_T), 0) % SB_T
    cc = lax.broadcasted_iota(jnp.int32, (2 * SB_T, 2 * SB_T), 1)
    suffix = jnp.where((cc >= SB_T) | (rr > cc), 1.0, 0.0).astype(BF16)

    def suffix_sums(x):
        hi = x.astype(BF16)
        lo = (x - hi.astype(F32)).astype(BF16)
        return jnp.dot(jnp.concatenate([hi, lo], axis=1), suffix, preferred_element_type=F32)

    def subtile_group(grp, _):
        q_los = [pl.multiple_of((grp * SB_GROUP + s) * SB_T, SB_T) for s in range(SB_GROUP)]
        q_starts = [qi * SB_QBLK + q_lo for q_lo in q_los]
        q_heads = []
        for q_lo in q_los:
            q = q_ref[0, pl.ds(q_lo, SB_T), :]
            zero = jnp.zeros_like(q)
            q_heads.append(jnp.concatenate(
                [jnp.where(lane < HEAD_DIM, q, zero), jnp.where(lane < HEAD_DIM, zero, q)], axis=0))
        acc_scr[...] = jnp.zeros_like(acc_scr)
        run_scr[...] = jnp.zeros_like(run_scr)

        def cond(carry):
            his, dones = carry
            active = [(hi > 0) & (done == 0) for hi, done in zip(his, dones)]
            return functools.reduce(jnp.logical_or, active)

        def body(carry):
            his, _ = carry
            group = range(SB_GROUP)
            kss = [pl.multiple_of(jnp.maximum(his[s] - SB_TK, 0), SB_T) for s in group]
            valid = [col < (jnp.minimum(row + q_starts[s], his[s]) - kss[s]) for s in group]
            zs = [lax.dot_general(q_heads[s], k_ref[0, pl.ds(kss[s], SB_TK), :], NT_DIMS,
                                  preferred_element_type=F32) for s in group]
            log_beta, log_rest = [], []
            for s in group:
                z = jnp.where(valid[s], zs[s], NEG_BIG)
                sp = jnp.log2(1.0 + jnp.exp2(-jnp.abs(z)))
                log_beta.append(jnp.minimum(z, 0.0) - sp)
                log_rest.append(log_beta[s] - z)
            sums = [[suffix_sums(log_rest[s][:, blk * SB_T:(blk + 1) * SB_T]) for blk in range(SB_NB)]
                    for s in group]
            dones = []
            for s in group:
                run = run_scr[s]
                pieces = [None] * SB_NB
                for blk in reversed(range(SB_NB)):
                    tt = sums[s][blk]
                    pieces[blk] = jnp.exp2(log_beta[s][:, blk * SB_T:(blk + 1) * SB_T] + tt[:, :SB_T] + run)
                    run = run + tt[:, SB_T:]
                att = jnp.concatenate(pieces, axis=1)
                acc_scr[s] += jnp.dot(att.astype(BF16), v_ref[0, pl.ds(kss[s], SB_TK), :],
                                      preferred_element_type=F32)
                run_scr[s] = run
                dones.append((jnp.max(run) <= SB_UNDERFLOW).astype(jnp.int32))
            return tuple(kss), tuple(dones)

        lax.while_loop(cond, body, (tuple(qs + SB_T for qs in q_starts),
                                    tuple(jnp.int32(0) for _ in range(SB_GROUP))))
        for s in range(SB_GROUP):
            o_ref[0, pl.ds(q_los[s], SB_T), :] = jnp.where(
                lane < HEAD_DIM, acc_scr[s, :SB_T], acc_scr[s, SB_T:]).astype(BF16)
        return 0

    lax.fori_loop(0, SB_QBLK // (SB_T * SB_GROUP), subtile_group, 0)


def _sb_attention(q, k, v):
    b, s, w = q.shape
    assert s % SB_QBLK == 0 and s >= SB_TK
    qspec = pl.BlockSpec((1, SB_QBLK, LANES), lambda bi, hp, i: (bi, i, hp))
    kvspec = pl.BlockSpec((1, s, LANES), lambda bi, hp, i: (bi, 0, hp))
    return pl.pallas_call(
        _sb_body,
        grid=(b, w // LANES, s // SB_QBLK),
        in_specs=[qspec, kvspec, kvspec],
        out_specs=qspec,
        out_shape=jax.ShapeDtypeStruct((b, s, w), BF16),
        scratch_shapes=[pltpu.VMEM((SB_GROUP, 2 * SB_T, LANES), F32),
                        pltpu.VMEM((SB_GROUP, 2 * SB_T, LANES), F32)],
        compiler_params=_cparams(("arbitrary", "arbitrary", "arbitrary")),
        name="sb_attention",
    )(q, k, v)


def _out_odd_body(u_ref, vn_ref, yd_ref, h_ref, ws_ref, bs_ref, wo_ref, fg_ref, wr_ref,
                  o_ref, xn_ref, route_ref, yc_scr):
    tm = u_ref.shape[0]
    r = lax.broadcasted_iota(jnp.int32, (SGU_BLOCK, SGU_BLOCK), 0)
    c = lax.broadcasted_iota(jnp.int32, (SGU_BLOCK, SGU_BLOCK), 1)
    for g in range(MIX_WIDTH // LANES):
        ls = slice(g * LANES, (g + 1) * LANES)
        ws = jnp.where(c <= r, ws_ref[g], jnp.zeros_like(ws_ref[g]))
        for blk in range(tm // SGU_BLOCK):
            rs = slice(blk * SGU_BLOCK, (blk + 1) * SGU_BLOCK)
            mixed = jnp.dot(ws, vn_ref[rs, ls], preferred_element_type=F32) + bs_ref[g]
            yc_scr[rs, ls] = (u_ref[rs, ls].astype(F32) * mixed).astype(BF16)
    w = MIX_WIDTH
    h_new = (h_ref[...]
             + jnp.dot(yc_scr[...], wo_ref[0:w, :], preferred_element_type=F32)
             + jnp.dot(yd_ref[...], wo_ref[w:, :], preferred_element_type=F32))
    o_ref[...] = h_new
    xn, route_ref[...] = _route_tokens(h_new, fg_ref[...], wr_ref[...])
    _store_token_major(xn_ref, xn)


def _out_odd(u, vn, yd, h, sgu_w, sgu_b, w_out, ffn_gain, w_router, tm):
    n, d = h.shape
    w = MIX_WIDTH
    row = lambda i: (i, 0)
    const2 = lambda i: (0, 0)
    const3 = lambda i: (0, 0, 0)
    return pl.pallas_call(
        _out_odd_body,
        grid=(n // tm,),
        in_specs=[pl.BlockSpec((tm, w), row), pl.BlockSpec((tm, w), row), pl.BlockSpec((tm, w), row),
                  pl.BlockSpec((tm, d), row), pl.BlockSpec(sgu_w.shape, const3),
                  pl.BlockSpec(sgu_b.shape, const3), pl.BlockSpec(w_out.shape, const2),
                  pl.BlockSpec((1, d), const2), pl.BlockSpec(w_router.shape, const2)],
        out_specs=[pl.BlockSpec((tm, d), row), pl.BlockSpec((tm * SUBLANES, LANES), row),
                   pl.BlockSpec((tm, ROUTER_LANES), row)],
        out_shape=[jax.ShapeDtypeStruct((n, d), F32), jax.ShapeDtypeStruct((n * SUBLANES, LANES), F32),
                   jax.ShapeDtypeStruct((n, ROUTER_LANES), F32)],
        scratch_shapes=[pltpu.VMEM((tm, w), BF16)],
        compiler_params=_cparams(("arbitrary",)),
        name="out_odd",
    )(u, vn, yd, h, sgu_w, sgu_b, w_out, ffn_gain, w_router)


def _moe_layer(h, xn, route, w_gate, w_up, w_down, layer):
    pos1, pos2, tile_map = _slots(route, tm=512)
    pos = jnp.concatenate([pos1[0], pos2[0]]).astype(jnp.int32) * SUBLANES
    tile_map = tile_map[0].astype(jnp.int32)
    xs = _dispatch(pos, tile_map, xn, tm=256)
    ys = _experts(tile_map, xs, w_gate, w_up, w_down, layer)
    return _combine(pos, h, route, ys, tm=256)


def kernel(x, mix_norm_even, w_in_even, att_q_norm, att_k_norm, att_rel_bias, pool_w, pool_scale,
           w_out_even, mix_norm_odd, w_in_odd, sgu_v_norm, sgu_w, sgu_b, w_out_odd, ffn_norm,
           w_router_group, w_router_expert, w_exp_gate, w_exp_up, w_exp_down):
    b, s, d = x.shape
    n = b * s
    depth = ffn_norm.shape[0]
    heads = MIX_WIDTH // HEAD_DIM
    h = x.reshape(n, d)
    w_gate = w_exp_gate.reshape(depth * N_EXPERTS, d, EXPERT_FF)
    w_up = w_exp_up.reshape(depth * N_EXPERTS, d, EXPERT_FF)
    w_down = w_exp_down.reshape(depth * N_EXPERTS, EXPERT_FF, d)
    for layer in range(depth):
        i = layer // 2
        ffn_gain = ffn_norm[layer][None, :]
        w_router = _router_weights(w_router_group[layer], w_router_expert[layer])
        if layer % 2 == 0:
            q, k_pad, v_pad, p = _in_even(
                h.reshape(b, s, d), mix_norm_even[i][None, :], w_in_even[i].astype(BF16),
                jnp.tile(att_q_norm[i], heads)[None, :], jnp.tile(att_k_norm[i], heads)[None, :],
                tm=ATT_LEFT)
            ya = _band_attention(q, k_pad, v_pad, _band_bias(att_rel_bias[i]))
            h, xn, route = _out_even(ya.reshape(n, MIX_WIDTH), p.reshape(n, MIX_WIDTH), h,
                                     pool_w[i].astype(BF16), pool_scale[i][None, :],
                                     w_out_even[i].astype(BF16), ffn_gain, w_router, seq=s, tm=512)
        else:
            u, vn, q, k, v = _in_odd(h, mix_norm_odd[i][None, :], w_in_odd[i].astype(BF16),
                                     sgu_v_norm[i][None, :], tm=512)
            to3 = lambda t: t.reshape(b, s, MIX_WIDTH)
            yd = _sb_attention(to3(q), to3(k), to3(v))
            bias = jnp.broadcast_to(sgu_b[i][:, :, None], (N_GROUPS, SGU_BLOCK, LANES))
            h, xn, route = _out_odd(u, vn, yd.reshape(n, MIX_WIDTH), h, sgu_w[i].astype(BF16), bias,
                                    w_out_odd[i].astype(BF16), ffn_gain, w_router, tm=512)
        h = _moe_layer(h, xn, route, w_gate, w_up, w_down, layer)
    return h.reshape(b, s, d)
```

```python
import functools
import math

import jax
import jax.numpy as jnp
from jax import lax
from jax.experimental import pallas as pl
from jax.experimental.pallas import tpu as pltpu

F32 = jnp.float32
BF16 = jnp.bfloat16

D_MODEL = 1024
CHUNK = 64
EPS = 1e-6
HEAD_DIM = 64
MIX_WIDTH = 512
LANES = 128
SUBLANES = 8
ATT_LEFT = 8 * CHUNK
ATT_MAX_REL = 128
POOL_WINDOWS = (2, 4, 8, 16)
POOL_HALO = 16
SGU_BLOCK = 128
N_GROUPS = 4
N_EXP_PER_GROUP = 8
N_EXPERTS = N_GROUPS * N_EXP_PER_GROUP
EXPERT_FF = 256
ROUTER_LANES = 128
ROUTER_ROWS = 40
NEG_BIG = -1e30
VMEM_LIMIT = 56 * 1024 * 1024

NT_DIMS = (((1,), (1,)), ((), ()))
LOG2E = math.log2(math.e)
Q_SCALE = LOG2E / math.sqrt(HEAD_DIM)


def _cparams(sem):
    return pltpu.CompilerParams(dimension_semantics=sem, vmem_limit_bytes=VMEM_LIMIT)


def _store_token_major(ref, x):
    rows = x.shape[0]
    for s in range(SUBLANES):
        ref[pl.ds(s, rows, stride=SUBLANES), :] = x[:, s * LANES:(s + 1) * LANES]


def _load_token_major(ref, rows, lead=()):
    return jnp.concatenate(
        [ref[lead + (pl.ds(s, rows, stride=SUBLANES), slice(None))] for s in range(SUBLANES)], axis=1)


def _rms(x, gain):
    return x * lax.rsqrt(jnp.mean(x * x, axis=-1, keepdims=True) + EPS) * gain


def _split_dot(x, m):
    hi = x.astype(BF16)
    lo = (x - hi.astype(F32)).astype(BF16)
    return (jnp.dot(hi, m, preferred_element_type=F32)
            + jnp.dot(lo, m, preferred_element_type=F32))


def _head_rms(t, gain):
    n = t.shape[-1]
    r = lax.broadcasted_iota(jnp.int32, (n, n), 0) // HEAD_DIM
    c = lax.broadcasted_iota(jnp.int32, (n, n), 1) // HEAD_DIM
    bd = jnp.where(r == c, 1.0, 0.0).astype(BF16)
    ms = _split_dot(t * t, bd) * (1.0 / HEAD_DIM)
    return t * lax.rsqrt(ms + EPS) * gain


def _in_even_body(h_ref, g_ref, w_ref, qg_ref, kg_ref, q_ref, k_ref, v_ref, p_ref):
    j = pl.program_id(1)

    @pl.when(j == 0)
    def _():
        k_ref[...] = jnp.zeros_like(k_ref)
        v_ref[...] = jnp.zeros_like(v_ref)

    @pl.when(j > 0)
    def _():
        xn = _rms(h_ref[0], g_ref[...])
        proj = jnp.dot(xn.astype(BF16), w_ref[...], preferred_element_type=F32)
        w = MIX_WIDTH
        q_ref[0] = (_head_rms(proj[:, :w], qg_ref[...]) * Q_SCALE).astype(BF16)
        k_ref[0] = _head_rms(proj[:, w:2 * w], kg_ref[...]).astype(BF16)
        v_ref[0] = proj[:, 2 * w:3 * w].astype(BF16)
        p_ref[0] = proj[:, 3 * w:].astype(BF16)


def _in_even(h, gain, w, q_gain, k_gain, tm):
    b, s, d = h.shape
    assert tm == ATT_LEFT and s % tm == 0
    nt = s // tm
    cur = lambda bi, j: (bi, jnp.maximum(j - 1, 0), 0)
    const = lambda bi, j: (0, 0)
    out_sds = lambda rows: jax.ShapeDtypeStruct((b, rows, MIX_WIDTH), BF16)
    return pl.pallas_call(
        _in_even_body,
        grid=(b, nt + 1),
        in_specs=[
            pl.BlockSpec((1, tm, d), cur),
            pl.BlockSpec((1, d), const),
            pl.BlockSpec(w.shape, const),
            pl.BlockSpec((1, MIX_WIDTH), const),
            pl.BlockSpec((1, MIX_WIDTH), const),
        ],
        out_specs=[
            pl.BlockSpec((1, tm, MIX_WIDTH), cur),
            pl.BlockSpec((1, tm, MIX_WIDTH), lambda bi, j: (bi, j, 0)),
            pl.BlockSpec((1, tm, MIX_WIDTH), lambda bi, j: (bi, j, 0)),
            pl.BlockSpec((1, tm, MIX_WIDTH), cur),
        ],
        out_shape=[out_sds(s), out_sds(s + ATT_LEFT), out_sds(s + ATT_LEFT), out_sds(s)],
        compiler_params=_cparams(("arbitrary", "arbitrary")),
        name="in_even",
    )(h, gain, w, q_gain, k_gain)


BAND_TQ = 2 * CHUNK
BAND_TK = BAND_TQ + ATT_LEFT


def _band_body(q_ref, k_ref, v_ref, bias_ref, o_ref):
    i = pl.program_id(1)
    start = pl.multiple_of(i * BAND_TQ, BAND_TQ)
    lane = lax.broadcasted_iota(jnp.int32, (BAND_TQ, LANES), 1)
    col = lax.broadcasted_iota(jnp.int32, (2 * BAND_TQ, BAND_TK), 1)
    is_pad = (col + start) < ATT_LEFT
    pairs = range(MIX_WIDTH // LANES)
    lanes = [slice(hp * LANES, (hp + 1) * LANES) for hp in pairs]
    scores = []
    for hp in pairs:
        q = q_ref[0, :, lanes[hp]]
        kb = k_ref[0, pl.ds(start, BAND_TK), lanes[hp]]
        zero = jnp.zeros_like(q)
        q2 = jnp.concatenate([jnp.where(lane < HEAD_DIM, q, zero), jnp.where(lane < HEAD_DIM, zero, q)], axis=0)
        scores.append(lax.dot_general(q2, kb, NT_DIMS, preferred_element_type=F32))
    probs, denoms = [], []
    for hp in pairs:
        bias = bias_ref[2 * hp:2 * hp + 2].reshape(2 * BAND_TQ, BAND_TK)
        s = jnp.where(is_pad, NEG_BIG, scores[hp] + bias)
        p = jnp.exp2(s - jnp.max(s, axis=-1, keepdims=True))
        denoms.append(jnp.sum(p, axis=-1, keepdims=True))
        probs.append(p.astype(BF16))
    for hp in pairs:
        vb = v_ref[0, pl.ds(start, BAND_TK), lanes[hp]]
        o = jnp.dot(probs[hp], vb, preferred_element_type=F32) / denoms[hp]
        o_ref[0, :, lanes[hp]] = jnp.where(lane < HEAD_DIM, o[:BAND_TQ], o[BAND_TQ:]).astype(BF16)


def _band_bias(rel_bias):
    heads = rel_bias.shape[0]
    r = jnp.arange(BAND_TQ)[:, None]
    j = jnp.arange(BAND_TK)[None, :]
    jb = j - CHUNK * (r // CHUNK)
    in_band = (jb >= 0) & (jb < ATT_LEFT + CHUNK)
    period = BAND_TK + BAND_TQ
    far = jnp.broadcast_to(rel_bias[:, 2 * ATT_MAX_REL:], (heads, ATT_LEFT - ATT_MAX_REL + 1))
    near = rel_bias[:, 2 * ATT_MAX_REL - 1:0:-1]
    wrap = jnp.broadcast_to(rel_bias[:, 2 * ATT_MAX_REL:], (heads, period - BAND_TK))
    g = jnp.concatenate([far, near, wrap], axis=1).astype(F32)
    assert g.shape[1] == period
    toep = jnp.tile(g, (1, BAND_TQ))[:, :BAND_TQ * (period - 1)].reshape(heads, BAND_TQ, period - 1)
    return jnp.where(in_band[None], toep[:, :, :BAND_TK] * LOG2E, NEG_BIG)


def _band_attention(q, k_pad, v_pad, bias):
    b, s, w = q.shape
    sp = k_pad.shape[1]
    return pl.pallas_call(
        _band_body,
        grid=(b, s // BAND_TQ),
        in_specs=[
            pl.BlockSpec((1, BAND_TQ, w), lambda bi, i: (bi, i, 0)),
            pl.BlockSpec((1, sp, w), lambda bi, i: (bi, 0, 0)),
            pl.BlockSpec((1, sp, w), lambda bi, i: (bi, 0, 0)),
            pl.BlockSpec(bias.shape, lambda bi, i: (0, 0, 0)),
        ],
        out_specs=pl.BlockSpec((1, BAND_TQ, w), lambda bi, i: (bi, i, 0)),
        out_shape=jax.ShapeDtypeStruct((b, s, w), BF16),
        compiler_params=_cparams(("arbitrary", "arbitrary")),
        name="band_attention",
    )(q, k_pad, v_pad, bias)


def _route_tokens(h, gain, w_router):
    xn = _rms(h, gain)
    x_hi = xn.astype(BF16)
    x_lo = (xn - x_hi.astype(F32)).astype(BF16)
    w_hi = w_router.astype(BF16)
    w_lo = (w_router - w_hi.astype(F32)).astype(BF16)
    logits = (jnp.dot(x_hi, w_hi, preferred_element_type=F32)
              + jnp.dot(x_lo, w_hi, preferred_element_type=F32)
              + jnp.dot(x_hi, w_lo, preferred_element_type=F32))
    lt = logits.T[:ROUTER_ROWS]
    sub = lax.broadcasted_iota(jnp.int32, lt.shape, 0).astype(F32)
    ninf = -jnp.inf

    def top(vals):
        m = jnp.max(vals, axis=0, keepdims=True)
        idx = jnp.min(jnp.where(vals == m, sub, float(ROUTER_LANES)), axis=0, keepdims=True)
        return m, idx

    is_group = sub < N_GROUPS
    g_max, g_sel = top(jnp.where(is_group, lt, ninf))
    g_den = jnp.sum(jnp.where(is_group, jnp.exp(lt - g_max), 0.0), axis=0, keepdims=True)
    g_weight = 1.0 / g_den
    lo = N_GROUPS + N_EXP_PER_GROUP * g_sel
    e_logits = jnp.where((sub >= lo) & (sub < lo + N_EXP_PER_GROUP), lt, ninf)
    e1, i1 = top(e_logits)
    e2, i2 = top(jnp.where(sub == i1, ninf, e_logits))
    t = jnp.exp(e2 - e1)
    w1 = g_weight / (1.0 + t)
    w2 = g_weight * t / (1.0 + t)
    rows = lax.broadcasted_iota(jnp.int32, (ROUTER_LANES, lt.shape[1]), 0)
    route_t = jnp.where(rows == 0, i1 - N_GROUPS, jnp.where(rows == 1, i2 - N_GROUPS, 0.0))
    route_t = jnp.where(rows == 2, w1, jnp.where(rows == 3, w2, route_t))
    return xn, route_t.T


def _router_weights(w_rg, w_re):
    pad = jnp.zeros((w_rg.shape[0], ROUTER_LANES - N_GROUPS - N_EXPERTS), F32)
    return jnp.concatenate([w_rg, w_re, pad], axis=1)


def _out_even_body(tiles_per_seq, ya_ref, p_ref, halo_ref, h_ref, pw_ref, ps_ref, wo_ref, fg_ref, wr_ref,
                   o_ref, xn_ref, route_ref, p_scr, yb_scr):
    tm = p_ref.shape[0]
    it = pl.program_id(0) % tiles_per_seq
    halo = halo_ref[...].astype(F32)
    p_scr[0:POOL_HALO, :] = jnp.where(it == 0, jnp.zeros_like(halo), halo)
    p_scr[POOL_HALO:, :] = p_ref[...].astype(F32)
    t = it * tm + lax.broadcasted_iota(jnp.int32, (tm, 1), 0)
    for g, win in enumerate(POOL_WINDOWS):
        ls = slice(g * LANES, (g + 1) * LANES)
        cur = p_scr[POOL_HALO:POOL_HALO + tm, ls]
        acc = cur
        for dlt in range(1, win):
            acc = acc + p_scr[POOL_HALO - dlt:POOL_HALO - dlt + tm, ls]
        cnt = jnp.minimum(t + 1, win).astype(F32)
        mixed = acc / cnt - cur
        yb = jnp.dot(mixed.astype(BF16), pw_ref[g], preferred_element_type=F32) * ps_ref[:, ls]
        yb_scr[:, ls] = yb.astype(BF16)
    w = MIX_WIDTH
    h_new = (h_ref[...]
             + jnp.dot(ya_ref[...], wo_ref[0:w, :], preferred_element_type=F32)
             + jnp.dot(yb_scr[...], wo_ref[w:, :], preferred_element_type=F32))
    o_ref[...] = h_new
    xn, route_ref[...] = _route_tokens(h_new, fg_ref[...], wr_ref[...])
    _store_token_major(xn_ref, xn)


def _out_even(ya, p, h, pool_w, pool_scale, w_out, ffn_gain, w_router, seq, tm):
    n, d = h.shape
    w = MIX_WIDTH
    row = lambda i: (i, 0)
    const2 = lambda i: (0, 0)
    halo_blocks = tm // POOL_HALO
    return pl.pallas_call(
        functools.partial(_out_even_body, seq // tm),
        grid=(n // tm,),
        in_specs=[
            pl.BlockSpec((tm, w), row),
            pl.BlockSpec((tm, w), row),
            pl.BlockSpec((POOL_HALO, w), lambda i: (jnp.maximum(i * halo_blocks - 1, 0), 0)),
            pl.BlockSpec((tm, d), row),
            pl.BlockSpec(pool_w.shape, lambda i: (0, 0, 0)),
            pl.BlockSpec((1, w), const2),
            pl.BlockSpec(w_out.shape, const2),
            pl.BlockSpec((1, d), const2),
            pl.BlockSpec(w_router.shape, const2),
        ],
        out_specs=[pl.BlockSpec((tm, d), row), pl.BlockSpec((tm * SUBLANES, LANES), row),
                   pl.BlockSpec((tm, ROUTER_LANES), row)],
        out_shape=[jax.ShapeDtypeStruct((n, d), F32), jax.ShapeDtypeStruct((n * SUBLANES, LANES), F32),
                   jax.ShapeDtypeStruct((n, ROUTER_LANES), F32)],
        scratch_shapes=[pltpu.VMEM((tm + POOL_HALO, w), F32), pltpu.VMEM((tm, w), BF16)],
        compiler_params=_cparams(("arbitrary",)),
        name="out_even",
    )(ya, p, p, h, pool_w, pool_scale, w_out, ffn_gain, w_router)


MOE_TM = 256
MOE_TILE_LANES = 256


def _moe_tiles(n):
    return (2 * n) // MOE_TM + N_EXPERTS


def _exact_dot_nt(ones, x):
    out = None
    for _ in range(3):
        part = x.astype(BF16)
        x = x - part.astype(F32)
        term = lax.dot_general(ones, part, NT_DIMS, preferred_element_type=F32)
        out = term if out is None else out + term
    return out


def _slots_body(n_tiles, route_ref, pos1_ref, pos2_ref, tile_ref, run_scr, start_scr):
    phase = pl.program_id(0)
    i = pl.program_id(1)
    tm = route_ref.shape[0]
    route = route_ref[...]
    lane = lax.broadcasted_iota(jnp.int32, (tm, ROUTER_LANES), 1).astype(F32)
    pick1 = jnp.where(lane == route[:, 0:1], 1.0, 0.0)
    pick2 = jnp.where(lane == route[:, 1:2], 1.0, 0.0)
    occ = (pick1 + pick2).astype(BF16)
    ones_rows = jnp.ones((SUBLANES, tm), BF16)
    ones_lanes = jnp.ones((SUBLANES, ROUTER_LANES), BF16)

    @pl.when(i == 0)
    def _():
        run_scr[...] = jnp.zeros_like(run_scr)

    @pl.when(phase == 0)
    def _():
        run_scr[...] += jnp.dot(ones_rows, occ, preferred_element_type=F32)

        @pl.when(i == pl.num_programs(1) - 1)
        def _():
            padded = jnp.floor((run_scr[...] + (MOE_TM - 1)) * (1.0 / MOE_TM)) * MOE_TM
            r = lax.broadcasted_iota(jnp.int32, (ROUTER_LANES, ROUTER_LANES), 0)
            c = lax.broadcasted_iota(jnp.int32, (ROUTER_LANES, ROUTER_LANES), 1)
            before = jnp.where(r < c, 1.0, 0.0).astype(BF16)
            hi = padded.astype(BF16)
            mid = (padded - hi.astype(F32)).astype(BF16)
            low = (padded - hi.astype(F32) - mid.astype(F32)).astype(BF16)
            start = (jnp.dot(hi, before, preferred_element_type=F32)
                     + jnp.dot(mid, before, preferred_element_type=F32)
                     + jnp.dot(low, before, preferred_element_type=F32))
            start_scr[...] = start
            seg_end = start[0:1, :] + padded[0:1, :]
            tile_lo = (lax.broadcasted_iota(jnp.int32, (MOE_TILE_LANES, ROUTER_LANES), 0) * MOE_TM).astype(F32)
            e_lane = lax.broadcasted_iota(jnp.int32, (MOE_TILE_LANES, ROUTER_LANES), 1)
            ended = jnp.where((seg_end <= tile_lo) & (e_lane < N_EXPERTS), 1.0, 0.0).astype(BF16)
            tile_ref[...] = lax.dot_general(ones_lanes, ended, NT_DIMS, preferred_element_type=F32)

    @pl.when(phase == 1)
    def _():
        r = lax.broadcasted_iota(jnp.int32, (tm, tm), 0)
        c = lax.broadcasted_iota(jnp.int32, (tm, tm), 1)
        earlier = jnp.where(c < r, 1.0, 0.0).astype(BF16)
        base = (jnp.dot(earlier, occ, preferred_element_type=F32)
                + run_scr[0:1, :] + start_scr[0:1, :])
        pos1_ref[...] = _exact_dot_nt(ones_lanes, pick1 * base)
        pos2_ref[...] = _exact_dot_nt(ones_lanes, pick2 * base)
        run_scr[...] += jnp.dot(ones_rows, occ, preferred_element_type=F32)


def _slots(route, tm):
    n = route.shape[0]
    n_tiles = _moe_tiles(n)
    assert n_tiles <= MOE_TILE_LANES and 2 * n + N_EXPERTS * MOE_TM < 2 ** 24
    row_out = pl.BlockSpec((SUBLANES, tm), lambda ph, i: (0, i * ph))
    sds = jax.ShapeDtypeStruct((SUBLANES, n), F32)
    return pl.pallas_call(
        functools.partial(_slots_body, n_tiles),
        grid=(2, n // tm),
        in_specs=[pl.BlockSpec((tm, ROUTER_LANES), lambda ph, i: (i, 0))],
        out_specs=[row_out, row_out, pl.BlockSpec((SUBLANES, MOE_TILE_LANES), lambda ph, i: (0, 0))],
        out_shape=[sds, sds, jax.ShapeDtypeStruct((SUBLANES, MOE_TILE_LANES), F32)],
        scratch_shapes=[pltpu.VMEM((SUBLANES, ROUTER_LANES), F32), pltpu.VMEM((SUBLANES, ROUTER_LANES), F32)],
        compiler_params=_cparams(("arbitrary", "arbitrary")),
        name="moe_slots",
    )(route)


def _dispatch_body(n_tiles, pos_ref, tile_ref, xn_ref, xs_hbm, zero_scr, zero_sem, row_sem):
    tm = xn_ref.shape[0] // SUBLANES
    n = pl.num_programs(0) * tm
    base = pl.program_id(0) * tm
    tile_rows = MOE_TM * SUBLANES

    @pl.when(pl.program_id(0) == 0)
    def _():
        zero_scr[...] = jnp.zeros_like(zero_scr)

        def fill_copy(t):
            return pltpu.make_async_copy(zero_scr, xs_hbm.at[pl.ds(t * tile_rows, tile_rows), :], zero_sem)

        def has_padding(t):
            return (tile_ref[t] >= N_EXPERTS) | (tile_ref[t] != tile_ref[t + 1])

        @pl.loop(0, n_tiles)
        def _(t):
            @pl.when(has_padding(t))
            def _():
                fill_copy(t).start()

        @pl.loop(0, n_tiles)
        def _(t):
            @pl.when(has_padding(t))
            def _():
                fill_copy(t).wait()

    def issue(j, carry):
        src = xn_ref.at[pl.ds(pl.multiple_of(j * SUBLANES, SUBLANES), SUBLANES), :]
        for pick in range(2):
            dst = pl.multiple_of(pos_ref[pick * n + base + j], SUBLANES)
            pltpu.make_async_copy(src, xs_hbm.at[pl.ds(dst, SUBLANES), :], row_sem).start(priority=pick)
        return carry

    lax.fori_loop(0, tm, issue, 0, unroll=8)
    for _ in range(2):
        pltpu.make_async_copy(xn_ref, xs_hbm.at[pl.ds(0, tm * SUBLANES), :], row_sem).wait()


def _dispatch(pos, tile_map, xn, tm):
    n = xn.shape[0] // SUBLANES
    n_tiles = _moe_tiles(n)
    return pl.pallas_call(
        functools.partial(_dispatch_body, n_tiles),
        grid_spec=pltpu.PrefetchScalarGridSpec(
            num_scalar_prefetch=2,
            grid=(n // tm,),
            in_specs=[pl.BlockSpec((tm * SUBLANES, LANES), lambda i, p, t: (i, 0))],
            out_specs=pl.BlockSpec(memory_space=pl.ANY),
            scratch_shapes=[pltpu.VMEM((MOE_TM * SUBLANES, LANES), F32), pltpu.SemaphoreType.DMA(()),
                            pltpu.SemaphoreType.DMA(())],
        ),
        out_shape=jax.ShapeDtypeStruct((n_tiles * MOE_TM * SUBLANES, LANES), F32),
        compiler_params=_cparams(("arbitrary",)),
        name="moe_dispatch",
    )(pos, tile_map, xn)


EXPERT_SLOTS = 3


def _experts_body(tile_ref, xs_hbm, wg_ref, wu_ref, wd_ref, ys_ref, xbuf, sems):
    i = pl.program_id(0)
    tile_rows = MOE_TM * SUBLANES
    used = tile_ref[i] < N_EXPERTS

    def tile_copy(t, slot):
        rows = pl.ds(pl.multiple_of(t * tile_rows, tile_rows), tile_rows)
        return pltpu.make_async_copy(xs_hbm.at[rows, :], xbuf.at[slot], sems.at[slot])

    def prefetch(t):
        @pl.when(tile_ref[t] < N_EXPERTS)
        def _():
            tile_copy(t, t % EXPERT_SLOTS).start()

    @pl.when(i == 0)
    def _():
        prefetch(0)
        prefetch(1)

    prefetch(i + 2)

    @pl.when(used)
    def _():
        slot = i % EXPERT_SLOTS
        tile_copy(i, slot).wait()
        x = _load_token_major(xbuf, MOE_TM, (slot,)).astype(BF16)
        gate = jnp.dot(x, wg_ref[0].astype(BF16), preferred_element_type=F32)
        up = jnp.dot(x, wu_ref[0].astype(BF16), preferred_element_type=F32)
        hid = gate * jax.nn.sigmoid(gate) * up
        _store_token_major(ys_ref, jnp.dot(hid.astype(BF16), wd_ref[0].astype(BF16),
                                           preferred_element_type=F32))

    @pl.when(jnp.logical_not(used))
    def _():
        ys_ref[...] = jnp.zeros_like(ys_ref)


def _experts(tile_map, xs, w_gate, w_up, w_down, layer):
    d, f = w_gate.shape[1:]
    tile_rows = MOE_TM * SUBLANES
    n_tiles = xs.shape[0] // tile_rows
    y_map = lambda i, tm_ref: (i, 0)
    w_map = lambda i, tm_ref: (layer * N_EXPERTS + jnp.minimum(tm_ref[i], N_EXPERTS - 1), 0, 0)
    return pl.pallas_call(
        _experts_body,
        grid_spec=pltpu.PrefetchScalarGridSpec(
            num_scalar_prefetch=1,
            grid=(n_tiles,),
            in_specs=[pl.BlockSpec(memory_space=pl.ANY), pl.BlockSpec((1, d, f), w_map),
                      pl.BlockSpec((1, d, f), w_map), pl.BlockSpec((1, f, d), w_map)],
            out_specs=pl.BlockSpec((tile_rows, LANES), y_map),
            scratch_shapes=[pltpu.VMEM((EXPERT_SLOTS, tile_rows, LANES), F32),
                            pltpu.SemaphoreType.DMA((EXPERT_SLOTS,))],
        ),
        out_shape=jax.ShapeDtypeStruct(xs.shape, F32),
        compiler_params=_cparams(("arbitrary",)),
        name="moe_experts",
    )(tile_map, xs, w_gate, w_up, w_down)


def _combine_body(pos_ref, h_ref, route_ref, ys_hbm, o_ref, buf, sems):
    tm = h_ref.shape[0]
    steps = pl.num_programs(0)
    n = steps * tm
    i = pl.program_id(0)

    def start_gather(step, slot):
        base = step * tm

        def issue(g, carry):
            for u in range(SUBLANES):
                for pick in range(2):
                    src = pl.multiple_of(pos_ref[pick * n + base + g * SUBLANES + u], SUBLANES)
                    dst = pl.multiple_of(g * SUBLANES * SUBLANES, SUBLANES) + u * SUBLANES
                    pltpu.make_async_copy(ys_hbm.at[pl.ds(src, SUBLANES), :],
                                          buf.at[slot, pick, pl.ds(dst, SUBLANES), :],
                                          sems.at[slot]).start(priority=pick)
            return carry

        lax.fori_loop(0, tm // SUBLANES, issue, 0)

    @pl.when(i == 0)
    def _():
        start_gather(0, 0)

    @pl.when(i + 1 < steps)
    def _():
        start_gather(i + 1, (i + 1) % 2)

    slot = i % 2
    pltpu.make_async_copy(buf.at[slot], buf.at[slot], sems.at[slot]).wait()
    route = route_ref[...]
    y1 = _load_token_major(buf, tm, (slot, 0))
    y2 = _load_token_major(buf, tm, (slot, 1))
    o_ref[...] = h_ref[...] + route[:, 2:3] * y1 + route[:, 3:4] * y2


def _combine(pos, h, route, ys, tm):
    n, d = h.shape
    row = lambda i, p: (i, 0)
    return pl.pallas_call(
        _combine_body,
        grid_spec=pltpu.PrefetchScalarGridSpec(
            num_scalar_prefetch=1,
            grid=(n // tm,),
            in_specs=[pl.BlockSpec((tm, d), row), pl.BlockSpec((tm, ROUTER_LANES), row),
                      pl.BlockSpec(memory_space=pl.ANY)],
            out_specs=pl.BlockSpec((tm, d), row),
            scratch_shapes=[pltpu.VMEM((2, 2, tm * SUBLANES, LANES), F32), pltpu.SemaphoreType.DMA((2,))],
        ),
        out_shape=jax.ShapeDtypeStruct((n, d), F32),
        compiler_params=_cparams(("arbitrary",)),
        name="moe_combine",
    )(pos, h, route, ys)


def _gelu(x):
    return 0.5 * x * (1.0 + lax.erf(x * (1.0 / math.sqrt(2.0))))


def _in_odd_body(h_ref, g_ref, w_ref, vg_ref, u_ref, vn_ref, q_ref, k_ref, v_ref):
    xn = _rms(h_ref[...], g_ref[...])
    proj = jnp.dot(xn.astype(BF16), w_ref[...], preferred_element_type=F32)
    w = MIX_WIDTH
    u_ref[...] = _gelu(proj[:, :w]).astype(BF16)
    vn_ref[...] = _rms(_gelu(proj[:, w:2 * w]), vg_ref[...]).astype(BF16)
    q_ref[...] = (proj[:, 2 * w:3 * w] * Q_SCALE).astype(BF16)
    k_ref[...] = proj[:, 3 * w:4 * w].astype(BF16)
    v_ref[...] = proj[:, 4 * w:].astype(BF16)


def _in_odd(h, gain, w, v_gain, tm):
    n, d = h.shape
    row = lambda i: (i, 0)
    const2 = lambda i: (0, 0)
    sds = jax.ShapeDtypeStruct((n, MIX_WIDTH), BF16)
    return pl.pallas_call(
        _in_odd_body,
        grid=(n // tm,),
        in_specs=[pl.BlockSpec((tm, d), row), pl.BlockSpec((1, d), const2),
                  pl.BlockSpec(w.shape, const2), pl.BlockSpec((1, MIX_WIDTH), const2)],
        out_specs=[pl.BlockSpec((tm, MIX_WIDTH), row)] * 5,
        out_shape=[sds] * 5,
        compiler_params=_cparams(("arbitrary",)),
        name="in_odd",
    )(h, gain, w, v_gain)


SB_TQ = 64
SB_KB = 128
SB_NB = 2
SB_TK = SB_NB * SB_KB
SB_QBLK = 512
SB_GROUP = 4
SB_UNDERFLOW = -150.0


def _sb_body(q_ref, k_ref, v_ref, o_ref, acc_scr, run_scr):
    qi = pl.program_id(2)
    lane = lax.broadcasted_iota(jnp.int32, (SB_TQ, LANES), 1)
    row = lax.broadcasted_iota(jnp.int32, (2 * SB_TQ, 1), 0) % SB_TQ
    col = lax.broadcasted_iota(jnp.int32, (2 * SB_TQ, SB_TK), 1)
    rr = lax.broadcasted_iota(jnp.int32, (2 * SB_KB, 2 * SB_KB), 0) % SB_KB
    cc = lax.broadcasted_iota(jnp.int32, (2 * SB_KB, 2 * SB_KB), 1)
    suffix = jnp.where((cc >= SB_KB) | (rr > cc), 1.0, 0.0).astype(BF16)

    def suffix_sums(x):
        hi = x.astype(BF16)
        lo = (x - hi.astype(F32)).astype(BF16)
        return jnp.dot(jnp.concatenate([hi, lo], axis=1), suffix, preferred_element_type=F32)

    def subtile_group(grp, _):
        q_los = [pl.multiple_of((grp * SB_GROUP + s) * SB_TQ, SB_TQ) for s in range(SB_GROUP)]
        q_starts = [qi * SB_QBLK + q_lo for q_lo in q_los]
        q_heads = []
        for q_lo in q_los:
            q = q_ref[0, pl.ds(q_lo, SB_TQ), :]
            zero = jnp.zeros_like(q)
            q_heads.append(jnp.concatenate(
                [jnp.where(lane < HEAD_DIM, q, zero), jnp.where(lane < HEAD_DIM, zero, q)], axis=0))
        acc_scr[...] = jnp.zeros_like(acc_scr)
        run_scr[...] = jnp.zeros_like(run_scr)

        def cond(carry):
            his, dones = carry
            active = [(hi > 0) & (done == 0) for hi, done in zip(his, dones)]
            return functools.reduce(jnp.logical_or, active)

        def body(carry):
            his, _ = carry
            group = range(SB_GROUP)
            kss = [pl.multiple_of(jnp.maximum(his[s] - SB_TK, 0), SB_TQ) for s in group]
            valid = [col < (jnp.minimum(row + q_starts[s], his[s]) - kss[s]) for s in group]
            zs = [lax.dot_general(q_heads[s], k_ref[0, pl.ds(kss[s], SB_TK), :], NT_DIMS,
                                  preferred_element_type=F32) for s in group]
            log_beta, log_rest = [], []
            for s in group:
                z = jnp.where(valid[s], zs[s], NEG_BIG)
                sp = jnp.log2(1.0 + jnp.exp2(-jnp.abs(z)))
                log_beta.append(jnp.minimum(z, 0.0) - sp)
                log_rest.append(log_beta[s] - z)
            sums = [[suffix_sums(log_rest[s][:, blk * SB_KB:(blk + 1) * SB_KB]) for blk in range(SB_NB)]
                    for s in group]
            dones = []
            for s in group:
                run = run_scr[s]
                pieces = [None] * SB_NB
                for blk in reversed(range(SB_NB)):
                    tt = sums[s][blk]
                    pieces[blk] = jnp.exp2(log_beta[s][:, blk * SB_KB:(blk + 1) * SB_KB] + tt[:, :SB_KB] + run)
                    run = run + tt[:, SB_KB:]
                att = jnp.concatenate(pieces, axis=1)
                acc_scr[s] += jnp.dot(att.astype(BF16), v_ref[0, pl.ds(kss[s], SB_TK), :],
                                      preferred_element_type=F32)
                run_scr[s] = run
                dones.append((jnp.max(run) <= SB_UNDERFLOW).astype(jnp.int32))
            return tuple(kss), tuple(dones)

        lax.while_loop(cond, body, (tuple(qs + SB_TQ for qs in q_starts),
                                    tuple(jnp.int32(0) for _ in range(SB_GROUP))))
        for s in range(SB_GROUP):
            o_ref[0, pl.ds(q_los[s], SB_TQ), :] = jnp.where(
                lane < HEAD_DIM, acc_scr[s, :SB_TQ], acc_scr[s, SB_TQ:]).astype(BF16)
        return 0

    lax.fori_loop(0, SB_QBLK // (SB_TQ * SB_GROUP), subtile_group, 0)


def _sb_attention(q, k, v):
    b, s, w = q.shape
    assert s % SB_QBLK == 0 and s >= SB_TK
    qspec = pl.BlockSpec((1, SB_QBLK, LANES), lambda bi, hp, i: (bi, i, hp))
    kvspec = pl.BlockSpec((1, s, LANES), lambda bi, hp, i: (bi, 0, hp))
    return pl.pallas_call(
        _sb_body,
        grid=(b, w // LANES, s // SB_QBLK),
        in_specs=[qspec, kvspec, kvspec],
        out_specs=qspec,
        out_shape=jax.ShapeDtypeStruct((b, s, w), BF16),
        scratch_shapes=[pltpu.VMEM((SB_GROUP, 2 * SB_TQ, LANES), F32),
                        pltpu.VMEM((SB_GROUP, 2 * SB_TQ, LANES), F32)],
        compiler_params=_cparams(("arbitrary", "arbitrary", "arbitrary")),
        name="sb_attention",
    )(q, k, v)


def _out_odd_body(u_ref, vn_ref, yd_ref, h_ref, ws_ref, bs_ref, wo_ref, fg_ref, wr_ref,
                  o_ref, xn_ref, route_ref, yc_scr):
    tm = u_ref.shape[0]
    r = lax.broadcasted_iota(jnp.int32, (SGU_BLOCK, SGU_BLOCK), 0)
    c = lax.broadcasted_iota(jnp.int32, (SGU_BLOCK, SGU_BLOCK), 1)
    for g in range(MIX_WIDTH // LANES):
        ls = slice(g * LANES, (g + 1) * LANES)
        ws = jnp.where(c <= r, ws_ref[g], jnp.zeros_like(ws_ref[g]))
        for blk in range(tm // SGU_BLOCK):
            rs = slice(blk * SGU_BLOCK, (blk + 1) * SGU_BLOCK)
            mixed = jnp.dot(ws, vn_ref[rs, ls], preferred_element_type=F32) + bs_ref[g]
            yc_scr[rs, ls] = (u_ref[rs, ls].astype(F32) * mixed).astype(BF16)
    w = MIX_WIDTH
    h_new = (h_ref[...]
             + jnp.dot(yc_scr[...], wo_ref[0:w, :], preferred_element_type=F32)
             + jnp.dot(yd_ref[...], wo_ref[w:, :], preferred_element_type=F32))
    o_ref[...] = h_new
    xn, route_ref[...] = _route_tokens(h_new, fg_ref[...], wr_ref[...])
    _store_token_major(xn_ref, xn)


def _out_odd(u, vn, yd, h, sgu_w, sgu_b, w_out, ffn_gain, w_router, tm):
    n, d = h.shape
    w = MIX_WIDTH
    row = lambda i: (i, 0)
    const2 = lambda i: (0, 0)
    const3 = lambda i: (0, 0, 0)
    return pl.pallas_call(
        _out_odd_body,
        grid=(n // tm,),
        in_specs=[pl.BlockSpec((tm, w), row), pl.BlockSpec((tm, w), row), pl.BlockSpec((tm, w), row),
                  pl.BlockSpec((tm, d), row), pl.BlockSpec(sgu_w.shape, const3),
                  pl.BlockSpec(sgu_b.shape, const3), pl.BlockSpec(w_out.shape, const2),
                  pl.BlockSpec((1, d), const2), pl.BlockSpec(w_router.shape, const2)],
        out_specs=[pl.BlockSpec((tm, d), row), pl.BlockSpec((tm * SUBLANES, LANES), row),
                   pl.BlockSpec((tm, ROUTER_LANES), row)],
        out_shape=[jax.ShapeDtypeStruct((n, d), F32), jax.ShapeDtypeStruct((n * SUBLANES, LANES), F32),
                   jax.ShapeDtypeStruct((n, ROUTER_LANES), F32)],
        scratch_shapes=[pltpu.VMEM((tm, w), BF16)],
        compiler_params=_cparams(("arbitrary",)),
        name="out_odd",
    )(u, vn, yd, h, sgu_w, sgu_b, w_out, ffn_gain, w_router)


def _moe_layer(h, xn, route, w_gate, w_up, w_down, layer):
    pos1, pos2, tile_map = _slots(route, tm=1024)
    pos = jnp.concatenate([pos1[0], pos2[0]]).astype(jnp.int32) * SUBLANES
    tile_map = tile_map[0].astype(jnp.int32)
    xs = _dispatch(pos, tile_map, xn, tm=256)
    ys = _experts(tile_map, xs, w_gate, w_up, w_down, layer)
    return _combine(pos, h, route, ys, tm=256)


def kernel(x, mix_norm_even, w_in_even, att_q_norm, att_k_norm, att_rel_bias, pool_w, pool_scale,
           w_out_even, mix_norm_odd, w_in_odd, sgu_v_norm, sgu_w, sgu_b, w_out_odd, ffn_norm,
           w_router_group, w_router_expert, w_exp_gate, w_exp_up, w_exp_down):
    b, s, d = x.shape
    n = b * s
    depth = ffn_norm.shape[0]
    heads = MIX_WIDTH // HEAD_DIM
    h = x.reshape(n, d)
    w_gate = w_exp_gate.reshape(depth * N_EXPERTS, d, EXPERT_FF)
    w_up = w_exp_up.reshape(depth * N_EXPERTS, d, EXPERT_FF)
    w_down = w_exp_down.reshape(depth * N_EXPERTS, EXPERT_FF, d)
    for layer in range(depth):
        i = layer // 2
        ffn_gain = ffn_norm[layer][None, :]
        w_router = _router_weights(w_router_group[layer], w_router_expert[layer])
        if layer % 2 == 0:
            q, k_pad, v_pad, p = _in_even(
                h.reshape(b, s, d), mix_norm_even[i][None, :], w_in_even[i].astype(BF16),
                jnp.tile(att_q_norm[i], heads)[None, :], jnp.tile(att_k_norm[i], heads)[None, :],
                tm=ATT_LEFT)
            ya = _band_attention(q, k_pad, v_pad, _band_bias(att_rel_bias[i]))
            h, xn, route = _out_even(ya.reshape(n, MIX_WIDTH), p.reshape(n, MIX_WIDTH), h,
                                     pool_w[i].astype(BF16), pool_scale[i][None, :],
                                     w_out_even[i].astype(BF16), ffn_gain, w_router, seq=s, tm=512)
        else:
            u, vn, q, k, v = _in_odd(h, mix_norm_odd[i][None, :], w_in_odd[i].astype(BF16),
                                     sgu_v_norm[i][None, :], tm=512)
            to3 = lambda t: t.reshape(b, s, MIX_WIDTH)
            yd = _sb_attention(to3(q), to3(k), to3(v))
            bias = jnp.broadcast_to(sgu_b[i][:, :, None], (N_GROUPS, SGU_BLOCK, LANES))
            h, xn, route = _out_odd(u, vn, yd.reshape(n, MIX_WIDTH), h, sgu_w[i].astype(BF16), bias,
                                    w_out_odd[i].astype(BF16), ffn_gain, w_router, tm=512)
        h = _moe_layer(h, xn, route, w_gate, w_up, w_down, layer)
    return h.reshape(b, s, d)
```

```python
import functools
import math

import jax
import jax.numpy as jnp
from jax import lax
from jax.experimental import pallas as pl
from jax.experimental.pallas import tpu as pltpu

F32 = jnp.float32
BF16 = jnp.bfloat16

D_MODEL = 1024
CHUNK = 64
EPS = 1e-6
HEAD_DIM = 64
MIX_WIDTH = 512
LANES = 128
SUBLANES = 8
ATT_LEFT = 8 * CHUNK
ATT_MAX_REL = 128
POOL_WINDOWS = (2, 4, 8, 16)
POOL_HALO = 16
SGU_BLOCK = 128
N_GROUPS = 4
N_EXP_PER_GROUP = 8
N_EXPERTS = N_GROUPS * N_EXP_PER_GROUP
EXPERT_FF = 256
ROUTER_LANES = 128
ROUTER_ROWS = 40
NEG_BIG = -1e30
VMEM_LIMIT = 56 * 1024 * 1024

NT_DIMS = (((1,), (1,)), ((), ()))
LOG2E = math.log2(math.e)
Q_SCALE = LOG2E / math.sqrt(HEAD_DIM)


def _cparams(sem):
    return pltpu.CompilerParams(dimension_semantics=sem, vmem_limit_bytes=VMEM_LIMIT)


def _store_token_major(ref, x):
    rows = x.shape[0]
    for s in range(SUBLANES):
        ref[pl.ds(s, rows, stride=SUBLANES), :] = x[:, s * LANES:(s + 1) * LANES]


def _load_token_major(ref, rows, lead=()):
    return jnp.concatenate(
        [ref[lead + (pl.ds(s, rows, stride=SUBLANES), slice(None))] for s in range(SUBLANES)], axis=1)


def _rms(x, gain):
    return x * lax.rsqrt(jnp.mean(x * x, axis=-1, keepdims=True) + EPS) * gain


def _split_dot(x, m):
    hi = x.astype(BF16)
    lo = (x - hi.astype(F32)).astype(BF16)
    return (jnp.dot(hi, m, preferred_element_type=F32)
            + jnp.dot(lo, m, preferred_element_type=F32))


def _head_rms(t, gain):
    n = t.shape[-1]
    r = lax.broadcasted_iota(jnp.int32, (n, n), 0) // HEAD_DIM
    c = lax.broadcasted_iota(jnp.int32, (n, n), 1) // HEAD_DIM
    bd = jnp.where(r == c, 1.0, 0.0).astype(BF16)
    ms = _split_dot(t * t, bd) * (1.0 / HEAD_DIM)
    return t * lax.rsqrt(ms + EPS) * gain


def _in_even_body(h_ref, g_ref, w_ref, qg_ref, kg_ref, q_ref, k_ref, v_ref, p_ref):
    j = pl.program_id(1)

    @pl.when(j == 0)
    def _():
        k_ref[...] = jnp.zeros_like(k_ref)
        v_ref[...] = jnp.zeros_like(v_ref)

    @pl.when(j > 0)
    def _():
        xn = _rms(h_ref[0], g_ref[...])
        proj = jnp.dot(xn.astype(BF16), w_ref[...], preferred_element_type=F32)
        w = MIX_WIDTH
        q_ref[0] = (_head_rms(proj[:, :w], qg_ref[...]) * Q_SCALE).astype(BF16)
        k_ref[0] = _head_rms(proj[:, w:2 * w], kg_ref[...]).astype(BF16)
        v_ref[0] = proj[:, 2 * w:3 * w].astype(BF16)
        p_ref[0] = proj[:, 3 * w:].astype(BF16)


def _in_even(h, gain, w, q_gain, k_gain, tm):
    b, s, d = h.shape
    assert tm == ATT_LEFT and s % tm == 0
    nt = s // tm
    cur = lambda bi, j: (bi, jnp.maximum(j - 1, 0), 0)
    const = lambda bi, j: (0, 0)
    out_sds = lambda rows: jax.ShapeDtypeStruct((b, rows, MIX_WIDTH), BF16)
    return pl.pallas_call(
        _in_even_body,
        grid=(b, nt + 1),
        in_specs=[
            pl.BlockSpec((1, tm, d), cur),
            pl.BlockSpec((1, d), const),
            pl.BlockSpec(w.shape, const),
            pl.BlockSpec((1, MIX_WIDTH), const),
            pl.BlockSpec((1, MIX_WIDTH), const),
        ],
        out_specs=[
            pl.BlockSpec((1, tm, MIX_WIDTH), cur),
            pl.BlockSpec((1, tm, MIX_WIDTH), lambda bi, j: (bi, j, 0)),
            pl.BlockSpec((1, tm, MIX_WIDTH), lambda bi, j: (bi, j, 0)),
            pl.BlockSpec((1, tm, MIX_WIDTH), cur),
        ],
        out_shape=[out_sds(s), out_sds(s + ATT_LEFT), out_sds(s + ATT_LEFT), out_sds(s)],
        compiler_params=_cparams(("arbitrary", "arbitrary")),
        name="in_even",
    )(h, gain, w, q_gain, k_gain)


BAND_TQ = 2 * CHUNK
BAND_TK = BAND_TQ + ATT_LEFT


def _band_body(q_ref, k_ref, v_ref, bias_ref, o_ref):
    i = pl.program_id(1)
    start = pl.multiple_of(i * BAND_TQ, BAND_TQ)
    lane = lax.broadcasted_iota(jnp.int32, (BAND_TQ, LANES), 1)
    col = lax.broadcasted_iota(jnp.int32, (2 * BAND_TQ, BAND_TK), 1)
    is_pad = (col + start) < ATT_LEFT
    pairs = range(MIX_WIDTH // LANES)
    lanes = [slice(hp * LANES, (hp + 1) * LANES) for hp in pairs]
    scores = []
    for hp in pairs:
        q = q_ref[0, :, lanes[hp]]
        kb = k_ref[0, pl.ds(start, BAND_TK), lanes[hp]]
        zero = jnp.zeros_like(q)
        q2 = jnp.concatenate([jnp.where(lane < HEAD_DIM, q, zero), jnp.where(lane < HEAD_DIM, zero, q)], axis=0)
        scores.append(lax.dot_general(q2, kb, NT_DIMS, preferred_element_type=F32))
    probs, denoms = [], []
    for hp in pairs:
        bias = bias_ref[2 * hp:2 * hp + 2].reshape(2 * BAND_TQ, BAND_TK)
        s = jnp.where(is_pad, NEG_BIG, scores[hp] + bias)
        p = jnp.exp2(s - jnp.max(s, axis=-1, keepdims=True))
        denoms.append(jnp.sum(p, axis=-1, keepdims=True))
        probs.append(p.astype(BF16))
    for hp in pairs:
        vb = v_ref[0, pl.ds(start, BAND_TK), lanes[hp]]
        o = jnp.dot(probs[hp], vb, preferred_element_type=F32) / denoms[hp]
        o_ref[0, :, lanes[hp]] = jnp.where(lane < HEAD_DIM, o[:BAND_TQ], o[BAND_TQ:]).astype(BF16)


def _band_bias(rel_bias):
    heads = rel_bias.shape[0]
    r = jnp.arange(BAND_TQ)[:, None]
    j = jnp.arange(BAND_TK)[None, :]
    jb = j - CHUNK * (r // CHUNK)
    in_band = (jb >= 0) & (jb < ATT_LEFT + CHUNK)
    period = BAND_TK + BAND_TQ
    far = jnp.broadcast_to(rel_bias[:, 2 * ATT_MAX_REL:], (heads, ATT_LEFT - ATT_MAX_REL + 1))
    near = rel_bias[:, 2 * ATT_MAX_REL - 1:0:-1]
    wrap = jnp.broadcast_to(rel_bias[:, 2 * ATT_MAX_REL:], (heads, period - BAND_TK))
    g = jnp.concatenate([far, near, wrap], axis=1).astype(F32)
    assert g.shape[1] == period
    toep = jnp.tile(g, (1, BAND_TQ))[:, :BAND_TQ * (period - 1)].reshape(heads, BAND_TQ, period - 1)
    return jnp.where(in_band[None], toep[:, :, :BAND_TK] * LOG2E, NEG_BIG)


def _band_attention(q, k_pad, v_pad, bias):
    b, s, w = q.shape
    sp = k_pad.shape[1]
    return pl.pallas_call(
        _band_body,
        grid=(b, s // BAND_TQ),
        in_specs=[
            pl.BlockSpec((1, BAND_TQ, w), lambda bi, i: (bi, i, 0)),
            pl.BlockSpec((1, sp, w), lambda bi, i: (bi, 0, 0)),
            pl.BlockSpec((1, sp, w), lambda bi, i: (bi, 0, 0)),
            pl.BlockSpec(bias.shape, lambda bi, i: (0, 0, 0)),
        ],
        out_specs=pl.BlockSpec((1, BAND_TQ, w), lambda bi, i: (bi, i, 0)),
        out_shape=jax.ShapeDtypeStruct((b, s, w), BF16),
        compiler_params=_cparams(("arbitrary", "arbitrary")),
        name="band_attention",
    )(q, k_pad, v_pad, bias)


def _route_tokens(h, gain, w_router):
    xn = _rms(h, gain)
    x_hi = xn.astype(BF16)
    x_lo = (xn - x_hi.astype(F32)).astype(BF16)
    w_hi = w_router.astype(BF16)
    w_lo = (w_router - w_hi.astype(F32)).astype(BF16)
    logits = (jnp.dot(x_hi, w_hi, preferred_element_type=F32)
              + jnp.dot(x_lo, w_hi, preferred_element_type=F32)
              + jnp.dot(x_hi, w_lo, preferred_element_type=F32))
    lt = logits.T[:ROUTER_ROWS]
    sub = lax.broadcasted_iota(jnp.int32, lt.shape, 0).astype(F32)
    ninf = -jnp.inf

    def top(vals):
        m = jnp.max(vals, axis=0, keepdims=True)
        idx = jnp.min(jnp.where(vals == m, sub, float(ROUTER_LANES)), axis=0, keepdims=True)
        return m, idx

    is_group = sub < N_GROUPS
    g_max, g_sel = top(jnp.where(is_group, lt, ninf))
    g_den = jnp.sum(jnp.where(is_group, jnp.exp(lt - g_max), 0.0), axis=0, keepdims=True)
    g_weight = 1.0 / g_den
    lo = N_GROUPS + N_EXP_PER_GROUP * g_sel
    e_logits = jnp.where((sub >= lo) & (sub < lo + N_EXP_PER_GROUP), lt, ninf)
    e1, i1 = top(e_logits)
    e2, i2 = top(jnp.where(sub == i1, ninf, e_logits))
    t = jnp.exp(e2 - e1)
    w1 = g_weight / (1.0 + t)
    w2 = g_weight * t / (1.0 + t)
    rows = lax.broadcasted_iota(jnp.int32, (ROUTER_LANES, lt.shape[1]), 0)
    route_t = jnp.where(rows == 0, i1 - N_GROUPS, jnp.where(rows == 1, i2 - N_GROUPS, 0.0))
    route_t = jnp.where(rows == 2, w1, jnp.where(rows == 3, w2, route_t))
    return xn, route_t.T


def _router_weights(w_rg, w_re):
    pad = jnp.zeros((w_rg.shape[0], ROUTER_LANES - N_GROUPS - N_EXPERTS), F32)
    return jnp.concatenate([w_rg, w_re, pad], axis=1)


def _out_even_body(tiles_per_seq, ya_ref, p_ref, halo_ref, h_ref, pw_ref, ps_ref, wo_ref, fg_ref, wr_ref,
                   o_ref, xn_ref, route_ref, p_scr, yb_scr):
    tm = p_ref.shape[0]
    it = pl.program_id(0) % tiles_per_seq
    halo = halo_ref[...].astype(F32)
    p_scr[0:POOL_HALO, :] = jnp.where(it == 0, jnp.zeros_like(halo), halo)
    p_scr[POOL_HALO:, :] = p_ref[...].astype(F32)
    t = it * tm + lax.broadcasted_iota(jnp.int32, (tm, 1), 0)
    for g, win in enumerate(POOL_WINDOWS):
        ls = slice(g * LANES, (g + 1) * LANES)
        cur = p_scr[POOL_HALO:POOL_HALO + tm, ls]
        acc = cur
        for dlt in range(1, win):
            acc = acc + p_scr[POOL_HALO - dlt:POOL_HALO - dlt + tm, ls]
        cnt = jnp.minimum(t + 1, win).astype(F32)
        mixed = acc / cnt - cur
        yb = jnp.dot(mixed.astype(BF16), pw_ref[g], preferred_element_type=F32) * ps_ref[:, ls]
        yb_scr[:, ls] = yb.astype(BF16)
    w = MIX_WIDTH
    h_new = (h_ref[...]
             + jnp.dot(ya_ref[...], wo_ref[0:w, :], preferred_element_type=F32)
             + jnp.dot(yb_scr[...], wo_ref[w:, :], preferred_element_type=F32))
    o_ref[...] = h_new
    xn, route_ref[...] = _route_tokens(h_new, fg_ref[...], wr_ref[...])
    _store_token_major(xn_ref, xn)


def _out_even(ya, p, h, pool_w, pool_scale, w_out, ffn_gain, w_router, seq, tm):
    n, d = h.shape
    w = MIX_WIDTH
    row = lambda i: (i, 0)
    const2 = lambda i: (0, 0)
    halo_blocks = tm // POOL_HALO
    return pl.pallas_call(
        functools.partial(_out_even_body, seq // tm),
        grid=(n // tm,),
        in_specs=[
            pl.BlockSpec((tm, w), row),
            pl.BlockSpec((tm, w), row),
            pl.BlockSpec((POOL_HALO, w), lambda i: (jnp.maximum(i * halo_blocks - 1, 0), 0)),
            pl.BlockSpec((tm, d), row),
            pl.BlockSpec(pool_w.shape, lambda i: (0, 0, 0)),
            pl.BlockSpec((1, w), const2),
            pl.BlockSpec(w_out.shape, const2),
            pl.BlockSpec((1, d), const2),
            pl.BlockSpec(w_router.shape, const2),
        ],
        out_specs=[pl.BlockSpec((tm, d), row), pl.BlockSpec((tm * SUBLANES, LANES), row),
                   pl.BlockSpec((tm, ROUTER_LANES), row)],
        out_shape=[jax.ShapeDtypeStruct((n, d), F32), jax.ShapeDtypeStruct((n * SUBLANES, LANES), F32),
                   jax.ShapeDtypeStruct((n, ROUTER_LANES), F32)],
        scratch_shapes=[pltpu.VMEM((tm + POOL_HALO, w), F32), pltpu.VMEM((tm, w), BF16)],
        compiler_params=_cparams(("arbitrary",)),
        name="out_even",
    )(ya, p, p, h, pool_w, pool_scale, w_out, ffn_gain, w_router)


MOE_TM = 256
MOE_TILE_LANES = 256


def _moe_tiles(n):
    return (2 * n) // MOE_TM + N_EXPERTS


def _exact_dot_nt(ones, x):
    out = None
    for _ in range(3):
        part = x.astype(BF16)
        x = x - part.astype(F32)
        term = lax.dot_general(ones, part, NT_DIMS, preferred_element_type=F32)
        out = term if out is None else out + term
    return out


def _slots_body(n_tiles, route_ref, pos1_ref, pos2_ref, tile_ref, run_scr, start_scr):
    phase = pl.program_id(0)
    i = pl.program_id(1)
    tm = route_ref.shape[0]
    route = route_ref[...]
    lane = lax.broadcasted_iota(jnp.int32, (tm, ROUTER_LANES), 1).astype(F32)
    pick1 = jnp.where(lane == route[:, 0:1], 1.0, 0.0)
    pick2 = jnp.where(lane == route[:, 1:2], 1.0, 0.0)
    occ = (pick1 + pick2).astype(BF16)
    ones_rows = jnp.ones((SUBLANES, tm), BF16)
    ones_lanes = jnp.ones((SUBLANES, ROUTER_LANES), BF16)

    @pl.when(i == 0)
    def _():
        run_scr[...] = jnp.zeros_like(run_scr)

    @pl.when(phase == 0)
    def _():
        run_scr[...] += jnp.dot(ones_rows, occ, preferred_element_type=F32)

        @pl.when(i == pl.num_programs(1) - 1)
        def _():
            padded = jnp.floor((run_scr[...] + (MOE_TM - 1)) * (1.0 / MOE_TM)) * MOE_TM
            r = lax.broadcasted_iota(jnp.int32, (ROUTER_LANES, ROUTER_LANES), 0)
            c = lax.broadcasted_iota(jnp.int32, (ROUTER_LANES, ROUTER_LANES), 1)
            before = jnp.where(r < c, 1.0, 0.0).astype(BF16)
            hi = padded.astype(BF16)
            mid = (padded - hi.astype(F32)).astype(BF16)
            low = (padded - hi.astype(F32) - mid.astype(F32)).astype(BF16)
            start = (jnp.dot(hi, before, preferred_element_type=F32)
                     + jnp.dot(mid, before, preferred_element_type=F32)
                     + jnp.dot(low, before, preferred_element_type=F32))
            start_scr[...] = start
            seg_end = start[0:1, :] + padded[0:1, :]
            tile_lo = (lax.broadcasted_iota(jnp.int32, (MOE_TILE_LANES, ROUTER_LANES), 0) * MOE_TM).astype(F32)
            e_lane = lax.broadcasted_iota(jnp.int32, (MOE_TILE_LANES, ROUTER_LANES), 1)
            ended = jnp.where((seg_end <= tile_lo) & (e_lane < N_EXPERTS), 1.0, 0.0).astype(BF16)
            tile_ref[...] = lax.dot_general(ones_lanes, ended, NT_DIMS, preferred_element_type=F32)

    @pl.when(phase == 1)
    def _():
        r = lax.broadcasted_iota(jnp.int32, (tm, tm), 0)
        c = lax.broadcasted_iota(jnp.int32, (tm, tm), 1)
        earlier = jnp.where(c < r, 1.0, 0.0).astype(BF16)
        base = (jnp.dot(earlier, occ, preferred_element_type=F32)
                + run_scr[0:1, :] + start_scr[0:1, :])
        pos1_ref[...] = _exact_dot_nt(ones_lanes, pick1 * base)
        pos2_ref[...] = _exact_dot_nt(ones_lanes, pick2 * base)
        run_scr[...] += jnp.dot(ones_rows, occ, preferred_element_type=F32)


def _slots(route, tm):
    n = route.shape[0]
    n_tiles = _moe_tiles(n)
    assert n_tiles <= MOE_TILE_LANES and 2 * n + N_EXPERTS * MOE_TM < 2 ** 24
    row_out = pl.BlockSpec((SUBLANES, tm), lambda ph, i: (0, i * ph))
    sds = jax.ShapeDtypeStruct((SUBLANES, n), F32)
    return pl.pallas_call(
        functools.partial(_slots_body, n_tiles),
        grid=(2, n // tm),
        in_specs=[pl.BlockSpec((tm, ROUTER_LANES), lambda ph, i: (i, 0))],
        out_specs=[row_out, row_out, pl.BlockSpec((SUBLANES, MOE_TILE_LANES), lambda ph, i: (0, 0))],
        out_shape=[sds, sds, jax.ShapeDtypeStruct((SUBLANES, MOE_TILE_LANES), F32)],
        scratch_shapes=[pltpu.VMEM((SUBLANES, ROUTER_LANES), F32), pltpu.VMEM((SUBLANES, ROUTER_LANES), F32)],
        compiler_params=_cparams(("arbitrary", "arbitrary")),
        name="moe_slots",
    )(route)


def _dispatch_body(n_tiles, pos_ref, tile_ref, xn_ref, xs_hbm, zero_scr, zero_sem, row_sem):
    tm = xn_ref.shape[0] // SUBLANES
    n = pl.num_programs(0) * tm
    base = pl.program_id(0) * tm
    tile_rows = MOE_TM * SUBLANES

    @pl.when(pl.program_id(0) == 0)
    def _():
        zero_scr[...] = jnp.zeros_like(zero_scr)

        def fill_copy(t):
            return pltpu.make_async_copy(zero_scr, xs_hbm.at[pl.ds(t * tile_rows, tile_rows), :], zero_sem)

        def has_padding(t):
            return (tile_ref[t] >= N_EXPERTS) | (tile_ref[t] != tile_ref[t + 1])

        @pl.loop(0, n_tiles)
        def _(t):
            @pl.when(has_padding(t))
            def _():
                fill_copy(t).start()

        @pl.loop(0, n_tiles)
        def _(t):
            @pl.when(has_padding(t))
            def _():
                fill_copy(t).wait()

    def issue(j, carry):
        src = xn_ref.at[pl.ds(pl.multiple_of(j * SUBLANES, SUBLANES), SUBLANES), :]
        for pick in range(2):
            dst = pl.multiple_of(pos_ref[pick * n + base + j], SUBLANES)
            pltpu.make_async_copy(src, xs_hbm.at[pl.ds(dst, SUBLANES), :], row_sem).start(priority=pick)
        return carry

    lax.fori_loop(0, tm, issue, 0, unroll=8)
    for _ in range(2):
        pltpu.make_async_copy(xn_ref, xs_hbm.at[pl.ds(0, tm * SUBLANES), :], row_sem).wait()


def _dispatch(pos, tile_map, xn, tm):
    n = xn.shape[0] // SUBLANES
    n_tiles = _moe_tiles(n)
    return pl.pallas_call(
        functools.partial(_dispatch_body, n_tiles),
        grid_spec=pltpu.PrefetchScalarGridSpec(
            num_scalar_prefetch=2,
            grid=(n // tm,),
            in_specs=[pl.BlockSpec((tm * SUBLANES, LANES), lambda i, p, t: (i, 0))],
            out_specs=pl.BlockSpec(memory_space=pl.ANY),
            scratch_shapes=[pltpu.VMEM((MOE_TM * SUBLANES, LANES), F32), pltpu.SemaphoreType.DMA(()),
                            pltpu.SemaphoreType.DMA(())],
        ),
        out_shape=jax.ShapeDtypeStruct((n_tiles * MOE_TM * SUBLANES, LANES), F32),
        compiler_params=_cparams(("arbitrary",)),
        name="moe_dispatch",
    )(pos, tile_map, xn)


EXPERT_SLOTS = 3


def _experts_body(tile_ref, xs_hbm, wg_ref, wu_ref, wd_ref, ys_ref, xbuf, sems):
    i = pl.program_id(0)
    tile_rows = MOE_TM * SUBLANES
    used = tile_ref[i] < N_EXPERTS

    def tile_copy(t, slot):
        rows = pl.ds(pl.multiple_of(t * tile_rows, tile_rows), tile_rows)
        return pltpu.make_async_copy(xs_hbm.at[rows, :], xbuf.at[slot], sems.at[slot])

    def prefetch(t):
        @pl.when(tile_ref[t] < N_EXPERTS)
        def _():
            tile_copy(t, t % EXPERT_SLOTS).start()

    @pl.when(i == 0)
    def _():
        prefetch(0)
        prefetch(1)

    prefetch(i + 2)

    @pl.when(used)
    def _():
        slot = i % EXPERT_SLOTS
        tile_copy(i, slot).wait()
        x = _load_token_major(xbuf, MOE_TM, (slot,)).astype(BF16)
        gate = jnp.dot(x, wg_ref[0].astype(BF16), preferred_element_type=F32)
        up = jnp.dot(x, wu_ref[0].astype(BF16), preferred_element_type=F32)
        hid = gate * jax.nn.sigmoid(gate) * up
        _store_token_major(ys_ref, jnp.dot(hid.astype(BF16), wd_ref[0].astype(BF16),
                                           preferred_element_type=F32))

    @pl.when(jnp.logical_not(used))
    def _():
        ys_ref[...] = jnp.zeros_like(ys_ref)


def _experts(tile_map, xs, w_gate, w_up, w_down, layer):
    d, f = w_gate.shape[1:]
    tile_rows = MOE_TM * SUBLANES
    n_tiles = xs.shape[0] // tile_rows
    y_map = lambda i, tm_ref: (i, 0)
    w_map = lambda i, tm_ref: (layer * N_EXPERTS + jnp.minimum(tm_ref[i], N_EXPERTS - 1), 0, 0)
    return pl.pallas_call(
        _experts_body,
        grid_spec=pltpu.PrefetchScalarGridSpec(
            num_scalar_prefetch=1,
            grid=(n_tiles,),
            in_specs=[pl.BlockSpec(memory_space=pl.ANY), pl.BlockSpec((1, d, f), w_map),
                      pl.BlockSpec((1, d, f), w_map), pl.BlockSpec((1, f, d), w_map)],
            out_specs=pl.BlockSpec((tile_rows, LANES), y_map),
            scratch_shapes=[pltpu.VMEM((EXPERT_SLOTS, tile_rows, LANES), F32),
                            pltpu.SemaphoreType.DMA((EXPERT_SLOTS,))],
        ),
        out_shape=jax.ShapeDtypeStruct(xs.shape, F32),
        compiler_params=_cparams(("arbitrary",)),
        name="moe_experts",
    )(tile_map, xs, w_gate, w_up, w_down)


def _combined_tile(pos_ref, h_ref, route_ref, ys_hbm, buf, sems):
    tm = h_ref.shape[0]
    steps = pl.num_programs(0)
    n = steps * tm
    i = pl.program_id(0)

    def start_gather(step, slot):
        base = step * tm

        def issue(g, carry):
            for u in range(SUBLANES):
                for pick in range(2):
                    src = pl.multiple_of(pos_ref[pick * n + base + g * SUBLANES + u], SUBLANES)
                    dst = pl.multiple_of(g * SUBLANES * SUBLANES, SUBLANES) + u * SUBLANES
                    pltpu.make_async_copy(ys_hbm.at[pl.ds(src, SUBLANES), :],
                                          buf.at[slot, pick, pl.ds(dst, SUBLANES), :],
                                          sems.at[slot]).start(priority=pick)
            return carry

        lax.fori_loop(0, tm // SUBLANES, issue, 0)

    @pl.when(i == 0)
    def _():
        start_gather(0, 0)

    @pl.when(i + 1 < steps)
    def _():
        start_gather(i + 1, (i + 1) % 2)

    slot = i % 2
    pltpu.make_async_copy(buf.at[slot], buf.at[slot], sems.at[slot]).wait()
    route = route_ref[...]
    y1 = _load_token_major(buf, tm, (slot, 0))
    y2 = _load_token_major(buf, tm, (slot, 1))
    return h_ref[...] + route[:, 2:3] * y1 + route[:, 3:4] * y2


def _combine_body(pos_ref, h_ref, route_ref, ys_hbm, o_ref, buf, sems):
    o_ref[...] = _combined_tile(pos_ref, h_ref, route_ref, ys_hbm, buf, sems)


def _combine_specs(tm, d):
    row = lambda i, p: (i, 0)
    in_specs = [pl.BlockSpec((tm, d), row), pl.BlockSpec((tm, ROUTER_LANES), row),
                pl.BlockSpec(memory_space=pl.ANY)]
    scratch = [pltpu.VMEM((2, 2, tm * SUBLANES, LANES), F32), pltpu.SemaphoreType.DMA((2,))]
    return in_specs, scratch


def _combine(pos, h, route, ys, tm):
    n, d = h.shape
    row = lambda i, p: (i, 0)
    in_specs, scratch = _combine_specs(tm, d)
    return pl.pallas_call(
        _combine_body,
        grid_spec=pltpu.PrefetchScalarGridSpec(
            num_scalar_prefetch=1,
            grid=(n // tm,),
            in_specs=in_specs,
            out_specs=pl.BlockSpec((tm, d), row),
            scratch_shapes=scratch,
        ),
        out_shape=jax.ShapeDtypeStruct((n, d), F32),
        compiler_params=_cparams(("arbitrary",)),
        name="moe_combine",
    )(pos, h, route, ys)


def _gelu(x):
    return 0.5 * x * (1.0 + lax.erf(x * (1.0 / math.sqrt(2.0))))


def _in_odd_combined_body(pos_ref, h_ref, route_ref, ys_hbm, g_ref, w_ref, vg_ref,
                          hn_ref, u_ref, vn_ref, q_ref, k_ref, v_ref, buf, sems):
    hn_ref[...] = _combined_tile(pos_ref, h_ref, route_ref, ys_hbm, buf, sems)
    _in_odd_body(hn_ref, g_ref, w_ref, vg_ref, u_ref, vn_ref, q_ref, k_ref, v_ref)


def _in_odd_body(h_ref, g_ref, w_ref, vg_ref, u_ref, vn_ref, q_ref, k_ref, v_ref):
    xn = _rms(h_ref[...], g_ref[...])
    proj = jnp.dot(xn.astype(BF16), w_ref[...], preferred_element_type=F32)
    w = MIX_WIDTH
    u_ref[...] = _gelu(proj[:, :w]).astype(BF16)
    vn_ref[...] = _rms(_gelu(proj[:, w:2 * w]), vg_ref[...]).astype(BF16)
    q_ref[...] = (proj[:, 2 * w:3 * w] * Q_SCALE).astype(BF16)
    k_ref[...] = proj[:, 3 * w:4 * w].astype(BF16)
    v_ref[...] = proj[:, 4 * w:].astype(BF16)


def _in_odd(h, gain, w, v_gain, tm):
    n, d = h.shape
    row = lambda i: (i, 0)
    const2 = lambda i: (0, 0)
    sds = jax.ShapeDtypeStruct((n, MIX_WIDTH), BF16)
    return pl.pallas_call(
        _in_odd_body,
        grid=(n // tm,),
        in_specs=[pl.BlockSpec((tm, d), row), pl.BlockSpec((1, d), const2),
                  pl.BlockSpec(w.shape, const2), pl.BlockSpec((1, MIX_WIDTH), const2)],
        out_specs=[pl.BlockSpec((tm, MIX_WIDTH), row)] * 5,
        out_shape=[sds] * 5,
        compiler_params=_cparams(("arbitrary",)),
        name="in_odd",
    )(h, gain, w, v_gain)


def _in_odd_combined(pending, gain, w, v_gain, tm):
    pos, h, route, ys = pending
    n, d = h.shape
    row = lambda i, p: (i, 0)
    const2 = lambda i, p: (0, 0)
    sds = jax.ShapeDtypeStruct((n, MIX_WIDTH), BF16)
    combine_specs, scratch = _combine_specs(tm, d)
    return pl.pallas_call(
        _in_odd_combined_body,
        grid_spec=pltpu.PrefetchScalarGridSpec(
            num_scalar_prefetch=1,
            grid=(n // tm,),
            in_specs=combine_specs + [pl.BlockSpec((1, d), const2), pl.BlockSpec(w.shape, const2),
                                      pl.BlockSpec((1, MIX_WIDTH), const2)],
            out_specs=[pl.BlockSpec((tm, d), row)] + [pl.BlockSpec((tm, MIX_WIDTH), row)] * 5,
            scratch_shapes=scratch,
        ),
        out_shape=[jax.ShapeDtypeStruct((n, d), F32)] + [sds] * 5,
        compiler_params=_cparams(("arbitrary",)),
        name="in_odd_combined",
    )(pos, h, route, ys, gain, w, v_gain)


SB_TQ = 64
SB_KB = 128
SB_NB = 2
SB_TK = SB_NB * SB_KB
SB_QBLK = 512
SB_GROUP = 4
SB_UNDERFLOW = -150.0


def _sb_body(q_ref, k_ref, v_ref, o_ref, acc_scr, run_scr):
    qi = pl.program_id(2)
    lane = lax.broadcasted_iota(jnp.int32, (SB_TQ, LANES), 1)
    row = lax.broadcasted_iota(jnp.int32, (2 * SB_TQ, 1), 0) % SB_TQ
    col = lax.broadcasted_iota(jnp.int32, (2 * SB_TQ, SB_TK), 1)
    rr = lax.broadcasted_iota(jnp.int32, (2 * SB_KB, 2 * SB_KB), 0) % SB_KB
    cc = lax.broadcasted_iota(jnp.int32, (2 * SB_KB, 2 * SB_KB), 1)
    suffix = jnp.where((cc >= SB_KB) | (rr > cc), 1.0, 0.0).astype(BF16)

    def suffix_sums(x):
        hi = x.astype(BF16)
        lo = (x - hi.astype(F32)).astype(BF16)
        return jnp.dot(jnp.concatenate([hi, lo], axis=1), suffix, preferred_element_type=F32)

    def subtile_group(grp, _):
        q_los = [pl.multiple_of((grp * SB_GROUP + s) * SB_TQ, SB_TQ) for s in range(SB_GROUP)]
        q_starts = [qi * SB_QBLK + q_lo for q_lo in q_los]
        q_heads = []
        for q_lo in q_los:
            q = q_ref[0, pl.ds(q_lo, SB_TQ), :]
            zero = jnp.zeros_like(q)
            q_heads.append(jnp.concatenate(
                [jnp.where(lane < HEAD_DIM, q, zero), jnp.where(lane < HEAD_DIM, zero, q)], axis=0))
        acc_scr[...] = jnp.zeros_like(acc_scr)
        run_scr[...] = jnp.zeros_like(run_scr)

        def cond(carry):
            his, dones = carry
            active = [(hi > 0) & (done == 0) for hi, done in zip(his, dones)]
            return functools.reduce(jnp.logical_or, active)

        def body(carry):
            his, _ = carry
            group = range(SB_GROUP)
            kss = [pl.multiple_of(jnp.maximum(his[s] - SB_TK, 0), SB_TQ) for s in group]
            valid = [col < (jnp.minimum(row + q_starts[s], his[s]) - kss[s]) for s in group]
            zs = [lax.dot_general(q_heads[s], k_ref[0, pl.ds(kss[s], SB_TK), :], NT_DIMS,
                                  preferred_element_type=F32) for s in group]
            log_beta, log_rest = [], []
            for s in group:
                z = jnp.where(valid[s], zs[s], NEG_BIG)
                sp = jnp.log2(1.0 + jnp.exp2(-jnp.abs(z)))
                log_beta.append(jnp.minimum(z, 0.0) - sp)
                log_rest.append(log_beta[s] - z)
            sums = [[suffix_sums(log_rest[s][:, blk * SB_KB:(blk + 1) * SB_KB]) for blk in range(SB_NB)]
                    for s in group]
            dones = []
            for s in group:
                run = run_scr[s]
                pieces = [None] * SB_NB
                for blk in reversed(range(SB_NB)):
                    tt = sums[s][blk]
                    pieces[blk] = jnp.exp2(log_beta[s][:, blk * SB_KB:(blk + 1) * SB_KB] + tt[:, :SB_KB] + run)
                    run = run + tt[:, SB_KB:]
                att = jnp.concatenate(pieces, axis=1)
                acc_scr[s] += jnp.dot(att.astype(BF16), v_ref[0, pl.ds(kss[s], SB_TK), :],
                                      preferred_element_type=F32)
                run_scr[s] = run
                dones.append((jnp.max(run) <= SB_UNDERFLOW).astype(jnp.int32))
            return tuple(kss), tuple(dones)

        lax.while_loop(cond, body, (tuple(qs + SB_TQ for qs in q_starts),
                                    tuple(jnp.int32(0) for _ in range(SB_GROUP))))
        for s in range(SB_GROUP):
            o_ref[0, pl.ds(q_los[s], SB_TQ), :] = jnp.where(
                lane < HEAD_DIM, acc_scr[s, :SB_TQ], acc_scr[s, SB_TQ:]).astype(BF16)
        return 0

    lax.fori_loop(0, SB_QBLK // (SB_TQ * SB_GROUP), subtile_group, 0)


def _sb_attention(q, k, v):
    b, s, w = q.shape
    assert s % SB_QBLK == 0 and s >= SB_TK
    qspec = pl.BlockSpec((1, SB_QBLK, LANES), lambda bi, hp, i: (bi, i, hp))
    kvspec = pl.BlockSpec((1, s, LANES), lambda bi, hp, i: (bi, 0, hp))
    return pl.pallas_call(
        _sb_body,
        grid=(b, w // LANES, s // SB_QBLK),
        in_specs=[qspec, kvspec, kvspec],
        out_specs=qspec,
        out_shape=jax.ShapeDtypeStruct((b, s, w), BF16),
        scratch_shapes=[pltpu.VMEM((SB_GROUP, 2 * SB_TQ, LANES), F32),
                        pltpu.VMEM((SB_GROUP, 2 * SB_TQ, LANES), F32)],
        compiler_params=_cparams(("arbitrary", "arbitrary", "arbitrary")),
        name="sb_attention",
    )(q, k, v)


def _out_odd_body(u_ref, vn_ref, yd_ref, h_ref, ws_ref, bs_ref, wo_ref, fg_ref, wr_ref,
                  o_ref, xn_ref, route_ref, yc_scr):
    tm = u_ref.shape[0]
    r = lax.broadcasted_iota(jnp.int32, (SGU_BLOCK, SGU_BLOCK), 0)
    c = lax.broadcasted_iota(jnp.int32, (SGU_BLOCK, SGU_BLOCK), 1)
    for g in range(MIX_WIDTH // LANES):
        ls = slice(g * LANES, (g + 1) * LANES)
        ws = jnp.where(c <= r, ws_ref[g], jnp.zeros_like(ws_ref[g]))
        for blk in range(tm // SGU_BLOCK):
            rs = slice(blk * SGU_BLOCK, (blk + 1) * SGU_BLOCK)
            mixed = jnp.dot(ws, vn_ref[rs, ls], preferred_element_type=F32) + bs_ref[g]
            yc_scr[rs, ls] = (u_ref[rs, ls].astype(F32) * mixed).astype(BF16)
    w = MIX_WIDTH
    h_new = (h_ref[...]
             + jnp.dot(yc_scr[...], wo_ref[0:w, :], preferred_element_type=F32)
             + jnp.dot(yd_ref[...], wo_ref[w:, :], preferred_element_type=F32))
    o_ref[...] = h_new
    xn, route_ref[...] = _route_tokens(h_new, fg_ref[...], wr_ref[...])
    _store_token_major(xn_ref, xn)


def _out_odd(u, vn, yd, h, sgu_w, sgu_b, w_out, ffn_gain, w_router, tm):
    n, d = h.shape
    w = MIX_WIDTH
    row = lambda i: (i, 0)
    const2 = lambda i: (0, 0)
    const3 = lambda i: (0, 0, 0)
    return pl.pallas_call(
        _out_odd_body,
        grid=(n // tm,),
        in_specs=[pl.BlockSpec((tm, w), row), pl.BlockSpec((tm, w), row), pl.BlockSpec((tm, w), row),
                  pl.BlockSpec((tm, d), row), pl.BlockSpec(sgu_w.shape, const3),
                  pl.BlockSpec(sgu_b.shape, const3), pl.BlockSpec(w_out.shape, const2),
                  pl.BlockSpec((1, d), const2), pl.BlockSpec(w_router.shape, const2)],
        out_specs=[pl.BlockSpec((tm, d), row), pl.BlockSpec((tm * SUBLANES, LANES), row),
                   pl.BlockSpec((tm, ROUTER_LANES), row)],
        out_shape=[jax.ShapeDtypeStruct((n, d), F32), jax.ShapeDtypeStruct((n * SUBLANES, LANES), F32),
                   jax.ShapeDtypeStruct((n, ROUTER_LANES), F32)],
        scratch_shapes=[pltpu.VMEM((tm, w), BF16)],
        compiler_params=_cparams(("arbitrary",)),
        name="out_odd",
    )(u, vn, yd, h, sgu_w, sgu_b, w_out, ffn_gain, w_router)


def _moe_layer(h, xn, route, w_gate, w_up, w_down, layer):
    pos1, pos2, tile_map = _slots(route, tm=1024)
    pos = jnp.concatenate([pos1[0], pos2[0]]).astype(jnp.int32) * SUBLANES
    tile_map = tile_map[0].astype(jnp.int32)
    xs = _dispatch(pos, tile_map, xn, tm=256)
    ys = _experts(tile_map, xs, w_gate, w_up, w_down, layer)
    return pos, h, route, ys


def kernel(x, mix_norm_even, w_in_even, att_q_norm, att_k_norm, att_rel_bias, pool_w, pool_scale,
           w_out_even, mix_norm_odd, w_in_odd, sgu_v_norm, sgu_w, sgu_b, w_out_odd, ffn_norm,
           w_router_group, w_router_expert, w_exp_gate, w_exp_up, w_exp_down):
    b, s, d = x.shape
    n = b * s
    depth = ffn_norm.shape[0]
    heads = MIX_WIDTH // HEAD_DIM
    h = x.reshape(n, d)
    w_gate = w_exp_gate.reshape(depth * N_EXPERTS, d, EXPERT_FF)
    w_up = w_exp_up.reshape(depth * N_EXPERTS, d, EXPERT_FF)
    w_down = w_exp_down.reshape(depth * N_EXPERTS, EXPERT_FF, d)
    pending = None
    for layer in range(depth):
        i = layer // 2
        ffn_gain = ffn_norm[layer][None, :]
        w_router = _router_weights(w_router_group[layer], w_router_expert[layer])
        if layer % 2 == 0:
            if pending is not None:
                h = _combine(*pending, tm=256)
            q, k_pad, v_pad, p = _in_even(
                h.reshape(b, s, d), mix_norm_even[i][None, :], w_in_even[i].astype(BF16),
                jnp.tile(att_q_norm[i], heads)[None, :], jnp.tile(att_k_norm[i], heads)[None, :],
                tm=ATT_LEFT)
            ya = _band_attention(q, k_pad, v_pad, _band_bias(att_rel_bias[i]))
            h, xn, route = _out_even(ya.reshape(n, MIX_WIDTH), p.reshape(n, MIX_WIDTH), h,
                                     pool_w[i].astype(BF16), pool_scale[i][None, :],
                                     w_out_even[i].astype(BF16), ffn_gain, w_router, seq=s, tm=512)
        else:
            odd_args = (mix_norm_odd[i][None, :], w_in_odd[i].astype(BF16), sgu_v_norm[i][None, :])
            if pending is None:
                u, vn, q, k, v = _in_odd(h, *odd_args, tm=512)
            else:
                h, u, vn, q, k, v = _in_odd_combined(pending, *odd_args, tm=512)
            to3 = lambda t: t.reshape(b, s, MIX_WIDTH)
            yd = _sb_attention(to3(q), to3(k), to3(v))
            bias = jnp.broadcast_to(sgu_b[i][:, :, None], (N_GROUPS, SGU_BLOCK, LANES))
            h, xn, route = _out_odd(u, vn, yd.reshape(n, MIX_WIDTH), h, sgu_w[i].astype(BF16), bias,
                                    w_out_odd[i].astype(BF16), ffn_gain, w_router, tm=512)
        pending = _moe_layer(h, xn, route, w_gate, w_up, w_down, layer)
    return _combine(*pending, tm=256).reshape(b, s, d)
```

```python
import functools
import math

import jax
import jax.numpy as jnp
from jax import lax
from jax.experimental import pallas as pl
from jax.experimental.pallas import tpu as pltpu

F32 = jnp.float32
BF16 = jnp.bfloat16

D_MODEL = 1024
CHUNK = 64
EPS = 1e-6
HEAD_DIM = 64
MIX_WIDTH = 512
LANES = 128
SUBLANES = 8
ATT_LEFT = 8 * CHUNK
ATT_MAX_REL = 128
POOL_WINDOWS = (2, 4, 8, 16)
POOL_HALO = 16
SGU_BLOCK = 128
N_GROUPS = 4
N_EXP_PER_GROUP = 8
N_EXPERTS = N_GROUPS * N_EXP_PER_GROUP
EXPERT_FF = 256
ROUTER_LANES = 128
ROUTER_ROWS = 40
NEG_BIG = -1e30
VMEM_LIMIT = 56 * 1024 * 1024

NT_DIMS = (((1,), (1,)), ((), ()))
LOG2E = math.log2(math.e)
Q_SCALE = LOG2E / math.sqrt(HEAD_DIM)


def _cparams(sem):
    return pltpu.CompilerParams(dimension_semantics=sem, vmem_limit_bytes=VMEM_LIMIT)


def _store_token_major(ref, x):
    rows = x.shape[0]
    for s in range(SUBLANES):
        ref[pl.ds(s, rows, stride=SUBLANES), :] = x[:, s * LANES:(s + 1) * LANES]


def _load_token_major(ref, rows, lead=()):
    return jnp.concatenate(
        [ref[lead + (pl.ds(s, rows, stride=SUBLANES), slice(None))] for s in range(SUBLANES)], axis=1)


def _rms(x, gain):
    return x * lax.rsqrt(jnp.mean(x * x, axis=-1, keepdims=True) + EPS) * gain


def _split_dot(x, m):
    hi = x.astype(BF16)
    lo = (x - hi.astype(F32)).astype(BF16)
    return (jnp.dot(hi, m, preferred_element_type=F32)
            + jnp.dot(lo, m, preferred_element_type=F32))


def _head_rms(t, gain):
    n = t.shape[-1]
    r = lax.broadcasted_iota(jnp.int32, (n, n), 0) // HEAD_DIM
    c = lax.broadcasted_iota(jnp.int32, (n, n), 1) // HEAD_DIM
    bd = jnp.where(r == c, 1.0, 0.0).astype(BF16)
    ms = _split_dot(t * t, bd) * (1.0 / HEAD_DIM)
    return t * lax.rsqrt(ms + EPS) * gain


def _in_even_body(h_ref, g_ref, w_ref, qg_ref, kg_ref, q_ref, k_ref, v_ref, p_ref):
    j = pl.program_id(1)

    @pl.when(j == 0)
    def _():
        k_ref[...] = jnp.zeros_like(k_ref)
        v_ref[...] = jnp.zeros_like(v_ref)

    @pl.when(j > 0)
    def _():
        xn = _rms(h_ref[0], g_ref[...])
        proj = jnp.dot(xn.astype(BF16), w_ref[...], preferred_element_type=F32)
        w = MIX_WIDTH
        q_ref[0] = (_head_rms(proj[:, :w], qg_ref[...]) * Q_SCALE).astype(BF16)
        k_ref[0] = _head_rms(proj[:, w:2 * w], kg_ref[...]).astype(BF16)
        v_ref[0] = proj[:, 2 * w:3 * w].astype(BF16)
        p_ref[0] = proj[:, 3 * w:].astype(BF16)


def _in_even(h, gain, w, q_gain, k_gain, tm):
    b, s, d = h.shape
    assert tm == ATT_LEFT and s % tm == 0
    nt = s // tm
    cur = lambda bi, j: (bi, jnp.maximum(j - 1, 0), 0)
    const = lambda bi, j: (0, 0)
    out_sds = lambda rows: jax.ShapeDtypeStruct((b, rows, MIX_WIDTH), BF16)
    return pl.pallas_call(
        _in_even_body,
        grid=(b, nt + 1),
        in_specs=[
            pl.BlockSpec((1, tm, d), cur),
            pl.BlockSpec((1, d), const),
            pl.BlockSpec(w.shape, const),
            pl.BlockSpec((1, MIX_WIDTH), const),
            pl.BlockSpec((1, MIX_WIDTH), const),
        ],
        out_specs=[
            pl.BlockSpec((1, tm, MIX_WIDTH), cur),
            pl.BlockSpec((1, tm, MIX_WIDTH), lambda bi, j: (bi, j, 0)),
            pl.BlockSpec((1, tm, MIX_WIDTH), lambda bi, j: (bi, j, 0)),
            pl.BlockSpec((1, tm, MIX_WIDTH), cur),
        ],
        out_shape=[out_sds(s), out_sds(s + ATT_LEFT), out_sds(s + ATT_LEFT), out_sds(s)],
        compiler_params=_cparams(("arbitrary", "arbitrary")),
        name="in_even",
    )(h, gain, w, q_gain, k_gain)


BAND_TQ = 2 * CHUNK
BAND_TK = BAND_TQ + ATT_LEFT


BAND_STEP_TILES = 4


def _band_body(q_ref, k_ref, v_ref, bias_ref, o_ref):
    @pl.loop(0, BAND_STEP_TILES)
    def _(t):
        rows = pl.ds(pl.multiple_of(t * BAND_TQ, BAND_TQ), BAND_TQ)
        _band_tile(pl.program_id(1) * BAND_STEP_TILES + t, q_ref.at[0, rows, :], k_ref, v_ref, bias_ref,
                   o_ref.at[0, rows, :])


def _band_tile(i, q_ref, k_ref, v_ref, bias_ref, o_ref):
    start = pl.multiple_of(i * BAND_TQ, BAND_TQ)
    lane = lax.broadcasted_iota(jnp.int32, (BAND_TQ, LANES), 1)
    col = lax.broadcasted_iota(jnp.int32, (2 * BAND_TQ, BAND_TK), 1)
    is_pad = (col + start) < ATT_LEFT
    pairs = range(MIX_WIDTH // LANES)
    lanes = [slice(hp * LANES, (hp + 1) * LANES) for hp in pairs]
    scores = []
    for hp in pairs:
        q = q_ref[:, lanes[hp]]
        kb = k_ref[0, pl.ds(start, BAND_TK), lanes[hp]]
        zero = jnp.zeros_like(q)
        q2 = jnp.concatenate([jnp.where(lane < HEAD_DIM, q, zero), jnp.where(lane < HEAD_DIM, zero, q)], axis=0)
        scores.append(lax.dot_general(q2, kb, NT_DIMS, preferred_element_type=F32))
    probs, denoms = [], []
    for hp in pairs:
        bias = bias_ref[2 * hp:2 * hp + 2].reshape(2 * BAND_TQ, BAND_TK)
        s = jnp.where(is_pad, NEG_BIG, scores[hp] + bias)
        p = jnp.exp2(s - jnp.max(s, axis=-1, keepdims=True))
        denoms.append(jnp.sum(p, axis=-1, keepdims=True))
        probs.append(p.astype(BF16))
    for hp in pairs:
        vb = v_ref[0, pl.ds(start, BAND_TK), lanes[hp]]
        o = jnp.dot(probs[hp], vb, preferred_element_type=F32) / denoms[hp]
        o_ref[:, lanes[hp]] = jnp.where(lane < HEAD_DIM, o[:BAND_TQ], o[BAND_TQ:]).astype(BF16)


def _band_bias(rel_bias):
    heads = rel_bias.shape[0]
    r = jnp.arange(BAND_TQ)[:, None]
    j = jnp.arange(BAND_TK)[None, :]
    jb = j - CHUNK * (r // CHUNK)
    in_band = (jb >= 0) & (jb < ATT_LEFT + CHUNK)
    period = BAND_TK + BAND_TQ
    far = jnp.broadcast_to(rel_bias[:, 2 * ATT_MAX_REL:], (heads, ATT_LEFT - ATT_MAX_REL + 1))
    near = rel_bias[:, 2 * ATT_MAX_REL - 1:0:-1]
    wrap = jnp.broadcast_to(rel_bias[:, 2 * ATT_MAX_REL:], (heads, period - BAND_TK))
    g = jnp.concatenate([far, near, wrap], axis=1).astype(F32)
    assert g.shape[1] == period
    toep = jnp.tile(g, (1, BAND_TQ))[:, :BAND_TQ * (period - 1)].reshape(heads, BAND_TQ, period - 1)
    return jnp.where(in_band[None], toep[:, :, :BAND_TK] * LOG2E, NEG_BIG)


def _band_attention(q, k_pad, v_pad, bias):
    b, s, w = q.shape
    sp = k_pad.shape[1]
    step_rows = BAND_STEP_TILES * BAND_TQ
    return pl.pallas_call(
        _band_body,
        grid=(b, s // step_rows),
        in_specs=[
            pl.BlockSpec((1, step_rows, w), lambda bi, i: (bi, i, 0)),
            pl.BlockSpec((1, sp, w), lambda bi, i: (bi, 0, 0)),
            pl.BlockSpec((1, sp, w), lambda bi, i: (bi, 0, 0)),
            pl.BlockSpec(bias.shape, lambda bi, i: (0, 0, 0)),
        ],
        out_specs=pl.BlockSpec((1, step_rows, w), lambda bi, i: (bi, i, 0)),
        out_shape=jax.ShapeDtypeStruct((b, s, w), BF16),
        compiler_params=_cparams(("arbitrary", "arbitrary")),
        name="band_attention",
    )(q, k_pad, v_pad, bias)


def _route_tokens(h, gain, w_router):
    xn = _rms(h, gain)
    x_hi = xn.astype(BF16)
    x_lo = (xn - x_hi.astype(F32)).astype(BF16)
    w_hi = w_router.astype(BF16)
    w_lo = (w_router - w_hi.astype(F32)).astype(BF16)
    logits = (jnp.dot(x_hi, w_hi, preferred_element_type=F32)
              + jnp.dot(x_lo, w_hi, preferred_element_type=F32)
              + jnp.dot(x_hi, w_lo, preferred_element_type=F32))
    lt = logits.T[:ROUTER_ROWS]
    sub = lax.broadcasted_iota(jnp.int32, lt.shape, 0).astype(F32)
    ninf = -jnp.inf

    def top(vals):
        m = jnp.max(vals, axis=0, keepdims=True)
        idx = jnp.min(jnp.where(vals == m, sub, float(ROUTER_LANES)), axis=0, keepdims=True)
        return m, idx

    is_group = sub < N_GROUPS
    g_max, g_sel = top(jnp.where(is_group, lt, ninf))
    g_den = jnp.sum(jnp.where(is_group, jnp.exp(lt - g_max), 0.0), axis=0, keepdims=True)
    g_weight = 1.0 / g_den
    lo = N_GROUPS + N_EXP_PER_GROUP * g_sel
    e_logits = jnp.where((sub >= lo) & (sub < lo + N_EXP_PER_GROUP), lt, ninf)
    e1, i1 = top(e_logits)
    e2, i2 = top(jnp.where(sub == i1, ninf, e_logits))
    t = jnp.exp(e2 - e1)
    w1 = g_weight / (1.0 + t)
    w2 = g_weight * t / (1.0 + t)
    rows = lax.broadcasted_iota(jnp.int32, (ROUTER_LANES, lt.shape[1]), 0)
    route_t = jnp.where(rows == 0, i1 - N_GROUPS, jnp.where(rows == 1, i2 - N_GROUPS, 0.0))
    route_t = jnp.where(rows == 2, w1, jnp.where(rows == 3, w2, route_t))
    return xn, route_t.T


def _router_weights(w_rg, w_re):
    pad = jnp.zeros((w_rg.shape[0], ROUTER_LANES - N_GROUPS - N_EXPERTS), F32)
    return jnp.concatenate([w_rg, w_re, pad], axis=1)


def _out_even_body(tiles_per_seq, ya_ref, p_ref, halo_ref, h_ref, pw_ref, ps_ref, wo_ref, fg_ref, wr_ref,
                   o_ref, xn_ref, route_ref, p_scr, yb_scr):
    tm = p_ref.shape[0]
    it = pl.program_id(0) % tiles_per_seq
    halo = halo_ref[...].astype(F32)
    p_scr[0:POOL_HALO, :] = jnp.where(it == 0, jnp.zeros_like(halo), halo)
    p_scr[POOL_HALO:, :] = p_ref[...].astype(F32)
    t = it * tm + lax.broadcasted_iota(jnp.int32, (tm, 1), 0)
    for g, win in enumerate(POOL_WINDOWS):
        ls = slice(g * LANES, (g + 1) * LANES)
        cur = p_scr[POOL_HALO:POOL_HALO + tm, ls]
        acc = cur
        for dlt in range(1, win):
            acc = acc + p_scr[POOL_HALO - dlt:POOL_HALO - dlt + tm, ls]
        cnt = jnp.minimum(t + 1, win).astype(F32)
        mixed = acc / cnt - cur
        yb = jnp.dot(mixed.astype(BF16), pw_ref[g], preferred_element_type=F32) * ps_ref[:, ls]
        yb_scr[:, ls] = yb.astype(BF16)
    w = MIX_WIDTH
    h_new = (h_ref[...]
             + jnp.dot(ya_ref[...], wo_ref[0:w, :], preferred_element_type=F32)
             + jnp.dot(yb_scr[...], wo_ref[w:, :], preferred_element_type=F32))
    o_ref[...] = h_new
    xn, route_ref[...] = _route_tokens(h_new, fg_ref[...], wr_ref[...])
    _store_token_major(xn_ref, xn)


def _out_even(ya, p, h, pool_w, pool_scale, w_out, ffn_gain, w_router, seq, tm):
    n, d = h.shape
    w = MIX_WIDTH
    row = lambda i: (i, 0)
    const2 = lambda i: (0, 0)
    halo_blocks = tm // POOL_HALO
    return pl.pallas_call(
        functools.partial(_out_even_body, seq // tm),
        grid=(n // tm,),
        in_specs=[
            pl.BlockSpec((tm, w), row),
            pl.BlockSpec((tm, w), row),
            pl.BlockSpec((POOL_HALO, w), lambda i: (jnp.maximum(i * halo_blocks - 1, 0), 0)),
            pl.BlockSpec((tm, d), row),
            pl.BlockSpec(pool_w.shape, lambda i: (0, 0, 0)),
            pl.BlockSpec((1, w), const2),
            pl.BlockSpec(w_out.shape, const2),
            pl.BlockSpec((1, d), const2),
            pl.BlockSpec(w_router.shape, const2),
        ],
        out_specs=[pl.BlockSpec((tm, d), row), pl.BlockSpec((tm * SUBLANES, LANES), row),
                   pl.BlockSpec((tm, ROUTER_LANES), row)],
        out_shape=[jax.ShapeDtypeStruct((n, d), F32), jax.ShapeDtypeStruct((n * SUBLANES, LANES), F32),
                   jax.ShapeDtypeStruct((n, ROUTER_LANES), F32)],
        scratch_shapes=[pltpu.VMEM((tm + POOL_HALO, w), F32), pltpu.VMEM((tm, w), BF16)],
        compiler_params=_cparams(("arbitrary",)),
        name="out_even",
    )(ya, p, p, h, pool_w, pool_scale, w_out, ffn_gain, w_router)


MOE_TM = 256
MOE_TILE_LANES = 256


def _moe_tiles(n):
    return (2 * n) // MOE_TM + N_EXPERTS


def _exact_dot_nt(ones, x):
    out = None
    for _ in range(3):
        part = x.astype(BF16)
        x = x - part.astype(F32)
        term = lax.dot_general(ones, part, NT_DIMS, preferred_element_type=F32)
        out = term if out is None else out + term
    return out


def _slots_body(n_tiles, route_ref, pos1_ref, pos2_ref, tile_ref, run_scr, start_scr):
    phase = pl.program_id(0)
    i = pl.program_id(1)
    tm = route_ref.shape[0]
    route = route_ref[...]
    lane = lax.broadcasted_iota(jnp.int32, (tm, ROUTER_LANES), 1).astype(F32)
    pick1 = jnp.where(lane == route[:, 0:1], 1.0, 0.0)
    pick2 = jnp.where(lane == route[:, 1:2], 1.0, 0.0)
    occ = (pick1 + pick2).astype(BF16)
    ones_rows = jnp.ones((SUBLANES, tm), BF16)
    ones_lanes = jnp.ones((SUBLANES, ROUTER_LANES), BF16)

    @pl.when(i == 0)
    def _():
        run_scr[...] = jnp.zeros_like(run_scr)

    @pl.when(phase == 0)
    def _():
        run_scr[...] += jnp.dot(ones_rows, occ, preferred_element_type=F32)

        @pl.when(i == pl.num_programs(1) - 1)
        def _():
            padded = jnp.floor((run_scr[...] + (MOE_TM - 1)) * (1.0 / MOE_TM)) * MOE_TM
            r = lax.broadcasted_iota(jnp.int32, (ROUTER_LANES, ROUTER_LANES), 0)
            c = lax.broadcasted_iota(jnp.int32, (ROUTER_LANES, ROUTER_LANES), 1)
            before = jnp.where(r < c, 1.0, 0.0).astype(BF16)
            hi = padded.astype(BF16)
            mid = (padded - hi.astype(F32)).astype(BF16)
            low = (padded - hi.astype(F32) - mid.astype(F32)).astype(BF16)
            start = (jnp.dot(hi, before, preferred_element_type=F32)
                     + jnp.dot(mid, before, preferred_element_type=F32)
                     + jnp.dot(low, before, preferred_element_type=F32))
            start_scr[...] = start
            seg_end = start[0:1, :] + padded[0:1, :]
            tile_lo = (lax.broadcasted_iota(jnp.int32, (MOE_TILE_LANES, ROUTER_LANES), 0) * MOE_TM).astype(F32)
            e_lane = lax.broadcasted_iota(jnp.int32, (MOE_TILE_LANES, ROUTER_LANES), 1)
            ended = jnp.where((seg_end <= tile_lo) & (e_lane < N_EXPERTS), 1.0, 0.0).astype(BF16)
            tile_ref[...] = lax.dot_general(ones_lanes, ended, NT_DIMS, preferred_element_type=F32)

    @pl.when(phase == 1)
    def _():
        r = lax.broadcasted_iota(jnp.int32, (tm, tm), 0)
        c = lax.broadcasted_iota(jnp.int32, (tm, tm), 1)
        earlier = jnp.where(c < r, 1.0, 0.0).astype(BF16)
        base = (jnp.dot(earlier, occ, preferred_element_type=F32)
                + run_scr[0:1, :] + start_scr[0:1, :])
        pos1_ref[...] = _exact_dot_nt(ones_lanes, pick1 * base)
        pos2_ref[...] = _exact_dot_nt(ones_lanes, pick2 * base)
        run_scr[...] += jnp.dot(ones_rows, occ, preferred_element_type=F32)


def _slots(route, tm):
    n = route.shape[0]
    n_tiles = _moe_tiles(n)
    assert n_tiles <= MOE_TILE_LANES and 2 * n + N_EXPERTS * MOE_TM < 2 ** 24
    row_out = pl.BlockSpec((SUBLANES, tm), lambda ph, i: (0, i * ph))
    sds = jax.ShapeDtypeStruct((SUBLANES, n), F32)
    return pl.pallas_call(
        functools.partial(_slots_body, n_tiles),
        grid=(2, n // tm),
        in_specs=[pl.BlockSpec((tm, ROUTER_LANES), lambda ph, i: (i, 0))],
        out_specs=[row_out, row_out, pl.BlockSpec((SUBLANES, MOE_TILE_LANES), lambda ph, i: (0, 0))],
        out_shape=[sds, sds, jax.ShapeDtypeStruct((SUBLANES, MOE_TILE_LANES), F32)],
        scratch_shapes=[pltpu.VMEM((SUBLANES, ROUTER_LANES), F32), pltpu.VMEM((SUBLANES, ROUTER_LANES), F32)],
        compiler_params=_cparams(("arbitrary", "arbitrary")),
        name="moe_slots",
    )(route)


def _dispatch_body(n_tiles, pos_ref, tile_ref, xn_ref, xs_hbm, zero_scr, zero_sem, row_sem):
    tm = xn_ref.shape[0] // SUBLANES
    n = pl.num_programs(0) * tm
    base = pl.program_id(0) * tm
    tile_rows = MOE_TM * SUBLANES

    @pl.when(pl.program_id(0) == 0)
    def _():
        zero_scr[...] = jnp.zeros_like(zero_scr)

        def fill_copy(t):
            return pltpu.make_async_copy(zero_scr, xs_hbm.at[pl.ds(t * tile_rows, tile_rows), :], zero_sem)

        def has_padding(t):
            return (tile_ref[t] >= N_EXPERTS) | (tile_ref[t] != tile_ref[t + 1])

        @pl.loop(0, n_tiles)
        def _(t):
            @pl.when(has_padding(t))
            def _():
                fill_copy(t).start()

        @pl.loop(0, n_tiles)
        def _(t):
            @pl.when(has_padding(t))
            def _():
                fill_copy(t).wait()

    def issue(j, carry):
        src = xn_ref.at[pl.ds(pl.multiple_of(j * SUBLANES, SUBLANES), SUBLANES), :]
        for pick in range(2):
            dst = pl.multiple_of(pos_ref[pick * n + base + j], SUBLANES)
            pltpu.make_async_copy(src, xs_hbm.at[pl.ds(dst, SUBLANES), :], row_sem).start(priority=pick)
        return carry

    lax.fori_loop(0, tm, issue, 0, unroll=8)
    for _ in range(2):
        pltpu.make_async_copy(xn_ref, xs_hbm.at[pl.ds(0, tm * SUBLANES), :], row_sem).wait()


def _dispatch(pos, tile_map, xn, tm):
    n = xn.shape[0] // SUBLANES
    n_tiles = _moe_tiles(n)
    return pl.pallas_call(
        functools.partial(_dispatch_body, n_tiles),
        grid_spec=pltpu.PrefetchScalarGridSpec(
            num_scalar_prefetch=2,
            grid=(n // tm,),
            in_specs=[pl.BlockSpec((tm * SUBLANES, LANES), lambda i, p, t: (i, 0))],
            out_specs=pl.BlockSpec(memory_space=pl.ANY),
            scratch_shapes=[pltpu.VMEM((MOE_TM * SUBLANES, LANES), F32), pltpu.SemaphoreType.DMA(()),
                            pltpu.SemaphoreType.DMA(())],
        ),
        out_shape=jax.ShapeDtypeStruct((n_tiles * MOE_TM * SUBLANES, LANES), F32),
        compiler_params=_cparams(("arbitrary",)),
        name="moe_dispatch",
    )(pos, tile_map, xn)


EXPERT_SLOTS = 3


def _experts_body(tile_ref, xs_hbm, wg_ref, wu_ref, wd_ref, ys_ref, xbuf, sems):
    i = pl.program_id(0)
    tile_rows = MOE_TM * SUBLANES
    used = tile_ref[i] < N_EXPERTS

    def tile_copy(t, slot):
        rows = pl.ds(pl.multiple_of(t * tile_rows, tile_rows), tile_rows)
        return pltpu.make_async_copy(xs_hbm.at[rows, :], xbuf.at[slot], sems.at[slot])

    def prefetch(t):
        @pl.when(tile_ref[t] < N_EXPERTS)
        def _():
            tile_copy(t, t % EXPERT_SLOTS).start()

    @pl.when(i == 0)
    def _():
        prefetch(0)
        prefetch(1)

    prefetch(i + 2)

    @pl.when(used)
    def _():
        slot = i % EXPERT_SLOTS
        tile_copy(i, slot).wait()
        x = _load_token_major(xbuf, MOE_TM, (slot,)).astype(BF16)
        gate = jnp.dot(x, wg_ref[0].astype(BF16), preferred_element_type=F32)
        up = jnp.dot(x, wu_ref[0].astype(BF16), preferred_element_type=F32)
        hid = gate * jax.nn.sigmoid(gate) * up
        _store_token_major(ys_ref, jnp.dot(hid.astype(BF16), wd_ref[0].astype(BF16),
                                           preferred_element_type=F32))

    @pl.when(jnp.logical_not(used))
    def _():
        ys_ref[...] = jnp.zeros_like(ys_ref)


def _experts(tile_map, xs, w_gate, w_up, w_down, layer):
    d, f = w_gate.shape[1:]
    tile_rows = MOE_TM * SUBLANES
    n_tiles = xs.shape[0] // tile_rows
    y_map = lambda i, tm_ref: (i, 0)
    w_map = lambda i, tm_ref: (layer * N_EXPERTS + jnp.minimum(tm_ref[i], N_EXPERTS - 1), 0, 0)
    return pl.pallas_call(
        _experts_body,
        grid_spec=pltpu.PrefetchScalarGridSpec(
            num_scalar_prefetch=1,
            grid=(n_tiles,),
            in_specs=[pl.BlockSpec(memory_space=pl.ANY), pl.BlockSpec((1, d, f), w_map),
                      pl.BlockSpec((1, d, f), w_map), pl.BlockSpec((1, f, d), w_map)],
            out_specs=pl.BlockSpec((tile_rows, LANES), y_map),
            scratch_shapes=[pltpu.VMEM((EXPERT_SLOTS, tile_rows, LANES), F32),
                            pltpu.SemaphoreType.DMA((EXPERT_SLOTS,))],
        ),
        out_shape=jax.ShapeDtypeStruct(xs.shape, F32),
        compiler_params=_cparams(("arbitrary",)),
        name="moe_experts",
    )(tile_map, xs, w_gate, w_up, w_down)


def _combine_body(pos_ref, h_ref, route_ref, ys_hbm, o_ref, buf, sems):
    tm = h_ref.shape[0]
    steps = pl.num_programs(0)
    n = steps * tm
    i = pl.program_id(0)

    def start_gather(step, slot):
        base = step * tm

        def issue(g, carry):
            for u in range(SUBLANES):
                for pick in range(2):
                    src = pl.multiple_of(pos_ref[pick * n + base + g * SUBLANES + u], SUBLANES)
                    dst = pl.multiple_of(g * SUBLANES * SUBLANES, SUBLANES) + u * SUBLANES
                    pltpu.make_async_copy(ys_hbm.at[pl.ds(src, SUBLANES), :],
                                          buf.at[slot, pick, pl.ds(dst, SUBLANES), :],
                                          sems.at[slot]).start(priority=pick)
            return carry

        lax.fori_loop(0, tm // SUBLANES, issue, 0)

    @pl.when(i == 0)
    def _():
        start_gather(0, 0)

    @pl.when(i + 1 < steps)
    def _():
        start_gather(i + 1, (i + 1) % 2)

    slot = i % 2
    pltpu.make_async_copy(buf.at[slot], buf.at[slot], sems.at[slot]).wait()
    route = route_ref[...]
    y1 = _load_token_major(buf, tm, (slot, 0))
    y2 = _load_token_major(buf, tm, (slot, 1))
    o_ref[...] = h_ref[...] + route[:, 2:3] * y1 + route[:, 3:4] * y2


def _combine(pos, h, route, ys, tm):
    n, d = h.shape
    row = lambda i, p: (i, 0)
    return pl.pallas_call(
        _combine_body,
        grid_spec=pltpu.PrefetchScalarGridSpec(
            num_scalar_prefetch=1,
            grid=(n // tm,),
            in_specs=[pl.BlockSpec((tm, d), row), pl.BlockSpec((tm, ROUTER_LANES), row),
                      pl.BlockSpec(memory_space=pl.ANY)],
            out_specs=pl.BlockSpec((tm, d), row),
            scratch_shapes=[pltpu.VMEM((2, 2, tm * SUBLANES, LANES), F32), pltpu.SemaphoreType.DMA((2,))],
        ),
        out_shape=jax.ShapeDtypeStruct((n, d), F32),
        compiler_params=_cparams(("arbitrary",)),
        name="moe_combine",
    )(pos, h, route, ys)


def _gelu(x):
    return 0.5 * x * (1.0 + lax.erf(x * (1.0 / math.sqrt(2.0))))


def _in_odd_body(h_ref, g_ref, w_ref, vg_ref, u_ref, vn_ref, q_ref, k_ref, v_ref):
    xn = _rms(h_ref[...], g_ref[...])
    proj = jnp.dot(xn.astype(BF16), w_ref[...], preferred_element_type=F32)
    w = MIX_WIDTH
    u_ref[...] = _gelu(proj[:, :w]).astype(BF16)
    vn_ref[...] = _rms(_gelu(proj[:, w:2 * w]), vg_ref[...]).astype(BF16)
    q_ref[...] = (proj[:, 2 * w:3 * w] * Q_SCALE).astype(BF16)
    k_ref[...] = proj[:, 3 * w:4 * w].astype(BF16)
    v_ref[...] = proj[:, 4 * w:].astype(BF16)


def _in_odd(h, gain, w, v_gain, tm):
    n, d = h.shape
    row = lambda i: (i, 0)
    const2 = lambda i: (0, 0)
    sds = jax.ShapeDtypeStruct((n, MIX_WIDTH), BF16)
    return pl.pallas_call(
        _in_odd_body,
        grid=(n // tm,),
        in_specs=[pl.BlockSpec((tm, d), row), pl.BlockSpec((1, d), const2),
                  pl.BlockSpec(w.shape, const2), pl.BlockSpec((1, MIX_WIDTH), const2)],
        out_specs=[pl.BlockSpec((tm, MIX_WIDTH), row)] * 5,
        out_shape=[sds] * 5,
        compiler_params=_cparams(("arbitrary",)),
        name="in_odd",
    )(h, gain, w, v_gain)


SB_TQ = 64
SB_KB = 128
SB_NB = 2
SB_TK = SB_NB * SB_KB
SB_QBLK = 1024
SB_GROUP = 4
SB_UNDERFLOW = -150.0


def _sb_body(q_ref, k_ref, v_ref, o_ref, acc_scr, run_scr):
    qi = pl.program_id(2)
    lane = lax.broadcasted_iota(jnp.int32, (SB_TQ, LANES), 1)
    row = lax.broadcasted_iota(jnp.int32, (2 * SB_TQ, 1), 0) % SB_TQ
    col = lax.broadcasted_iota(jnp.int32, (2 * SB_TQ, SB_TK), 1)
    rr = lax.broadcasted_iota(jnp.int32, (2 * SB_KB, 2 * SB_KB), 0) % SB_KB
    cc = lax.broadcasted_iota(jnp.int32, (2 * SB_KB, 2 * SB_KB), 1)
    suffix = jnp.where((cc >= SB_KB) | (rr > cc), 1.0, 0.0).astype(BF16)

    def suffix_sums(x):
        hi = x.astype(BF16)
        lo = (x - hi.astype(F32)).astype(BF16)
        return jnp.dot(jnp.concatenate([hi, lo], axis=1), suffix, preferred_element_type=F32)

    def subtile_group(grp, _):
        q_los = [pl.multiple_of((grp * SB_GROUP + s) * SB_TQ, SB_TQ) for s in range(SB_GROUP)]
        q_starts = [qi * SB_QBLK + q_lo for q_lo in q_los]
        q_heads = []
        for q_lo in q_los:
            q = q_ref[0, pl.ds(q_lo, SB_TQ), :]
            zero = jnp.zeros_like(q)
            q_heads.append(jnp.concatenate(
                [jnp.where(lane < HEAD_DIM, q, zero), jnp.where(lane < HEAD_DIM, zero, q)], axis=0))
        acc_scr[...] = jnp.zeros_like(acc_scr)
        run_scr[...] = jnp.zeros_like(run_scr)

        def cond(carry):
            his, dones = carry
            active = [(hi > 0) & (done == 0) for hi, done in zip(his, dones)]
            return functools.reduce(jnp.logical_or, active)

        def body(carry):
            his, _ = carry
            group = range(SB_GROUP)
            kss = [pl.multiple_of(jnp.maximum(his[s] - SB_TK, 0), SB_TQ) for s in group]
            valid = [col < (jnp.minimum(row + q_starts[s], his[s]) - kss[s]) for s in group]
            zs = [lax.dot_general(q_heads[s], k_ref[0, pl.ds(kss[s], SB_TK), :], NT_DIMS,
                                  preferred_element_type=F32) for s in group]
            log_beta, log_rest = [], []
            for s in group:
                z = jnp.where(valid[s], zs[s], NEG_BIG)
                sp = jnp.log2(1.0 + jnp.exp2(-jnp.abs(z)))
                log_beta.append(jnp.minimum(z, 0.0) - sp)
                log_rest.append(log_beta[s] - z)
            sums = [[suffix_sums(log_rest[s][:, blk * SB_KB:(blk + 1) * SB_KB]) for blk in range(SB_NB)]
                    for s in group]
            dones = []
            for s in group:
                run = run_scr[s]
                pieces = [None] * SB_NB
                for blk in reversed(range(SB_NB)):
                    tt = sums[s][blk]
                    pieces[blk] = jnp.exp2(log_beta[s][:, blk * SB_KB:(blk + 1) * SB_KB] + tt[:, :SB_KB] + run)
                    run = run + tt[:, SB_KB:]
                att = jnp.concatenate(pieces, axis=1)
                acc_scr[s] += jnp.dot(att.astype(BF16), v_ref[0, pl.ds(kss[s], SB_TK), :],
                                      preferred_element_type=F32)
                run_scr[s] = run
                dones.append((jnp.max(run) <= SB_UNDERFLOW).astype(jnp.int32))
            return tuple(kss), tuple(dones)

        lax.while_loop(cond, body, (tuple(qs + SB_TQ for qs in q_starts),
                                    tuple(jnp.int32(0) for _ in range(SB_GROUP))))
        for s in range(SB_GROUP):
            o_ref[0, pl.ds(q_los[s], SB_TQ), :] = jnp.where(
                lane < HEAD_DIM, acc_scr[s, :SB_TQ], acc_scr[s, SB_TQ:]).astype(BF16)
        return 0

    lax.fori_loop(0, SB_QBLK // (SB_TQ * SB_GROUP), subtile_group, 0)


def _sb_attention(q, k, v):
    b, s, w = q.shape
    assert s % SB_QBLK == 0 and s >= SB_TK
    qspec = pl.BlockSpec((1, SB_QBLK, LANES), lambda bi, hp, i: (bi, i, hp))
    kvspec = pl.BlockSpec((1, s, LANES), lambda bi, hp, i: (bi, 0, hp))
    return pl.pallas_call(
        _sb_body,
        grid=(b, w // LANES, s // SB_QBLK),
        in_specs=[qspec, kvspec, kvspec],
        out_specs=qspec,
        out_shape=jax.ShapeDtypeStruct((b, s, w), BF16),
        scratch_shapes=[pltpu.VMEM((SB_GROUP, 2 * SB_TQ, LANES), F32),
                        pltpu.VMEM((SB_GROUP, 2 * SB_TQ, LANES), F32)],
        compiler_params=_cparams(("arbitrary", "arbitrary", "arbitrary")),
        name="sb_attention",
    )(q, k, v)


def _out_odd_body(u_ref, vn_ref, yd_ref, h_ref, ws_ref, bs_ref, wo_ref, fg_ref, wr_ref,
                  o_ref, xn_ref, route_ref, yc_scr):
    tm = u_ref.shape[0]
    r = lax.broadcasted_iota(jnp.int32, (SGU_BLOCK, SGU_BLOCK), 0)
    c = lax.broadcasted_iota(jnp.int32, (SGU_BLOCK, SGU_BLOCK), 1)
    for g in range(MIX_WIDTH // LANES):
        ls = slice(g * LANES, (g + 1) * LANES)
        ws = jnp.where(c <= r, ws_ref[g], jnp.zeros_like(ws_ref[g]))
        for blk in range(tm // SGU_BLOCK):
            rs = slice(blk * SGU_BLOCK, (blk + 1) * SGU_BLOCK)
            mixed = jnp.dot(ws, vn_ref[rs, ls], preferred_element_type=F32) + bs_ref[g]
            yc_scr[rs, ls] = (u_ref[rs, ls].astype(F32) * mixed).astype(BF16)
    w = MIX_WIDTH
    h_new = (h_ref[...]
             + jnp.dot(yc_scr[...], wo_ref[0:w, :], preferred_element_type=F32)
             + jnp.dot(yd_ref[...], wo_ref[w:, :], preferred_element_type=F32))
    o_ref[...] = h_new
    xn, route_ref[...] = _route_tokens(h_new, fg_ref[...], wr_ref[...])
    _store_token_major(xn_ref, xn)


def _out_odd(u, vn, yd, h, sgu_w, sgu_b, w_out, ffn_gain, w_router, tm):
    n, d = h.shape
    w = MIX_WIDTH
    row = lambda i: (i, 0)
    const2 = lambda i: (0, 0)
    const3 = lambda i: (0, 0, 0)
    return pl.pallas_call(
        _out_odd_body,
        grid=(n // tm,),
        in_specs=[pl.BlockSpec((tm, w), row), pl.BlockSpec((tm, w), row), pl.BlockSpec((tm, w), row),
                  pl.BlockSpec((tm, d), row), pl.BlockSpec(sgu_w.shape, const3),
                  pl.BlockSpec(sgu_b.shape, const3), pl.BlockSpec(w_out.shape, const2),
                  pl.BlockSpec((1, d), const2), pl.BlockSpec(w_router.shape, const2)],
        out_specs=[pl.BlockSpec((tm, d), row), pl.BlockSpec((tm * SUBLANES, LANES), row),
                   pl.BlockSpec((tm, ROUTER_LANES), row)],
        out_shape=[jax.ShapeDtypeStruct((n, d), F32), jax.ShapeDtypeStruct((n * SUBLANES, LANES), F32),
                   jax.ShapeDtypeStruct((n, ROUTER_LANES), F32)],
        scratch_shapes=[pltpu.VMEM((tm, w), BF16)],
        compiler_params=_cparams(("arbitrary",)),
        name="out_odd",
    )(u, vn, yd, h, sgu_w, sgu_b, w_out, ffn_gain, w_router)


def _moe_layer(h, xn, route, w_gate, w_up, w_down, layer):
    pos1, pos2, tile_map = _slots(route, tm=1024)
    pos = jnp.concatenate([pos1[0], pos2[0]]).astype(jnp.int32) * SUBLANES
    tile_map = tile_map[0].astype(jnp.int32)
    xs = _dispatch(pos, tile_map, xn, tm=512)
    ys = _experts(tile_map, xs, w_gate, w_up, w_down, layer)
    return _combine(pos, h, route, ys, tm=512)


def kernel(x, mix_norm_even, w_in_even, att_q_norm, att_k_norm, att_rel_bias, pool_w, pool_scale,
           w_out_even, mix_norm_odd, w_in_odd, sgu_v_norm, sgu_w, sgu_b, w_out_odd, ffn_norm,
           w_router_group, w_router_expert, w_exp_gate, w_exp_up, w_exp_down):
    b, s, d = x.shape
    n = b * s
    depth = ffn_norm.shape[0]
    heads = MIX_WIDTH // HEAD_DIM
    h = x.reshape(n, d)
    w_gate = w_exp_gate.reshape(depth * N_EXPERTS, d, EXPERT_FF)
    w_up = w_exp_up.reshape(depth * N_EXPERTS, d, EXPERT_FF)
    w_down = w_exp_down.reshape(depth * N_EXPERTS, EXPERT_FF, d)
    for layer in range(depth):
        i = layer // 2
        ffn_gain = ffn_norm[layer][None, :]
        w_router = _router_weights(w_router_group[layer], w_router_expert[layer])
        if layer % 2 == 0:
            q, k_pad, v_pad, p = _in_even(
                h.reshape(b, s, d), mix_norm_even[i][None, :], w_in_even[i].astype(BF16),
                jnp.tile(att_q_norm[i], heads)[None, :], jnp.tile(att_k_norm[i], heads)[None, :],
                tm=ATT_LEFT)
            ya = _band_attention(q, k_pad, v_pad, _band_bias(att_rel_bias[i]))
            h, xn, route = _out_even(ya.reshape(n, MIX_WIDTH), p.reshape(n, MIX_WIDTH), h,
                                     pool_w[i].astype(BF16), pool_scale[i][None, :],
                                     w_out_even[i].astype(BF16), ffn_gain, w_router, seq=s, tm=512)
        else:
            u, vn, q, k, v = _in_odd(h, mix_norm_odd[i][None, :], w_in_odd[i].astype(BF16),
                                     sgu_v_norm[i][None, :], tm=512)
            to3 = lambda t: t.reshape(b, s, MIX_WIDTH)
            yd = _sb_attention(to3(q), to3(k), to3(v))
            bias = jnp.broadcast_to(sgu_b[i][:, :, None], (N_GROUPS, SGU_BLOCK, LANES))
            h, xn, route = _out_odd(u, vn, yd.reshape(n, MIX_WIDTH), h, sgu_w[i].astype(BF16), bias,
                                    w_out_odd[i].astype(BF16), ffn_gain, w_router, tm=512)
        h = _moe_layer(h, xn, route, w_gate, w_up, w_down, layer)
    return h.reshape(b, s, d)
```

```python
import functools
import math

import jax
import jax.numpy as jnp
from jax import lax
from jax.experimental import pallas as pl
from jax.experimental.pallas import tpu as pltpu

F32 = jnp.float32
BF16 = jnp.bfloat16

D_MODEL = 1024
CHUNK = 64
EPS = 1e-6
HEAD_DIM = 64
MIX_WIDTH = 512
LANES = 128
SUBLANES = 8
ATT_LEFT = 8 * CHUNK
ATT_MAX_REL = 128
POOL_WINDOWS = (2, 4, 8, 16)
POOL_HALO = 16
SGU_BLOCK = 128
N_GROUPS = 4
N_EXP_PER_GROUP = 8
N_EXPERTS = N_GROUPS * N_EXP_PER_GROUP
EXPERT_FF = 256
ROUTER_LANES = 128
ROUTER_ROWS = 40
NEG_BIG = -1e30
VMEM_LIMIT = 56 * 1024 * 1024

NT_DIMS = (((1,), (1,)), ((), ()))
LOG2E = math.log2(math.e)
Q_SCALE = LOG2E / math.sqrt(HEAD_DIM)


def _cparams(sem):
    return pltpu.CompilerParams(dimension_semantics=sem, vmem_limit_bytes=VMEM_LIMIT)


def _store_token_major(ref, x, first_token=0):
    rows = x.shape[0]
    for s in range(SUBLANES):
        ref[pl.ds(first_token * SUBLANES + s, rows, stride=SUBLANES), :] = x[:, s * LANES:(s + 1) * LANES]


def _row_halves(rows):
    return [slice(0, rows // 2), slice(rows // 2, rows)]


def _project_and_route(h_ref, mixed, wo_ref, fg_ref, wr_ref, o_ref, xn_ref, route_ref):
    halves = _row_halves(h_ref.shape[0])
    h_new = [h_ref[rs, :] + jnp.dot(mixed[rs], wo_ref[...], preferred_element_type=F32) for rs in halves]
    for rs, hn in zip(halves, h_new):
        o_ref[rs, :] = hn
    for rs, hn in zip(halves, h_new):
        xn, route_ref[rs, :] = _route_tokens(hn, fg_ref[...], wr_ref[...])
        _store_token_major(xn_ref, xn, first_token=rs.start)


def _load_token_major(ref, rows, lead=()):
    return jnp.concatenate(
        [ref[lead + (pl.ds(s, rows, stride=SUBLANES), slice(None))] for s in range(SUBLANES)], axis=1)


def _rms(x, gain):
    return x * lax.rsqrt(jnp.mean(x * x, axis=-1, keepdims=True) + EPS) * gain


def _split_dot(x, m):
    hi = x.astype(BF16)
    lo = (x - hi.astype(F32)).astype(BF16)
    return (jnp.dot(hi, m, preferred_element_type=F32)
            + jnp.dot(lo, m, preferred_element_type=F32))


def _head_rms(t, gain):
    n = t.shape[-1]
    r = lax.broadcasted_iota(jnp.int32, (n, n), 0) // HEAD_DIM
    c = lax.broadcasted_iota(jnp.int32, (n, n), 1) // HEAD_DIM
    bd = jnp.where(r == c, 1.0, 0.0).astype(BF16)
    ms = _split_dot(t * t, bd) * (1.0 / HEAD_DIM)
    return t * lax.rsqrt(ms + EPS) * gain


def _in_even_body(h_ref, g_ref, w_ref, qg_ref, kg_ref, q_ref, k_ref, v_ref, p_ref):
    j = pl.program_id(1)

    @pl.when(j == 0)
    def _():
        k_ref[...] = jnp.zeros_like(k_ref)
        v_ref[...] = jnp.zeros_like(v_ref)

    @pl.when(j > 0)
    def _():
        xn = _rms(h_ref[0], g_ref[...])
        proj = jnp.dot(xn.astype(BF16), w_ref[...], preferred_element_type=F32)
        w = MIX_WIDTH
        q_ref[0] = (_head_rms(proj[:, :w], qg_ref[...]) * Q_SCALE).astype(BF16)
        k_ref[0] = _head_rms(proj[:, w:2 * w], kg_ref[...]).astype(BF16)
        v_ref[0] = proj[:, 2 * w:3 * w].astype(BF16)
        p_ref[0] = proj[:, 3 * w:].astype(BF16)


def _in_even(h, gain, w, q_gain, k_gain, tm):
    b, s, d = h.shape
    assert tm == ATT_LEFT and s % tm == 0
    nt = s // tm
    cur = lambda bi, j: (bi, jnp.maximum(j - 1, 0), 0)
    const = lambda bi, j: (0, 0)
    out_sds = lambda rows: jax.ShapeDtypeStruct((b, rows, MIX_WIDTH), BF16)
    return pl.pallas_call(
        _in_even_body,
        grid=(b, nt + 1),
        in_specs=[
            pl.BlockSpec((1, tm, d), cur),
            pl.BlockSpec((1, d), const),
            pl.BlockSpec(w.shape, const),
            pl.BlockSpec((1, MIX_WIDTH), const),
            pl.BlockSpec((1, MIX_WIDTH), const),
        ],
        out_specs=[
            pl.BlockSpec((1, tm, MIX_WIDTH), cur),
            pl.BlockSpec((1, tm, MIX_WIDTH), lambda bi, j: (bi, j, 0)),
            pl.BlockSpec((1, tm, MIX_WIDTH), lambda bi, j: (bi, j, 0)),
            pl.BlockSpec((1, tm, MIX_WIDTH), cur),
        ],
        out_shape=[out_sds(s), out_sds(s + ATT_LEFT), out_sds(s + ATT_LEFT), out_sds(s)],
        compiler_params=_cparams(("arbitrary", "arbitrary")),
        name="in_even",
    )(h, gain, w, q_gain, k_gain)


BAND_TQ = 2 * CHUNK
BAND_TK = BAND_TQ + ATT_LEFT


BAND_STEP_TILES = 4


def _band_body(q_ref, k_ref, v_ref, bias_ref, o_ref):
    @pl.loop(0, BAND_STEP_TILES)
    def _(t):
        rows = pl.ds(pl.multiple_of(t * BAND_TQ, BAND_TQ), BAND_TQ)
        _band_tile(pl.program_id(1) * BAND_STEP_TILES + t, q_ref.at[0, rows, :], k_ref, v_ref, bias_ref,
                   o_ref.at[0, rows, :])


def _band_tile(i, q_ref, k_ref, v_ref, bias_ref, o_ref):
    start = pl.multiple_of(i * BAND_TQ, BAND_TQ)
    lane = lax.broadcasted_iota(jnp.int32, (BAND_TQ, LANES), 1)
    col = lax.broadcasted_iota(jnp.int32, (2 * BAND_TQ, BAND_TK), 1)
    is_pad = (col + start) < ATT_LEFT
    pairs = range(MIX_WIDTH // LANES)
    lanes = [slice(hp * LANES, (hp + 1) * LANES) for hp in pairs]
    scores = []
    for hp in pairs:
        q = q_ref[:, lanes[hp]]
        kb = k_ref[0, pl.ds(start, BAND_TK), lanes[hp]]
        zero = jnp.zeros_like(q)
        q2 = jnp.concatenate([jnp.where(lane < HEAD_DIM, q, zero), jnp.where(lane < HEAD_DIM, zero, q)], axis=0)
        scores.append(lax.dot_general(q2, kb, NT_DIMS, preferred_element_type=F32))
    probs, denoms = [], []
    for hp in pairs:
        bias = bias_ref[2 * hp:2 * hp + 2].reshape(2 * BAND_TQ, BAND_TK)
        s = jnp.where(is_pad, NEG_BIG, scores[hp] + bias)
        p = jnp.exp2(s - jnp.max(s, axis=-1, keepdims=True))
        denoms.append(jnp.sum(p, axis=-1, keepdims=True))
        probs.append(p.astype(BF16))
    for hp in pairs:
        vb = v_ref[0, pl.ds(start, BAND_TK), lanes[hp]]
        o = jnp.dot(probs[hp], vb, preferred_element_type=F32) / denoms[hp]
        o_ref[:, lanes[hp]] = jnp.where(lane < HEAD_DIM, o[:BAND_TQ], o[BAND_TQ:]).astype(BF16)


def _band_bias(rel_bias):
    heads = rel_bias.shape[0]
    r = jnp.arange(BAND_TQ)[:, None]
    j = jnp.arange(BAND_TK)[None, :]
    jb = j - CHUNK * (r // CHUNK)
    in_band = (jb >= 0) & (jb < ATT_LEFT + CHUNK)
    period = BAND_TK + BAND_TQ
    far = jnp.broadcast_to(rel_bias[:, 2 * ATT_MAX_REL:], (heads, ATT_LEFT - ATT_MAX_REL + 1))
    near = rel_bias[:, 2 * ATT_MAX_REL - 1:0:-1]
    wrap = jnp.broadcast_to(rel_bias[:, 2 * ATT_MAX_REL:], (heads, period - BAND_TK))
    g = jnp.concatenate([far, near, wrap], axis=1).astype(F32)
    assert g.shape[1] == period
    toep = jnp.tile(g, (1, BAND_TQ))[:, :BAND_TQ * (period - 1)].reshape(heads, BAND_TQ, period - 1)
    return jnp.where(in_band[None], toep[:, :, :BAND_TK] * LOG2E, NEG_BIG)


def _band_attention(q, k_pad, v_pad, bias):
    b, s, w = q.shape
    sp = k_pad.shape[1]
    step_rows = BAND_STEP_TILES * BAND_TQ
    return pl.pallas_call(
        _band_body,
        grid=(b, s // step_rows),
        in_specs=[
            pl.BlockSpec((1, step_rows, w), lambda bi, i: (bi, i, 0)),
            pl.BlockSpec((1, sp, w), lambda bi, i: (bi, 0, 0)),
            pl.BlockSpec((1, sp, w), lambda bi, i: (bi, 0, 0)),
            pl.BlockSpec(bias.shape, lambda bi, i: (0, 0, 0)),
        ],
        out_specs=pl.BlockSpec((1, step_rows, w), lambda bi, i: (bi, i, 0)),
        out_shape=jax.ShapeDtypeStruct((b, s, w), BF16),
        compiler_params=_cparams(("arbitrary", "arbitrary")),
        name="band_attention",
    )(q, k_pad, v_pad, bias)


def _route_tokens(h, gain, w_router):
    xn = _rms(h, gain)
    x_hi = xn.astype(BF16)
    x_lo = (xn - x_hi.astype(F32)).astype(BF16)
    w_hi = w_router.astype(BF16)
    w_lo = (w_router - w_hi.astype(F32)).astype(BF16)
    logits = (jnp.dot(x_hi, w_hi, preferred_element_type=F32)
              + jnp.dot(x_lo, w_hi, preferred_element_type=F32)
              + jnp.dot(x_hi, w_lo, preferred_element_type=F32))
    lt = logits.T[:ROUTER_ROWS]
    sub = lax.broadcasted_iota(jnp.int32, lt.shape, 0).astype(F32)
    ninf = -jnp.inf

    def top(vals):
        m = jnp.max(vals, axis=0, keepdims=True)
        idx = jnp.min(jnp.where(vals == m, sub, float(ROUTER_LANES)), axis=0, keepdims=True)
        return m, idx

    is_group = sub < N_GROUPS
    g_max, g_sel = top(jnp.where(is_group, lt, ninf))
    g_den = jnp.sum(jnp.where(is_group, jnp.exp(lt - g_max), 0.0), axis=0, keepdims=True)
    g_weight = 1.0 / g_den
    lo = N_GROUPS + N_EXP_PER_GROUP * g_sel
    e_logits = jnp.where((sub >= lo) & (sub < lo + N_EXP_PER_GROUP), lt, ninf)
    e1, i1 = top(e_logits)
    e2, i2 = top(jnp.where(sub == i1, ninf, e_logits))
    t = jnp.exp(e2 - e1)
    w1 = g_weight / (1.0 + t)
    w2 = g_weight * t / (1.0 + t)
    rows = lax.broadcasted_iota(jnp.int32, (ROUTER_LANES, lt.shape[1]), 0)
    route_t = jnp.where(rows == 0, i1 - N_GROUPS, jnp.where(rows == 1, i2 - N_GROUPS, 0.0))
    route_t = jnp.where(rows == 2, w1, jnp.where(rows == 3, w2, route_t))
    return xn, route_t.T


def _router_weights(w_rg, w_re):
    pad = jnp.zeros((w_rg.shape[0], ROUTER_LANES - N_GROUPS - N_EXPERTS), F32)
    return jnp.concatenate([w_rg, w_re, pad], axis=1)


def _out_even_body(tiles_per_seq, ya_ref, p_ref, halo_ref, h_ref, pw_ref, ps_ref, wo_ref, fg_ref, wr_ref,
                   o_ref, xn_ref, route_ref, p_scr, yb_scr):
    tm = p_ref.shape[0]
    it = pl.program_id(0) % tiles_per_seq
    halo = halo_ref[...].astype(F32)
    p_scr[0:POOL_HALO, :] = jnp.where(it == 0, jnp.zeros_like(halo), halo)
    p_scr[POOL_HALO:, :] = p_ref[...].astype(F32)
    t = it * tm + lax.broadcasted_iota(jnp.int32, (tm, 1), 0)
    for g, win in enumerate(POOL_WINDOWS):
        ls = slice(g * LANES, (g + 1) * LANES)
        cur = p_scr[POOL_HALO:POOL_HALO + tm, ls]
        acc = cur
        for dlt in range(1, win):
            acc = acc + p_scr[POOL_HALO - dlt:POOL_HALO - dlt + tm, ls]
        cnt = jnp.minimum(t + 1, win).astype(F32)
        mixed = acc / cnt - cur
        yb = jnp.dot(mixed.astype(BF16), pw_ref[g], preferred_element_type=F32) * ps_ref[:, ls]
        yb_scr[:, ls] = yb.astype(BF16)
    mixed = jnp.concatenate([ya_ref[...], yb_scr[...]], axis=1)
    _project_and_route(h_ref, mixed, wo_ref, fg_ref, wr_ref, o_ref, xn_ref, route_ref)


def _out_even(ya, p, h, pool_w, pool_scale, w_out, ffn_gain, w_router, seq, tm):
    n, d = h.shape
    w = MIX_WIDTH
    row = lambda i: (i, 0)
    const2 = lambda i: (0, 0)
    halo_blocks = tm // POOL_HALO
    return pl.pallas_call(
        functools.partial(_out_even_body, seq // tm),
        grid=(n // tm,),
        in_specs=[
            pl.BlockSpec((tm, w), row),
            pl.BlockSpec((tm, w), row),
            pl.BlockSpec((POOL_HALO, w), lambda i: (jnp.maximum(i * halo_blocks - 1, 0), 0)),
            pl.BlockSpec((tm, d), row),
            pl.BlockSpec(pool_w.shape, lambda i: (0, 0, 0)),
            pl.BlockSpec((1, w), const2),
            pl.BlockSpec(w_out.shape, const2),
            pl.BlockSpec((1, d), const2),
            pl.BlockSpec(w_router.shape, const2),
        ],
        out_specs=[pl.BlockSpec((tm, d), row), pl.BlockSpec((tm * SUBLANES, LANES), row),
                   pl.BlockSpec((tm, ROUTER_LANES), row)],
        out_shape=[jax.ShapeDtypeStruct((n, d), F32), jax.ShapeDtypeStruct((n * SUBLANES, LANES), F32),
                   jax.ShapeDtypeStruct((n, ROUTER_LANES), F32)],
        scratch_shapes=[pltpu.VMEM((tm + POOL_HALO, w), F32), pltpu.VMEM((tm, w), BF16)],
        compiler_params=_cparams(("arbitrary",)),
        name="out_even",
    )(ya, p, p, h, pool_w, pool_scale, w_out, ffn_gain, w_router)


MOE_TM = 256
MOE_TILE_LANES = 256


def _moe_tiles(n):
    return (2 * n) // MOE_TM + N_EXPERTS


def _exact_dot_nt(ones, x):
    out = None
    for _ in range(3):
        part = x.astype(BF16)
        x = x - part.astype(F32)
        term = lax.dot_general(ones, part, NT_DIMS, preferred_element_type=F32)
        out = term if out is None else out + term
    return out


def _slots_body(n_tiles, route_ref, pos1_ref, pos2_ref, tile_ref, run_scr, start_scr):
    phase = pl.program_id(0)
    i = pl.program_id(1)
    tm = route_ref.shape[0]
    route = route_ref[...]
    lane = lax.broadcasted_iota(jnp.int32, (tm, ROUTER_LANES), 1).astype(F32)
    pick1 = jnp.where(lane == route[:, 0:1], 1.0, 0.0)
    pick2 = jnp.where(lane == route[:, 1:2], 1.0, 0.0)
    occ = (pick1 + pick2).astype(BF16)
    ones_rows = jnp.ones((SUBLANES, tm), BF16)
    ones_lanes = jnp.ones((SUBLANES, ROUTER_LANES), BF16)

    @pl.when(i == 0)
    def _():
        run_scr[...] = jnp.zeros_like(run_scr)

    @pl.when(phase == 0)
    def _():
        run_scr[...] += jnp.dot(ones_rows, occ, preferred_element_type=F32)

        @pl.when(i == pl.num_programs(1) - 1)
        def _():
            padded = jnp.floor((run_scr[...] + (MOE_TM - 1)) * (1.0 / MOE_TM)) * MOE_TM
            r = lax.broadcasted_iota(jnp.int32, (ROUTER_LANES, ROUTER_LANES), 0)
            c = lax.broadcasted_iota(jnp.int32, (ROUTER_LANES, ROUTER_LANES), 1)
            before = jnp.where(r < c, 1.0, 0.0).astype(BF16)
            hi = padded.astype(BF16)
            mid = (padded - hi.astype(F32)).astype(BF16)
            low = (padded - hi.astype(F32) - mid.astype(F32)).astype(BF16)
            start = (jnp.dot(hi, before, preferred_element_type=F32)
                     + jnp.dot(mid, before, preferred_element_type=F32)
                     + jnp.dot(low, before, preferred_element_type=F32))
            start_scr[...] = start
            seg_end = start[0:1, :] + padded[0:1, :]
            tile_lo = (lax.broadcasted_iota(jnp.int32, (MOE_TILE_LANES, ROUTER_LANES), 0) * MOE_TM).astype(F32)
            e_lane = lax.broadcasted_iota(jnp.int32, (MOE_TILE_LANES, ROUTER_LANES), 1)
            ended = jnp.where((seg_end <= tile_lo) & (e_lane < N_EXPERTS), 1.0, 0.0).astype(BF16)
            tile_ref[...] = lax.dot_general(ones_lanes, ended, NT_DIMS, preferred_element_type=F32)

    @pl.when(phase == 1)
    def _():
        r = lax.broadcasted_iota(jnp.int32, (tm, tm), 0)
        c = lax.broadcasted_iota(jnp.int32, (tm, tm), 1)
        earlier = jnp.where(c < r, 1.0, 0.0).astype(BF16)
        base = (jnp.dot(earlier, occ, preferred_element_type=F32)
                + run_scr[0:1, :] + start_scr[0:1, :])
        pos1_ref[...] = _exact_dot_nt(ones_lanes, pick1 * base)
        pos2_ref[...] = _exact_dot_nt(ones_lanes, pick2 * base)
        run_scr[...] += jnp.dot(ones_rows, occ, preferred_element_type=F32)


def _slots(route, tm):
    n = route.shape[0]
    n_tiles = _moe_tiles(n)
    assert n_tiles <= MOE_TILE_LANES and 2 * n + N_EXPERTS * MOE_TM < 2 ** 24
    row_out = pl.BlockSpec((SUBLANES, tm), lambda ph, i: (0, i * ph))
    sds = jax.ShapeDtypeStruct((SUBLANES, n), F32)
    return pl.pallas_call(
        functools.partial(_slots_body, n_tiles),
        grid=(2, n // tm),
        in_specs=[pl.BlockSpec((tm, ROUTER_LANES), lambda ph, i: (i, 0))],
        out_specs=[row_out, row_out, pl.BlockSpec((SUBLANES, MOE_TILE_LANES), lambda ph, i: (0, 0))],
        out_shape=[sds, sds, jax.ShapeDtypeStruct((SUBLANES, MOE_TILE_LANES), F32)],
        scratch_shapes=[pltpu.VMEM((SUBLANES, ROUTER_LANES), F32), pltpu.VMEM((SUBLANES, ROUTER_LANES), F32)],
        compiler_params=_cparams(("arbitrary", "arbitrary")),
        name="moe_slots",
    )(route)


def _dispatch_body(n_tiles, pos_ref, tile_ref, xn_ref, xs_hbm, zero_scr, zero_sem, row_sem):
    tm = xn_ref.shape[0] // SUBLANES
    n = pl.num_programs(0) * tm
    base = pl.program_id(0) * tm
    tile_rows = MOE_TM * SUBLANES

    @pl.when(pl.program_id(0) == 0)
    def _():
        zero_scr[...] = jnp.zeros_like(zero_scr)

        def fill_copy(t):
            return pltpu.make_async_copy(zero_scr, xs_hbm.at[pl.ds(t * tile_rows, tile_rows), :], zero_sem)

        def has_padding(t):
            return (tile_ref[t] >= N_EXPERTS) | (tile_ref[t] != tile_ref[t + 1])

        @pl.loop(0, n_tiles)
        def _(t):
            @pl.when(has_padding(t))
            def _():
                fill_copy(t).start()

        @pl.loop(0, n_tiles)
        def _(t):
            @pl.when(has_padding(t))
            def _():
                fill_copy(t).wait()

    def issue(j, carry):
        src = xn_ref.at[pl.ds(pl.multiple_of(j * SUBLANES, SUBLANES), SUBLANES), :]
        for pick in range(2):
            dst = pl.multiple_of(pos_ref[pick * n + base + j], SUBLANES)
            pltpu.make_async_copy(src, xs_hbm.at[pl.ds(dst, SUBLANES), :], row_sem).start(priority=pick)
        return carry

    lax.fori_loop(0, tm, issue, 0, unroll=8)
    for _ in range(2):
        pltpu.make_async_copy(xn_ref, xs_hbm.at[pl.ds(0, tm * SUBLANES), :], row_sem).wait()


def _dispatch(pos, tile_map, xn, tm):
    n = xn.shape[0] // SUBLANES
    n_tiles = _moe_tiles(n)
    return pl.pallas_call(
        functools.partial(_dispatch_body, n_tiles),
        grid_spec=pltpu.PrefetchScalarGridSpec(
            num_scalar_prefetch=2,
            grid=(n // tm,),
            in_specs=[pl.BlockSpec((tm * SUBLANES, LANES), lambda i, p, t: (i, 0))],
            out_specs=pl.BlockSpec(memory_space=pl.ANY),
            scratch_shapes=[pltpu.VMEM((MOE_TM * SUBLANES, LANES), F32), pltpu.SemaphoreType.DMA(()),
                            pltpu.SemaphoreType.DMA(())],
        ),
        out_shape=jax.ShapeDtypeStruct((n_tiles * MOE_TM * SUBLANES, LANES), F32),
        compiler_params=_cparams(("arbitrary",)),
        name="moe_dispatch",
    )(pos, tile_map, xn)


EXPERT_SLOTS = 3


def _experts_body(tile_ref, xs_hbm, wg_ref, wu_ref, wd_ref, ys_ref, xbuf, sems):
    i = pl.program_id(0)
    tile_rows = MOE_TM * SUBLANES
    used = tile_ref[i] < N_EXPERTS

    def tile_copy(t, slot):
        rows = pl.ds(pl.multiple_of(t * tile_rows, tile_rows), tile_rows)
        return pltpu.make_async_copy(xs_hbm.at[rows, :], xbuf.at[slot], sems.at[slot])

    def prefetch(t):
        @pl.when(tile_ref[t] < N_EXPERTS)
        def _():
            tile_copy(t, t % EXPERT_SLOTS).start()

    @pl.when(i == 0)
    def _():
        prefetch(0)
        prefetch(1)

    prefetch(i + 2)

    @pl.when(used)
    def _():
        slot = i % EXPERT_SLOTS
        tile_copy(i, slot).wait()
        x = _load_token_major(xbuf, MOE_TM, (slot,)).astype(BF16)
        gate = jnp.dot(x, wg_ref[0].astype(BF16), preferred_element_type=F32)
        up = jnp.dot(x, wu_ref[0].astype(BF16), preferred_element_type=F32)
        hid = gate * jax.nn.sigmoid(gate) * up
        _store_token_major(ys_ref, jnp.dot(hid.astype(BF16), wd_ref[0].astype(BF16),
                                           preferred_element_type=F32))

    @pl.when(jnp.logical_not(used))
    def _():
        ys_ref[...] = jnp.zeros_like(ys_ref)


def _experts(tile_map, xs, w_gate, w_up, w_down, layer):
    d, f = w_gate.shape[1:]
    tile_rows = MOE_TM * SUBLANES
    n_tiles = xs.shape[0] // tile_rows
    y_map = lambda i, tm_ref: (i, 0)
    w_map = lambda i, tm_ref: (layer * N_EXPERTS + jnp.minimum(tm_ref[i], N_EXPERTS - 1), 0, 0)
    return pl.pallas_call(
        _experts_body,
        grid_spec=pltpu.PrefetchScalarGridSpec(
            num_scalar_prefetch=1,
            grid=(n_tiles,),
            in_specs=[pl.BlockSpec(memory_space=pl.ANY), pl.BlockSpec((1, d, f), w_map),
                      pl.BlockSpec((1, d, f), w_map), pl.BlockSpec((1, f, d), w_map)],
            out_specs=pl.BlockSpec((tile_rows, LANES), y_map),
            scratch_shapes=[pltpu.VMEM((EXPERT_SLOTS, tile_rows, LANES), F32),
                            pltpu.SemaphoreType.DMA((EXPERT_SLOTS,))],
        ),
        out_shape=jax.ShapeDtypeStruct(xs.shape, F32),
        compiler_params=_cparams(("arbitrary",)),
        name="moe_experts",
    )(tile_map, xs, w_gate, w_up, w_down)


def _combine_body(pos_ref, h_ref, route_ref, ys_hbm, o_ref, buf, sems):
    tm = h_ref.shape[0]
    steps = pl.num_programs(0)
    n = steps * tm
    i = pl.program_id(0)

    def start_gather(step, slot):
        base = step * tm

        def issue(g, carry):
            for u in range(SUBLANES):
                for pick in range(2):
                    src = pl.multiple_of(pos_ref[pick * n + base + g * SUBLANES + u], SUBLANES)
                    dst = pl.multiple_of(g * SUBLANES * SUBLANES, SUBLANES) + u * SUBLANES
                    pltpu.make_async_copy(ys_hbm.at[pl.ds(src, SUBLANES), :],
                                          buf.at[slot, pick, pl.ds(dst, SUBLANES), :],
                                          sems.at[slot]).start(priority=pick)
            return carry

        lax.fori_loop(0, tm // SUBLANES, issue, 0)

    @pl.when(i == 0)
    def _():
        start_gather(0, 0)

    @pl.when(i + 1 < steps)
    def _():
        start_gather(i + 1, (i + 1) % 2)

    slot = i % 2
    pltpu.make_async_copy(buf.at[slot], buf.at[slot], sems.at[slot]).wait()
    route = route_ref[...]
    y1 = _load_token_major(buf, tm, (slot, 0))
    y2 = _load_token_major(buf, tm, (slot, 1))
    o_ref[...] = h_ref[...] + route[:, 2:3] * y1 + route[:, 3:4] * y2


def _combine(pos, h, route, ys, tm):
    n, d = h.shape
    row = lambda i, p: (i, 0)
    return pl.pallas_call(
        _combine_body,
        grid_spec=pltpu.PrefetchScalarGridSpec(
            num_scalar_prefetch=1,
            grid=(n // tm,),
            in_specs=[pl.BlockSpec((tm, d), row), pl.BlockSpec((tm, ROUTER_LANES), row),
                      pl.BlockSpec(memory_space=pl.ANY)],
            out_specs=pl.BlockSpec((tm, d), row),
            scratch_shapes=[pltpu.VMEM((2, 2, tm * SUBLANES, LANES), F32), pltpu.SemaphoreType.DMA((2,))],
        ),
        out_shape=jax.ShapeDtypeStruct((n, d), F32),
        compiler_params=_cparams(("arbitrary",)),
        name="moe_combine",
    )(pos, h, route, ys)


def _gelu(x):
    return 0.5 * x * (1.0 + lax.erf(x * (1.0 / math.sqrt(2.0))))


def _in_odd_body(h_ref, g_ref, w_ref, vg_ref, u_ref, vn_ref, q_ref, k_ref, v_ref):
    xn = _rms(h_ref[...], g_ref[...])
    proj = jnp.dot(xn.astype(BF16), w_ref[...], preferred_element_type=F32)
    w = MIX_WIDTH
    u_ref[...] = _gelu(proj[:, :w]).astype(BF16)
    vn_ref[...] = _rms(_gelu(proj[:, w:2 * w]), vg_ref[...]).astype(BF16)
    q_ref[...] = (proj[:, 2 * w:3 * w] * Q_SCALE).astype(BF16)
    k_ref[...] = proj[:, 3 * w:4 * w].astype(BF16)
    v_ref[...] = proj[:, 4 * w:].astype(BF16)


def _in_odd(h, gain, w, v_gain, tm):
    n, d = h.shape
    row = lambda i: (i, 0)
    const2 = lambda i: (0, 0)
    sds = jax.ShapeDtypeStruct((n, MIX_WIDTH), BF16)
    return pl.pallas_call(
        _in_odd_body,
        grid=(n // tm,),
        in_specs=[pl.BlockSpec((tm, d), row), pl.BlockSpec((1, d), const2),
                  pl.BlockSpec(w.shape, const2), pl.BlockSpec((1, MIX_WIDTH), const2)],
        out_specs=[pl.BlockSpec((tm, MIX_WIDTH), row)] * 5,
        out_shape=[sds] * 5,
        compiler_params=_cparams(("arbitrary",)),
        name="in_odd",
    )(h, gain, w, v_gain)


SB_TQ = 64
SB_KB = 128
SB_NB = 2
SB_TK = SB_NB * SB_KB
SB_QBLK = 1024
SB_GROUP = 4
SB_UNDERFLOW = -150.0


def _sb_body(q_ref, k_ref, v_ref, o_ref, acc_scr, run_scr):
    qi = pl.program_id(2)
    lane = lax.broadcasted_iota(jnp.int32, (SB_TQ, LANES), 1)
    row = lax.broadcasted_iota(jnp.int32, (2 * SB_TQ, 1), 0) % SB_TQ
    col = lax.broadcasted_iota(jnp.int32, (2 * SB_TQ, SB_TK), 1)
    rr = lax.broadcasted_iota(jnp.int32, (2 * SB_KB, 2 * SB_KB), 0) % SB_KB
    cc = lax.broadcasted_iota(jnp.int32, (2 * SB_KB, 2 * SB_KB), 1)
    suffix = jnp.where((cc >= SB_KB) | (rr > cc), 1.0, 0.0).astype(BF16)

    def suffix_sums(x):
        hi = x.astype(BF16)
        lo = (x - hi.astype(F32)).astype(BF16)
        return jnp.dot(jnp.concatenate([hi, lo], axis=1), suffix, preferred_element_type=F32)

    def subtile_group(grp, _):
        q_los = [pl.multiple_of((grp * SB_GROUP + s) * SB_TQ, SB_TQ) for s in range(SB_GROUP)]
        q_starts = [qi * SB_QBLK + q_lo for q_lo in q_los]
        q_heads = []
        for q_lo in q_los:
            q = q_ref[0, pl.ds(q_lo, SB_TQ), :]
            zero = jnp.zeros_like(q)
            q_heads.append(jnp.concatenate(
                [jnp.where(lane < HEAD_DIM, q, zero), jnp.where(lane < HEAD_DIM, zero, q)], axis=0))
        acc_scr[...] = jnp.zeros_like(acc_scr)
        run_scr[...] = jnp.zeros_like(run_scr)

        def cond(carry):
            his, dones = carry
            active = [(hi > 0) & (done == 0) for hi, done in zip(his, dones)]
            return functools.reduce(jnp.logical_or, active)

        def body(carry):
            his, _ = carry
            group = range(SB_GROUP)
            kss = [pl.multiple_of(jnp.maximum(his[s] - SB_TK, 0), SB_TQ) for s in group]
            valid = [col < (jnp.minimum(row + q_starts[s], his[s]) - kss[s]) for s in group]
            zs = [lax.dot_general(q_heads[s], k_ref[0, pl.ds(kss[s], SB_TK), :], NT_DIMS,
                                  preferred_element_type=F32) for s in group]
            log_beta, log_rest = [], []
            for s in group:
                z = jnp.where(valid[s], zs[s], NEG_BIG)
                sp = jnp.log2(1.0 + jnp.exp2(-jnp.abs(z)))
                log_beta.append(jnp.minimum(z, 0.0) - sp)
                log_rest.append(log_beta[s] - z)
            sums = [[suffix_sums(log_rest[s][:, blk * SB_KB:(blk + 1) * SB_KB]) for blk in range(SB_NB)]
                    for s in group]
            dones = []
            for s in group:
                run = run_scr[s]
                pieces = [None] * SB_NB
                for blk in reversed(range(SB_NB)):
                    tt = sums[s][blk]
                    pieces[blk] = jnp.exp2(log_beta[s][:, blk * SB_KB:(blk + 1) * SB_KB] + tt[:, :SB_KB] + run)
                    run = run + tt[:, SB_KB:]
                att = jnp.concatenate(pieces, axis=1)
                acc_scr[s] += jnp.dot(att.astype(BF16), v_ref[0, pl.ds(kss[s], SB_TK), :],
                                      preferred_element_type=F32)
                run_scr[s] = run
                dones.append((jnp.max(run) <= SB_UNDERFLOW).astype(jnp.int32))
            return tuple(kss), tuple(dones)

        lax.while_loop(cond, body, (tuple(qs + SB_TQ for qs in q_starts),
                                    tuple(jnp.int32(0) for _ in range(SB_GROUP))))
        for s in range(SB_GROUP):
            o_ref[0, pl.ds(q_los[s], SB_TQ), :] = jnp.where(
                lane < HEAD_DIM, acc_scr[s, :SB_TQ], acc_scr[s, SB_TQ:]).astype(BF16)
        return 0

    lax.fori_loop(0, SB_QBLK // (SB_TQ * SB_GROUP), subtile_group, 0)


def _sb_attention(q, k, v):
    b, s, w = q.shape
    assert s % SB_QBLK == 0 and s >= SB_TK
    qspec = pl.BlockSpec((1, SB_QBLK, LANES), lambda bi, hp, i: (bi, i, hp))
    kvspec = pl.BlockSpec((1, s, LANES), lambda bi, hp, i: (bi, 0, hp))
    return pl.pallas_call(
        _sb_body,
        grid=(b, w // LANES, s // SB_QBLK),
        in_specs=[qspec, kvspec, kvspec],
        out_specs=qspec,
        out_shape=jax.ShapeDtypeStruct((b, s, w), BF16),
        scratch_shapes=[pltpu.VMEM((SB_GROUP, 2 * SB_TQ, LANES), F32),
                        pltpu.VMEM((SB_GROUP, 2 * SB_TQ, LANES), F32)],
        compiler_params=_cparams(("arbitrary", "arbitrary", "arbitrary")),
        name="sb_attention",
    )(q, k, v)


def _out_odd_body(u_ref, vn_ref, yd_ref, h_ref, ws_ref, bs_ref, wo_ref, fg_ref, wr_ref,
                  o_ref, xn_ref, route_ref, yc_scr):
    tm = u_ref.shape[0]
    r = lax.broadcasted_iota(jnp.int32, (SGU_BLOCK, SGU_BLOCK), 0)
    c = lax.broadcasted_iota(jnp.int32, (SGU_BLOCK, SGU_BLOCK), 1)
    for g in range(MIX_WIDTH // LANES):
        ls = slice(g * LANES, (g + 1) * LANES)
        ws = jnp.where(c <= r, ws_ref[g], jnp.zeros_like(ws_ref[g]))
        for blk in range(tm // SGU_BLOCK):
            rs = slice(blk * SGU_BLOCK, (blk + 1) * SGU_BLOCK)
            mixed = jnp.dot(ws, vn_ref[rs, ls], preferred_element_type=F32) + bs_ref[g]
            yc_scr[rs, ls] = (u_ref[rs, ls].astype(F32) * mixed).astype(BF16)
    mixed = jnp.concatenate([yc_scr[...], yd_ref[...]], axis=1)
    _project_and_route(h_ref, mixed, wo_ref, fg_ref, wr_ref, o_ref, xn_ref, route_ref)


def _out_odd(u, vn, yd, h, sgu_w, sgu_b, w_out, ffn_gain, w_router, tm):
    n, d = h.shape
    w = MIX_WIDTH
    row = lambda i: (i, 0)
    const2 = lambda i: (0, 0)
    const3 = lambda i: (0, 0, 0)
    return pl.pallas_call(
        _out_odd_body,
        grid=(n // tm,),
        in_specs=[pl.BlockSpec((tm, w), row), pl.BlockSpec((tm, w), row), pl.BlockSpec((tm, w), row),
                  pl.BlockSpec((tm, d), row), pl.BlockSpec(sgu_w.shape, const3),
                  pl.BlockSpec(sgu_b.shape, const3), pl.BlockSpec(w_out.shape, const2),
                  pl.BlockSpec((1, d), const2), pl.BlockSpec(w_router.shape, const2)],
        out_specs=[pl.BlockSpec((tm, d), row), pl.BlockSpec((tm * SUBLANES, LANES), row),
                   pl.BlockSpec((tm, ROUTER_LANES), row)],
        out_shape=[jax.ShapeDtypeStruct((n, d), F32), jax.ShapeDtypeStruct((n * SUBLANES, LANES), F32),
                   jax.ShapeDtypeStruct((n, ROUTER_LANES), F32)],
        scratch_shapes=[pltpu.VMEM((tm, w), BF16)],
        compiler_params=_cparams(("arbitrary",)),
        name="out_odd",
    )(u, vn, yd, h, sgu_w, sgu_b, w_out, ffn_gain, w_router)


def _moe_layer(h, xn, route, w_gate, w_up, w_down, layer):
    pos1, pos2, tile_map = _slots(route, tm=1024)
    pos = jnp.concatenate([pos1[0], pos2[0]]).astype(jnp.int32) * SUBLANES
    tile_map = tile_map[0].astype(jnp.int32)
    xs = _dispatch(pos, tile_map, xn, tm=512)
    ys = _experts(tile_map, xs, w_gate, w_up, w_down, layer)
    return _combine(pos, h, route, ys, tm=256)


def kernel(x, mix_norm_even, w_in_even, att_q_norm, att_k_norm, att_rel_bias, pool_w, pool_scale,
           w_out_even, mix_norm_odd, w_in_odd, sgu_v_norm, sgu_w, sgu_b, w_out_odd, ffn_norm,
           w_router_group, w_router_expert, w_exp_gate, w_exp_up, w_exp_down):
    b, s, d = x.shape
    n = b * s
    depth = ffn_norm.shape[0]
    heads = MIX_WIDTH // HEAD_DIM
    h = x.reshape(n, d)
    w_gate = w_exp_gate.reshape(depth * N_EXPERTS, d, EXPERT_FF)
    w_up = w_exp_up.reshape(depth * N_EXPERTS, d, EXPERT_FF)
    w_down = w_exp_down.reshape(depth * N_EXPERTS, EXPERT_FF, d)
    for layer in range(depth):
        i = layer // 2
        ffn_gain = ffn_norm[layer][None, :]
        w_router = _router_weights(w_router_group[layer], w_router_expert[layer])
        if layer % 2 == 0:
            q, k_pad, v_pad, p = _in_even(
                h.reshape(b, s, d), mix_norm_even[i][None, :], w_in_even[i].astype(BF16),
                jnp.tile(att_q_norm[i], heads)[None, :], jnp.tile(att_k_norm[i], heads)[None, :],
                tm=ATT_LEFT)
            ya = _band_attention(q, k_pad, v_pad, _band_bias(att_rel_bias[i]))
            h, xn, route = _out_even(ya.reshape(n, MIX_WIDTH), p.reshape(n, MIX_WIDTH), h,
                                     pool_w[i].astype(BF16), pool_scale[i][None, :],
                                     w_out_even[i].astype(BF16), ffn_gain, w_router, seq=s, tm=512)
        else:
            u, vn, q, k, v = _in_odd(h, mix_norm_odd[i][None, :], w_in_odd[i].astype(BF16),
                                     sgu_v_norm[i][None, :], tm=512)
            to3 = lambda t: t.reshape(b, s, MIX_WIDTH)
            yd = _sb_attention(to3(q), to3(k), to3(v))
            bias = jnp.broadcast_to(sgu_b[i][:, :, None], (N_GROUPS, SGU_BLOCK, LANES))
            h, xn, route = _out_odd(u, vn, yd.reshape(n, MIX_WIDTH), h, sgu_w[i].astype(BF16), bias,
                                    w_out_odd[i].astype(BF16), ffn_gain, w_router, tm=512)
        h = _moe_layer(h, xn, route, w_gate, w_up, w_down, layer)
    return h.reshape(b, s, d)
```

```python
import functools
import math

import jax
import jax.numpy as jnp
from jax import lax
from jax.experimental import pallas as pl
from jax.experimental.pallas import tpu as pltpu

F32 = jnp.float32
BF16 = jnp.bfloat16

D_MODEL = 1024
CHUNK = 64
EPS = 1e-6
HEAD_DIM = 64
MIX_WIDTH = 512
LANES = 128
SUBLANES = 8
ATT_LEFT = 8 * CHUNK
ATT_MAX_REL = 128
POOL_WINDOWS = (2, 4, 8, 16)
POOL_HALO = 16
SGU_BLOCK = 128
N_GROUPS = 4
N_EXP_PER_GROUP = 8
N_EXPERTS = N_GROUPS * N_EXP_PER_GROUP
EXPERT_FF = 256
ROUTER_LANES = 128
ROUTER_ROWS = 40
NEG_BIG = -1e30
VMEM_LIMIT = 56 * 1024 * 1024

NT_DIMS = (((1,), (1,)), ((), ()))
LOG2E = math.log2(math.e)
Q_SCALE = LOG2E / math.sqrt(HEAD_DIM)


def _cparams(sem):
    return pltpu.CompilerParams(dimension_semantics=sem, vmem_limit_bytes=VMEM_LIMIT)


def _store_token_major(ref, x, first_token=0):
    rows = x.shape[0]
    for s in range(SUBLANES):
        ref[pl.ds(first_token * SUBLANES + s, rows, stride=SUBLANES), :] = x[:, s * LANES:(s + 1) * LANES]


def _row_halves(rows):
    return [slice(0, rows // 2), slice(rows // 2, rows)]


def _project_and_route(h_ref, mixed, wo_ref, fg_ref, wr_ref, o_ref, xn_ref, route_ref):
    halves = _row_halves(h_ref.shape[0])
    h_new = [h_ref[rs, :] + jnp.dot(mixed[rs], wo_ref[...], preferred_element_type=F32) for rs in halves]
    for rs, hn in zip(halves, h_new):
        o_ref[rs, :] = hn
    for rs, hn in zip(halves, h_new):
        xn, route_ref[rs, :] = _route_tokens(hn, fg_ref[...], wr_ref[...])
        _store_token_major(xn_ref, xn, first_token=rs.start)


def _load_token_major(ref, rows, lead=()):
    return jnp.concatenate(
        [ref[lead + (pl.ds(s, rows, stride=SUBLANES), slice(None))] for s in range(SUBLANES)], axis=1)


def _rms(x, gain):
    return x * lax.rsqrt(jnp.mean(x * x, axis=-1, keepdims=True) + EPS) * gain


def _split_dot(x, m):
    hi = x.astype(BF16)
    lo = (x - hi.astype(F32)).astype(BF16)
    return (jnp.dot(hi, m, preferred_element_type=F32)
            + jnp.dot(lo, m, preferred_element_type=F32))


def _head_rms(t, gain):
    n = t.shape[-1]
    r = lax.broadcasted_iota(jnp.int32, (n, n), 0) // HEAD_DIM
    c = lax.broadcasted_iota(jnp.int32, (n, n), 1) // HEAD_DIM
    bd = jnp.where(r == c, 1.0, 0.0).astype(BF16)
    ms = _split_dot(t * t, bd) * (1.0 / HEAD_DIM)
    return t * lax.rsqrt(ms + EPS) * gain


def _in_even_body(h_ref, g_ref, w_ref, qg_ref, kg_ref, q_ref, k_ref, v_ref, p_ref):
    j = pl.program_id(1)

    @pl.when(j == 0)
    def _():
        k_ref[...] = jnp.zeros_like(k_ref)
        v_ref[...] = jnp.zeros_like(v_ref)

    @pl.when(j > 0)
    def _():
        xn = _rms(h_ref[0], g_ref[...])
        proj = jnp.dot(xn.astype(BF16), w_ref[...], preferred_element_type=F32)
        w = MIX_WIDTH
        q_ref[0] = (_head_rms(proj[:, :w], qg_ref[...]) * Q_SCALE).astype(BF16)
        k_ref[0] = _head_rms(proj[:, w:2 * w], kg_ref[...]).astype(BF16)
        v_ref[0] = proj[:, 2 * w:3 * w].astype(BF16)
        p_ref[0] = proj[:, 3 * w:].astype(BF16)


def _in_even(h, gain, w, q_gain, k_gain, tm):
    b, s, d = h.shape
    assert tm == ATT_LEFT and s % tm == 0
    nt = s // tm
    cur = lambda bi, j: (bi, jnp.maximum(j - 1, 0), 0)
    const = lambda bi, j: (0, 0)
    out_sds = lambda rows: jax.ShapeDtypeStruct((b, rows, MIX_WIDTH), BF16)
    return pl.pallas_call(
        _in_even_body,
        grid=(b, nt + 1),
        in_specs=[
            pl.BlockSpec((1, tm, d), cur),
            pl.BlockSpec((1, d), const),
            pl.BlockSpec(w.shape, const),
            pl.BlockSpec((1, MIX_WIDTH), const),
            pl.BlockSpec((1, MIX_WIDTH), const),
        ],
        out_specs=[
            pl.BlockSpec((1, tm, MIX_WIDTH), cur),
            pl.BlockSpec((1, tm, MIX_WIDTH), lambda bi, j: (bi, j, 0)),
            pl.BlockSpec((1, tm, MIX_WIDTH), lambda bi, j: (bi, j, 0)),
            pl.BlockSpec((1, tm, MIX_WIDTH), cur),
        ],
        out_shape=[out_sds(s), out_sds(s + ATT_LEFT), out_sds(s + ATT_LEFT), out_sds(s)],
        compiler_params=_cparams(("arbitrary", "arbitrary")),
        name="in_even",
    )(h, gain, w, q_gain, k_gain)


BAND_TQ = 2 * CHUNK
BAND_TK = BAND_TQ + ATT_LEFT


BAND_STEP_TILES = 4
BAND_STAGED = 2


def _band_body(q_ref, k_ref, v_ref, bias_ref, o_ref):
    @pl.loop(0, BAND_STEP_TILES)
    def _(t):
        rows = pl.ds(pl.multiple_of(t * BAND_TQ, BAND_TQ), BAND_TQ)
        _band_tile(pl.program_id(1) * BAND_STEP_TILES + t, q_ref.at[0, rows, :], k_ref, v_ref, bias_ref,
                   o_ref.at[0, rows, :])


def _band_tile(i, q_ref, k_ref, v_ref, bias_ref, o_ref):
    start = pl.multiple_of(i * BAND_TQ, BAND_TQ)
    lane = lax.broadcasted_iota(jnp.int32, (BAND_TQ, LANES), 1)
    col = lax.broadcasted_iota(jnp.int32, (2 * BAND_TQ, BAND_TK), 1)
    is_pad = (col + start) < ATT_LEFT
    n_pairs = MIX_WIDTH // LANES
    lanes = [slice(hp * LANES, (hp + 1) * LANES) for hp in range(n_pairs)]
    for first in range(0, n_pairs, BAND_STAGED):
        pairs = range(first, first + BAND_STAGED)
        scores, probs, denoms = {}, {}, {}
        for hp in pairs:
            q = q_ref[:, lanes[hp]]
            kb = k_ref[0, pl.ds(start, BAND_TK), lanes[hp]]
            zero = jnp.zeros_like(q)
            q2 = jnp.concatenate([jnp.where(lane < HEAD_DIM, q, zero), jnp.where(lane < HEAD_DIM, zero, q)],
                                 axis=0)
            scores[hp] = lax.dot_general(q2, kb, NT_DIMS, preferred_element_type=F32)
        for hp in pairs:
            bias = bias_ref[2 * hp:2 * hp + 2].reshape(2 * BAND_TQ, BAND_TK)
            s = jnp.where(is_pad, NEG_BIG, scores[hp] + bias)
            p = jnp.exp2(s - jnp.max(s, axis=-1, keepdims=True))
            denoms[hp] = jnp.sum(p, axis=-1, keepdims=True)
            probs[hp] = p.astype(BF16)
        for hp in pairs:
            vb = v_ref[0, pl.ds(start, BAND_TK), lanes[hp]]
            o = jnp.dot(probs[hp], vb, preferred_element_type=F32) / denoms[hp]
            o_ref[:, lanes[hp]] = jnp.where(lane < HEAD_DIM, o[:BAND_TQ], o[BAND_TQ:]).astype(BF16)


def _band_bias(rel_bias):
    heads = rel_bias.shape[0]
    r = jnp.arange(BAND_TQ)[:, None]
    j = jnp.arange(BAND_TK)[None, :]
    jb = j - CHUNK * (r // CHUNK)
    in_band = (jb >= 0) & (jb < ATT_LEFT + CHUNK)
    period = BAND_TK + BAND_TQ
    far = jnp.broadcast_to(rel_bias[:, 2 * ATT_MAX_REL:], (heads, ATT_LEFT - ATT_MAX_REL + 1))
    near = rel_bias[:, 2 * ATT_MAX_REL - 1:0:-1]
    wrap = jnp.broadcast_to(rel_bias[:, 2 * ATT_MAX_REL:], (heads, period - BAND_TK))
    g = jnp.concatenate([far, near, wrap], axis=1).astype(F32)
    assert g.shape[1] == period
    toep = jnp.tile(g, (1, BAND_TQ))[:, :BAND_TQ * (period - 1)].reshape(heads, BAND_TQ, period - 1)
    return jnp.where(in_band[None], toep[:, :, :BAND_TK] * LOG2E, NEG_BIG)


def _band_attention(q, k_pad, v_pad, bias):
    b, s, w = q.shape
    sp = k_pad.shape[1]
    step_rows = BAND_STEP_TILES * BAND_TQ
    return pl.pallas_call(
        _band_body,
        grid=(b, s // step_rows),
        in_specs=[
            pl.BlockSpec((1, step_rows, w), lambda bi, i: (bi, i, 0)),
            pl.BlockSpec((1, sp, w), lambda bi, i: (bi, 0, 0)),
            pl.BlockSpec((1, sp, w), lambda bi, i: (bi, 0, 0)),
            pl.BlockSpec(bias.shape, lambda bi, i: (0, 0, 0)),
        ],
        out_specs=pl.BlockSpec((1, step_rows, w), lambda bi, i: (bi, i, 0)),
        out_shape=jax.ShapeDtypeStruct((b, s, w), BF16),
        compiler_params=_cparams(("arbitrary", "arbitrary")),
        name="band_attention",
    )(q, k_pad, v_pad, bias)


def _route_tokens(h, gain, w_router):
    xn = _rms(h, gain)
    x_hi = xn.astype(BF16)
    x_lo = (xn - x_hi.astype(F32)).astype(BF16)
    w_hi = w_router.astype(BF16)
    w_lo = (w_router - w_hi.astype(F32)).astype(BF16)
    logits = (jnp.dot(x_hi, w_hi, preferred_element_type=F32)
              + jnp.dot(x_lo, w_hi, preferred_element_type=F32)
              + jnp.dot(x_hi, w_lo, preferred_element_type=F32))
    lt = logits.T[:ROUTER_ROWS]
    sub = lax.broadcasted_iota(jnp.int32, lt.shape, 0).astype(F32)
    ninf = -jnp.inf

    def top(vals):
        m = jnp.max(vals, axis=0, keepdims=True)
        idx = jnp.min(jnp.where(vals == m, sub, float(ROUTER_LANES)), axis=0, keepdims=True)
        return m, idx

    is_group = sub < N_GROUPS
    g_max, g_sel = top(jnp.where(is_group, lt, ninf))
    g_den = jnp.sum(jnp.where(is_group, jnp.exp(lt - g_max), 0.0), axis=0, keepdims=True)
    g_weight = 1.0 / g_den
    lo = N_GROUPS + N_EXP_PER_GROUP * g_sel
    e_logits = jnp.where((sub >= lo) & (sub < lo + N_EXP_PER_GROUP), lt, ninf)
    e1, i1 = top(e_logits)
    e2, i2 = top(jnp.where(sub == i1, ninf, e_logits))
    t = jnp.exp(e2 - e1)
    w1 = g_weight / (1.0 + t)
    w2 = g_weight * t / (1.0 + t)
    rows = lax.broadcasted_iota(jnp.int32, (ROUTER_LANES, lt.shape[1]), 0)
    route_t = jnp.where(rows == 0, i1 - N_GROUPS, jnp.where(rows == 1, i2 - N_GROUPS, 0.0))
    route_t = jnp.where(rows == 2, w1, jnp.where(rows == 3, w2, route_t))
    return xn, route_t.T


def _router_weights(w_rg, w_re):
    pad = jnp.zeros((w_rg.shape[0], ROUTER_LANES - N_GROUPS - N_EXPERTS), F32)
    return jnp.concatenate([w_rg, w_re, pad], axis=1)


def _out_even_body(tiles_per_seq, ya_ref, p_ref, halo_ref, h_ref, pw_ref, ps_ref, wo_ref, fg_ref, wr_ref,
                   o_ref, xn_ref, route_ref, p_scr, yb_scr):
    tm = p_ref.shape[0]
    it = pl.program_id(0) % tiles_per_seq
    halo = halo_ref[...].astype(F32)
    p_scr[0:POOL_HALO, :] = jnp.where(it == 0, jnp.zeros_like(halo), halo)
    p_scr[POOL_HALO:, :] = p_ref[...].astype(F32)
    t = it * tm + lax.broadcasted_iota(jnp.int32, (tm, 1), 0)
    for g, win in enumerate(POOL_WINDOWS):
        ls = slice(g * LANES, (g + 1) * LANES)
        cur = p_scr[POOL_HALO:POOL_HALO + tm, ls]
        acc = cur
        for dlt in range(1, win):
            acc = acc + p_scr[POOL_HALO - dlt:POOL_HALO - dlt + tm, ls]
        cnt = jnp.minimum(t + 1, win).astype(F32)
        mixed = acc / cnt - cur
        yb = jnp.dot(mixed.astype(BF16), pw_ref[g], preferred_element_type=F32) * ps_ref[:, ls]
        yb_scr[:, ls] = yb.astype(BF16)
    mixed = jnp.concatenate([ya_ref[...], yb_scr[...]], axis=1)
    _project_and_route(h_ref, mixed, wo_ref, fg_ref, wr_ref, o_ref, xn_ref, route_ref)


def _out_even(ya, p, h, pool_w, pool_scale, w_out, ffn_gain, w_router, seq, tm):
    n, d = h.shape
    w = MIX_WIDTH
    row = lambda i: (i, 0)
    const2 = lambda i: (0, 0)
    halo_blocks = tm // POOL_HALO
    return pl.pallas_call(
        functools.partial(_out_even_body, seq // tm),
        grid=(n // tm,),
        in_specs=[
            pl.BlockSpec((tm, w), row),
            pl.BlockSpec((tm, w), row),
            pl.BlockSpec((POOL_HALO, w), lambda i: (jnp.maximum(i * halo_blocks - 1, 0), 0)),
            pl.BlockSpec((tm, d), row),
            pl.BlockSpec(pool_w.shape, lambda i: (0, 0, 0)),
            pl.BlockSpec((1, w), const2),
            pl.BlockSpec(w_out.shape, const2),
            pl.BlockSpec((1, d), const2),
            pl.BlockSpec(w_router.shape, const2),
        ],
        out_specs=[pl.BlockSpec((tm, d), row), pl.BlockSpec((tm * SUBLANES, LANES), row),
                   pl.BlockSpec((tm, ROUTER_LANES), row)],
        out_shape=[jax.ShapeDtypeStruct((n, d), F32), jax.ShapeDtypeStruct((n * SUBLANES, LANES), F32),
                   jax.ShapeDtypeStruct((n, ROUTER_LANES), F32)],
        scratch_shapes=[pltpu.VMEM((tm + POOL_HALO, w), F32), pltpu.VMEM((tm, w), BF16)],
        compiler_params=_cparams(("arbitrary",)),
        name="out_even",
    )(ya, p, p, h, pool_w, pool_scale, w_out, ffn_gain, w_router)


MOE_TM = 256
MOE_TILE_LANES = 256


def _moe_tiles(n):
    return (2 * n) // MOE_TM + N_EXPERTS


def _exact_dot_nt(ones, x):
    out = None
    for _ in range(3):
        part = x.astype(BF16)
        x = x - part.astype(F32)
        term = lax.dot_general(ones, part, NT_DIMS, preferred_element_type=F32)
        out = term if out is None else out + term
    return out


def _slots_body(n_tiles, route_ref, pos1_ref, pos2_ref, tile_ref, run_scr, start_scr):
    phase = pl.program_id(0)
    i = pl.program_id(1)
    tm = route_ref.shape[0]
    route = route_ref[...]
    lane = lax.broadcasted_iota(jnp.int32, (tm, ROUTER_LANES), 1).astype(F32)
    pick1 = jnp.where(lane == route[:, 0:1], 1.0, 0.0)
    pick2 = jnp.where(lane == route[:, 1:2], 1.0, 0.0)
    occ = (pick1 + pick2).astype(BF16)
    ones_rows = jnp.ones((SUBLANES, tm), BF16)
    ones_lanes = jnp.ones((SUBLANES, ROUTER_LANES), BF16)

    @pl.when(i == 0)
    def _():
        run_scr[...] = jnp.zeros_like(run_scr)

    @pl.when(phase == 0)
    def _():
        run_scr[...] += jnp.dot(ones_rows, occ, preferred_element_type=F32)

        @pl.when(i == pl.num_programs(1) - 1)
        def _():
            padded = jnp.floor((run_scr[...] + (MOE_TM - 1)) * (1.0 / MOE_TM)) * MOE_TM
            r = lax.broadcasted_iota(jnp.int32, (ROUTER_LANES, ROUTER_LANES), 0)
            c = lax.broadcasted_iota(jnp.int32, (ROUTER_LANES, ROUTER_LANES), 1)
            before = jnp.where(r < c, 1.0, 0.0).astype(BF16)
            hi = padded.astype(BF16)
            mid = (padded - hi.astype(F32)).astype(BF16)
            low = (padded - hi.astype(F32) - mid.astype(F32)).astype(BF16)
            start = (jnp.dot(hi, before, preferred_element_type=F32)
                     + jnp.dot(mid, before, preferred_element_type=F32)
                     + jnp.dot(low, before, preferred_element_type=F32))
            start_scr[...] = start
            seg_end = start[0:1, :] + padded[0:1, :]
            tile_lo = (lax.broadcasted_iota(jnp.int32, (MOE_TILE_LANES, ROUTER_LANES), 0) * MOE_TM).astype(F32)
            e_lane = lax.broadcasted_iota(jnp.int32, (MOE_TILE_LANES, ROUTER_LANES), 1)
            ended = jnp.where((seg_end <= tile_lo) & (e_lane < N_EXPERTS), 1.0, 0.0).astype(BF16)
            tile_ref[...] = lax.dot_general(ones_lanes, ended, NT_DIMS, preferred_element_type=F32)

    @pl.when(phase == 1)
    def _():
        r = lax.broadcasted_iota(jnp.int32, (tm, tm), 0)
        c = lax.broadcasted_iota(jnp.int32, (tm, tm), 1)
        earlier = jnp.where(c < r, 1.0, 0.0).astype(BF16)
        base = (jnp.dot(earlier, occ, preferred_element_type=F32)
                + run_scr[0:1, :] + start_scr[0:1, :])
        pos1_ref[...] = _exact_dot_nt(ones_lanes, pick1 * base)
        pos2_ref[...] = _exact_dot_nt(ones_lanes, pick2 * base)
        run_scr[...] += jnp.dot(ones_rows, occ, preferred_element_type=F32)


def _slots(route, tm):
    n = route.shape[0]
    n_tiles = _moe_tiles(n)
    assert n_tiles <= MOE_TILE_LANES and 2 * n + N_EXPERTS * MOE_TM < 2 ** 24
    row_out = pl.BlockSpec((SUBLANES, tm), lambda ph, i: (0, i * ph))
    sds = jax.ShapeDtypeStruct((SUBLANES, n), F32)
    return pl.pallas_call(
        functools.partial(_slots_body, n_tiles),
        grid=(2, n // tm),
        in_specs=[pl.BlockSpec((tm, ROUTER_LANES), lambda ph, i: (i, 0))],
        out_specs=[row_out, row_out, pl.BlockSpec((SUBLANES, MOE_TILE_LANES), lambda ph, i: (0, 0))],
        out_shape=[sds, sds, jax.ShapeDtypeStruct((SUBLANES, MOE_TILE_LANES), F32)],
        scratch_shapes=[pltpu.VMEM((SUBLANES, ROUTER_LANES), F32), pltpu.VMEM((SUBLANES, ROUTER_LANES), F32)],
        compiler_params=_cparams(("arbitrary", "arbitrary")),
        name="moe_slots",
    )(route)


def _dispatch_body(n_tiles, pos_ref, tile_ref, xn_ref, xs_hbm, zero_scr, zero_sem, row_sem):
    tm = xn_ref.shape[0] // SUBLANES
    n = pl.num_programs(0) * tm
    base = pl.program_id(0) * tm
    tile_rows = MOE_TM * SUBLANES

    @pl.when(pl.program_id(0) == 0)
    def _():
        zero_scr[...] = jnp.zeros_like(zero_scr)

        def fill_copy(t):
            return pltpu.make_async_copy(zero_scr, xs_hbm.at[pl.ds(t * tile_rows, tile_rows), :], zero_sem)

        def has_padding(t):
            return (tile_ref[t] >= N_EXPERTS) | (tile_ref[t] != tile_ref[t + 1])

        @pl.loop(0, n_tiles)
        def _(t):
            @pl.when(has_padding(t))
            def _():
                fill_copy(t).start()

        @pl.loop(0, n_tiles)
        def _(t):
            @pl.when(has_padding(t))
            def _():
                fill_copy(t).wait()

    def issue(j, carry):
        src = xn_ref.at[pl.ds(pl.multiple_of(j * SUBLANES, SUBLANES), SUBLANES), :]
        for pick in range(2):
            dst = pl.multiple_of(pos_ref[pick * n + base + j], SUBLANES)
            pltpu.make_async_copy(src, xs_hbm.at[pl.ds(dst, SUBLANES), :], row_sem).start(priority=pick)
        return carry

    lax.fori_loop(0, tm, issue, 0, unroll=8)
    for _ in range(2):
        pltpu.make_async_copy(xn_ref, xs_hbm.at[pl.ds(0, tm * SUBLANES), :], row_sem).wait()


def _dispatch(pos, tile_map, xn, tm):
    n = xn.shape[0] // SUBLANES
    n_tiles = _moe_tiles(n)
    return pl.pallas_call(
        functools.partial(_dispatch_body, n_tiles),
        grid_spec=pltpu.PrefetchScalarGridSpec(
            num_scalar_prefetch=2,
            grid=(n // tm,),
            in_specs=[pl.BlockSpec((tm * SUBLANES, LANES), lambda i, p, t: (i, 0))],
            out_specs=pl.BlockSpec(memory_space=pl.ANY),
            scratch_shapes=[pltpu.VMEM((MOE_TM * SUBLANES, LANES), F32), pltpu.SemaphoreType.DMA(()),
                            pltpu.SemaphoreType.DMA(())],
        ),
        out_shape=jax.ShapeDtypeStruct((n_tiles * MOE_TM * SUBLANES, LANES), F32),
        compiler_params=_cparams(("arbitrary",)),
        name="moe_dispatch",
    )(pos, tile_map, xn)


EXPERT_SLOTS = 3


def _experts_body(tile_ref, xs_hbm, wg_ref, wu_ref, wd_ref, ys_ref, xbuf, sems):
    i = pl.program_id(0)
    tile_rows = MOE_TM * SUBLANES
    used = tile_ref[i] < N_EXPERTS

    def tile_copy(t, slot):
        rows = pl.ds(pl.multiple_of(t * tile_rows, tile_rows), tile_rows)
        return pltpu.make_async_copy(xs_hbm.at[rows, :], xbuf.at[slot], sems.at[slot])

    def prefetch(t):
        @pl.when(tile_ref[t] < N_EXPERTS)
        def _():
            tile_copy(t, t % EXPERT_SLOTS).start()

    @pl.when(i == 0)
    def _():
        prefetch(0)
        prefetch(1)

    prefetch(i + 2)

    @pl.when(used)
    def _():
        slot = i % EXPERT_SLOTS
        tile_copy(i, slot).wait()
        x = _load_token_major(xbuf, MOE_TM, (slot,)).astype(BF16)
        gate = jnp.dot(x, wg_ref[0].astype(BF16), preferred_element_type=F32)
        up = jnp.dot(x, wu_ref[0].astype(BF16), preferred_element_type=F32)
        hid = gate * jax.nn.sigmoid(gate) * up
        _store_token_major(ys_ref, jnp.dot(hid.astype(BF16), wd_ref[0].astype(BF16),
                                           preferred_element_type=F32))

    @pl.when(jnp.logical_not(used))
    def _():
        ys_ref[...] = jnp.zeros_like(ys_ref)


def _experts(tile_map, xs, w_gate, w_up, w_down, layer):
    d, f = w_gate.shape[1:]
    tile_rows = MOE_TM * SUBLANES
    n_tiles = xs.shape[0] // tile_rows
    y_map = lambda i, tm_ref: (i, 0)
    w_map = lambda i, tm_ref: (layer * N_EXPERTS + jnp.minimum(tm_ref[i], N_EXPERTS - 1), 0, 0)
    return pl.pallas_call(
        _experts_body,
        grid_spec=pltpu.PrefetchScalarGridSpec(
            num_scalar_prefetch=1,
            grid=(n_tiles,),
            in_specs=[pl.BlockSpec(memory_space=pl.ANY), pl.BlockSpec((1, d, f), w_map),
                      pl.BlockSpec((1, d, f), w_map), pl.BlockSpec((1, f, d), w_map)],
            out_specs=pl.BlockSpec((tile_rows, LANES), y_map),
            scratch_shapes=[pltpu.VMEM((EXPERT_SLOTS, tile_rows, LANES), F32),
                            pltpu.SemaphoreType.DMA((EXPERT_SLOTS,))],
        ),
        out_shape=jax.ShapeDtypeStruct(xs.shape, F32),
        compiler_params=_cparams(("arbitrary",)),
        name="moe_experts",
    )(tile_map, xs, w_gate, w_up, w_down)


def _combine_body(pos_ref, h_ref, route_ref, ys_hbm, o_ref, buf, sems):
    tm = h_ref.shape[0]
    steps = pl.num_programs(0)
    n = steps * tm
    i = pl.program_id(0)

    def start_gather(step, slot):
        base = step * tm

        def issue(g, carry):
            for u in range(SUBLANES):
                for pick in range(2):
                    src = pl.multiple_of(pos_ref[pick * n + base + g * SUBLANES + u], SUBLANES)
                    dst = pl.multiple_of(g * SUBLANES * SUBLANES, SUBLANES) + u * SUBLANES
                    pltpu.make_async_copy(ys_hbm.at[pl.ds(src, SUBLANES), :],
                                          buf.at[slot, pick, pl.ds(dst, SUBLANES), :],
                                          sems.at[slot]).start(priority=pick)
            return carry

        lax.fori_loop(0, tm // SUBLANES, issue, 0)

    @pl.when(i == 0)
    def _():
        start_gather(0, 0)

    @pl.when(i + 1 < steps)
    def _():
        start_gather(i + 1, (i + 1) % 2)

    slot = i % 2
    pltpu.make_async_copy(buf.at[slot], buf.at[slot], sems.at[slot]).wait()
    route = route_ref[...]
    y1 = _load_token_major(buf, tm, (slot, 0))
    y2 = _load_token_major(buf, tm, (slot, 1))
    o_ref[...] = h_ref[...] + route[:, 2:3] * y1 + route[:, 3:4] * y2


def _combine(pos, h, route, ys, tm):
    n, d = h.shape
    row = lambda i, p: (i, 0)
    return pl.pallas_call(
        _combine_body,
        grid_spec=pltpu.PrefetchScalarGridSpec(
            num_scalar_prefetch=1,
            grid=(n // tm,),
            in_specs=[pl.BlockSpec((tm, d), row), pl.BlockSpec((tm, ROUTER_LANES), row),
                      pl.BlockSpec(memory_space=pl.ANY)],
            out_specs=pl.BlockSpec((tm, d), row),
            scratch_shapes=[pltpu.VMEM((2, 2, tm * SUBLANES, LANES), F32), pltpu.SemaphoreType.DMA((2,))],
        ),
        out_shape=jax.ShapeDtypeStruct((n, d), F32),
        compiler_params=_cparams(("arbitrary",)),
        name="moe_combine",
    )(pos, h, route, ys)


def _gelu(x):
    return 0.5 * x * (1.0 + lax.erf(x * (1.0 / math.sqrt(2.0))))


def _in_odd_body(h_ref, g_ref, w_ref, vg_ref, u_ref, vn_ref, q_ref, k_ref, v_ref):
    xn = _rms(h_ref[...], g_ref[...])
    proj = jnp.dot(xn.astype(BF16), w_ref[...], preferred_element_type=F32)
    w = MIX_WIDTH
    u_ref[...] = _gelu(proj[:, :w]).astype(BF16)
    vn_ref[...] = _rms(_gelu(proj[:, w:2 * w]), vg_ref[...]).astype(BF16)
    q_ref[...] = (proj[:, 2 * w:3 * w] * Q_SCALE).astype(BF16)
    k_ref[...] = proj[:, 3 * w:4 * w].astype(BF16)
    v_ref[...] = proj[:, 4 * w:].astype(BF16)


def _in_odd(h, gain, w, v_gain, tm):
    n, d = h.shape
    row = lambda i: (i, 0)
    const2 = lambda i: (0, 0)
    sds = jax.ShapeDtypeStruct((n, MIX_WIDTH), BF16)
    return pl.pallas_call(
        _in_odd_body,
        grid=(n // tm,),
        in_specs=[pl.BlockSpec((tm, d), row), pl.BlockSpec((1, d), const2),
                  pl.BlockSpec(w.shape, const2), pl.BlockSpec((1, MIX_WIDTH), const2)],
        out_specs=[pl.BlockSpec((tm, MIX_WIDTH), row)] * 5,
        out_shape=[sds] * 5,
        compiler_params=_cparams(("arbitrary",)),
        name="in_odd",
    )(h, gain, w, v_gain)


SB_TQ = 64
SB_KB = 128
SB_NB = 2
SB_TK = SB_NB * SB_KB
SB_QBLK = 1024
SB_GROUP = 8
SB_UNDERFLOW = -150.0


def _sb_body(q_ref, k_ref, v_ref, o_ref, acc_scr, run_scr):
    qi = pl.program_id(2)
    lane = lax.broadcasted_iota(jnp.int32, (SB_TQ, LANES), 1)
    row = lax.broadcasted_iota(jnp.int32, (2 * SB_TQ, 1), 0) % SB_TQ
    col = lax.broadcasted_iota(jnp.int32, (2 * SB_TQ, SB_TK), 1)
    rr = lax.broadcasted_iota(jnp.int32, (2 * SB_KB, 2 * SB_KB), 0) % SB_KB
    cc = lax.broadcasted_iota(jnp.int32, (2 * SB_KB, 2 * SB_KB), 1)
    suffix = jnp.where((cc >= SB_KB) | (rr > cc), 1.0, 0.0).astype(BF16)

    def suffix_sums(x):
        hi = x.astype(BF16)
        lo = (x - hi.astype(F32)).astype(BF16)
        return jnp.dot(jnp.concatenate([hi, lo], axis=1), suffix, preferred_element_type=F32)

    def subtile_group(grp, _):
        q_los = [pl.multiple_of((grp * SB_GROUP + s) * SB_TQ, SB_TQ) for s in range(SB_GROUP)]
        q_starts = [qi * SB_QBLK + q_lo for q_lo in q_los]
        q_heads = []
        for q_lo in q_los:
            q = q_ref[0, pl.ds(q_lo, SB_TQ), :]
            zero = jnp.zeros_like(q)
            q_heads.append(jnp.concatenate(
                [jnp.where(lane < HEAD_DIM, q, zero), jnp.where(lane < HEAD_DIM, zero, q)], axis=0))
        acc_scr[...] = jnp.zeros_like(acc_scr)
        run_scr[...] = jnp.zeros_like(run_scr)

        def cond(carry):
            his, dones = carry
            active = [(hi > 0) & (done == 0) for hi, done in zip(his, dones)]
            return functools.reduce(jnp.logical_or, active)

        def body(carry):
            his, _ = carry
            group = range(SB_GROUP)
            kss = [pl.multiple_of(jnp.maximum(his[s] - SB_TK, 0), SB_TQ) for s in group]
            valid = [col < (jnp.minimum(row + q_starts[s], his[s]) - kss[s]) for s in group]
            zs = [lax.dot_general(q_heads[s], k_ref[0, pl.ds(kss[s], SB_TK), :], NT_DIMS,
                                  preferred_element_type=F32) for s in group]
            log_beta, log_rest = [], []
            for s in group:
                z = jnp.where(valid[s], zs[s], NEG_BIG)
                sp = jnp.log2(1.0 + jnp.exp2(-jnp.abs(z)))
                log_beta.append(jnp.minimum(z, 0.0) - sp)
                log_rest.append(log_beta[s] - z)
            sums = [[suffix_sums(log_rest[s][:, blk * SB_KB:(blk + 1) * SB_KB]) for blk in range(SB_NB)]
                    for s in group]
            dones = []
            for s in group:
                run = run_scr[s]
                pieces = [None] * SB_NB
                for blk in reversed(range(SB_NB)):
                    tt = sums[s][blk]
                    pieces[blk] = jnp.exp2(log_beta[s][:, blk * SB_KB:(blk + 1) * SB_KB] + tt[:, :SB_KB] + run)
                    run = run + tt[:, SB_KB:]
                att = jnp.concatenate(pieces, axis=1)
                acc_scr[s] += jnp.dot(att.astype(BF16), v_ref[0, pl.ds(kss[s], SB_TK), :],
                                      preferred_element_type=F32)
                run_scr[s] = run
                dones.append((jnp.max(run) <= SB_UNDERFLOW).astype(jnp.int32))
            return tuple(kss), tuple(dones)

        lax.while_loop(cond, body, (tuple(qs + SB_TQ for qs in q_starts),
                                    tuple(jnp.int32(0) for _ in range(SB_GROUP))))
        for s in range(SB_GROUP):
            o_ref[0, pl.ds(q_los[s], SB_TQ), :] = jnp.where(
                lane < HEAD_DIM, acc_scr[s, :SB_TQ], acc_scr[s, SB_TQ:]).astype(BF16)
        return 0

    lax.fori_loop(0, SB_QBLK // (SB_TQ * SB_GROUP), subtile_group, 0)


def _sb_attention(q, k, v):
    b, s, w = q.shape
    assert s % SB_QBLK == 0 and s >= SB_TK
    qspec = pl.BlockSpec((1, SB_QBLK, LANES), lambda bi, hp, i: (bi, i, hp))
    kvspec = pl.BlockSpec((1, s, LANES), lambda bi, hp, i: (bi, 0, hp))
    return pl.pallas_call(
        _sb_body,
        grid=(b, w // LANES, s // SB_QBLK),
        in_specs=[qspec, kvspec, kvspec],
        out_specs=qspec,
        out_shape=jax.ShapeDtypeStruct((b, s, w), BF16),
        scratch_shapes=[pltpu.VMEM((SB_GROUP, 2 * SB_TQ, LANES), F32),
                        pltpu.VMEM((SB_GROUP, 2 * SB_TQ, LANES), F32)],
        compiler_params=_cparams(("arbitrary", "arbitrary", "arbitrary")),
        name="sb_attention",
    )(q, k, v)


def _out_odd_body(u_ref, vn_ref, yd_ref, h_ref, ws_ref, bs_ref, wo_ref, fg_ref, wr_ref,
                  o_ref, xn_ref, route_ref, yc_scr):
    tm = u_ref.shape[0]
    r = lax.broadcasted_iota(jnp.int32, (SGU_BLOCK, SGU_BLOCK), 0)
    c = lax.broadcasted_iota(jnp.int32, (SGU_BLOCK, SGU_BLOCK), 1)
    for g in range(MIX_WIDTH // LANES):
        ls = slice(g * LANES, (g + 1) * LANES)
        ws = jnp.where(c <= r, ws_ref[g], jnp.zeros_like(ws_ref[g]))
        for blk in range(tm // SGU_BLOCK):
            rs = slice(blk * SGU_BLOCK, (blk + 1) * SGU_BLOCK)
            mixed = jnp.dot(ws, vn_ref[rs, ls], preferred_element_type=F32) + bs_ref[g]
            yc_scr[rs, ls] = (u_ref[rs, ls].astype(F32) * mixed).astype(BF16)
    mixed = jnp.concatenate([yc_scr[...], yd_ref[...]], axis=1)
    _project_and_route(h_ref, mixed, wo_ref, fg_ref, wr_ref, o_ref, xn_ref, route_ref)


def _out_odd(u, vn, yd, h, sgu_w, sgu_b, w_out, ffn_gain, w_router, tm):
    n, d = h.shape
    w = MIX_WIDTH
    row = lambda i: (i, 0)
    const2 = lambda i: (0, 0)
    const3 = lambda i: (0, 0, 0)
    return pl.pallas_call(
        _out_odd_body,
        grid=(n // tm,),
        in_specs=[pl.BlockSpec((tm, w), row), pl.BlockSpec((tm, w), row), pl.BlockSpec((tm, w), row),
                  pl.BlockSpec((tm, d), row), pl.BlockSpec(sgu_w.shape, const3),
                  pl.BlockSpec(sgu_b.shape, const3), pl.BlockSpec(w_out.shape, const2),
                  pl.BlockSpec((1, d), const2), pl.BlockSpec(w_router.shape, const2)],
        out_specs=[pl.BlockSpec((tm, d), row), pl.BlockSpec((tm * SUBLANES, LANES), row),
                   pl.BlockSpec((tm, ROUTER_LANES), row)],
        out_shape=[jax.ShapeDtypeStruct((n, d), F32), jax.ShapeDtypeStruct((n * SUBLANES, LANES), F32),
                   jax.ShapeDtypeStruct((n, ROUTER_LANES), F32)],
        scratch_shapes=[pltpu.VMEM((tm, w), BF16)],
        compiler_params=_cparams(("arbitrary",)),
        name="out_odd",
    )(u, vn, yd, h, sgu_w, sgu_b, w_out, ffn_gain, w_router)


def _moe_layer(h, xn, route, w_gate, w_up, w_down, layer):
    pos1, pos2, tile_map = _slots(route, tm=1024)
    pos = jnp.concatenate([pos1[0], pos2[0]]).astype(jnp.int32) * SUBLANES
    tile_map = tile_map[0].astype(jnp.int32)
    xs = _dispatch(pos, tile_map, xn, tm=512)
    ys = _experts(tile_map, xs, w_gate, w_up, w_down, layer)
    return _combine(pos, h, route, ys, tm=256)


def kernel(x, mix_norm_even, w_in_even, att_q_norm, att_k_norm, att_rel_bias, pool_w, pool_scale,
           w_out_even, mix_norm_odd, w_in_odd, sgu_v_norm, sgu_w, sgu_b, w_out_odd, ffn_norm,
           w_router_group, w_router_expert, w_exp_gate, w_exp_up, w_exp_down):
    b, s, d = x.shape
    n = b * s
    depth = ffn_norm.shape[0]
    heads = MIX_WIDTH // HEAD_DIM
    h = x.reshape(n, d)
    w_gate = w_exp_gate.reshape(depth * N_EXPERTS, d, EXPERT_FF)
    w_up = w_exp_up.reshape(depth * N_EXPERTS, d, EXPERT_FF)
    w_down = w_exp_down.reshape(depth * N_EXPERTS, EXPERT_FF, d)
    for layer in range(depth):
        i = layer // 2
        ffn_gain = ffn_norm[layer][None, :]
        w_router = _router_weights(w_router_group[layer], w_router_expert[layer])
        if layer % 2 == 0:
            q, k_pad, v_pad, p = _in_even(
                h.reshape(b, s, d), mix_norm_even[i][None, :], w_in_even[i].astype(BF16),
                jnp.tile(att_q_norm[i], heads)[None, :], jnp.tile(att_k_norm[i], heads)[None, :],
                tm=ATT_LEFT)
            ya = _band_attention(q, k_pad, v_pad, _band_bias(att_rel_bias[i]))
            h, xn, route = _out_even(ya.reshape(n, MIX_WIDTH), p.reshape(n, MIX_WIDTH), h,
                                     pool_w[i].astype(BF16), pool_scale[i][None, :],
                                     w_out_even[i].astype(BF16), ffn_gain, w_router, seq=s, tm=512)
        else:
            u, vn, q, k, v = _in_odd(h, mix_norm_odd[i][None, :], w_in_odd[i].astype(BF16),
                                     sgu_v_norm[i][None, :], tm=512)
            to3 = lambda t: t.reshape(b, s, MIX_WIDTH)
            yd = _sb_attention(to3(q), to3(k), to3(v))
            bias = jnp.broadcast_to(sgu_b[i][:, :, None], (N_GROUPS, SGU_BLOCK, LANES))
            h, xn, route = _out_odd(u, vn, yd.reshape(n, MIX_WIDTH), h, sgu_w[i].astype(BF16), bias,
                                    w_out_odd[i].astype(BF16), ffn_gain, w_router, tm=512)
        h = _moe_layer(h, xn, route, w_gate, w_up, w_down, layer)
    return h.reshape(b, s, d)
```

```python
import functools
import math

import jax
import jax.numpy as jnp
from jax import lax
from jax.experimental import pallas as pl
from jax.experimental.pallas import tpu as pltpu

F32 = jnp.float32
BF16 = jnp.bfloat16

D_MODEL = 1024
CHUNK = 64
EPS = 1e-6
HEAD_DIM = 64
MIX_WIDTH = 512
LANES = 128
SUBLANES = 8
ATT_LEFT = 8 * CHUNK
ATT_MAX_REL = 128
POOL_WINDOWS = (2, 4, 8, 16)
POOL_HALO = 16
SGU_BLOCK = 128
N_GROUPS = 4
N_EXP_PER_GROUP = 8
N_EXPERTS = N_GROUPS * N_EXP_PER_GROUP
EXPERT_FF = 256
ROUTER_LANES = 128
ROUTER_ROWS = 40
NEG_BIG = -1e30
VMEM_LIMIT = 56 * 1024 * 1024

NT_DIMS = (((1,), (1,)), ((), ()))
LOG2E = math.log2(math.e)
Q_SCALE = LOG2E / math.sqrt(HEAD_DIM)


def _cparams(sem):
    return pltpu.CompilerParams(dimension_semantics=sem, vmem_limit_bytes=VMEM_LIMIT)


def _store_token_major(ref, x, first_token=0):
    rows = x.shape[0]
    for s in range(SUBLANES):
        ref[pl.ds(first_token * SUBLANES + s, rows, stride=SUBLANES), :] = x[:, s * LANES:(s + 1) * LANES]


def _row_halves(rows):
    return [slice(0, rows // 2), slice(rows // 2, rows)]


def _project_and_route(h_ref, mixed, wo_ref, fg_ref, wr_ref, o_ref, xn_ref, route_ref):
    halves = _row_halves(h_ref.shape[0])
    h_new = [h_ref[rs, :] + jnp.dot(mixed[rs], wo_ref[...], preferred_element_type=F32) for rs in halves]
    for rs, hn in zip(halves, h_new):
        o_ref[rs, :] = hn
    for rs, hn in zip(halves, h_new):
        xn, route_ref[rs, :] = _route_tokens(hn, fg_ref[...], wr_ref[...])
        _store_token_major(xn_ref, xn, first_token=rs.start)


def _load_token_major(ref, rows, lead=()):
    return jnp.concatenate(
        [ref[lead + (pl.ds(s, rows, stride=SUBLANES), slice(None))] for s in range(SUBLANES)], axis=1)


def _rms(x, gain):
    return x * lax.rsqrt(jnp.mean(x * x, axis=-1, keepdims=True) + EPS) * gain


def _split_dot(x, m):
    hi = x.astype(BF16)
    lo = (x - hi.astype(F32)).astype(BF16)
    return (jnp.dot(hi, m, preferred_element_type=F32)
            + jnp.dot(lo, m, preferred_element_type=F32))


def _head_rms(t, gain):
    n = t.shape[-1]
    r = lax.broadcasted_iota(jnp.int32, (n, n), 0) // HEAD_DIM
    c = lax.broadcasted_iota(jnp.int32, (n, n), 1) // HEAD_DIM
    bd = jnp.where(r == c, 1.0, 0.0).astype(BF16)
    ms = _split_dot(t * t, bd) * (1.0 / HEAD_DIM)
    return t * lax.rsqrt(ms + EPS) * gain


def _in_even_body(h_ref, g_ref, w_ref, qg_ref, kg_ref, q_ref, k_ref, v_ref, p_ref):
    j = pl.program_id(1)

    @pl.when(j == 0)
    def _():
        k_ref[...] = jnp.zeros_like(k_ref)
        v_ref[...] = jnp.zeros_like(v_ref)

    @pl.when(j > 0)
    def _():
        xn = _rms(h_ref[0], g_ref[...])
        proj = jnp.dot(xn.astype(BF16), w_ref[...], preferred_element_type=F32)
        w = MIX_WIDTH
        q_ref[0] = (_head_rms(proj[:, :w], qg_ref[...]) * Q_SCALE).astype(BF16)
        k_ref[0] = _head_rms(proj[:, w:2 * w], kg_ref[...]).astype(BF16)
        v_ref[0] = proj[:, 2 * w:3 * w].astype(BF16)
        p_ref[0] = proj[:, 3 * w:].astype(BF16)


def _in_even(h, gain, w, q_gain, k_gain, tm):
    b, s, d = h.shape
    assert tm == ATT_LEFT and s % tm == 0
    nt = s // tm
    cur = lambda bi, j: (bi, jnp.maximum(j - 1, 0), 0)
    const = lambda bi, j: (0, 0)
    out_sds = lambda rows: jax.ShapeDtypeStruct((b, rows, MIX_WIDTH), BF16)
    return pl.pallas_call(
        _in_even_body,
        grid=(b, nt + 1),
        in_specs=[
            pl.BlockSpec((1, tm, d), cur),
            pl.BlockSpec((1, d), const),
            pl.BlockSpec(w.shape, const),
            pl.BlockSpec((1, MIX_WIDTH), const),
            pl.BlockSpec((1, MIX_WIDTH), const),
        ],
        out_specs=[
            pl.BlockSpec((1, tm, MIX_WIDTH), cur),
            pl.BlockSpec((1, tm, MIX_WIDTH), lambda bi, j: (bi, j, 0)),
            pl.BlockSpec((1, tm, MIX_WIDTH), lambda bi, j: (bi, j, 0)),
            pl.BlockSpec((1, tm, MIX_WIDTH), cur),
        ],
        out_shape=[out_sds(s), out_sds(s + ATT_LEFT), out_sds(s + ATT_LEFT), out_sds(s)],
        compiler_params=_cparams(("arbitrary", "arbitrary")),
        name="in_even",
    )(h, gain, w, q_gain, k_gain)


BAND_TQ = 2 * CHUNK
BAND_TK = BAND_TQ + ATT_LEFT


BAND_STEP_TILES = 4
BAND_STAGED = 2


def _band_body(q_ref, k_ref, v_ref, bias_ref, o_ref):
    @pl.loop(0, BAND_STEP_TILES)
    def _(t):
        rows = pl.ds(pl.multiple_of(t * BAND_TQ, BAND_TQ), BAND_TQ)
        _band_tile(pl.program_id(1) * BAND_STEP_TILES + t, q_ref.at[0, rows, :], k_ref, v_ref, bias_ref,
                   o_ref.at[0, rows, :])


def _band_tile(i, q_ref, k_ref, v_ref, bias_ref, o_ref):
    start = pl.multiple_of(i * BAND_TQ, BAND_TQ)
    lane = lax.broadcasted_iota(jnp.int32, (BAND_TQ, LANES), 1)
    col = lax.broadcasted_iota(jnp.int32, (2 * BAND_TQ, BAND_TK), 1)
    is_pad = (col + start) < ATT_LEFT
    n_pairs = MIX_WIDTH // LANES
    lanes = [slice(hp * LANES, (hp + 1) * LANES) for hp in range(n_pairs)]
    for first in range(0, n_pairs, BAND_STAGED):
        pairs = range(first, first + BAND_STAGED)
        scores, probs, denoms = {}, {}, {}
        for hp in pairs:
            q = q_ref[:, lanes[hp]]
            kb = k_ref[0, pl.ds(start, BAND_TK), lanes[hp]]
            zero = jnp.zeros_like(q)
            q2 = jnp.concatenate([jnp.where(lane < HEAD_DIM, q, zero), jnp.where(lane < HEAD_DIM, zero, q)],
                                 axis=0)
            scores[hp] = lax.dot_general(q2, kb, NT_DIMS, preferred_element_type=F32)
        for hp in pairs:
            bias = bias_ref[2 * hp:2 * hp + 2].reshape(2 * BAND_TQ, BAND_TK)
            s = jnp.where(is_pad, NEG_BIG, scores[hp] + bias)
            p = jnp.exp2(s - jnp.max(s, axis=-1, keepdims=True))
            denoms[hp] = jnp.sum(p, axis=-1, keepdims=True)
            probs[hp] = p.astype(BF16)
        for hp in pairs:
            vb = v_ref[0, pl.ds(start, BAND_TK), lanes[hp]]
            o = jnp.dot(probs[hp], vb, preferred_element_type=F32) / denoms[hp]
            o_ref[:, lanes[hp]] = jnp.where(lane < HEAD_DIM, o[:BAND_TQ], o[BAND_TQ:]).astype(BF16)


def _band_bias(rel_bias):
    heads = rel_bias.shape[0]
    r = jnp.arange(BAND_TQ)[:, None]
    j = jnp.arange(BAND_TK)[None, :]
    jb = j - CHUNK * (r // CHUNK)
    in_band = (jb >= 0) & (jb < ATT_LEFT + CHUNK)
    period = BAND_TK + BAND_TQ
    far = jnp.broadcast_to(rel_bias[:, 2 * ATT_MAX_REL:], (heads, ATT_LEFT - ATT_MAX_REL + 1))
    near = rel_bias[:, 2 * ATT_MAX_REL - 1:0:-1]
    wrap = jnp.broadcast_to(rel_bias[:, 2 * ATT_MAX_REL:], (heads, period - BAND_TK))
    g = jnp.concatenate([far, near, wrap], axis=1).astype(F32)
    assert g.shape[1] == period
    toep = jnp.tile(g, (1, BAND_TQ))[:, :BAND_TQ * (period - 1)].reshape(heads, BAND_TQ, period - 1)
    return jnp.where(in_band[None], toep[:, :, :BAND_TK] * LOG2E, NEG_BIG)


def _band_attention(q, k_pad, v_pad, bias):
    b, s, w = q.shape
    sp = k_pad.shape[1]
    step_rows = BAND_STEP_TILES * BAND_TQ
    return pl.pallas_call(
        _band_body,
        grid=(b, s // step_rows),
        in_specs=[
            pl.BlockSpec((1, step_rows, w), lambda bi, i: (bi, i, 0)),
            pl.BlockSpec((1, sp, w), lambda bi, i: (bi, 0, 0)),
            pl.BlockSpec((1, sp, w), lambda bi, i: (bi, 0, 0)),
            pl.BlockSpec(bias.shape, lambda bi, i: (0, 0, 0)),
        ],
        out_specs=pl.BlockSpec((1, step_rows, w), lambda bi, i: (bi, i, 0)),
        out_shape=jax.ShapeDtypeStruct((b, s, w), BF16),
        compiler_params=_cparams(("arbitrary", "arbitrary")),
        name="band_attention",
    )(q, k_pad, v_pad, bias)


def _route_tokens(h, gain, w_router):
    xn = _rms(h, gain)
    x_hi = xn.astype(BF16)
    x_lo = (xn - x_hi.astype(F32)).astype(BF16)
    w_hi = w_router.astype(BF16)
    w_lo = (w_router - w_hi.astype(F32)).astype(BF16)
    logits = (jnp.dot(x_hi, w_hi, preferred_element_type=F32)
              + jnp.dot(x_lo, w_hi, preferred_element_type=F32)
              + jnp.dot(x_hi, w_lo, preferred_element_type=F32))
    lt = logits.T[:ROUTER_ROWS]
    sub = lax.broadcasted_iota(jnp.int32, lt.shape, 0).astype(F32)
    ninf = -jnp.inf

    def top(vals):
        m = jnp.max(vals, axis=0, keepdims=True)
        idx = jnp.min(jnp.where(vals == m, sub, float(ROUTER_LANES)), axis=0, keepdims=True)
        return m, idx

    is_group = sub < N_GROUPS
    g_max, g_sel = top(jnp.where(is_group, lt, ninf))
    g_den = jnp.sum(jnp.where(is_group, jnp.exp(lt - g_max), 0.0), axis=0, keepdims=True)
    g_weight = 1.0 / g_den
    lo = N_GROUPS + N_EXP_PER_GROUP * g_sel
    e_logits = jnp.where((sub >= lo) & (sub < lo + N_EXP_PER_GROUP), lt, ninf)
    e1, i1 = top(e_logits)
    e2, i2 = top(jnp.where(sub == i1, ninf, e_logits))
    t = jnp.exp(e2 - e1)
    w1 = g_weight / (1.0 + t)
    w2 = g_weight * t / (1.0 + t)
    rows = lax.broadcasted_iota(jnp.int32, (ROUTER_LANES, lt.shape[1]), 0)
    route_t = jnp.where(rows == 0, i1 - N_GROUPS, jnp.where(rows == 1, i2 - N_GROUPS, 0.0))
    route_t = jnp.where(rows == 2, w1, jnp.where(rows == 3, w2, route_t))
    return xn, route_t.T


def _router_weights(w_rg, w_re):
    pad = jnp.zeros((w_rg.shape[0], ROUTER_LANES - N_GROUPS - N_EXPERTS), F32)
    return jnp.concatenate([w_rg, w_re, pad], axis=1)


def _out_even_body(tiles_per_seq, ya_ref, p_ref, halo_ref, h_ref, pw_ref, ps_ref, wo_ref, fg_ref, wr_ref,
                   o_ref, xn_ref, route_ref, p_scr, yb_scr):
    tm = p_ref.shape[0]
    it = pl.program_id(0) % tiles_per_seq
    halo = halo_ref[...].astype(F32)
    p_scr[0:POOL_HALO, :] = jnp.where(it == 0, jnp.zeros_like(halo), halo)
    p_scr[POOL_HALO:, :] = p_ref[...].astype(F32)
    t = it * tm + lax.broadcasted_iota(jnp.int32, (tm, 1), 0)
    for g, win in enumerate(POOL_WINDOWS):
        ls = slice(g * LANES, (g + 1) * LANES)
        cur = p_scr[POOL_HALO:POOL_HALO + tm, ls]
        acc = cur
        for dlt in range(1, win):
            acc = acc + p_scr[POOL_HALO - dlt:POOL_HALO - dlt + tm, ls]
        cnt = jnp.minimum(t + 1, win).astype(F32)
        mixed = acc / cnt - cur
        yb = jnp.dot(mixed.astype(BF16), pw_ref[g], preferred_element_type=F32) * ps_ref[:, ls]
        yb_scr[:, ls] = yb.astype(BF16)
    mixed = jnp.concatenate([ya_ref[...], yb_scr[...]], axis=1)
    _project_and_route(h_ref, mixed, wo_ref, fg_ref, wr_ref, o_ref, xn_ref, route_ref)


def _out_even(ya, p, h, pool_w, pool_scale, w_out, ffn_gain, w_router, seq, tm):
    n, d = h.shape
    w = MIX_WIDTH
    row = lambda i: (i, 0)
    const2 = lambda i: (0, 0)
    halo_blocks = tm // POOL_HALO
    return pl.pallas_call(
        functools.partial(_out_even_body, seq // tm),
        grid=(n // tm,),
        in_specs=[
            pl.BlockSpec((tm, w), row),
            pl.BlockSpec((tm, w), row),
            pl.BlockSpec((POOL_HALO, w), lambda i: (jnp.maximum(i * halo_blocks - 1, 0), 0)),
            pl.BlockSpec((tm, d), row),
            pl.BlockSpec(pool_w.shape, lambda i: (0, 0, 0)),
            pl.BlockSpec((1, w), const2),
            pl.BlockSpec(w_out.shape, const2),
            pl.BlockSpec((1, d), const2),
            pl.BlockSpec(w_router.shape, const2),
        ],
        out_specs=[pl.BlockSpec((tm, d), row), pl.BlockSpec((tm * SUBLANES, LANES), row),
                   pl.BlockSpec((tm, ROUTER_LANES), row)],
        out_shape=[jax.ShapeDtypeStruct((n, d), F32), jax.ShapeDtypeStruct((n * SUBLANES, LANES), F32),
                   jax.ShapeDtypeStruct((n, ROUTER_LANES), F32)],
        scratch_shapes=[pltpu.VMEM((tm + POOL_HALO, w), F32), pltpu.VMEM((tm, w), BF16)],
        compiler_params=_cparams(("arbitrary",)),
        name="out_even",
    )(ya, p, p, h, pool_w, pool_scale, w_out, ffn_gain, w_router)


MOE_TM = 256
MOE_TILE_LANES = 256


def _moe_tiles(n):
    return (2 * n) // MOE_TM + N_EXPERTS


def _exact_dot_nt(ones, x):
    out = None
    for _ in range(3):
        part = x.astype(BF16)
        x = x - part.astype(F32)
        term = lax.dot_general(ones, part, NT_DIMS, preferred_element_type=F32)
        out = term if out is None else out + term
    return out


def _slots_body(n_tiles, route_ref, pos1_ref, pos2_ref, tile_ref, run_scr, start_scr):
    phase = pl.program_id(0)
    i = pl.program_id(1)
    tm = route_ref.shape[0]
    route = route_ref[...]
    lane = lax.broadcasted_iota(jnp.int32, (tm, ROUTER_LANES), 1).astype(F32)
    pick1 = jnp.where(lane == route[:, 0:1], 1.0, 0.0)
    pick2 = jnp.where(lane == route[:, 1:2], 1.0, 0.0)
    occ = (pick1 + pick2).astype(BF16)
    ones_rows = jnp.ones((SUBLANES, tm), BF16)
    ones_lanes = jnp.ones((SUBLANES, ROUTER_LANES), BF16)

    @pl.when(i == 0)
    def _():
        run_scr[...] = jnp.zeros_like(run_scr)

    @pl.when(phase == 0)
    def _():
        run_scr[...] += jnp.dot(ones_rows, occ, preferred_element_type=F32)

        @pl.when(i == pl.num_programs(1) - 1)
        def _():
            padded = jnp.floor((run_scr[...] + (MOE_TM - 1)) * (1.0 / MOE_TM)) * MOE_TM
            r = lax.broadcasted_iota(jnp.int32, (ROUTER_LANES, ROUTER_LANES), 0)
            c = lax.broadcasted_iota(jnp.int32, (ROUTER_LANES, ROUTER_LANES), 1)
            before = jnp.where(r < c, 1.0, 0.0).astype(BF16)
            hi = padded.astype(BF16)
            mid = (padded - hi.astype(F32)).astype(BF16)
            low = (padded - hi.astype(F32) - mid.astype(F32)).astype(BF16)
            start = (jnp.dot(hi, before, preferred_element_type=F32)
                     + jnp.dot(mid, before, preferred_element_type=F32)
                     + jnp.dot(low, before, preferred_element_type=F32))
            start_scr[...] = start
            seg_end = start[0:1, :] + padded[0:1, :]
            tile_lo = (lax.broadcasted_iota(jnp.int32, (MOE_TILE_LANES, ROUTER_LANES), 0) * MOE_TM).astype(F32)
            e_lane = lax.broadcasted_iota(jnp.int32, (MOE_TILE_LANES, ROUTER_LANES), 1)
            ended = jnp.where((seg_end <= tile_lo) & (e_lane < N_EXPERTS), 1.0, 0.0).astype(BF16)
            tile_ref[...] = lax.dot_general(ones_lanes, ended, NT_DIMS, preferred_element_type=F32)

    @pl.when(phase == 1)
    def _():
        r = lax.broadcasted_iota(jnp.int32, (tm, tm), 0)
        c = lax.broadcasted_iota(jnp.int32, (tm, tm), 1)
        earlier = jnp.where(c < r, 1.0, 0.0).astype(BF16)
        base = (jnp.dot(earlier, occ, preferred_element_type=F32)
                + run_scr[0:1, :] + start_scr[0:1, :])
        pos1_ref[...] = _exact_dot_nt(ones_lanes, pick1 * base)
        pos2_ref[...] = _exact_dot_nt(ones_lanes, pick2 * base)
        run_scr[...] += jnp.dot(ones_rows, occ, preferred_element_type=F32)


def _slots(route, tm):
    n = route.shape[0]
    n_tiles = _moe_tiles(n)
    assert n_tiles <= MOE_TILE_LANES and 2 * n + N_EXPERTS * MOE_TM < 2 ** 24
    row_out = pl.BlockSpec((SUBLANES, tm), lambda ph, i: (0, i * ph))
    sds = jax.ShapeDtypeStruct((SUBLANES, n), F32)
    return pl.pallas_call(
        functools.partial(_slots_body, n_tiles),
        grid=(2, n // tm),
        in_specs=[pl.BlockSpec((tm, ROUTER_LANES), lambda ph, i: (i, 0))],
        out_specs=[row_out, row_out, pl.BlockSpec((SUBLANES, MOE_TILE_LANES), lambda ph, i: (0, 0))],
        out_shape=[sds, sds, jax.ShapeDtypeStruct((SUBLANES, MOE_TILE_LANES), F32)],
        scratch_shapes=[pltpu.VMEM((SUBLANES, ROUTER_LANES), F32), pltpu.VMEM((SUBLANES, ROUTER_LANES), F32)],
        compiler_params=_cparams(("arbitrary", "arbitrary")),
        name="moe_slots",
    )(route)


def _dispatch_body(n_tiles, pos_ref, tile_ref, xn_ref, xs_hbm, zero_scr, zero_sem, row_sem):
    tm = xn_ref.shape[0] // SUBLANES
    n = pl.num_programs(0) * tm
    base = pl.program_id(0) * tm
    tile_rows = MOE_TM * SUBLANES

    @pl.when(pl.program_id(0) == 0)
    def _():
        zero_scr[...] = jnp.zeros_like(zero_scr)

        def fill_copy(t):
            return pltpu.make_async_copy(zero_scr, xs_hbm.at[pl.ds(t * tile_rows, tile_rows), :], zero_sem)

        def has_padding(t):
            return (tile_ref[t] >= N_EXPERTS) | (tile_ref[t] != tile_ref[t + 1])

        @pl.loop(0, n_tiles)
        def _(t):
            @pl.when(has_padding(t))
            def _():
                fill_copy(t).start()

        @pl.loop(0, n_tiles)
        def _(t):
            @pl.when(has_padding(t))
            def _():
                fill_copy(t).wait()

    def issue(j, carry):
        src = xn_ref.at[pl.ds(pl.multiple_of(j * SUBLANES, SUBLANES), SUBLANES), :]
        for pick in range(2):
            dst = pl.multiple_of(pos_ref[pick * n + base + j], SUBLANES)
            pltpu.make_async_copy(src, xs_hbm.at[pl.ds(dst, SUBLANES), :], row_sem).start(priority=pick)
        return carry

    lax.fori_loop(0, tm, issue, 0, unroll=8)
    for _ in range(2):
        pltpu.make_async_copy(xn_ref, xs_hbm.at[pl.ds(0, tm * SUBLANES), :], row_sem).wait()


def _dispatch(pos, tile_map, xn, tm):
    n = xn.shape[0] // SUBLANES
    n_tiles = _moe_tiles(n)
    return pl.pallas_call(
        functools.partial(_dispatch_body, n_tiles),
        grid_spec=pltpu.PrefetchScalarGridSpec(
            num_scalar_prefetch=2,
            grid=(n // tm,),
            in_specs=[pl.BlockSpec((tm * SUBLANES, LANES), lambda i, p, t: (i, 0))],
            out_specs=pl.BlockSpec(memory_space=pl.ANY),
            scratch_shapes=[pltpu.VMEM((MOE_TM * SUBLANES, LANES), F32), pltpu.SemaphoreType.DMA(()),
                            pltpu.SemaphoreType.DMA(())],
        ),
        out_shape=jax.ShapeDtypeStruct((n_tiles * MOE_TM * SUBLANES, LANES), F32),
        compiler_params=_cparams(("arbitrary",)),
        name="moe_dispatch",
    )(pos, tile_map, xn)


EXPERT_SLOTS = 3


def _experts_body(tile_ref, xs_hbm, wg_ref, wu_ref, wd_ref, ys_ref, xbuf, sems, wg_bf, wu_bf, wd_bf):
    i = pl.program_id(0)
    tile_rows = MOE_TM * SUBLANES
    used = tile_ref[i] < N_EXPERTS

    def tile_copy(t, slot):
        rows = pl.ds(pl.multiple_of(t * tile_rows, tile_rows), tile_rows)
        return pltpu.make_async_copy(xs_hbm.at[rows, :], xbuf.at[slot], sems.at[slot])

    def prefetch(t):
        @pl.when(tile_ref[t] < N_EXPERTS)
        def _():
            tile_copy(t, t % EXPERT_SLOTS).start()

    @pl.when(i == 0)
    def _():
        prefetch(0)
        prefetch(1)

    prefetch(i + 2)

    @pl.when(used & ((i == 0) | (tile_ref[i] != tile_ref[jnp.maximum(i - 1, 0)])))
    def _():
        wg_bf[...] = wg_ref[0].astype(BF16)
        wu_bf[...] = wu_ref[0].astype(BF16)
        wd_bf[...] = wd_ref[0].astype(BF16)

    @pl.when(used)
    def _():
        slot = i % EXPERT_SLOTS
        tile_copy(i, slot).wait()
        x = _load_token_major(xbuf, MOE_TM, (slot,)).astype(BF16)
        gate = jnp.dot(x, wg_bf[...], preferred_element_type=F32)
        up = jnp.dot(x, wu_bf[...], preferred_element_type=F32)
        hid = gate * jax.nn.sigmoid(gate) * up
        _store_token_major(ys_ref, jnp.dot(hid.astype(BF16), wd_bf[...], preferred_element_type=F32))

    @pl.when(jnp.logical_not(used))
    def _():
        ys_ref[...] = jnp.zeros_like(ys_ref)


def _experts(tile_map, xs, w_gate, w_up, w_down, layer):
    d, f = w_gate.shape[1:]
    tile_rows = MOE_TM * SUBLANES
    n_tiles = xs.shape[0] // tile_rows
    y_map = lambda i, tm_ref: (i, 0)
    w_map = lambda i, tm_ref: (layer * N_EXPERTS + jnp.minimum(tm_ref[i], N_EXPERTS - 1), 0, 0)
    return pl.pallas_call(
        _experts_body,
        grid_spec=pltpu.PrefetchScalarGridSpec(
            num_scalar_prefetch=1,
            grid=(n_tiles,),
            in_specs=[pl.BlockSpec(memory_space=pl.ANY), pl.BlockSpec((1, d, f), w_map),
                      pl.BlockSpec((1, d, f), w_map), pl.BlockSpec((1, f, d), w_map)],
            out_specs=pl.BlockSpec((tile_rows, LANES), y_map),
            scratch_shapes=[pltpu.VMEM((EXPERT_SLOTS, tile_rows, LANES), F32),
                            pltpu.SemaphoreType.DMA((EXPERT_SLOTS,)),
                            pltpu.VMEM((d, f), BF16), pltpu.VMEM((d, f), BF16), pltpu.VMEM((f, d), BF16)],
        ),
        out_shape=jax.ShapeDtypeStruct(xs.shape, F32),
        compiler_params=_cparams(("arbitrary",)),
        name="moe_experts",
    )(tile_map, xs, w_gate, w_up, w_down)


def _combine_body(pos_ref, h_ref, route_ref, ys_hbm, o_ref, buf, sems):
    tm = h_ref.shape[0]
    steps = pl.num_programs(0)
    n = steps * tm
    i = pl.program_id(0)

    def start_gather(step, slot):
        base = step * tm

        def issue(g, carry):
            for u in range(SUBLANES):
                for pick in range(2):
                    src = pl.multiple_of(pos_ref[pick * n + base + g * SUBLANES + u], SUBLANES)
                    dst = pl.multiple_of(g * SUBLANES * SUBLANES, SUBLANES) + u * SUBLANES
                    pltpu.make_async_copy(ys_hbm.at[pl.ds(src, SUBLANES), :],
                                          buf.at[slot, pick, pl.ds(dst, SUBLANES), :],
                                          sems.at[slot]).start(priority=pick)
            return carry

        lax.fori_loop(0, tm // SUBLANES, issue, 0)

    @pl.when(i == 0)
    def _():
        start_gather(0, 0)

    @pl.when(i + 1 < steps)
    def _():
        start_gather(i + 1, (i + 1) % 2)

    slot = i % 2
    pltpu.make_async_copy(buf.at[slot], buf.at[slot], sems.at[slot]).wait()
    route = route_ref[...]
    y1 = _load_token_major(buf, tm, (slot, 0))
    y2 = _load_token_major(buf, tm, (slot, 1))
    o_ref[...] = h_ref[...] + route[:, 2:3] * y1 + route[:, 3:4] * y2


def _combine(pos, h, route, ys, tm):
    n, d = h.shape
    row = lambda i, p: (i, 0)
    return pl.pallas_call(
        _combine_body,
        grid_spec=pltpu.PrefetchScalarGridSpec(
            num_scalar_prefetch=1,
            grid=(n // tm,),
            in_specs=[pl.BlockSpec((tm, d), row), pl.BlockSpec((tm, ROUTER_LANES), row),
                      pl.BlockSpec(memory_space=pl.ANY)],
            out_specs=pl.BlockSpec((tm, d), row),
            scratch_shapes=[pltpu.VMEM((2, 2, tm * SUBLANES, LANES), F32), pltpu.SemaphoreType.DMA((2,))],
        ),
        out_shape=jax.ShapeDtypeStruct((n, d), F32),
        compiler_params=_cparams(("arbitrary",)),
        name="moe_combine",
    )(pos, h, route, ys)


def _gelu(x):
    return 0.5 * x * (1.0 + lax.erf(x * (1.0 / math.sqrt(2.0))))


def _in_odd_body(h_ref, g_ref, w_ref, vg_ref, u_ref, vn_ref, q_ref, k_ref, v_ref):
    xn = _rms(h_ref[...], g_ref[...])
    proj = jnp.dot(xn.astype(BF16), w_ref[...], preferred_element_type=F32)
    w = MIX_WIDTH
    u_ref[...] = _gelu(proj[:, :w]).astype(BF16)
    vn_ref[...] = _rms(_gelu(proj[:, w:2 * w]), vg_ref[...]).astype(BF16)
    q_ref[...] = (proj[:, 2 * w:3 * w] * Q_SCALE).astype(BF16)
    k_ref[...] = proj[:, 3 * w:4 * w].astype(BF16)
    v_ref[...] = proj[:, 4 * w:].astype(BF16)


def _in_odd(h, gain, w, v_gain, tm):
    n, d = h.shape
    row = lambda i: (i, 0)
    const2 = lambda i: (0, 0)
    sds = jax.ShapeDtypeStruct((n, MIX_WIDTH), BF16)
    return pl.pallas_call(
        _in_odd_body,
        grid=(n // tm,),
        in_specs=[pl.BlockSpec((tm, d), row), pl.BlockSpec((1, d), const2),
                  pl.BlockSpec(w.shape, const2), pl.BlockSpec((1, MIX_WIDTH), const2)],
        out_specs=[pl.BlockSpec((tm, MIX_WIDTH), row)] * 5,
        out_shape=[sds] * 5,
        compiler_params=_cparams(("arbitrary",)),
        name="in_odd",
    )(h, gain, w, v_gain)


SB_TQ = 64
SB_KB = 128
SB_NB = 2
SB_TK = SB_NB * SB_KB
SB_QBLK = 1024
SB_GROUP = 16
SB_UNDERFLOW = -150.0


def _sb_body(q_ref, k_ref, v_ref, o_ref, acc_scr, run_scr):
    qi = pl.program_id(2)
    lane = lax.broadcasted_iota(jnp.int32, (SB_TQ, LANES), 1)
    row = lax.broadcasted_iota(jnp.int32, (2 * SB_TQ, 1), 0) % SB_TQ
    col = lax.broadcasted_iota(jnp.int32, (2 * SB_TQ, SB_TK), 1)
    rr = lax.broadcasted_iota(jnp.int32, (2 * SB_KB, 2 * SB_KB), 0) % SB_KB
    cc = lax.broadcasted_iota(jnp.int32, (2 * SB_KB, 2 * SB_KB), 1)
    suffix = jnp.where((cc >= SB_KB) | (rr > cc), 1.0, 0.0).astype(BF16)

    def suffix_sums(x):
        hi = x.astype(BF16)
        lo = (x - hi.astype(F32)).astype(BF16)
        return jnp.dot(jnp.concatenate([hi, lo], axis=1), suffix, preferred_element_type=F32)

    def subtile_group(grp, _):
        q_los = [pl.multiple_of((grp * SB_GROUP + s) * SB_TQ, SB_TQ) for s in range(SB_GROUP)]
        q_starts = [qi * SB_QBLK + q_lo for q_lo in q_los]
        q_heads = []
        for q_lo in q_los:
            q = q_ref[0, pl.ds(q_lo, SB_TQ), :]
            zero = jnp.zeros_like(q)
            q_heads.append(jnp.concatenate(
                [jnp.where(lane < HEAD_DIM, q, zero), jnp.where(lane < HEAD_DIM, zero, q)], axis=0))
        acc_scr[...] = jnp.zeros_like(acc_scr)
        run_scr[...] = jnp.zeros_like(run_scr)

        def cond(carry):
            his, dones = carry
            active = [(hi > 0) & (done == 0) for hi, done in zip(his, dones)]
            return functools.reduce(jnp.logical_or, active)

        def body(carry):
            his, _ = carry
            group = range(SB_GROUP)
            kss = [pl.multiple_of(jnp.maximum(his[s] - SB_TK, 0), SB_TQ) for s in group]
            valid = [col < (jnp.minimum(row + q_starts[s], his[s]) - kss[s]) for s in group]
            zs = [lax.dot_general(q_heads[s], k_ref[0, pl.ds(kss[s], SB_TK), :], NT_DIMS,
                                  preferred_element_type=F32) for s in group]
            log_beta, log_rest = [], []
            for s in group:
                z = jnp.where(valid[s], zs[s], NEG_BIG)
                sp = jnp.log2(1.0 + jnp.exp2(-jnp.abs(z)))
                log_beta.append(jnp.minimum(z, 0.0) - sp)
                log_rest.append(log_beta[s] - z)
            sums = [[suffix_sums(log_rest[s][:, blk * SB_KB:(blk + 1) * SB_KB]) for blk in range(SB_NB)]
                    for s in group]
            dones = []
            for s in group:
                run = run_scr[s]
                pieces = [None] * SB_NB
                for blk in reversed(range(SB_NB)):
                    tt = sums[s][blk]
                    pieces[blk] = jnp.exp2(log_beta[s][:, blk * SB_KB:(blk + 1) * SB_KB] + tt[:, :SB_KB] + run)
                    run = run + tt[:, SB_KB:]
                att = jnp.concatenate(pieces, axis=1)
                acc_scr[s] += jnp.dot(att.astype(BF16), v_ref[0, pl.ds(kss[s], SB_TK), :],
                                      preferred_element_type=F32)
                run_scr[s] = run
                dones.append((jnp.max(run) <= SB_UNDERFLOW).astype(jnp.int32))
            return tuple(kss), tuple(dones)

        lax.while_loop(cond, body, (tuple(qs + SB_TQ for qs in q_starts),
                                    tuple(jnp.int32(0) for _ in range(SB_GROUP))))
        for s in range(SB_GROUP):
            o_ref[0, pl.ds(q_los[s], SB_TQ), :] = jnp.where(
                lane < HEAD_DIM, acc_scr[s, :SB_TQ], acc_scr[s, SB_TQ:]).astype(BF16)
        return 0

    lax.fori_loop(0, SB_QBLK // (SB_TQ * SB_GROUP), subtile_group, 0)


def _sb_attention(q, k, v):
    b, s, w = q.shape
    assert s % SB_QBLK == 0 and s >= SB_TK
    qspec = pl.BlockSpec((1, SB_QBLK, LANES), lambda bi, hp, i: (bi, i, hp))
    kvspec = pl.BlockSpec((1, s, LANES), lambda bi, hp, i: (bi, 0, hp))
    return pl.pallas_call(
        _sb_body,
        grid=(b, w // LANES, s // SB_QBLK),
        in_specs=[qspec, kvspec, kvspec],
        out_specs=qspec,
        out_shape=jax.ShapeDtypeStruct((b, s, w), BF16),
        scratch_shapes=[pltpu.VMEM((SB_GROUP, 2 * SB_TQ, LANES), F32),
                        pltpu.VMEM((SB_GROUP, 2 * SB_TQ, LANES), F32)],
        compiler_params=_cparams(("arbitrary", "arbitrary", "arbitrary")),
        name="sb_attention",
    )(q, k, v)


def _out_odd_body(u_ref, vn_ref, yd_ref, h_ref, ws_ref, bs_ref, wo_ref, fg_ref, wr_ref,
                  o_ref, xn_ref, route_ref, yc_scr):
    tm = u_ref.shape[0]
    r = lax.broadcasted_iota(jnp.int32, (SGU_BLOCK, SGU_BLOCK), 0)
    c = lax.broadcasted_iota(jnp.int32, (SGU_BLOCK, SGU_BLOCK), 1)
    for g in range(MIX_WIDTH // LANES):
        ls = slice(g * LANES, (g + 1) * LANES)
        ws = jnp.where(c <= r, ws_ref[g], jnp.zeros_like(ws_ref[g]))
        for blk in range(tm // SGU_BLOCK):
            rs = slice(blk * SGU_BLOCK, (blk + 1) * SGU_BLOCK)
            mixed = jnp.dot(ws, vn_ref[rs, ls], preferred_element_type=F32) + bs_ref[g]
            yc_scr[rs, ls] = (u_ref[rs, ls].astype(F32) * mixed).astype(BF16)
    mixed = jnp.concatenate([yc_scr[...], yd_ref[...]], axis=1)
    _project_and_route(h_ref, mixed, wo_ref, fg_ref, wr_ref, o_ref, xn_ref, route_ref)


def _out_odd(u, vn, yd, h, sgu_w, sgu_b, w_out, ffn_gain, w_router, tm):
    n, d = h.shape
    w = MIX_WIDTH
    row = lambda i: (i, 0)
    const2 = lambda i: (0, 0)
    const3 = lambda i: (0, 0, 0)
    return pl.pallas_call(
        _out_odd_body,
        grid=(n // tm,),
        in_specs=[pl.BlockSpec((tm, w), row), pl.BlockSpec((tm, w), row), pl.BlockSpec((tm, w), row),
                  pl.BlockSpec((tm, d), row), pl.BlockSpec(sgu_w.shape, const3),
                  pl.BlockSpec(sgu_b.shape, const3), pl.BlockSpec(w_out.shape, const2),
                  pl.BlockSpec((1, d), const2), pl.BlockSpec(w_router.shape, const2)],
        out_specs=[pl.BlockSpec((tm, d), row), pl.BlockSpec((tm * SUBLANES, LANES), row),
                   pl.BlockSpec((tm, ROUTER_LANES), row)],
        out_shape=[jax.ShapeDtypeStruct((n, d), F32), jax.ShapeDtypeStruct((n * SUBLANES, LANES), F32),
                   jax.ShapeDtypeStruct((n, ROUTER_LANES), F32)],
        scratch_shapes=[pltpu.VMEM((tm, w), BF16)],
        compiler_params=_cparams(("arbitrary",)),
        name="out_odd",
    )(u, vn, yd, h, sgu_w, sgu_b, w_out, ffn_gain, w_router)


def _moe_layer(h, xn, route, w_gate, w_up, w_down, layer):
    pos1, pos2, tile_map = _slots(route, tm=1024)
    pos = jnp.concatenate([pos1[0], pos2[0]]).astype(jnp.int32) * SUBLANES
    tile_map = tile_map[0].astype(jnp.int32)
    xs = _dispatch(pos, tile_map, xn, tm=1024)
    ys = _experts(tile_map, xs, w_gate, w_up, w_down, layer)
    return _combine(pos, h, route, ys, tm=256)


def kernel(x, mix_norm_even, w_in_even, att_q_norm, att_k_norm, att_rel_bias, pool_w, pool_scale,
           w_out_even, mix_norm_odd, w_in_odd, sgu_v_norm, sgu_w, sgu_b, w_out_odd, ffn_norm,
           w_router_group, w_router_expert, w_exp_gate, w_exp_up, w_exp_down):
    b, s, d = x.shape
    n = b * s
    depth = ffn_norm.shape[0]
    heads = MIX_WIDTH // HEAD_DIM
    h = x.reshape(n, d)
    w_gate = w_exp_gate.reshape(depth * N_EXPERTS, d, EXPERT_FF)
    w_up = w_exp_up.reshape(depth * N_EXPERTS, d, EXPERT_FF)
    w_down = w_exp_down.reshape(depth * N_EXPERTS, EXPERT_FF, d)
    for layer in range(depth):
        i = layer // 2
        ffn_gain = ffn_norm[layer][None, :]
        w_router = _router_weights(w_router_group[layer], w_router_expert[layer])
        if layer % 2 == 0:
            q, k_pad, v_pad, p = _in_even(
                h.reshape(b, s, d), mix_norm_even[i][None, :], w_in_even[i].astype(BF16),
                jnp.tile(att_q_norm[i], heads)[None, :], jnp.tile(att_k_norm[i], heads)[None, :],
                tm=ATT_LEFT)
            ya = _band_attention(q, k_pad, v_pad, _band_bias(att_rel_bias[i]))
            h, xn, route = _out_even(ya.reshape(n, MIX_WIDTH), p.reshape(n, MIX_WIDTH), h,
                                     pool_w[i].astype(BF16), pool_scale[i][None, :],
                                     w_out_even[i].astype(BF16), ffn_gain, w_router, seq=s, tm=512)
        else:
            u, vn, q, k, v = _in_odd(h, mix_norm_odd[i][None, :], w_in_odd[i].astype(BF16),
                                     sgu_v_norm[i][None, :], tm=512)
            to3 = lambda t: t.reshape(b, s, MIX_WIDTH)
            yd = _sb_attention(to3(q), to3(k), to3(v))
            bias = jnp.broadcast_to(sgu_b[i][:, :, None], (N_GROUPS, SGU_BLOCK, LANES))
            h, xn, route = _out_odd(u, vn, yd.reshape(n, MIX_WIDTH), h, sgu_w[i].astype(BF16), bias,
                                    w_out_odd[i].astype(BF16), ffn_gain, w_router, tm=512)
        h = _moe_layer(h, xn, route, w_gate, w_up, w_down, layer)
    return h.reshape(b, s, d)
```

```python
import functools
import math

import jax
import jax.numpy as jnp
from jax import lax
from jax.experimental import pallas as pl
from jax.experimental.pallas import tpu as pltpu

F32 = jnp.float32
BF16 = jnp.bfloat16

D_MODEL = 1024
CHUNK = 64
EPS = 1e-6
HEAD_DIM = 64
MIX_WIDTH = 512
LANES = 128
SUBLANES = 8
ATT_LEFT = 8 * CHUNK
ATT_MAX_REL = 128
POOL_WINDOWS = (2, 4, 8, 16)
POOL_HALO = 16
SGU_BLOCK = 128
N_GROUPS = 4
N_EXP_PER_GROUP = 8
N_EXPERTS = N_GROUPS * N_EXP_PER_GROUP
EXPERT_FF = 256
ROUTER_LANES = 128
ROUTER_ROWS = 40
NEG_BIG = -1e30
VMEM_LIMIT = 56 * 1024 * 1024

NT_DIMS = (((1,), (1,)), ((), ()))
LOG2E = math.log2(math.e)
Q_SCALE = LOG2E / math.sqrt(HEAD_DIM)


def _cparams(sem):
    return pltpu.CompilerParams(dimension_semantics=sem, vmem_limit_bytes=VMEM_LIMIT)


def _store_token_major(ref, x, first_token=0):
    rows = x.shape[0]
    for s in range(SUBLANES):
        ref[pl.ds(first_token * SUBLANES + s, rows, stride=SUBLANES), :] = x[:, s * LANES:(s + 1) * LANES]


def _row_halves(rows):
    return [slice(0, rows // 2), slice(rows // 2, rows)]


def _project_and_route(h_ref, mixed, wo_ref, fg_ref, wr_ref, o_ref, xn_ref, route_ref):
    halves = _row_halves(h_ref.shape[0])
    h_new = [h_ref[rs, :] + jnp.dot(mixed[rs], wo_ref[...], preferred_element_type=F32) for rs in halves]
    for rs, hn in zip(halves, h_new):
        o_ref[rs, :] = hn
    for rs, hn in zip(halves, h_new):
        xn, route_ref[rs, :] = _route_tokens(hn, fg_ref[...], wr_ref[...])
        _store_token_major(xn_ref, xn, first_token=rs.start)


def _load_token_major(ref, rows, lead=()):
    return jnp.concatenate(
        [ref[lead + (pl.ds(s, rows, stride=SUBLANES), slice(None))] for s in range(SUBLANES)], axis=1)


def _rms(x, gain):
    return x * lax.rsqrt(jnp.mean(x * x, axis=-1, keepdims=True) + EPS) * gain


def _split_dot(x, m):
    hi = x.astype(BF16)
    lo = (x - hi.astype(F32)).astype(BF16)
    return (jnp.dot(hi, m, preferred_element_type=F32)
            + jnp.dot(lo, m, preferred_element_type=F32))


def _head_rms(t, gain):
    n = t.shape[-1]
    r = lax.broadcasted_iota(jnp.int32, (n, n), 0) // HEAD_DIM
    c = lax.broadcasted_iota(jnp.int32, (n, n), 1) // HEAD_DIM
    bd = jnp.where(r == c, 1.0, 0.0).astype(BF16)
    ms = _split_dot(t * t, bd) * (1.0 / HEAD_DIM)
    return t * lax.rsqrt(ms + EPS) * gain


def _in_even_body(h_ref, g_ref, w_ref, qg_ref, kg_ref, q_ref, k_ref, v_ref, p_ref):
    j = pl.program_id(1)

    @pl.when(j == 0)
    def _():
        k_ref[...] = jnp.zeros_like(k_ref)
        v_ref[...] = jnp.zeros_like(v_ref)

    @pl.when(j > 0)
    def _():
        xn = _rms(h_ref[0], g_ref[...])
        proj = jnp.dot(xn.astype(BF16), w_ref[...], preferred_element_type=F32)
        w = MIX_WIDTH
        q_ref[0] = (_head_rms(proj[:, :w], qg_ref[...]) * Q_SCALE).astype(BF16)
        k_ref[0] = _head_rms(proj[:, w:2 * w], kg_ref[...]).astype(BF16)
        v_ref[0] = proj[:, 2 * w:3 * w].astype(BF16)
        p_ref[0] = proj[:, 3 * w:].astype(BF16)


def _in_even(h, gain, w, q_gain, k_gain, tm):
    b, s, d = h.shape
    assert tm == ATT_LEFT and s % tm == 0
    nt = s // tm
    cur = lambda bi, j: (bi, jnp.maximum(j - 1, 0), 0)
    const = lambda bi, j: (0, 0)
    out_sds = lambda rows: jax.ShapeDtypeStruct((b, rows, MIX_WIDTH), BF16)
    return pl.pallas_call(
        _in_even_body,
        grid=(b, nt + 1),
        in_specs=[
            pl.BlockSpec((1, tm, d), cur),
            pl.BlockSpec((1, d), const),
            pl.BlockSpec(w.shape, const),
            pl.BlockSpec((1, MIX_WIDTH), const),
            pl.BlockSpec((1, MIX_WIDTH), const),
        ],
        out_specs=[
            pl.BlockSpec((1, tm, MIX_WIDTH), cur),
            pl.BlockSpec((1, tm, MIX_WIDTH), lambda bi, j: (bi, j, 0)),
            pl.BlockSpec((1, tm, MIX_WIDTH), lambda bi, j: (bi, j, 0)),
            pl.BlockSpec((1, tm, MIX_WIDTH), cur),
        ],
        out_shape=[out_sds(s), out_sds(s + ATT_LEFT), out_sds(s + ATT_LEFT), out_sds(s)],
        compiler_params=_cparams(("arbitrary", "arbitrary")),
        name="in_even",
    )(h, gain, w, q_gain, k_gain)


BAND_TQ = 2 * CHUNK
BAND_TK = BAND_TQ + ATT_LEFT


BAND_STEP_TILES = 4
BAND_STAGED = 2


def _band_body(q_ref, k_ref, v_ref, bias_ref, o_ref):
    @pl.loop(0, BAND_STEP_TILES)
    def _(t):
        rows = pl.ds(pl.multiple_of(t * BAND_TQ, BAND_TQ), BAND_TQ)
        _band_tile(pl.program_id(1) * BAND_STEP_TILES + t, q_ref.at[0, rows, :], k_ref, v_ref, bias_ref,
                   o_ref.at[0, rows, :])


def _band_tile(i, q_ref, k_ref, v_ref, bias_ref, o_ref):
    start = pl.multiple_of(i * BAND_TQ, BAND_TQ)
    lane = lax.broadcasted_iota(jnp.int32, (BAND_TQ, LANES), 1)
    col = lax.broadcasted_iota(jnp.int32, (2 * BAND_TQ, BAND_TK), 1)
    is_pad = (col + start) < ATT_LEFT
    n_pairs = MIX_WIDTH // LANES
    lanes = [slice(hp * LANES, (hp + 1) * LANES) for hp in range(n_pairs)]
    for first in range(0, n_pairs, BAND_STAGED):
        pairs = range(first, first + BAND_STAGED)
        scores, probs, denoms = {}, {}, {}
        for hp in pairs:
            q = q_ref[:, lanes[hp]]
            kb = k_ref[0, pl.ds(start, BAND_TK), lanes[hp]]
            zero = jnp.zeros_like(q)
            q2 = jnp.concatenate([jnp.where(lane < HEAD_DIM, q, zero), jnp.where(lane < HEAD_DIM, zero, q)],
                                 axis=0)
            scores[hp] = lax.dot_general(q2, kb, NT_DIMS, preferred_element_type=F32)
        for hp in pairs:
            bias = bias_ref[2 * hp:2 * hp + 2].reshape(2 * BAND_TQ, BAND_TK)
            s = jnp.where(is_pad, NEG_BIG, scores[hp] + bias)
            p = jnp.exp2(s - jnp.max(s, axis=-1, keepdims=True))
            denoms[hp] = jnp.sum(p, axis=-1, keepdims=True)
            probs[hp] = p.astype(BF16)
        for hp in pairs:
            vb = v_ref[0, pl.ds(start, BAND_TK), lanes[hp]]
            o = jnp.dot(probs[hp], vb, preferred_element_type=F32) / denoms[hp]
            o_ref[:, lanes[hp]] = jnp.where(lane < HEAD_DIM, o[:BAND_TQ], o[BAND_TQ:]).astype(BF16)


def _band_bias(rel_bias):
    heads = rel_bias.shape[0]
    r = jnp.arange(BAND_TQ)[:, None]
    j = jnp.arange(BAND_TK)[None, :]
    jb = j - CHUNK * (r // CHUNK)
    in_band = (jb >= 0) & (jb < ATT_LEFT + CHUNK)
    period = BAND_TK + BAND_TQ
    far = jnp.broadcast_to(rel_bias[:, 2 * ATT_MAX_REL:], (heads, ATT_LEFT - ATT_MAX_REL + 1))
    near = rel_bias[:, 2 * ATT_MAX_REL - 1:0:-1]
    wrap = jnp.broadcast_to(rel_bias[:, 2 * ATT_MAX_REL:], (heads, period - BAND_TK))
    g = jnp.concatenate([far, near, wrap], axis=1).astype(F32)
    assert g.shape[1] == period
    toep = jnp.tile(g, (1, BAND_TQ))[:, :BAND_TQ * (period - 1)].reshape(heads, BAND_TQ, period - 1)
    return jnp.where(in_band[None], toep[:, :, :BAND_TK] * LOG2E, NEG_BIG)


def _band_attention(q, k_pad, v_pad, bias):
    b, s, w = q.shape
    sp = k_pad.shape[1]
    step_rows = BAND_STEP_TILES * BAND_TQ
    return pl.pallas_call(
        _band_body,
        grid=(b, s // step_rows),
        in_specs=[
            pl.BlockSpec((1, step_rows, w), lambda bi, i: (bi, i, 0)),
            pl.BlockSpec((1, sp, w), lambda bi, i: (bi, 0, 0)),
            pl.BlockSpec((1, sp, w), lambda bi, i: (bi, 0, 0)),
            pl.BlockSpec(bias.shape, lambda bi, i: (0, 0, 0)),
        ],
        out_specs=pl.BlockSpec((1, step_rows, w), lambda bi, i: (bi, i, 0)),
        out_shape=jax.ShapeDtypeStruct((b, s, w), BF16),
        compiler_params=_cparams(("arbitrary", "arbitrary")),
        name="band_attention",
    )(q, k_pad, v_pad, bias)


def _route_tokens(h, gain, w_router):
    xn = _rms(h, gain)
    x_hi = xn.astype(BF16)
    x_lo = (xn - x_hi.astype(F32)).astype(BF16)
    w_hi = w_router.astype(BF16)
    w_lo = (w_router - w_hi.astype(F32)).astype(BF16)
    logits = (jnp.dot(x_hi, w_hi, preferred_element_type=F32)
              + jnp.dot(x_lo, w_hi, preferred_element_type=F32)
              + jnp.dot(x_hi, w_lo, preferred_element_type=F32))
    lt = logits.T[:ROUTER_ROWS]
    sub = lax.broadcasted_iota(jnp.int32, lt.shape, 0).astype(F32)
    ninf = -jnp.inf

    def top(vals):
        m = jnp.max(vals, axis=0, keepdims=True)
        idx = jnp.min(jnp.where(vals == m, sub, float(ROUTER_LANES)), axis=0, keepdims=True)
        return m, idx

    is_group = sub < N_GROUPS
    g_max, g_sel = top(jnp.where(is_group, lt, ninf))
    g_den = jnp.sum(jnp.where(is_group, jnp.exp(lt - g_max), 0.0), axis=0, keepdims=True)
    g_weight = 1.0 / g_den
    lo = N_GROUPS + N_EXP_PER_GROUP * g_sel
    e_logits = jnp.where((sub >= lo) & (sub < lo + N_EXP_PER_GROUP), lt, ninf)
    e1, i1 = top(e_logits)
    e2, i2 = top(jnp.where(sub == i1, ninf, e_logits))
    t = jnp.exp(e2 - e1)
    w1 = g_weight / (1.0 + t)
    w2 = g_weight * t / (1.0 + t)
    rows = lax.broadcasted_iota(jnp.int32, (ROUTER_LANES, lt.shape[1]), 0)
    route_t = jnp.where(rows == 0, i1 - N_GROUPS, jnp.where(rows == 1, i2 - N_GROUPS, 0.0))
    route_t = jnp.where(rows == 2, w1, jnp.where(rows == 3, w2, route_t))
    return xn, route_t.T


def _router_weights(w_rg, w_re):
    pad = jnp.zeros((w_rg.shape[0], ROUTER_LANES - N_GROUPS - N_EXPERTS), F32)
    return jnp.concatenate([w_rg, w_re, pad], axis=1)


def _out_even_body(tiles_per_seq, ya_ref, p_ref, halo_ref, h_ref, pw_ref, ps_ref, wo_ref, fg_ref, wr_ref,
                   o_ref, xn_ref, route_ref, p_scr, yb_scr):
    tm = p_ref.shape[0]
    it = pl.program_id(0) % tiles_per_seq
    halo = halo_ref[...].astype(F32)
    p_scr[0:POOL_HALO, :] = jnp.where(it == 0, jnp.zeros_like(halo), halo)
    p_scr[POOL_HALO:, :] = p_ref[...].astype(F32)
    t = it * tm + lax.broadcasted_iota(jnp.int32, (tm, 1), 0)
    for g, win in enumerate(POOL_WINDOWS):
        ls = slice(g * LANES, (g + 1) * LANES)
        cur = p_scr[POOL_HALO:POOL_HALO + tm, ls]
        acc = cur
        for dlt in range(1, win):
            acc = acc + p_scr[POOL_HALO - dlt:POOL_HALO - dlt + tm, ls]
        cnt = jnp.minimum(t + 1, win).astype(F32)
        mixed = acc / cnt - cur
        yb = jnp.dot(mixed.astype(BF16), pw_ref[g], preferred_element_type=F32) * ps_ref[:, ls]
        yb_scr[:, ls] = yb.astype(BF16)
    mixed = jnp.concatenate([ya_ref[...], yb_scr[...]], axis=1)
    _project_and_route(h_ref, mixed, wo_ref, fg_ref, wr_ref, o_ref, xn_ref, route_ref)


def _out_even(ya, p, h, pool_w, pool_scale, w_out, ffn_gain, w_router, seq, tm):
    n, d = h.shape
    w = MIX_WIDTH
    row = lambda i: (i, 0)
    const2 = lambda i: (0, 0)
    halo_blocks = tm // POOL_HALO
    return pl.pallas_call(
        functools.partial(_out_even_body, seq // tm),
        grid=(n // tm,),
        in_specs=[
            pl.BlockSpec((tm, w), row),
            pl.BlockSpec((tm, w), row),
            pl.BlockSpec((POOL_HALO, w), lambda i: (jnp.maximum(i * halo_blocks - 1, 0), 0)),
            pl.BlockSpec((tm, d), row),
            pl.BlockSpec(pool_w.shape, lambda i: (0, 0, 0)),
            pl.BlockSpec((1, w), const2),
            pl.BlockSpec(w_out.shape, const2),
            pl.BlockSpec((1, d), const2),
            pl.BlockSpec(w_router.shape, const2),
        ],
        out_specs=[pl.BlockSpec((tm, d), row), pl.BlockSpec((tm * SUBLANES, LANES), row),
                   pl.BlockSpec((tm, ROUTER_LANES), row)],
        out_shape=[jax.ShapeDtypeStruct((n, d), F32), jax.ShapeDtypeStruct((n * SUBLANES, LANES), F32),
                   jax.ShapeDtypeStruct((n, ROUTER_LANES), F32)],
        scratch_shapes=[pltpu.VMEM((tm + POOL_HALO, w), F32), pltpu.VMEM((tm, w), BF16)],
        compiler_params=_cparams(("arbitrary",)),
        name="out_even",
    )(ya, p, p, h, pool_w, pool_scale, w_out, ffn_gain, w_router)


MOE_TM = 512
MOE_TILE_LANES = 256


def _moe_tiles(n):
    return (2 * n) // MOE_TM + N_EXPERTS


def _exact_dot_nt(ones, x):
    out = None
    for _ in range(3):
        part = x.astype(BF16)
        x = x - part.astype(F32)
        term = lax.dot_general(ones, part, NT_DIMS, preferred_element_type=F32)
        out = term if out is None else out + term
    return out


def _slots_body(n_tiles, route_ref, pos1_ref, pos2_ref, tile_ref, run_scr, start_scr, earlier_scr):
    phase = pl.program_id(0)
    i = pl.program_id(1)
    tm = route_ref.shape[0]
    route = route_ref[...]
    lane = lax.broadcasted_iota(jnp.int32, (tm, ROUTER_LANES), 1).astype(F32)
    pick1 = jnp.where(lane == route[:, 0:1], 1.0, 0.0)
    pick2 = jnp.where(lane == route[:, 1:2], 1.0, 0.0)
    occ = (pick1 + pick2).astype(BF16)
    ones_rows = jnp.ones((SUBLANES, tm), BF16)
    ones_lanes = jnp.ones((SUBLANES, ROUTER_LANES), BF16)

    @pl.when(i == 0)
    def _():
        run_scr[...] = jnp.zeros_like(run_scr)

    @pl.when(phase == 0)
    def _():
        run_scr[...] += jnp.dot(ones_rows, occ, preferred_element_type=F32)

        @pl.when(i == pl.num_programs(1) - 1)
        def _():
            padded = jnp.floor((run_scr[...] + (MOE_TM - 1)) * (1.0 / MOE_TM)) * MOE_TM
            r = lax.broadcasted_iota(jnp.int32, (ROUTER_LANES, ROUTER_LANES), 0)
            c = lax.broadcasted_iota(jnp.int32, (ROUTER_LANES, ROUTER_LANES), 1)
            before = jnp.where(r < c, 1.0, 0.0).astype(BF16)
            hi = padded.astype(BF16)
            mid = (padded - hi.astype(F32)).astype(BF16)
            low = (padded - hi.astype(F32) - mid.astype(F32)).astype(BF16)
            start = (jnp.dot(hi, before, preferred_element_type=F32)
                     + jnp.dot(mid, before, preferred_element_type=F32)
                     + jnp.dot(low, before, preferred_element_type=F32))
            start_scr[...] = start
            seg_end = start[0:1, :] + padded[0:1, :]
            tile_lo = (lax.broadcasted_iota(jnp.int32, (MOE_TILE_LANES, ROUTER_LANES), 0) * MOE_TM).astype(F32)
            e_lane = lax.broadcasted_iota(jnp.int32, (MOE_TILE_LANES, ROUTER_LANES), 1)
            ended = jnp.where((seg_end <= tile_lo) & (e_lane < N_EXPERTS), 1.0, 0.0).astype(BF16)
            tile_ref[...] = lax.dot_general(ones_lanes, ended, NT_DIMS, preferred_element_type=F32)

    @pl.when(phase == 1)
    def _():
        @pl.when(i == 0)
        def _():
            r = lax.broadcasted_iota(jnp.int32, (tm, tm), 0)
            c = lax.broadcasted_iota(jnp.int32, (tm, tm), 1)
            earlier_scr[...] = jnp.where(c < r, 1.0, 0.0).astype(BF16)

        base = (jnp.dot(earlier_scr[...], occ, preferred_element_type=F32)
                + run_scr[0:1, :] + start_scr[0:1, :])
        pos1_ref[...] = _exact_dot_nt(ones_lanes, pick1 * base)
        pos2_ref[...] = _exact_dot_nt(ones_lanes, pick2 * base)
        run_scr[...] += jnp.dot(ones_rows, occ, preferred_element_type=F32)


def _slots(route, tm):
    n = route.shape[0]
    n_tiles = _moe_tiles(n)
    assert n_tiles <= MOE_TILE_LANES and 2 * n + N_EXPERTS * MOE_TM < 2 ** 24
    row_out = pl.BlockSpec((SUBLANES, tm), lambda ph, i: (0, i * ph))
    sds = jax.ShapeDtypeStruct((SUBLANES, n), F32)
    return pl.pallas_call(
        functools.partial(_slots_body, n_tiles),
        grid=(2, n // tm),
        in_specs=[pl.BlockSpec((tm, ROUTER_LANES), lambda ph, i: (i, 0))],
        out_specs=[row_out, row_out, pl.BlockSpec((SUBLANES, MOE_TILE_LANES), lambda ph, i: (0, 0))],
        out_shape=[sds, sds, jax.ShapeDtypeStruct((SUBLANES, MOE_TILE_LANES), F32)],
        scratch_shapes=[pltpu.VMEM((SUBLANES, ROUTER_LANES), F32), pltpu.VMEM((SUBLANES, ROUTER_LANES), F32),
                        pltpu.VMEM((tm, tm), BF16)],
        compiler_params=_cparams(("arbitrary", "arbitrary")),
        name="moe_slots",
    )(route)


def _dispatch_body(n_tiles, pos_ref, tile_ref, xn_ref, xs_hbm, zero_scr, zero_sem, row_sem):
    tm = xn_ref.shape[0] // SUBLANES
    n = pl.num_programs(0) * tm
    base = pl.program_id(0) * tm
    tile_rows = MOE_TM * SUBLANES

    @pl.when(pl.program_id(0) == 0)
    def _():
        zero_scr[...] = jnp.zeros_like(zero_scr)

        def fill_copy(t):
            return pltpu.make_async_copy(zero_scr, xs_hbm.at[pl.ds(t * tile_rows, tile_rows), :], zero_sem)

        def has_padding(t):
            return (tile_ref[t] >= N_EXPERTS) | (tile_ref[t] != tile_ref[t + 1])

        @pl.loop(0, n_tiles)
        def _(t):
            @pl.when(has_padding(t))
            def _():
                fill_copy(t).start()

        @pl.loop(0, n_tiles)
        def _(t):
            @pl.when(has_padding(t))
            def _():
                fill_copy(t).wait()

    def issue(j, carry):
        src = xn_ref.at[pl.ds(pl.multiple_of(j * SUBLANES, SUBLANES), SUBLANES), :]
        for pick in range(2):
            dst = pl.multiple_of(pos_ref[pick * n + base + j], SUBLANES)
            pltpu.make_async_copy(src, xs_hbm.at[pl.ds(dst, SUBLANES), :], row_sem).start(priority=pick)
        return carry

    lax.fori_loop(0, tm, issue, 0, unroll=8)
    for _ in range(2):
        pltpu.make_async_copy(xn_ref, xs_hbm.at[pl.ds(0, tm * SUBLANES), :], row_sem).wait()


def _dispatch(pos, tile_map, xn, tm):
    n = xn.shape[0] // SUBLANES
    n_tiles = _moe_tiles(n)
    return pl.pallas_call(
        functools.partial(_dispatch_body, n_tiles),
        grid_spec=pltpu.PrefetchScalarGridSpec(
            num_scalar_prefetch=2,
            grid=(n // tm,),
            in_specs=[pl.BlockSpec((tm * SUBLANES, LANES), lambda i, p, t: (i, 0))],
            out_specs=pl.BlockSpec(memory_space=pl.ANY),
            scratch_shapes=[pltpu.VMEM((MOE_TM * SUBLANES, LANES), F32), pltpu.SemaphoreType.DMA(()),
                            pltpu.SemaphoreType.DMA(())],
        ),
        out_shape=jax.ShapeDtypeStruct((n_tiles * MOE_TM * SUBLANES, LANES), F32),
        compiler_params=_cparams(("arbitrary",)),
        name="moe_dispatch",
    )(pos, tile_map, xn)


EXPERT_SLOTS = 3


def _experts_body(tile_ref, xs_hbm, wg_ref, wu_ref, wd_ref, ys_ref, xbuf, sems, wg_bf, wu_bf, wd_bf):
    i = pl.program_id(0)
    tile_rows = MOE_TM * SUBLANES
    used = tile_ref[i] < N_EXPERTS

    def tile_copy(t, slot):
        rows = pl.ds(pl.multiple_of(t * tile_rows, tile_rows), tile_rows)
        return pltpu.make_async_copy(xs_hbm.at[rows, :], xbuf.at[slot], sems.at[slot])

    def prefetch(t):
        @pl.when(tile_ref[t] < N_EXPERTS)
        def _():
            tile_copy(t, t % EXPERT_SLOTS).start()

    @pl.when(i == 0)
    def _():
        prefetch(0)
        prefetch(1)

    prefetch(i + 2)

    @pl.when(used & ((i == 0) | (tile_ref[i] != tile_ref[jnp.maximum(i - 1, 0)])))
    def _():
        wg_bf[...] = wg_ref[0].astype(BF16)
        wu_bf[...] = wu_ref[0].astype(BF16)
        wd_bf[...] = wd_ref[0].astype(BF16)

    @pl.when(used)
    def _():
        slot = i % EXPERT_SLOTS
        tile_copy(i, slot).wait()
        x = _load_token_major(xbuf, MOE_TM, (slot,)).astype(BF16)
        gate = jnp.dot(x, wg_bf[...], preferred_element_type=F32)
        up = jnp.dot(x, wu_bf[...], preferred_element_type=F32)
        hid = gate * jax.nn.sigmoid(gate) * up
        _store_token_major(ys_ref, jnp.dot(hid.astype(BF16), wd_bf[...], preferred_element_type=F32))

    @pl.when(jnp.logical_not(used))
    def _():
        ys_ref[...] = jnp.zeros_like(ys_ref)


def _experts(tile_map, xs, w_gate, w_up, w_down, layer):
    d, f = w_gate.shape[1:]
    tile_rows = MOE_TM * SUBLANES
    n_tiles = xs.shape[0] // tile_rows
    y_map = lambda i, tm_ref: (i, 0)
    w_map = lambda i, tm_ref: (layer * N_EXPERTS + jnp.minimum(tm_ref[i], N_EXPERTS - 1), 0, 0)
    return pl.pallas_call(
        _experts_body,
        grid_spec=pltpu.PrefetchScalarGridSpec(
            num_scalar_prefetch=1,
            grid=(n_tiles,),
            in_specs=[pl.BlockSpec(memory_space=pl.ANY), pl.BlockSpec((1, d, f), w_map),
                      pl.BlockSpec((1, d, f), w_map), pl.BlockSpec((1, f, d), w_map)],
            out_specs=pl.BlockSpec((tile_rows, LANES), y_map),
            scratch_shapes=[pltpu.VMEM((EXPERT_SLOTS, tile_rows, LANES), F32),
                            pltpu.SemaphoreType.DMA((EXPERT_SLOTS,)),
                            pltpu.VMEM((d, f), BF16), pltpu.VMEM((d, f), BF16), pltpu.VMEM((f, d), BF16)],
        ),
        out_shape=jax.ShapeDtypeStruct(xs.shape, F32),
        compiler_params=_cparams(("arbitrary",)),
        name="moe_experts",
    )(tile_map, xs, w_gate, w_up, w_down)


def _combine_body(pos_ref, h_ref, route_ref, ys_hbm, o_ref, buf, sems):
    tm = h_ref.shape[0]
    steps = pl.num_programs(0)
    n = steps * tm
    i = pl.program_id(0)

    def start_gather(step, slot):
        base = step * tm

        def issue(g, carry):
            for u in range(SUBLANES):
                for pick in range(2):
                    src = pl.multiple_of(pos_ref[pick * n + base + g * SUBLANES + u], SUBLANES)
                    dst = pl.multiple_of(g * SUBLANES * SUBLANES, SUBLANES) + u * SUBLANES
                    pltpu.make_async_copy(ys_hbm.at[pl.ds(src, SUBLANES), :],
                                          buf.at[slot, pick, pl.ds(dst, SUBLANES), :],
                                          sems.at[slot]).start(priority=pick)
            return carry

        lax.fori_loop(0, tm // SUBLANES, issue, 0)

    @pl.when(i == 0)
    def _():
        start_gather(0, 0)

    @pl.when(i + 1 < steps)
    def _():
        start_gather(i + 1, (i + 1) % 2)

    slot = i % 2
    pltpu.make_async_copy(buf.at[slot], buf.at[slot], sems.at[slot]).wait()
    route = route_ref[...]
    y1 = _load_token_major(buf, tm, (slot, 0))
    y2 = _load_token_major(buf, tm, (slot, 1))
    o_ref[...] = h_ref[...] + route[:, 2:3] * y1 + route[:, 3:4] * y2


def _combine(pos, h, route, ys, tm):
    n, d = h.shape
    row = lambda i, p: (i, 0)
    return pl.pallas_call(
        _combine_body,
        grid_spec=pltpu.PrefetchScalarGridSpec(
            num_scalar_prefetch=1,
            grid=(n // tm,),
            in_specs=[pl.BlockSpec((tm, d), row), pl.BlockSpec((tm, ROUTER_LANES), row),
                      pl.BlockSpec(memory_space=pl.ANY)],
            out_specs=pl.BlockSpec((tm, d), row),
            scratch_shapes=[pltpu.VMEM((2, 2, tm * SUBLANES, LANES), F32), pltpu.SemaphoreType.DMA((2,))],
        ),
        out_shape=jax.ShapeDtypeStruct((n, d), F32),
        compiler_params=_cparams(("arbitrary",)),
        name="moe_combine",
    )(pos, h, route, ys)


def _gelu(x):
    return 0.5 * x * (1.0 + lax.erf(x * (1.0 / math.sqrt(2.0))))


def _in_odd_body(h_ref, g_ref, w_ref, vg_ref, u_ref, vn_ref, q_ref, k_ref, v_ref):
    xn = _rms(h_ref[...], g_ref[...])
    proj = jnp.dot(xn.astype(BF16), w_ref[...], preferred_element_type=F32)
    w = MIX_WIDTH
    u_ref[...] = _gelu(proj[:, :w]).astype(BF16)
    vn_ref[...] = _rms(_gelu(proj[:, w:2 * w]), vg_ref[...]).astype(BF16)
    q_ref[...] = (proj[:, 2 * w:3 * w] * Q_SCALE).astype(BF16)
    k_ref[...] = proj[:, 3 * w:4 * w].astype(BF16)
    v_ref[...] = proj[:, 4 * w:].astype(BF16)


def _in_odd(h, gain, w, v_gain, tm):
    n, d = h.shape
    row = lambda i: (i, 0)
    const2 = lambda i: (0, 0)
    sds = jax.ShapeDtypeStruct((n, MIX_WIDTH), BF16)
    return pl.pallas_call(
        _in_odd_body,
        grid=(n // tm,),
        in_specs=[pl.BlockSpec((tm, d), row), pl.BlockSpec((1, d), const2),
                  pl.BlockSpec(w.shape, const2), pl.BlockSpec((1, MIX_WIDTH), const2)],
        out_specs=[pl.BlockSpec((tm, MIX_WIDTH), row)] * 5,
        out_shape=[sds] * 5,
        compiler_params=_cparams(("arbitrary",)),
        name="in_odd",
    )(h, gain, w, v_gain)


SB_TQ = 64
SB_KB = 128
SB_NB = 2
SB_TK = SB_NB * SB_KB
SB_QBLK = 1024
SB_GROUP = 16
SB_UNDERFLOW = -150.0


def _sb_body(q_ref, k_ref, v_ref, o_ref, acc_scr, run_scr):
    qi = pl.program_id(2)
    lane = lax.broadcasted_iota(jnp.int32, (SB_TQ, LANES), 1)
    row = lax.broadcasted_iota(jnp.int32, (2 * SB_TQ, 1), 0) % SB_TQ
    col = lax.broadcasted_iota(jnp.int32, (2 * SB_TQ, SB_TK), 1)
    rr = lax.broadcasted_iota(jnp.int32, (2 * SB_KB, 2 * SB_KB), 0) % SB_KB
    cc = lax.broadcasted_iota(jnp.int32, (2 * SB_KB, 2 * SB_KB), 1)
    suffix = jnp.where((cc >= SB_KB) | (rr > cc), 1.0, 0.0).astype(BF16)

    def suffix_sums(x):
        hi = x.astype(BF16)
        lo = (x - hi.astype(F32)).astype(BF16)
        return jnp.dot(jnp.concatenate([hi, lo], axis=1), suffix, preferred_element_type=F32)

    def subtile_group(grp, _):
        q_los = [pl.multiple_of((grp * SB_GROUP + s) * SB_TQ, SB_TQ) for s in range(SB_GROUP)]
        q_starts = [qi * SB_QBLK + q_lo for q_lo in q_los]
        q_heads = []
        for q_lo in q_los:
            q = q_ref[0, pl.ds(q_lo, SB_TQ), :]
            zero = jnp.zeros_like(q)
            q_heads.append(jnp.concatenate(
                [jnp.where(lane < HEAD_DIM, q, zero), jnp.where(lane < HEAD_DIM, zero, q)], axis=0))
        acc_scr[...] = jnp.zeros_like(acc_scr)
        run_scr[...] = jnp.zeros_like(run_scr)

        def cond(carry):
            his, dones = carry
            active = [(hi > 0) & (done == 0) for hi, done in zip(his, dones)]
            return functools.reduce(jnp.logical_or, active)

        def body(carry):
            his, _ = carry
            group = range(SB_GROUP)
            kss = [pl.multiple_of(jnp.maximum(his[s] - SB_TK, 0), SB_TQ) for s in group]
            valid = [col < (jnp.minimum(row + q_starts[s], his[s]) - kss[s]) for s in group]
            zs = [lax.dot_general(q_heads[s], k_ref[0, pl.ds(kss[s], SB_TK), :], NT_DIMS,
                                  preferred_element_type=F32) for s in group]
            log_beta, log_rest = [], []
            for s in group:
                z = jnp.where(valid[s], zs[s], NEG_BIG)
                sp = jnp.log2(1.0 + jnp.exp2(-jnp.abs(z)))
                log_beta.append(jnp.minimum(z, 0.0) - sp)
                log_rest.append(log_beta[s] - z)
            sums = [[suffix_sums(log_rest[s][:, blk * SB_KB:(blk + 1) * SB_KB]) for blk in range(SB_NB)]
                    for s in group]
            dones = []
            for s in group:
                run = run_scr[s]
                pieces = [None] * SB_NB
                for blk in reversed(range(SB_NB)):
                    tt = sums[s][blk]
                    pieces[blk] = jnp.exp2(log_beta[s][:, blk * SB_KB:(blk + 1) * SB_KB] + tt[:, :SB_KB] + run)
                    run = run + tt[:, SB_KB:]
                att = jnp.concatenate(pieces, axis=1)
                acc_scr[s] += jnp.dot(att.astype(BF16), v_ref[0, pl.ds(kss[s], SB_TK), :],
                                      preferred_element_type=F32)
                run_scr[s] = run
                dones.append((jnp.max(run) <= SB_UNDERFLOW).astype(jnp.int32))
            return tuple(kss), tuple(dones)

        lax.while_loop(cond, body, (tuple(qs + SB_TQ for qs in q_starts),
                                    tuple(jnp.int32(0) for _ in range(SB_GROUP))))
        for s in range(SB_GROUP):
            o_ref[0, pl.ds(q_los[s], SB_TQ), :] = jnp.where(
                lane < HEAD_DIM, acc_scr[s, :SB_TQ], acc_scr[s, SB_TQ:]).astype(BF16)
        return 0

    lax.fori_loop(0, SB_QBLK // (SB_TQ * SB_GROUP), subtile_group, 0)


def _sb_attention(q, k, v):
    b, s, w = q.shape
    assert s % SB_QBLK == 0 and s >= SB_TK
    qspec = pl.BlockSpec((1, SB_QBLK, LANES), lambda bi, hp, i: (bi, i, hp))
    kvspec = pl.BlockSpec((1, s, LANES), lambda bi, hp, i: (bi, 0, hp))
    return pl.pallas_call(
        _sb_body,
        grid=(b, w // LANES, s // SB_QBLK),
        in_specs=[qspec, kvspec, kvspec],
        out_specs=qspec,
        out_shape=jax.ShapeDtypeStruct((b, s, w), BF16),
        scratch_shapes=[pltpu.VMEM((SB_GROUP, 2 * SB_TQ, LANES), F32),
                        pltpu.VMEM((SB_GROUP, 2 * SB_TQ, LANES), F32)],
        compiler_params=_cparams(("arbitrary", "arbitrary", "arbitrary")),
        name="sb_attention",
    )(q, k, v)


def _out_odd_body(u_ref, vn_ref, yd_ref, h_ref, ws_ref, bs_ref, wo_ref, fg_ref, wr_ref,
                  o_ref, xn_ref, route_ref, yc_scr):
    tm = u_ref.shape[0]
    r = lax.broadcasted_iota(jnp.int32, (SGU_BLOCK, SGU_BLOCK), 0)
    c = lax.broadcasted_iota(jnp.int32, (SGU_BLOCK, SGU_BLOCK), 1)
    for g in range(MIX_WIDTH // LANES):
        ls = slice(g * LANES, (g + 1) * LANES)
        ws = jnp.where(c <= r, ws_ref[g], jnp.zeros_like(ws_ref[g]))
        for blk in range(tm // SGU_BLOCK):
            rs = slice(blk * SGU_BLOCK, (blk + 1) * SGU_BLOCK)
            mixed = jnp.dot(ws, vn_ref[rs, ls], preferred_element_type=F32) + bs_ref[g]
            yc_scr[rs, ls] = (u_ref[rs, ls].astype(F32) * mixed).astype(BF16)
    mixed = jnp.concatenate([yc_scr[...], yd_ref[...]], axis=1)
    _project_and_route(h_ref, mixed, wo_ref, fg_ref, wr_ref, o_ref, xn_ref, route_ref)


def _out_odd(u, vn, yd, h, sgu_w, sgu_b, w_out, ffn_gain, w_router, tm):
    n, d = h.shape
    w = MIX_WIDTH
    row = lambda i: (i, 0)
    const2 = lambda i: (0, 0)
    const3 = lambda i: (0, 0, 0)
    return pl.pallas_call(
        _out_odd_body,
        grid=(n // tm,),
        in_specs=[pl.BlockSpec((tm, w), row), pl.BlockSpec((tm, w), row), pl.BlockSpec((tm, w), row),
                  pl.BlockSpec((tm, d), row), pl.BlockSpec(sgu_w.shape, const3),
                  pl.BlockSpec(sgu_b.shape, const3), pl.BlockSpec(w_out.shape, const2),
                  pl.BlockSpec((1, d), const2), pl.BlockSpec(w_router.shape, const2)],
        out_specs=[pl.BlockSpec((tm, d), row), pl.BlockSpec((tm * SUBLANES, LANES), row),
                   pl.BlockSpec((tm, ROUTER_LANES), row)],
        out_shape=[jax.ShapeDtypeStruct((n, d), F32), jax.ShapeDtypeStruct((n * SUBLANES, LANES), F32),
                   jax.ShapeDtypeStruct((n, ROUTER_LANES), F32)],
        scratch_shapes=[pltpu.VMEM((tm, w), BF16)],
        compiler_params=_cparams(("arbitrary",)),
        name="out_odd",
    )(u, vn, yd, h, sgu_w, sgu_b, w_out, ffn_gain, w_router)


def _moe_layer(h, xn, route, w_gate, w_up, w_down, layer):
    pos1, pos2, tile_map = _slots(route, tm=1024)
    pos = jnp.concatenate([pos1[0], pos2[0]]).astype(jnp.int32) * SUBLANES
    tile_map = tile_map[0].astype(jnp.int32)
    xs = _dispatch(pos, tile_map, xn, tm=1024)
    ys = _experts(tile_map, xs, w_gate, w_up, w_down, layer)
    return _combine(pos, h, route, ys, tm=256)


def kernel(x, mix_norm_even, w_in_even, att_q_norm, att_k_norm, att_rel_bias, pool_w, pool_scale,
           w_out_even, mix_norm_odd, w_in_odd, sgu_v_norm, sgu_w, sgu_b, w_out_odd, ffn_norm,
           w_router_group, w_router_expert, w_exp_gate, w_exp_up, w_exp_down):
    b, s, d = x.shape
    n = b * s
    depth = ffn_norm.shape[0]
    heads = MIX_WIDTH // HEAD_DIM
    h = x.reshape(n, d)
    w_gate = w_exp_gate.reshape(depth * N_EXPERTS, d, EXPERT_FF)
    w_up = w_exp_up.reshape(depth * N_EXPERTS, d, EXPERT_FF)
    w_down = w_exp_down.reshape(depth * N_EXPERTS, EXPERT_FF, d)
    for layer in range(depth):
        i = layer // 2
        ffn_gain = ffn_norm[layer][None, :]
        w_router = _router_weights(w_router_group[layer], w_router_expert[layer])
        if layer % 2 == 0:
            q, k_pad, v_pad, p = _in_even(
                h.reshape(b, s, d), mix_norm_even[i][None, :], w_in_even[i].astype(BF16),
                jnp.tile(att_q_norm[i], heads)[None, :], jnp.tile(att_k_norm[i], heads)[None, :],
                tm=ATT_LEFT)
            ya = _band_attention(q, k_pad, v_pad, _band_bias(att_rel_bias[i]))
            h, xn, route = _out_even(ya.reshape(n, MIX_WIDTH), p.reshape(n, MIX_WIDTH), h,
                                     pool_w[i].astype(BF16), pool_scale[i][None, :],
                                     w_out_even[i].astype(BF16), ffn_gain, w_router, seq=s, tm=512)
        else:
            u, vn, q, k, v = _in_odd(h, mix_norm_odd[i][None, :], w_in_odd[i].astype(BF16),
                                     sgu_v_norm[i][None, :], tm=512)
            to3 = lambda t: t.reshape(b, s, MIX_WIDTH)
            yd = _sb_attention(to3(q), to3(k), to3(v))
            bias = jnp.broadcast_to(sgu_b[i][:, :, None], (N_GROUPS, SGU_BLOCK, LANES))
            h, xn, route = _out_odd(u, vn, yd.reshape(n, MIX_WIDTH), h, sgu_w[i].astype(BF16), bias,
                                    w_out_odd[i].astype(BF16), ffn_gain, w_router, tm=512)
        h = _moe_layer(h, xn, route, w_gate, w_up, w_down, layer)
    return h.reshape(b, s, d)
```

```python
import functools
import math

import jax
import jax.numpy as jnp
from jax import lax
from jax.experimental import pallas as pl
from jax.experimental.pallas import tpu as pltpu

F32 = jnp.float32
BF16 = jnp.bfloat16

CHUNK = 64
EPS = 1e-6
HEAD_DIM = 64
MIX_WIDTH = 512
LANES = 128
SUBLANES = 8
ATT_LEFT = 8 * CHUNK
ATT_MAX_REL = 128
POOL_WINDOWS = (2, 4, 8, 16)
POOL_HALO = 16
SGU_BLOCK = 128
N_GROUPS = 4
N_EXP_PER_GROUP = 8
N_EXPERTS = N_GROUPS * N_EXP_PER_GROUP
EXPERT_FF = 256
ROUTER_LANES = 128
ROUTER_ROWS = 40
NEG_BIG = -1e30
VMEM_LIMIT = 56 * 1024 * 1024

PROJ_TM = 512
SLOTS_TM = 1024
DISPATCH_TM = 1024
COMBINE_TM = 256

NT_DIMS = (((1,), (1,)), ((), ()))
LOG2E = math.log2(math.e)
Q_SCALE = LOG2E / math.sqrt(HEAD_DIM)


def _cparams(sem):
    return pltpu.CompilerParams(dimension_semantics=sem, vmem_limit_bytes=VMEM_LIMIT)


def _store_token_major(ref, x, first_token=0):
    rows = x.shape[0]
    for s in range(SUBLANES):
        ref[pl.ds(first_token * SUBLANES + s, rows, stride=SUBLANES), :] = x[:, s * LANES:(s + 1) * LANES]


def _load_token_major(ref, rows, lead=()):
    return jnp.concatenate(
        [ref[lead + (pl.ds(s, rows, stride=SUBLANES), slice(None))] for s in range(SUBLANES)], axis=1)


def _rms(x, gain):
    return x * lax.rsqrt(jnp.mean(x * x, axis=-1, keepdims=True) + EPS) * gain


def _split_dot(x, m):
    hi = x.astype(BF16)
    lo = (x - hi.astype(F32)).astype(BF16)
    return (jnp.dot(hi, m, preferred_element_type=F32)
            + jnp.dot(lo, m, preferred_element_type=F32))


def _head_rms(t, gain):
    n = t.shape[-1]
    r = lax.broadcasted_iota(jnp.int32, (n, n), 0) // HEAD_DIM
    c = lax.broadcasted_iota(jnp.int32, (n, n), 1) // HEAD_DIM
    bd = jnp.where(r == c, 1.0, 0.0).astype(BF16)
    ms = _split_dot(t * t, bd) * (1.0 / HEAD_DIM)
    return t * lax.rsqrt(ms + EPS) * gain


def _in_even_body(h_ref, g_ref, w_ref, qg_ref, kg_ref, q_ref, k_ref, v_ref, p_ref):
    j = pl.program_id(1)

    @pl.when(j == 0)
    def _():
        k_ref[...] = jnp.zeros_like(k_ref)
        v_ref[...] = jnp.zeros_like(v_ref)

    @pl.when(j > 0)
    def _():
        xn = _rms(h_ref[0], g_ref[...])
        proj = jnp.dot(xn.astype(BF16), w_ref[...], preferred_element_type=F32)
        w = MIX_WIDTH
        q_ref[0] = (_head_rms(proj[:, :w], qg_ref[...]) * Q_SCALE).astype(BF16)
        k_ref[0] = _head_rms(proj[:, w:2 * w], kg_ref[...]).astype(BF16)
        v_ref[0] = proj[:, 2 * w:3 * w].astype(BF16)
        p_ref[0] = proj[:, 3 * w:].astype(BF16)


def _in_even(h, gain, w, q_gain, k_gain, tm):
    b, s, d = h.shape
    assert tm == ATT_LEFT and s % tm == 0
    nt = s // tm
    cur = lambda bi, j: (bi, jnp.maximum(j - 1, 0), 0)
    const = lambda bi, j: (0, 0)
    out_sds = lambda rows: jax.ShapeDtypeStruct((b, rows, MIX_WIDTH), BF16)
    return pl.pallas_call(
        _in_even_body,
        grid=(b, nt + 1),
        in_specs=[
            pl.BlockSpec((1, tm, d), cur),
            pl.BlockSpec((1, d), const),
            pl.BlockSpec(w.shape, const),
            pl.BlockSpec((1, MIX_WIDTH), const),
            pl.BlockSpec((1, MIX_WIDTH), const),
        ],
        out_specs=[
            pl.BlockSpec((1, tm, MIX_WIDTH), cur),
            pl.BlockSpec((1, tm, MIX_WIDTH), lambda bi, j: (bi, j, 0)),
            pl.BlockSpec((1, tm, MIX_WIDTH), lambda bi, j: (bi, j, 0)),
            pl.BlockSpec((1, tm, MIX_WIDTH), cur),
        ],
        out_shape=[out_sds(s), out_sds(s + ATT_LEFT), out_sds(s + ATT_LEFT), out_sds(s)],
        compiler_params=_cparams(("arbitrary", "arbitrary")),
        name="in_even",
    )(h, gain, w, q_gain, k_gain)


BAND_TQ = 2 * CHUNK
BAND_TK = BAND_TQ + ATT_LEFT


BAND_STEP_TILES = 4
BAND_STAGED = 2


def _band_body(q_ref, k_ref, v_ref, bias_ref, o_ref):
    @pl.loop(0, BAND_STEP_TILES)
    def _(t):
        rows = pl.ds(pl.multiple_of(t * BAND_TQ, BAND_TQ), BAND_TQ)
        _band_tile(pl.program_id(1) * BAND_STEP_TILES + t, q_ref.at[0, rows, :], k_ref, v_ref, bias_ref,
                   o_ref.at[0, rows, :])


def _band_tile(i, q_ref, k_ref, v_ref, bias_ref, o_ref):
    start = pl.multiple_of(i * BAND_TQ, BAND_TQ)
    lane = lax.broadcasted_iota(jnp.int32, (BAND_TQ, LANES), 1)
    col = lax.broadcasted_iota(jnp.int32, (2 * BAND_TQ, BAND_TK), 1)
    is_pad = (col + start) < ATT_LEFT
    n_pairs = MIX_WIDTH // LANES
    lanes = [slice(hp * LANES, (hp + 1) * LANES) for hp in range(n_pairs)]
    for first in range(0, n_pairs, BAND_STAGED):
        pairs = range(first, first + BAND_STAGED)
        scores, probs, denoms = {}, {}, {}
        for hp in pairs:
            q = q_ref[:, lanes[hp]]
            kb = k_ref[0, pl.ds(start, BAND_TK), lanes[hp]]
            zero = jnp.zeros_like(q)
            q2 = jnp.concatenate([jnp.where(lane < HEAD_DIM, q, zero), jnp.where(lane < HEAD_DIM, zero, q)],
                                 axis=0)
            scores[hp] = lax.dot_general(q2, kb, NT_DIMS, preferred_element_type=F32)
        for hp in pairs:
            bias = bias_ref[2 * hp:2 * hp + 2].reshape(2 * BAND_TQ, BAND_TK)
            s = jnp.where(is_pad, NEG_BIG, scores[hp] + bias)
            p = jnp.exp2(s - jnp.max(s, axis=-1, keepdims=True))
            denoms[hp] = jnp.sum(p, axis=-1, keepdims=True)
            probs[hp] = p.astype(BF16)
        for hp in pairs:
            vb = v_ref[0, pl.ds(start, BAND_TK), lanes[hp]]
            o = jnp.dot(probs[hp], vb, preferred_element_type=F32) / denoms[hp]
            o_ref[:, lanes[hp]] = jnp.where(lane < HEAD_DIM, o[:BAND_TQ], o[BAND_TQ:]).astype(BF16)


def _band_bias(rel_bias):
    heads = rel_bias.shape[0]
    r = jnp.arange(BAND_TQ)[:, None]
    j = jnp.arange(BAND_TK)[None, :]
    jb = j - CHUNK * (r // CHUNK)
    in_band = (jb >= 0) & (jb < ATT_LEFT + CHUNK)
    period = BAND_TK + BAND_TQ
    far = jnp.broadcast_to(rel_bias[:, 2 * ATT_MAX_REL:], (heads, ATT_LEFT - ATT_MAX_REL + 1))
    near = rel_bias[:, 2 * ATT_MAX_REL - 1:0:-1]
    wrap = jnp.broadcast_to(rel_bias[:, 2 * ATT_MAX_REL:], (heads, period - BAND_TK))
    g = jnp.concatenate([far, near, wrap], axis=1).astype(F32)
    assert g.shape[1] == period
    toep = jnp.tile(g, (1, BAND_TQ))[:, :BAND_TQ * (period - 1)].reshape(heads, BAND_TQ, period - 1)
    return jnp.where(in_band[None], toep[:, :, :BAND_TK] * LOG2E, NEG_BIG)


def _band_attention(q, k_pad, v_pad, bias):
    b, s, w = q.shape
    sp = k_pad.shape[1]
    step_rows = BAND_STEP_TILES * BAND_TQ
    return pl.pallas_call(
        _band_body,
        grid=(b, s // step_rows),
        in_specs=[
            pl.BlockSpec((1, step_rows, w), lambda bi, i: (bi, i, 0)),
            pl.BlockSpec((1, sp, w), lambda bi, i: (bi, 0, 0)),
            pl.BlockSpec((1, sp, w), lambda bi, i: (bi, 0, 0)),
            pl.BlockSpec(bias.shape, lambda bi, i: (0, 0, 0)),
        ],
        out_specs=pl.BlockSpec((1, step_rows, w), lambda bi, i: (bi, i, 0)),
        out_shape=jax.ShapeDtypeStruct((b, s, w), BF16),
        compiler_params=_cparams(("arbitrary", "arbitrary")),
        name="band_attention",
    )(q, k_pad, v_pad, bias)


def _route_tokens(h, gain, w_router):
    xn = _rms(h, gain)
    x_hi = xn.astype(BF16)
    x_lo = (xn - x_hi.astype(F32)).astype(BF16)
    w_hi = w_router.astype(BF16)
    w_lo = (w_router - w_hi.astype(F32)).astype(BF16)
    logits = (jnp.dot(x_hi, w_hi, preferred_element_type=F32)
              + jnp.dot(x_lo, w_hi, preferred_element_type=F32)
              + jnp.dot(x_hi, w_lo, preferred_element_type=F32))
    lt = logits.T[:ROUTER_ROWS]
    sub = lax.broadcasted_iota(jnp.int32, lt.shape, 0).astype(F32)
    ninf = -jnp.inf

    def top(vals):
        m = jnp.max(vals, axis=0, keepdims=True)
        idx = jnp.min(jnp.where(vals == m, sub, float(ROUTER_LANES)), axis=0, keepdims=True)
        return m, idx

    is_group = sub < N_GROUPS
    g_max, g_sel = top(jnp.where(is_group, lt, ninf))
    g_den = jnp.sum(jnp.where(is_group, jnp.exp(lt - g_max), 0.0), axis=0, keepdims=True)
    g_weight = 1.0 / g_den
    lo = N_GROUPS + N_EXP_PER_GROUP * g_sel
    e_logits = jnp.where((sub >= lo) & (sub < lo + N_EXP_PER_GROUP), lt, ninf)
    e1, i1 = top(e_logits)
    e2, i2 = top(jnp.where(sub == i1, ninf, e_logits))
    t = jnp.exp(e2 - e1)
    w1 = g_weight / (1.0 + t)
    w2 = g_weight * t / (1.0 + t)
    rows = lax.broadcasted_iota(jnp.int32, (ROUTER_LANES, lt.shape[1]), 0)
    route_t = jnp.where(rows == 0, i1 - N_GROUPS, jnp.where(rows == 1, i2 - N_GROUPS, 0.0))
    route_t = jnp.where(rows == 2, w1, jnp.where(rows == 3, w2, route_t))
    return xn, route_t.T


def _router_weights(w_rg, w_re):
    pad = jnp.zeros((w_rg.shape[0], ROUTER_LANES - N_GROUPS - N_EXPERTS), F32)
    return jnp.concatenate([w_rg, w_re, pad], axis=1)


def _project_and_route(h_ref, mixed, wo_ref, fg_ref, wr_ref, o_ref, xn_ref, route_ref):
    rows = h_ref.shape[0]
    halves = [slice(0, rows // 2), slice(rows // 2, rows)]
    h_new = [h_ref[rs, :] + jnp.dot(mixed[rs], wo_ref[...], preferred_element_type=F32) for rs in halves]
    for rs, hn in zip(halves, h_new):
        o_ref[rs, :] = hn
    for rs, hn in zip(halves, h_new):
        xn, route_ref[rs, :] = _route_tokens(hn, fg_ref[...], wr_ref[...])
        _store_token_major(xn_ref, xn, first_token=rs.start)


def _out_even_body(tiles_per_seq, ya_ref, p_ref, halo_ref, h_ref, pw_ref, ps_ref, wo_ref, fg_ref, wr_ref,
                   o_ref, xn_ref, route_ref, p_scr, yb_scr):
    tm = p_ref.shape[0]
    it = pl.program_id(0) % tiles_per_seq
    halo = halo_ref[...].astype(F32)
    p_scr[0:POOL_HALO, :] = jnp.where(it == 0, jnp.zeros_like(halo), halo)
    p_scr[POOL_HALO:, :] = p_ref[...].astype(F32)
    t = it * tm + lax.broadcasted_iota(jnp.int32, (tm, 1), 0)
    for g, win in enumerate(POOL_WINDOWS):
        ls = slice(g * LANES, (g + 1) * LANES)
        cur = p_scr[POOL_HALO:POOL_HALO + tm, ls]
        acc = cur
        for dlt in range(1, win):
            acc = acc + p_scr[POOL_HALO - dlt:POOL_HALO - dlt + tm, ls]
        cnt = jnp.minimum(t + 1, win).astype(F32)
        mixed = acc / cnt - cur
        yb = jnp.dot(mixed.astype(BF16), pw_ref[g], preferred_element_type=F32) * ps_ref[:, ls]
        yb_scr[:, ls] = yb.astype(BF16)
    mixed = jnp.concatenate([ya_ref[...], yb_scr[...]], axis=1)
    _project_and_route(h_ref, mixed, wo_ref, fg_ref, wr_ref, o_ref, xn_ref, route_ref)


def _out_even(ya, p, h, pool_w, pool_scale, w_out, ffn_gain, w_router, seq, tm):
    n, d = h.shape
    w = MIX_WIDTH
    row = lambda i: (i, 0)
    const2 = lambda i: (0, 0)
    halo_blocks = tm // POOL_HALO
    return pl.pallas_call(
        functools.partial(_out_even_body, seq // tm),
        grid=(n // tm,),
        in_specs=[
            pl.BlockSpec((tm, w), row),
            pl.BlockSpec((tm, w), row),
            pl.BlockSpec((POOL_HALO, w), lambda i: (jnp.maximum(i * halo_blocks - 1, 0), 0)),
            pl.BlockSpec((tm, d), row),
            pl.BlockSpec(pool_w.shape, lambda i: (0, 0, 0)),
            pl.BlockSpec((1, w), const2),
            pl.BlockSpec(w_out.shape, const2),
            pl.BlockSpec((1, d), const2),
            pl.BlockSpec(w_router.shape, const2),
        ],
        out_specs=[pl.BlockSpec((tm, d), row), pl.BlockSpec((tm * SUBLANES, LANES), row),
                   pl.BlockSpec((tm, ROUTER_LANES), row)],
        out_shape=[jax.ShapeDtypeStruct((n, d), F32), jax.ShapeDtypeStruct((n * SUBLANES, LANES), F32),
                   jax.ShapeDtypeStruct((n, ROUTER_LANES), F32)],
        scratch_shapes=[pltpu.VMEM((tm + POOL_HALO, w), F32), pltpu.VMEM((tm, w), BF16)],
        compiler_params=_cparams(("arbitrary",)),
        name="out_even",
    )(ya, p, p, h, pool_w, pool_scale, w_out, ffn_gain, w_router)


MOE_TM = 512
MOE_TILE_LANES = 256


def _moe_tiles(n):
    return (2 * n) // MOE_TM + N_EXPERTS


def _exact_dot_nt(ones, x):
    out = None
    for _ in range(3):
        part = x.astype(BF16)
        x = x - part.astype(F32)
        term = lax.dot_general(ones, part, NT_DIMS, preferred_element_type=F32)
        out = term if out is None else out + term
    return out


def _slots_body(route_ref, pos1_ref, pos2_ref, tile_ref, run_scr, start_scr, earlier_scr):
    phase = pl.program_id(0)
    i = pl.program_id(1)
    tm = route_ref.shape[0]
    route = route_ref[...]
    lane = lax.broadcasted_iota(jnp.int32, (tm, ROUTER_LANES), 1).astype(F32)
    pick1 = jnp.where(lane == route[:, 0:1], 1.0, 0.0)
    pick2 = jnp.where(lane == route[:, 1:2], 1.0, 0.0)
    occ = (pick1 + pick2).astype(BF16)
    ones_rows = jnp.ones((SUBLANES, tm), BF16)
    ones_lanes = jnp.ones((SUBLANES, ROUTER_LANES), BF16)

    @pl.when(i == 0)
    def _():
        run_scr[...] = jnp.zeros_like(run_scr)

    @pl.when(phase == 0)
    def _():
        run_scr[...] += jnp.dot(ones_rows, occ, preferred_element_type=F32)

        @pl.when(i == pl.num_programs(1) - 1)
        def _():
            padded = jnp.floor((run_scr[...] + (MOE_TM - 1)) * (1.0 / MOE_TM)) * MOE_TM
            r = lax.broadcasted_iota(jnp.int32, (ROUTER_LANES, ROUTER_LANES), 0)
            c = lax.broadcasted_iota(jnp.int32, (ROUTER_LANES, ROUTER_LANES), 1)
            before = jnp.where(r < c, 1.0, 0.0).astype(BF16)
            hi = padded.astype(BF16)
            mid = (padded - hi.astype(F32)).astype(BF16)
            low = (padded - hi.astype(F32) - mid.astype(F32)).astype(BF16)
            start = (jnp.dot(hi, before, preferred_element_type=F32)
                     + jnp.dot(mid, before, preferred_element_type=F32)
                     + jnp.dot(low, before, preferred_element_type=F32))
            start_scr[...] = start
            seg_end = start[0:1, :] + padded[0:1, :]
            tile_lo = (lax.broadcasted_iota(jnp.int32, (MOE_TILE_LANES, ROUTER_LANES), 0) * MOE_TM).astype(F32)
            e_lane = lax.broadcasted_iota(jnp.int32, (MOE_TILE_LANES, ROUTER_LANES), 1)
            ended = jnp.where((seg_end <= tile_lo) & (e_lane < N_EXPERTS), 1.0, 0.0).astype(BF16)
            tile_ref[...] = lax.dot_general(ones_lanes, ended, NT_DIMS, preferred_element_type=F32)

    @pl.when(phase == 1)
    def _():
        @pl.when(i == 0)
        def _():
            r = lax.broadcasted_iota(jnp.int32, (tm, tm), 0)
            c = lax.broadcasted_iota(jnp.int32, (tm, tm), 1)
            earlier_scr[...] = jnp.where(c < r, 1.0, 0.0).astype(BF16)

        base = (jnp.dot(earlier_scr[...], occ, preferred_element_type=F32)
                + run_scr[0:1, :] + start_scr[0:1, :])
        pos1_ref[...] = _exact_dot_nt(ones_lanes, pick1 * base)
        pos2_ref[...] = _exact_dot_nt(ones_lanes, pick2 * base)
        run_scr[...] += jnp.dot(ones_rows, occ, preferred_element_type=F32)


def _slots(route, tm):
    n = route.shape[0]
    assert _moe_tiles(n) <= MOE_TILE_LANES and 2 * n + N_EXPERTS * MOE_TM < 2 ** 24
    row_out = pl.BlockSpec((SUBLANES, tm), lambda ph, i: (0, i * ph))
    sds = jax.ShapeDtypeStruct((SUBLANES, n), F32)
    return pl.pallas_call(
        _slots_body,
        grid=(2, n // tm),
        in_specs=[pl.BlockSpec((tm, ROUTER_LANES), lambda ph, i: (i, 0))],
        out_specs=[row_out, row_out, pl.BlockSpec((SUBLANES, MOE_TILE_LANES), lambda ph, i: (0, 0))],
        out_shape=[sds, sds, jax.ShapeDtypeStruct((SUBLANES, MOE_TILE_LANES), F32)],
        scratch_shapes=[pltpu.VMEM((SUBLANES, ROUTER_LANES), F32), pltpu.VMEM((SUBLANES, ROUTER_LANES), F32),
                        pltpu.VMEM((tm, tm), BF16)],
        compiler_params=_cparams(("arbitrary", "arbitrary")),
        name="moe_slots",
    )(route)


def _dispatch_body(n_tiles, pos_ref, tile_ref, xn_ref, xs_hbm, zero_scr, zero_sem, row_sem):
    tm = xn_ref.shape[0] // SUBLANES
    n = pl.num_programs(0) * tm
    base = pl.program_id(0) * tm
    tile_rows = MOE_TM * SUBLANES

    @pl.when(pl.program_id(0) == 0)
    def _():
        zero_scr[...] = jnp.zeros_like(zero_scr)

        def fill_copy(t):
            return pltpu.make_async_copy(zero_scr, xs_hbm.at[pl.ds(t * tile_rows, tile_rows), :], zero_sem)

        def has_padding(t):
            return (tile_ref[t] >= N_EXPERTS) | (tile_ref[t] != tile_ref[t + 1])

        @pl.loop(0, n_tiles)
        def _(t):
            @pl.when(has_padding(t))
            def _():
                fill_copy(t).start()

        @pl.loop(0, n_tiles)
        def _(t):
            @pl.when(has_padding(t))
            def _():
                fill_copy(t).wait()

    def issue(j, carry):
        src = xn_ref.at[pl.ds(pl.multiple_of(j * SUBLANES, SUBLANES), SUBLANES), :]
        for pick in range(2):
            dst = pl.multiple_of(pos_ref[pick * n + base + j], SUBLANES)
            pltpu.make_async_copy(src, xs_hbm.at[pl.ds(dst, SUBLANES), :], row_sem).start(priority=pick)
        return carry

    lax.fori_loop(0, tm, issue, 0, unroll=8)
    for _ in range(2):
        pltpu.make_async_copy(xn_ref, xs_hbm.at[pl.ds(0, tm * SUBLANES), :], row_sem).wait()


def _dispatch(pos, tile_map, xn, tm):
    n = xn.shape[0] // SUBLANES
    n_tiles = _moe_tiles(n)
    return pl.pallas_call(
        functools.partial(_dispatch_body, n_tiles),
        grid_spec=pltpu.PrefetchScalarGridSpec(
            num_scalar_prefetch=2,
            grid=(n // tm,),
            in_specs=[pl.BlockSpec((tm * SUBLANES, LANES), lambda i, p, t: (i, 0))],
            out_specs=pl.BlockSpec(memory_space=pl.ANY),
            scratch_shapes=[pltpu.VMEM((MOE_TM * SUBLANES, LANES), F32), pltpu.SemaphoreType.DMA(()),
                            pltpu.SemaphoreType.DMA(())],
        ),
        out_shape=jax.ShapeDtypeStruct((n_tiles * MOE_TM * SUBLANES, LANES), F32),
        compiler_params=_cparams(("arbitrary",)),
        name="moe_dispatch",
    )(pos, tile_map, xn)


EXPERT_SLOTS = 3


def _experts_body(tile_ref, xs_hbm, wg_ref, wu_ref, wd_ref, ys_ref, xbuf, sems, wg_bf, wu_bf, wd_bf):
    i = pl.program_id(0)
    tile_rows = MOE_TM * SUBLANES
    used = tile_ref[i] < N_EXPERTS

    def tile_copy(t, slot):
        rows = pl.ds(pl.multiple_of(t * tile_rows, tile_rows), tile_rows)
        return pltpu.make_async_copy(xs_hbm.at[rows, :], xbuf.at[slot], sems.at[slot])

    def prefetch(t):
        @pl.when(tile_ref[t] < N_EXPERTS)
        def _():
            tile_copy(t, t % EXPERT_SLOTS).start()

    @pl.when(i == 0)
    def _():
        prefetch(0)
        prefetch(1)

    prefetch(i + 2)

    @pl.when(used & ((i == 0) | (tile_ref[i] != tile_ref[jnp.maximum(i - 1, 0)])))
    def _():
        wg_bf[...] = wg_ref[0].astype(BF16)
        wu_bf[...] = wu_ref[0].astype(BF16)
        wd_bf[...] = wd_ref[0].astype(BF16)

    @pl.when(used)
    def _():
        slot = i % EXPERT_SLOTS
        tile_copy(i, slot).wait()
        x = _load_token_major(xbuf, MOE_TM, (slot,)).astype(BF16)
        gate = jnp.dot(x, wg_bf[...], preferred_element_type=F32)
        up = jnp.dot(x, wu_bf[...], preferred_element_type=F32)
        hid = gate * jax.nn.sigmoid(gate) * up
        _store_token_major(ys_ref, jnp.dot(hid.astype(BF16), wd_bf[...], preferred_element_type=F32))

    @pl.when(jnp.logical_not(used))
    def _():
        ys_ref[...] = jnp.zeros_like(ys_ref)


def _experts(tile_map, xs, w_gate, w_up, w_down, layer):
    d, f = w_gate.shape[1:]
    tile_rows = MOE_TM * SUBLANES
    n_tiles = xs.shape[0] // tile_rows
    y_map = lambda i, tm_ref: (i, 0)
    w_map = lambda i, tm_ref: (layer * N_EXPERTS + jnp.minimum(tm_ref[i], N_EXPERTS - 1), 0, 0)
    return pl.pallas_call(
        _experts_body,
        grid_spec=pltpu.PrefetchScalarGridSpec(
            num_scalar_prefetch=1,
            grid=(n_tiles,),
            in_specs=[pl.BlockSpec(memory_space=pl.ANY), pl.BlockSpec((1, d, f), w_map),
                      pl.BlockSpec((1, d, f), w_map), pl.BlockSpec((1, f, d), w_map)],
            out_specs=pl.BlockSpec((tile_rows, LANES), y_map),
            scratch_shapes=[pltpu.VMEM((EXPERT_SLOTS, tile_rows, LANES), F32),
                            pltpu.SemaphoreType.DMA((EXPERT_SLOTS,)),
                            pltpu.VMEM((d, f), BF16), pltpu.VMEM((d, f), BF16), pltpu.VMEM((f, d), BF16)],
        ),
        out_shape=jax.ShapeDtypeStruct(xs.shape, F32),
        compiler_params=_cparams(("arbitrary",)),
        name="moe_experts",
    )(tile_map, xs, w_gate, w_up, w_down)


def _combine_body(pos_ref, h_ref, route_ref, ys_hbm, o_ref, buf, sems):
    tm = h_ref.shape[0]
    steps = pl.num_programs(0)
    n = steps * tm
    i = pl.program_id(0)

    def start_gather(step, slot):
        base = step * tm

        def issue(g, carry):
            for u in range(SUBLANES):
                for pick in range(2):
                    src = pl.multiple_of(pos_ref[pick * n + base + g * SUBLANES + u], SUBLANES)
                    dst = pl.multiple_of(g * SUBLANES * SUBLANES, SUBLANES) + u * SUBLANES
                    pltpu.make_async_copy(ys_hbm.at[pl.ds(src, SUBLANES), :],
                                          buf.at[slot, pick, pl.ds(dst, SUBLANES), :],
                                          sems.at[slot]).start(priority=pick)
            return carry

        lax.fori_loop(0, tm // SUBLANES, issue, 0)

    @pl.when(i == 0)
    def _():
        start_gather(0, 0)

    @pl.when(i + 1 < steps)
    def _():
        start_gather(i + 1, (i + 1) % 2)

    slot = i % 2
    pltpu.make_async_copy(buf.at[slot], buf.at[slot], sems.at[slot]).wait()
    route = route_ref[...]
    y1 = _load_token_major(buf, tm, (slot, 0))
    y2 = _load_token_major(buf, tm, (slot, 1))
    o_ref[...] = h_ref[...] + route[:, 2:3] * y1 + route[:, 3:4] * y2


def _combine(pos, h, route, ys, tm):
    n, d = h.shape
    row = lambda i, p: (i, 0)
    return pl.pallas_call(
        _combine_body,
        grid_spec=pltpu.PrefetchScalarGridSpec(
            num_scalar_prefetch=1,
            grid=(n // tm,),
            in_specs=[pl.BlockSpec((tm, d), row), pl.BlockSpec((tm, ROUTER_LANES), row),
                      pl.BlockSpec(memory_space=pl.ANY)],
            out_specs=pl.BlockSpec((tm, d), row),
            scratch_shapes=[pltpu.VMEM((2, 2, tm * SUBLANES, LANES), F32), pltpu.SemaphoreType.DMA((2,))],
        ),
        out_shape=jax.ShapeDtypeStruct((n, d), F32),
        compiler_params=_cparams(("arbitrary",)),
        name="moe_combine",
    )(pos, h, route, ys)


def _gelu(x):
    return 0.5 * x * (1.0 + lax.erf(x * (1.0 / math.sqrt(2.0))))


def _in_odd_body(h_ref, g_ref, w_ref, vg_ref, u_ref, vn_ref, q_ref, k_ref, v_ref):
    xn = _rms(h_ref[...], g_ref[...])
    proj = jnp.dot(xn.astype(BF16), w_ref[...], preferred_element_type=F32)
    w = MIX_WIDTH
    u_ref[...] = _gelu(proj[:, :w]).astype(BF16)
    vn_ref[...] = _rms(_gelu(proj[:, w:2 * w]), vg_ref[...]).astype(BF16)
    q_ref[...] = (proj[:, 2 * w:3 * w] * Q_SCALE).astype(BF16)
    k_ref[...] = proj[:, 3 * w:4 * w].astype(BF16)
    v_ref[...] = proj[:, 4 * w:].astype(BF16)


def _in_odd(h, gain, w, v_gain, tm):
    n, d = h.shape
    row = lambda i: (i, 0)
    const2 = lambda i: (0, 0)
    sds = jax.ShapeDtypeStruct((n, MIX_WIDTH), BF16)
    return pl.pallas_call(
        _in_odd_body,
        grid=(n // tm,),
        in_specs=[pl.BlockSpec((tm, d), row), pl.BlockSpec((1, d), const2),
                  pl.BlockSpec(w.shape, const2), pl.BlockSpec((1, MIX_WIDTH), const2)],
        out_specs=[pl.BlockSpec((tm, MIX_WIDTH), row)] * 5,
        out_shape=[sds] * 5,
        compiler_params=_cparams(("arbitrary",)),
        name="in_odd",
    )(h, gain, w, v_gain)


SB_TQ = 64
SB_KB = 128
SB_NB = 2
SB_TK = SB_NB * SB_KB
SB_QBLK = 1024
SB_GROUP = 16
SB_UNDERFLOW = -150.0


def _sb_body(q_ref, k_ref, v_ref, o_ref, acc_scr, run_scr):
    qi = pl.program_id(2)
    lane = lax.broadcasted_iota(jnp.int32, (SB_TQ, LANES), 1)
    row = lax.broadcasted_iota(jnp.int32, (2 * SB_TQ, 1), 0) % SB_TQ
    col = lax.broadcasted_iota(jnp.int32, (2 * SB_TQ, SB_TK), 1)
    rr = lax.broadcasted_iota(jnp.int32, (2 * SB_KB, 2 * SB_KB), 0) % SB_KB
    cc = lax.broadcasted_iota(jnp.int32, (2 * SB_KB, 2 * SB_KB), 1)
    suffix = jnp.where((cc >= SB_KB) | (rr > cc), 1.0, 0.0).astype(BF16)

    def suffix_sums(x):
        hi = x.astype(BF16)
        lo = (x - hi.astype(F32)).astype(BF16)
        return jnp.dot(jnp.concatenate([hi, lo], axis=1), suffix, preferred_element_type=F32)

    def subtile_group(grp, _):
        q_los = [pl.multiple_of((grp * SB_GROUP + s) * SB_TQ, SB_TQ) for s in range(SB_GROUP)]
        q_starts = [qi * SB_QBLK + q_lo for q_lo in q_los]
        q_heads = []
        for q_lo in q_los:
            q = q_ref[0, pl.ds(q_lo, SB_TQ), :]
            zero = jnp.zeros_like(q)
            q_heads.append(jnp.concatenate(
                [jnp.where(lane < HEAD_DIM, q, zero), jnp.where(lane < HEAD_DIM, zero, q)], axis=0))
        acc_scr[...] = jnp.zeros_like(acc_scr)
        run_scr[...] = jnp.zeros_like(run_scr)

        def cond(carry):
            his, dones = carry
            active = [(hi > 0) & (done == 0) for hi, done in zip(his, dones)]
            return functools.reduce(jnp.logical_or, active)

        def body(carry):
            his, _ = carry
            group = range(SB_GROUP)
            kss = [pl.multiple_of(jnp.maximum(his[s] - SB_TK, 0), SB_TQ) for s in group]
            valid = [col < (jnp.minimum(row + q_starts[s], his[s]) - kss[s]) for s in group]
            zs = [lax.dot_general(q_heads[s], k_ref[0, pl.ds(kss[s], SB_TK), :], NT_DIMS,
                                  preferred_element_type=F32) for s in group]
            log_beta, log_rest = [], []
            for s in group:
                z = jnp.where(valid[s], zs[s], NEG_BIG)
                sp = jnp.log2(1.0 + jnp.exp2(-jnp.abs(z)))
                log_beta.append(jnp.minimum(z, 0.0) - sp)
                log_rest.append(log_beta[s] - z)
            sums = [[suffix_sums(log_rest[s][:, blk * SB_KB:(blk + 1) * SB_KB]) for blk in range(SB_NB)]
                    for s in group]
            dones = []
            for s in group:
                run = run_scr[s]
                pieces = [None] * SB_NB
                for blk in reversed(range(SB_NB)):
                    tt = sums[s][blk]
                    pieces[blk] = jnp.exp2(log_beta[s][:, blk * SB_KB:(blk + 1) * SB_KB] + tt[:, :SB_KB] + run)
                    run = run + tt[:, SB_KB:]
                att = jnp.concatenate(pieces, axis=1)
                acc_scr[s] += jnp.dot(att.astype(BF16), v_ref[0, pl.ds(kss[s], SB_TK), :],
                                      preferred_element_type=F32)
                run_scr[s] = run
                dones.append((jnp.max(run) <= SB_UNDERFLOW).astype(jnp.int32))
            return tuple(kss), tuple(dones)

        lax.while_loop(cond, body, (tuple(qs + SB_TQ for qs in q_starts),
                                    tuple(jnp.int32(0) for _ in range(SB_GROUP))))
        for s in range(SB_GROUP):
            o_ref[0, pl.ds(q_los[s], SB_TQ), :] = jnp.where(
                lane < HEAD_DIM, acc_scr[s, :SB_TQ], acc_scr[s, SB_TQ:]).astype(BF16)
        return 0

    lax.fori_loop(0, SB_QBLK // (SB_TQ * SB_GROUP), subtile_group, 0)


def _sb_attention(q, k, v):
    b, s, w = q.shape
    assert s % SB_QBLK == 0 and s >= SB_TK
    qspec = pl.BlockSpec((1, SB_QBLK, LANES), lambda bi, hp, i: (bi, i, hp))
    kvspec = pl.BlockSpec((1, s, LANES), lambda bi, hp, i: (bi, 0, hp))
    return pl.pallas_call(
        _sb_body,
        grid=(b, w // LANES, s // SB_QBLK),
        in_specs=[qspec, kvspec, kvspec],
        out_specs=qspec,
        out_shape=jax.ShapeDtypeStruct((b, s, w), BF16),
        scratch_shapes=[pltpu.VMEM((SB_GROUP, 2 * SB_TQ, LANES), F32),
                        pltpu.VMEM((SB_GROUP, 2 * SB_TQ, LANES), F32)],
        compiler_params=_cparams(("arbitrary", "arbitrary", "arbitrary")),
        name="sb_attention",
    )(q, k, v)


def _out_odd_body(u_ref, vn_ref, yd_ref, h_ref, ws_ref, bs_ref, wo_ref, fg_ref, wr_ref,
                  o_ref, xn_ref, route_ref, yc_scr):
    tm = u_ref.shape[0]
    r = lax.broadcasted_iota(jnp.int32, (SGU_BLOCK, SGU_BLOCK), 0)
    c = lax.broadcasted_iota(jnp.int32, (SGU_BLOCK, SGU_BLOCK), 1)
    for g in range(MIX_WIDTH // LANES):
        ls = slice(g * LANES, (g + 1) * LANES)
        ws = jnp.where(c <= r, ws_ref[g], jnp.zeros_like(ws_ref[g]))
        for blk in range(tm // SGU_BLOCK):
            rs = slice(blk * SGU_BLOCK, (blk + 1) * SGU_BLOCK)
            mixed = jnp.dot(ws, vn_ref[rs, ls], preferred_element_type=F32) + bs_ref[g]
            yc_scr[rs, ls] = (u_ref[rs, ls].astype(F32) * mixed).astype(BF16)
    mixed = jnp.concatenate([yc_scr[...], yd_ref[...]], axis=1)
    _project_and_route(h_ref, mixed, wo_ref, fg_ref, wr_ref, o_ref, xn_ref, route_ref)


def _out_odd(u, vn, yd, h, sgu_w, sgu_b, w_out, ffn_gain, w_router, tm):
    n, d = h.shape
    w = MIX_WIDTH
    row = lambda i: (i, 0)
    const2 = lambda i: (0, 0)
    const3 = lambda i: (0, 0, 0)
    return pl.pallas_call(
        _out_odd_body,
        grid=(n // tm,),
        in_specs=[pl.BlockSpec((tm, w), row), pl.BlockSpec((tm, w), row), pl.BlockSpec((tm, w), row),
                  pl.BlockSpec((tm, d), row), pl.BlockSpec(sgu_w.shape, const3),
                  pl.BlockSpec(sgu_b.shape, const3), pl.BlockSpec(w_out.shape, const2),
                  pl.BlockSpec((1, d), const2), pl.BlockSpec(w_router.shape, const2)],
        out_specs=[pl.BlockSpec((tm, d), row), pl.BlockSpec((tm * SUBLANES, LANES), row),
                   pl.BlockSpec((tm, ROUTER_LANES), row)],
        out_shape=[jax.ShapeDtypeStruct((n, d), F32), jax.ShapeDtypeStruct((n * SUBLANES, LANES), F32),
                   jax.ShapeDtypeStruct((n, ROUTER_LANES), F32)],
        scratch_shapes=[pltpu.VMEM((tm, w), BF16)],
        compiler_params=_cparams(("arbitrary",)),
        name="out_odd",
    )(u, vn, yd, h, sgu_w, sgu_b, w_out, ffn_gain, w_router)


def _moe_layer(h, xn, route, w_gate, w_up, w_down, layer):
    pos1, pos2, tile_map = _slots(route, tm=SLOTS_TM)
    pos = jnp.concatenate([pos1[0], pos2[0]]).astype(jnp.int32) * SUBLANES
    tile_map = tile_map[0].astype(jnp.int32)
    xs = _dispatch(pos, tile_map, xn, tm=DISPATCH_TM)
    ys = _experts(tile_map, xs, w_gate, w_up, w_down, layer)
    return _combine(pos, h, route, ys, tm=COMBINE_TM)


def kernel(x, mix_norm_even, w_in_even, att_q_norm, att_k_norm, att_rel_bias, pool_w, pool_scale,
           w_out_even, mix_norm_odd, w_in_odd, sgu_v_norm, sgu_w, sgu_b, w_out_odd, ffn_norm,
           w_router_group, w_router_expert, w_exp_gate, w_exp_up, w_exp_down):
    b, s, d = x.shape
    n = b * s
    depth = ffn_norm.shape[0]
    heads = MIX_WIDTH // HEAD_DIM
    h = x.reshape(n, d)
    w_gate = w_exp_gate.reshape(depth * N_EXPERTS, d, EXPERT_FF)
    w_up = w_exp_up.reshape(depth * N_EXPERTS, d, EXPERT_FF)
    w_down = w_exp_down.reshape(depth * N_EXPERTS, EXPERT_FF, d)
    for layer in range(depth):
        i = layer // 2
        ffn_gain = ffn_norm[layer][None, :]
        w_router = _router_weights(w_router_group[layer], w_router_expert[layer])
        if layer % 2 == 0:
            q, k_pad, v_pad, p = _in_even(
                h.reshape(b, s, d), mix_norm_even[i][None, :], w_in_even[i].astype(BF16),
                jnp.tile(att_q_norm[i], heads)[None, :], jnp.tile(att_k_norm[i], heads)[None, :],
                tm=ATT_LEFT)
            ya = _band_attention(q, k_pad, v_pad, _band_bias(att_rel_bias[i]))
            h, xn, route = _out_even(ya.reshape(n, MIX_WIDTH), p.reshape(n, MIX_WIDTH), h,
                                     pool_w[i].astype(BF16), pool_scale[i][None, :],
                                     w_out_even[i].astype(BF16), ffn_gain, w_router, seq=s, tm=PROJ_TM)
        else:
            u, vn, q, k, v = _in_odd(h, mix_norm_odd[i][None, :], w_in_odd[i].astype(BF16),
                                     sgu_v_norm[i][None, :], tm=PROJ_TM)
            to3 = lambda t: t.reshape(b, s, MIX_WIDTH)
            yd = _sb_attention(to3(q), to3(k), to3(v))
            bias = jnp.broadcast_to(sgu_b[i][:, :, None], (N_GROUPS, SGU_BLOCK, LANES))
            h, xn, route = _out_odd(u, vn, yd.reshape(n, MIX_WIDTH), h, sgu_w[i].astype(BF16), bias,
                                    w_out_odd[i].astype(BF16), ffn_gain, w_router, tm=PROJ_TM)
        h = _moe_layer(h, xn, route, w_gate, w_up, w_down, layer)
    return h.reshape(b, s, d)
```

```python
import functools
import math

import jax
import jax.numpy as jnp
from jax import lax
from jax.experimental import pallas as pl
from jax.experimental.pallas import tpu as pltpu

F32 = jnp.float32
BF16 = jnp.bfloat16

CHUNK = 64
EPS = 1e-6
HEAD_DIM = 64
MIX_WIDTH = 512
LANES = 128
SUBLANES = 8
ATT_LEFT = 8 * CHUNK
ATT_MAX_REL = 128
POOL_WINDOWS = (2, 4, 8, 16)
POOL_HALO = 16
SGU_BLOCK = 128
N_GROUPS = 4
N_EXP_PER_GROUP = 8
N_EXPERTS = N_GROUPS * N_EXP_PER_GROUP
EXPERT_FF = 256
ROUTER_LANES = 128
ROUTER_ROWS = 40
NEG_BIG = -1e30
VMEM_LIMIT = 56 * 1024 * 1024

PROJ_TM = 512
SLOTS_TM = 1024
DISPATCH_TM = 1024
COMBINE_TM = 256

NT_DIMS = (((1,), (1,)), ((), ()))
LOG2E = math.log2(math.e)
Q_SCALE = LOG2E / math.sqrt(HEAD_DIM)


def _cparams(sem):
    return pltpu.CompilerParams(dimension_semantics=sem, vmem_limit_bytes=VMEM_LIMIT)


def _store_token_major(ref, x, first_token=0):
    rows = x.shape[0]
    for s in range(SUBLANES):
        ref[pl.ds(first_token * SUBLANES + s, rows, stride=SUBLANES), :] = x[:, s * LANES:(s + 1) * LANES]


def _load_token_major(ref, rows, lead=()):
    return jnp.concatenate(
        [ref[lead + (pl.ds(s, rows, stride=SUBLANES), slice(None))] for s in range(SUBLANES)], axis=1)


def _rms(x, gain):
    return x * lax.rsqrt(jnp.mean(x * x, axis=-1, keepdims=True) + EPS) * gain


def _split_dot(x, m):
    hi = x.astype(BF16)
    lo = (x - hi.astype(F32)).astype(BF16)
    return (jnp.dot(hi, m, preferred_element_type=F32)
            + jnp.dot(lo, m, preferred_element_type=F32))


def _head_rms(t, gain):
    n = t.shape[-1]
    r = lax.broadcasted_iota(jnp.int32, (n, n), 0) // HEAD_DIM
    c = lax.broadcasted_iota(jnp.int32, (n, n), 1) // HEAD_DIM
    bd = jnp.where(r == c, 1.0, 0.0).astype(BF16)
    ms = _split_dot(t * t, bd) * (1.0 / HEAD_DIM)
    return t * lax.rsqrt(ms + EPS) * gain


def _in_even_body(h_ref, g_ref, w_ref, qg_ref, kg_ref, q_ref, k_ref, v_ref, p_ref):
    j = pl.program_id(1)

    @pl.when(j == 0)
    def _():
        k_ref[...] = jnp.zeros_like(k_ref)
        v_ref[...] = jnp.zeros_like(v_ref)

    @pl.when(j > 0)
    def _():
        xn = _rms(h_ref[0], g_ref[...])
        proj = jnp.dot(xn.astype(BF16), w_ref[...], preferred_element_type=F32)
        w = MIX_WIDTH
        q_ref[0] = (_head_rms(proj[:, :w], qg_ref[...]) * Q_SCALE).astype(BF16)
        k_ref[0] = _head_rms(proj[:, w:2 * w], kg_ref[...]).astype(BF16)
        v_ref[0] = proj[:, 2 * w:3 * w].astype(BF16)
        p_ref[0] = proj[:, 3 * w:].astype(BF16)


def _in_even(h, gain, w, q_gain, k_gain, tm):
    b, s, d = h.shape
    assert tm == ATT_LEFT and s % tm == 0
    nt = s // tm
    cur = lambda bi, j: (bi, jnp.maximum(j - 1, 0), 0)
    const = lambda bi, j: (0, 0)
    out_sds = lambda rows: jax.ShapeDtypeStruct((b, rows, MIX_WIDTH), BF16)
    return pl.pallas_call(
        _in_even_body,
        grid=(b, nt + 1),
        in_specs=[
            pl.BlockSpec((1, tm, d), cur),
            pl.BlockSpec((1, d), const),
            pl.BlockSpec(w.shape, const),
            pl.BlockSpec((1, MIX_WIDTH), const),
            pl.BlockSpec((1, MIX_WIDTH), const),
        ],
        out_specs=[
            pl.BlockSpec((1, tm, MIX_WIDTH), cur),
            pl.BlockSpec((1, tm, MIX_WIDTH), lambda bi, j: (bi, j, 0)),
            pl.BlockSpec((1, tm, MIX_WIDTH), lambda bi, j: (bi, j, 0)),
            pl.BlockSpec((1, tm, MIX_WIDTH), cur),
        ],
        out_shape=[out_sds(s), out_sds(s + ATT_LEFT), out_sds(s + ATT_LEFT), out_sds(s)],
        compiler_params=_cparams(("arbitrary", "arbitrary")),
        name="in_even",
    )(h, gain, w, q_gain, k_gain)


BAND_TQ = 2 * CHUNK
BAND_TK = BAND_TQ + ATT_LEFT


BAND_STEP_TILES = 4
BAND_STAGED = 2


def _band_body(q_ref, k_ref, v_ref, bias_ref, o_ref):
    @pl.loop(0, BAND_STEP_TILES)
    def _(t):
        rows = pl.ds(pl.multiple_of(t * BAND_TQ, BAND_TQ), BAND_TQ)
        _band_tile(pl.program_id(1) * BAND_STEP_TILES + t, q_ref.at[0, rows, :], k_ref, v_ref, bias_ref,
                   o_ref.at[0, rows, :])


def _band_tile(i, q_ref, k_ref, v_ref, bias_ref, o_ref):
    start = pl.multiple_of(i * BAND_TQ, BAND_TQ)
    lane = lax.broadcasted_iota(jnp.int32, (BAND_TQ, LANES), 1)
    col = lax.broadcasted_iota(jnp.int32, (2 * BAND_TQ, BAND_TK), 1)
    is_pad = (col + start) < ATT_LEFT
    n_pairs = MIX_WIDTH // LANES
    lanes = [slice(hp * LANES, (hp + 1) * LANES) for hp in range(n_pairs)]
    for first in range(0, n_pairs, BAND_STAGED):
        pairs = range(first, first + BAND_STAGED)
        scores, probs, denoms = {}, {}, {}
        for hp in pairs:
            q = q_ref[:, lanes[hp]]
            kb = k_ref[0, pl.ds(start, BAND_TK), lanes[hp]]
            zero = jnp.zeros_like(q)
            q2 = jnp.concatenate([jnp.where(lane < HEAD_DIM, q, zero), jnp.where(lane < HEAD_DIM, zero, q)],
                                 axis=0)
            scores[hp] = lax.dot_general(q2, kb, NT_DIMS, preferred_element_type=F32)
        for hp in pairs:
            bias = bias_ref[2 * hp:2 * hp + 2].reshape(2 * BAND_TQ, BAND_TK)
            s = jnp.where(is_pad, NEG_BIG, scores[hp] + bias)
            p = jnp.exp2(s - jnp.max(s, axis=-1, keepdims=True))
            denoms[hp] = jnp.sum(p, axis=-1, keepdims=True)
            probs[hp] = p.astype(BF16)
        for hp in pairs:
            vb = v_ref[0, pl.ds(start, BAND_TK), lanes[hp]]
            o = jnp.dot(probs[hp], vb, preferred_element_type=F32) / denoms[hp]
            o_ref[:, lanes[hp]] = jnp.where(lane < HEAD_DIM, o[:BAND_TQ], o[BAND_TQ:]).astype(BF16)


def _band_bias(rel_bias):
    heads = rel_bias.shape[0]
    r = jnp.arange(BAND_TQ)[:, None]
    j = jnp.arange(BAND_TK)[None, :]
    jb = j - CHUNK * (r // CHUNK)
    in_band = (jb >= 0) & (jb < ATT_LEFT + CHUNK)
    period = BAND_TK + BAND_TQ
    far = jnp.broadcast_to(rel_bias[:, 2 * ATT_MAX_REL:], (heads, ATT_LEFT - ATT_MAX_REL + 1))
    near = rel_bias[:, 2 * ATT_MAX_REL - 1:0:-1]
    wrap = jnp.broadcast_to(rel_bias[:, 2 * ATT_MAX_REL:], (heads, period - BAND_TK))
    g = jnp.concatenate([far, near, wrap], axis=1).astype(F32)
    assert g.shape[1] == period
    toep = jnp.tile(g, (1, BAND_TQ))[:, :BAND_TQ * (period - 1)].reshape(heads, BAND_TQ, period - 1)
    return jnp.where(in_band[None], toep[:, :, :BAND_TK] * LOG2E, NEG_BIG)


def _band_attention(q, k_pad, v_pad, bias):
    b, s, w = q.shape
    sp = k_pad.shape[1]
    step_rows = BAND_STEP_TILES * BAND_TQ
    return pl.pallas_call(
        _band_body,
        grid=(b, s // step_rows),
        in_specs=[
            pl.BlockSpec((1, step_rows, w), lambda bi, i: (bi, i, 0)),
            pl.BlockSpec((1, sp, w), lambda bi, i: (bi, 0, 0)),
            pl.BlockSpec((1, sp, w), lambda bi, i: (bi, 0, 0)),
            pl.BlockSpec(bias.shape, lambda bi, i: (0, 0, 0)),
        ],
        out_specs=pl.BlockSpec((1, step_rows, w), lambda bi, i: (bi, i, 0)),
        out_shape=jax.ShapeDtypeStruct((b, s, w), BF16),
        compiler_params=_cparams(("arbitrary", "arbitrary")),
        name="band_attention",
    )(q, k_pad, v_pad, bias)


def _route_tokens(h, gain, w_router):
    xn = _rms(h, gain)
    x_hi = xn.astype(BF16)
    x_lo = (xn - x_hi.astype(F32)).astype(BF16)
    w_hi = w_router.astype(BF16)
    w_lo = (w_router - w_hi.astype(F32)).astype(BF16)
    logits = (jnp.dot(x_hi, w_hi, preferred_element_type=F32)
              + jnp.dot(x_lo, w_hi, preferred_element_type=F32)
              + jnp.dot(x_hi, w_lo, preferred_element_type=F32))
    lt = logits.T[:ROUTER_ROWS]
    sub = lax.broadcasted_iota(jnp.int32, lt.shape, 0).astype(F32)
    ninf = -jnp.inf

    def top(vals):
        m = jnp.max(vals, axis=0, keepdims=True)
        idx = jnp.min(jnp.where(vals == m, sub, float(ROUTER_LANES)), axis=0, keepdims=True)
        return m, idx

    is_group = sub < N_GROUPS
    g_max, g_sel = top(jnp.where(is_group, lt, ninf))
    g_den = jnp.sum(jnp.where(is_group, jnp.exp(lt - g_max), 0.0), axis=0, keepdims=True)
    g_weight = 1.0 / g_den
    lo = N_GROUPS + N_EXP_PER_GROUP * g_sel
    e_logits = jnp.where((sub >= lo) & (sub < lo + N_EXP_PER_GROUP), lt, ninf)
    e1, i1 = top(e_logits)
    e2, i2 = top(jnp.where(sub == i1, ninf, e_logits))
    t = jnp.exp(e2 - e1)
    w1 = g_weight / (1.0 + t)
    w2 = g_weight * t / (1.0 + t)
    rows = lax.broadcasted_iota(jnp.int32, (ROUTER_LANES, lt.shape[1]), 0)
    route_t = jnp.where(rows == 0, i1 - N_GROUPS, jnp.where(rows == 1, i2 - N_GROUPS, 0.0))
    route_t = jnp.where(rows == 2, w1, jnp.where(rows == 3, w2, route_t))
    return xn, route_t.T


def _router_weights(w_rg, w_re):
    pad = jnp.zeros((w_rg.shape[0], ROUTER_LANES - N_GROUPS - N_EXPERTS), F32)
    return jnp.concatenate([w_rg, w_re, pad], axis=1)


def _project_and_route(h_ref, mixed, wo_ref, fg_ref, wr_ref, o_ref, xn_ref, route_ref):
    rows = h_ref.shape[0]
    halves = [slice(0, rows // 2), slice(rows // 2, rows)]
    h_new = [h_ref[rs, :] + jnp.dot(mixed[rs], wo_ref[...], preferred_element_type=F32) for rs in halves]
    for rs, hn in zip(halves, h_new):
        o_ref[rs, :] = hn
    for rs, hn in zip(halves, h_new):
        xn, route_ref[rs, :] = _route_tokens(hn, fg_ref[...], wr_ref[...])
        _store_token_major(xn_ref, xn, first_token=rs.start)


def _out_even_body(tiles_per_seq, ya_ref, p_ref, halo_ref, h_ref, pw_ref, ps_ref, wo_ref, fg_ref, wr_ref,
                   o_ref, xn_ref, route_ref, p_scr, yb_scr):
    tm = p_ref.shape[0]
    it = pl.program_id(0) % tiles_per_seq
    halo = halo_ref[...].astype(F32)
    p_scr[0:POOL_HALO, :] = jnp.where(it == 0, jnp.zeros_like(halo), halo)
    p_scr[POOL_HALO:, :] = p_ref[...].astype(F32)
    t = it * tm + lax.broadcasted_iota(jnp.int32, (tm, 1), 0)
    for g, win in enumerate(POOL_WINDOWS):
        ls = slice(g * LANES, (g + 1) * LANES)
        cur = p_scr[POOL_HALO:POOL_HALO + tm, ls]
        acc = cur
        for dlt in range(1, win):
            acc = acc + p_scr[POOL_HALO - dlt:POOL_HALO - dlt + tm, ls]
        cnt = jnp.minimum(t + 1, win).astype(F32)
        mixed = acc / cnt - cur
        yb = jnp.dot(mixed.astype(BF16), pw_ref[g], preferred_element_type=F32) * ps_ref[:, ls]
        yb_scr[:, ls] = yb.astype(BF16)
    mixed = jnp.concatenate([ya_ref[...], yb_scr[...]], axis=1)
    _project_and_route(h_ref, mixed, wo_ref, fg_ref, wr_ref, o_ref, xn_ref, route_ref)


def _out_even(ya, p, h, pool_w, pool_scale, w_out, ffn_gain, w_router, seq, tm):
    n, d = h.shape
    w = MIX_WIDTH
    row = lambda i: (i, 0)
    const2 = lambda i: (0, 0)
    halo_blocks = tm // POOL_HALO
    return pl.pallas_call(
        functools.partial(_out_even_body, seq // tm),
        grid=(n // tm,),
        in_specs=[
            pl.BlockSpec((tm, w), row),
            pl.BlockSpec((tm, w), row),
            pl.BlockSpec((POOL_HALO, w), lambda i: (jnp.maximum(i * halo_blocks - 1, 0), 0)),
            pl.BlockSpec((tm, d), row),
            pl.BlockSpec(pool_w.shape, lambda i: (0, 0, 0)),
            pl.BlockSpec((1, w), const2),
            pl.BlockSpec(w_out.shape, const2),
            pl.BlockSpec((1, d), const2),
            pl.BlockSpec(w_router.shape, const2),
        ],
        out_specs=[pl.BlockSpec((tm, d), row), pl.BlockSpec((tm * SUBLANES, LANES), row),
                   pl.BlockSpec((tm, ROUTER_LANES), row)],
        out_shape=[jax.ShapeDtypeStruct((n, d), F32), jax.ShapeDtypeStruct((n * SUBLANES, LANES), F32),
                   jax.ShapeDtypeStruct((n, ROUTER_LANES), F32)],
        scratch_shapes=[pltpu.VMEM((tm + POOL_HALO, w), F32), pltpu.VMEM((tm, w), BF16)],
        compiler_params=_cparams(("arbitrary",)),
        name="out_even",
    )(ya, p, p, h, pool_w, pool_scale, w_out, ffn_gain, w_router)


MOE_TM = 512
MOE_TILE_LANES = 256


def _moe_tiles(n):
    return (2 * n) // MOE_TM + N_EXPERTS


def _exact_dot_nt(ones, x):
    out = None
    for _ in range(3):
        part = x.astype(BF16)
        x = x - part.astype(F32)
        term = lax.dot_general(ones, part, NT_DIMS, preferred_element_type=F32)
        out = term if out is None else out + term
    return out


def _slots_body(route_ref, pos1_ref, pos2_ref, tile_ref, valid_ref, run_scr, start_scr, earlier_scr):
    phase = pl.program_id(0)
    i = pl.program_id(1)
    tm = route_ref.shape[0]
    route = route_ref[...]
    lane = lax.broadcasted_iota(jnp.int32, (tm, ROUTER_LANES), 1).astype(F32)
    pick1 = jnp.where(lane == route[:, 0:1], 1.0, 0.0)
    pick2 = jnp.where(lane == route[:, 1:2], 1.0, 0.0)
    occ = (pick1 + pick2).astype(BF16)
    ones_rows = jnp.ones((SUBLANES, tm), BF16)
    ones_lanes = jnp.ones((SUBLANES, ROUTER_LANES), BF16)

    @pl.when(i == 0)
    def _():
        run_scr[...] = jnp.zeros_like(run_scr)

    @pl.when(phase == 0)
    def _():
        run_scr[...] += jnp.dot(ones_rows, occ, preferred_element_type=F32)

        @pl.when(i == pl.num_programs(1) - 1)
        def _():
            padded = jnp.floor((run_scr[...] + (MOE_TM - 1)) * (1.0 / MOE_TM)) * MOE_TM
            r = lax.broadcasted_iota(jnp.int32, (ROUTER_LANES, ROUTER_LANES), 0)
            c = lax.broadcasted_iota(jnp.int32, (ROUTER_LANES, ROUTER_LANES), 1)
            before = jnp.where(r < c, 1.0, 0.0).astype(BF16)
            hi = padded.astype(BF16)
            mid = (padded - hi.astype(F32)).astype(BF16)
            low = (padded - hi.astype(F32) - mid.astype(F32)).astype(BF16)
            start = (jnp.dot(hi, before, preferred_element_type=F32)
                     + jnp.dot(mid, before, preferred_element_type=F32)
                     + jnp.dot(low, before, preferred_element_type=F32))
            start_scr[...] = start
            seg_end = start[0:1, :] + padded[0:1, :]
            tile_lo = (lax.broadcasted_iota(jnp.int32, (MOE_TILE_LANES, ROUTER_LANES), 0) * MOE_TM).astype(F32)
            e_lane = lax.broadcasted_iota(jnp.int32, (MOE_TILE_LANES, ROUTER_LANES), 1)
            ended = jnp.where((seg_end <= tile_lo) & (e_lane < N_EXPERTS), 1.0, 0.0).astype(BF16)
            tile_ref[...] = lax.dot_general(ones_lanes, ended, NT_DIMS, preferred_element_type=F32)
            in_segment = (start[0:1, :] <= tile_lo) & (tile_lo < seg_end) & (e_lane < N_EXPERTS)
            real_end = start[0:1, :] + run_scr[0:1, :]
            valid = jnp.where(in_segment, jnp.clip(real_end - tile_lo, 0.0, float(MOE_TM)), 0.0)
            valid_ref[...] = _exact_dot_nt(ones_lanes, valid)

    @pl.when(phase == 1)
    def _():
        @pl.when(i == 0)
        def _():
            r = lax.broadcasted_iota(jnp.int32, (tm, tm), 0)
            c = lax.broadcasted_iota(jnp.int32, (tm, tm), 1)
            earlier_scr[...] = jnp.where(c < r, 1.0, 0.0).astype(BF16)

        base = (jnp.dot(earlier_scr[...], occ, preferred_element_type=F32)
                + run_scr[0:1, :] + start_scr[0:1, :])
        pos1_ref[...] = _exact_dot_nt(ones_lanes, pick1 * base)
        pos2_ref[...] = _exact_dot_nt(ones_lanes, pick2 * base)
        run_scr[...] += jnp.dot(ones_rows, occ, preferred_element_type=F32)


def _slots(route, tm):
    n = route.shape[0]
    assert _moe_tiles(n) <= MOE_TILE_LANES and 2 * n + N_EXPERTS * MOE_TM < 2 ** 24
    row_out = pl.BlockSpec((SUBLANES, tm), lambda ph, i: (0, i * ph))
    sds = jax.ShapeDtypeStruct((SUBLANES, n), F32)
    return pl.pallas_call(
        _slots_body,
        grid=(2, n // tm),
        in_specs=[pl.BlockSpec((tm, ROUTER_LANES), lambda ph, i: (i, 0))],
        out_specs=[row_out, row_out] + [pl.BlockSpec((SUBLANES, MOE_TILE_LANES), lambda ph, i: (0, 0))] * 2,
        out_shape=[sds, sds] + [jax.ShapeDtypeStruct((SUBLANES, MOE_TILE_LANES), F32)] * 2,
        scratch_shapes=[pltpu.VMEM((SUBLANES, ROUTER_LANES), F32), pltpu.VMEM((SUBLANES, ROUTER_LANES), F32),
                        pltpu.VMEM((tm, tm), BF16)],
        compiler_params=_cparams(("arbitrary", "arbitrary")),
        name="moe_slots",
    )(route)


MOE_FILL = 64


def _dispatch_body(n_tiles, pos_ref, valid_ref, xn_ref, xs_hbm, zero_scr, zero_sem, row_sem):
    tm = xn_ref.shape[0] // SUBLANES
    n = pl.num_programs(0) * tm
    base = pl.program_id(0) * tm

    @pl.when(pl.program_id(0) == 0)
    def _():
        zero_scr[...] = jnp.zeros_like(zero_scr)
        chunk_rows = MOE_FILL * SUBLANES

        def for_each_fill(action):
            @pl.loop(0, n_tiles)
            def _(t):
                def chunk(c, carry):
                    rows = pl.ds(pl.multiple_of((t * MOE_TM + c * MOE_FILL) * SUBLANES, chunk_rows), chunk_rows)
                    action(pltpu.make_async_copy(zero_scr, xs_hbm.at[rows, :], zero_sem))
                    return carry

                lax.fori_loop(valid_ref[t] // MOE_FILL, MOE_TM // MOE_FILL, chunk, 0)

        for_each_fill(lambda copy: copy.start())
        for_each_fill(lambda copy: copy.wait())

    def issue(j, carry):
        src = xn_ref.at[pl.ds(pl.multiple_of(j * SUBLANES, SUBLANES), SUBLANES), :]
        for pick in range(2):
            dst = pl.multiple_of(pos_ref[pick * n + base + j], SUBLANES)
            pltpu.make_async_copy(src, xs_hbm.at[pl.ds(dst, SUBLANES), :], row_sem).start(priority=pick)
        return carry

    lax.fori_loop(0, tm, issue, 0, unroll=8)
    for _ in range(2):
        pltpu.make_async_copy(xn_ref, xs_hbm.at[pl.ds(0, tm * SUBLANES), :], row_sem).wait()


def _dispatch(pos, tile_valid, xn, tm):
    n = xn.shape[0] // SUBLANES
    n_tiles = _moe_tiles(n)
    return pl.pallas_call(
        functools.partial(_dispatch_body, n_tiles),
        grid_spec=pltpu.PrefetchScalarGridSpec(
            num_scalar_prefetch=2,
            grid=(n // tm,),
            in_specs=[pl.BlockSpec((tm * SUBLANES, LANES), lambda i, p, t: (i, 0))],
            out_specs=pl.BlockSpec(memory_space=pl.ANY),
            scratch_shapes=[pltpu.VMEM((MOE_FILL * SUBLANES, LANES), F32), pltpu.SemaphoreType.DMA(()),
                            pltpu.SemaphoreType.DMA(())],
        ),
        out_shape=jax.ShapeDtypeStruct((n_tiles * MOE_TM * SUBLANES, LANES), F32),
        compiler_params=_cparams(("arbitrary",)),
        name="moe_dispatch",
    )(pos, tile_valid, xn)


EXPERT_SLOTS = 3


def _experts_body(tile_ref, xs_hbm, wg_ref, wu_ref, wd_ref, ys_ref, xbuf, sems, wg_bf, wu_bf, wd_bf):
    i = pl.program_id(0)
    tile_rows = MOE_TM * SUBLANES
    used = tile_ref[i] < N_EXPERTS

    def tile_copy(t, slot):
        rows = pl.ds(pl.multiple_of(t * tile_rows, tile_rows), tile_rows)
        return pltpu.make_async_copy(xs_hbm.at[rows, :], xbuf.at[slot], sems.at[slot])

    def prefetch(t):
        @pl.when(tile_ref[t] < N_EXPERTS)
        def _():
            tile_copy(t, t % EXPERT_SLOTS).start()

    @pl.when(i == 0)
    def _():
        prefetch(0)
        prefetch(1)

    prefetch(i + 2)

    @pl.when(used & ((i == 0) | (tile_ref[i] != tile_ref[jnp.maximum(i - 1, 0)])))
    def _():
        wg_bf[...] = wg_ref[0].astype(BF16)
        wu_bf[...] = wu_ref[0].astype(BF16)
        wd_bf[...] = wd_ref[0].astype(BF16)

    @pl.when(used)
    def _():
        slot = i % EXPERT_SLOTS
        tile_copy(i, slot).wait()
        x = _load_token_major(xbuf, MOE_TM, (slot,)).astype(BF16)
        gate = jnp.dot(x, wg_bf[...], preferred_element_type=F32)
        up = jnp.dot(x, wu_bf[...], preferred_element_type=F32)
        hid = gate * jax.nn.sigmoid(gate) * up
        _store_token_major(ys_ref, jnp.dot(hid.astype(BF16), wd_bf[...], preferred_element_type=F32))

    @pl.when(jnp.logical_not(used))
    def _():
        ys_ref[...] = jnp.zeros_like(ys_ref)


def _experts(tile_map, xs, w_gate, w_up, w_down, layer):
    d, f = w_gate.shape[1:]
    tile_rows = MOE_TM * SUBLANES
    n_tiles = xs.shape[0] // tile_rows
    y_map = lambda i, tm_ref: (i, 0)
    w_map = lambda i, tm_ref: (layer * N_EXPERTS + jnp.minimum(tm_ref[i], N_EXPERTS - 1), 0, 0)
    return pl.pallas_call(
        _experts_body,
        grid_spec=pltpu.PrefetchScalarGridSpec(
            num_scalar_prefetch=1,
            grid=(n_tiles,),
            in_specs=[pl.BlockSpec(memory_space=pl.ANY), pl.BlockSpec((1, d, f), w_map),
                      pl.BlockSpec((1, d, f), w_map), pl.BlockSpec((1, f, d), w_map)],
            out_specs=pl.BlockSpec((tile_rows, LANES), y_map),
            scratch_shapes=[pltpu.VMEM((EXPERT_SLOTS, tile_rows, LANES), F32),
                            pltpu.SemaphoreType.DMA((EXPERT_SLOTS,)),
                            pltpu.VMEM((d, f), BF16), pltpu.VMEM((d, f), BF16), pltpu.VMEM((f, d), BF16)],
        ),
        out_shape=jax.ShapeDtypeStruct(xs.shape, F32),
        compiler_params=_cparams(("arbitrary",)),
        name="moe_experts",
    )(tile_map, xs, w_gate, w_up, w_down)


def _combine_body(pos_ref, h_ref, route_ref, ys_hbm, o_ref, buf, sems):
    tm = h_ref.shape[0]
    steps = pl.num_programs(0)
    n = steps * tm
    i = pl.program_id(0)

    def start_gather(step, slot):
        base = step * tm

        def issue(g, carry):
            for u in range(SUBLANES):
                for pick in range(2):
                    src = pl.multiple_of(pos_ref[pick * n + base + g * SUBLANES + u], SUBLANES)
                    dst = pl.multiple_of(g * SUBLANES * SUBLANES, SUBLANES) + u * SUBLANES
                    pltpu.make_async_copy(ys_hbm.at[pl.ds(src, SUBLANES), :],
                                          buf.at[slot, pick, pl.ds(dst, SUBLANES), :],
                                          sems.at[slot]).start(priority=pick)
            return carry

        lax.fori_loop(0, tm // SUBLANES, issue, 0)

    @pl.when(i == 0)
    def _():
        start_gather(0, 0)

    @pl.when(i + 1 < steps)
    def _():
        start_gather(i + 1, (i + 1) % 2)

    slot = i % 2
    pltpu.make_async_copy(buf.at[slot], buf.at[slot], sems.at[slot]).wait()
    route = route_ref[...]
    y1 = _load_token_major(buf, tm, (slot, 0))
    y2 = _load_token_major(buf, tm, (slot, 1))
    o_ref[...] = h_ref[...] + route[:, 2:3] * y1 + route[:, 3:4] * y2


def _combine(pos, h, route, ys, tm):
    n, d = h.shape
    row = lambda i, p: (i, 0)
    return pl.pallas_call(
        _combine_body,
        grid_spec=pltpu.PrefetchScalarGridSpec(
            num_scalar_prefetch=1,
            grid=(n // tm,),
            in_specs=[pl.BlockSpec((tm, d), row), pl.BlockSpec((tm, ROUTER_LANES), row),
                      pl.BlockSpec(memory_space=pl.ANY)],
            out_specs=pl.BlockSpec((tm, d), row),
            scratch_shapes=[pltpu.VMEM((2, 2, tm * SUBLANES, LANES), F32), pltpu.SemaphoreType.DMA((2,))],
        ),
        out_shape=jax.ShapeDtypeStruct((n, d), F32),
        compiler_params=_cparams(("arbitrary",)),
        name="moe_combine",
    )(pos, h, route, ys)


def _gelu(x):
    return 0.5 * x * (1.0 + lax.erf(x * (1.0 / math.sqrt(2.0))))


def _in_odd_body(h_ref, g_ref, w_ref, vg_ref, u_ref, vn_ref, q_ref, k_ref, v_ref):
    xn = _rms(h_ref[...], g_ref[...])
    proj = jnp.dot(xn.astype(BF16), w_ref[...], preferred_element_type=F32)
    w = MIX_WIDTH
    u_ref[...] = _gelu(proj[:, :w]).astype(BF16)
    vn_ref[...] = _rms(_gelu(proj[:, w:2 * w]), vg_ref[...]).astype(BF16)
    q_ref[...] = (proj[:, 2 * w:3 * w] * Q_SCALE).astype(BF16)
    k_ref[...] = proj[:, 3 * w:4 * w].astype(BF16)
    v_ref[...] = proj[:, 4 * w:].astype(BF16)


def _in_odd(h, gain, w, v_gain, tm):
    n, d = h.shape
    row = lambda i: (i, 0)
    const2 = lambda i: (0, 0)
    sds = jax.ShapeDtypeStruct((n, MIX_WIDTH), BF16)
    return pl.pallas_call(
        _in_odd_body,
        grid=(n // tm,),
        in_specs=[pl.BlockSpec((tm, d), row), pl.BlockSpec((1, d), const2),
                  pl.BlockSpec(w.shape, const2), pl.BlockSpec((1, MIX_WIDTH), const2)],
        out_specs=[pl.BlockSpec((tm, MIX_WIDTH), row)] * 5,
        out_shape=[sds] * 5,
        compiler_params=_cparams(("arbitrary",)),
        name="in_odd",
    )(h, gain, w, v_gain)


SB_TQ = 64
SB_KB = 128
SB_NB = 2
SB_TK = SB_NB * SB_KB
SB_QBLK = 1024
SB_GROUP = 16
SB_UNDERFLOW = -150.0


def _sb_body(q_ref, k_ref, v_ref, o_ref, acc_scr, run_scr):
    qi = pl.program_id(2)
    lane = lax.broadcasted_iota(jnp.int32, (SB_TQ, LANES), 1)
    row = lax.broadcasted_iota(jnp.int32, (2 * SB_TQ, 1), 0) % SB_TQ
    col = lax.broadcasted_iota(jnp.int32, (2 * SB_TQ, SB_TK), 1)
    rr = lax.broadcasted_iota(jnp.int32, (2 * SB_KB, 2 * SB_KB), 0) % SB_KB
    cc = lax.broadcasted_iota(jnp.int32, (2 * SB_KB, 2 * SB_KB), 1)
    suffix = jnp.where((cc >= SB_KB) | (rr > cc), 1.0, 0.0).astype(BF16)

    def suffix_sums(x):
        hi = x.astype(BF16)
        lo = (x - hi.astype(F32)).astype(BF16)
        return jnp.dot(jnp.concatenate([hi, lo], axis=1), suffix, preferred_element_type=F32)

    def subtile_group(grp, _):
        q_los = [pl.multiple_of((grp * SB_GROUP + s) * SB_TQ, SB_TQ) for s in range(SB_GROUP)]
        q_starts = [qi * SB_QBLK + q_lo for q_lo in q_los]
        q_heads = []
        for q_lo in q_los:
            q = q_ref[0, pl.ds(q_lo, SB_TQ), :]
            zero = jnp.zeros_like(q)
            q_heads.append(jnp.concatenate(
                [jnp.where(lane < HEAD_DIM, q, zero), jnp.where(lane < HEAD_DIM, zero, q)], axis=0))
        acc_scr[...] = jnp.zeros_like(acc_scr)
        run_scr[...] = jnp.zeros_like(run_scr)

        def cond(carry):
            his, dones = carry
            active = [(hi > 0) & (done == 0) for hi, done in zip(his, dones)]
            return functools.reduce(jnp.logical_or, active)

        def body(carry):
            his, _ = carry
            group = range(SB_GROUP)
            kss = [pl.multiple_of(jnp.maximum(his[s] - SB_TK, 0), SB_TQ) for s in group]
            valid = [col < (jnp.minimum(row + q_starts[s], his[s]) - kss[s]) for s in group]
            zs = [lax.dot_general(q_heads[s], k_ref[0, pl.ds(kss[s], SB_TK), :], NT_DIMS,
                                  preferred_element_type=F32) for s in group]
            log_beta, log_rest = [], []
            for s in group:
                z = jnp.where(valid[s], zs[s], NEG_BIG)
                sp = jnp.log2(1.0 + jnp.exp2(-jnp.abs(z)))
                log_beta.append(jnp.minimum(z, 0.0) - sp)
                log_rest.append(log_beta[s] - z)
            sums = [[suffix_sums(log_rest[s][:, blk * SB_KB:(blk + 1) * SB_KB]) for blk in range(SB_NB)]
                    for s in group]
            dones = []
            for s in group:
                run = run_scr[s]
                pieces = [None] * SB_NB
                for blk in reversed(range(SB_NB)):
                    tt = sums[s][blk]
                    pieces[blk] = jnp.exp2(log_beta[s][:, blk * SB_KB:(blk + 1) * SB_KB] + tt[:, :SB_KB] + run)
                    run = run + tt[:, SB_KB:]
                att = jnp.concatenate(pieces, axis=1)
                acc_scr[s] += jnp.dot(att.astype(BF16), v_ref[0, pl.ds(kss[s], SB_TK), :],
                                      preferred_element_type=F32)
                run_scr[s] = run
                dones.append((jnp.max(run) <= SB_UNDERFLOW).astype(jnp.int32))
            return tuple(kss), tuple(dones)

        lax.while_loop(cond, body, (tuple(qs + SB_TQ for qs in q_starts),
                                    tuple(jnp.int32(0) for _ in range(SB_GROUP))))
        for s in range(SB_GROUP):
            o_ref[0, pl.ds(q_los[s], SB_TQ), :] = jnp.where(
                lane < HEAD_DIM, acc_scr[s, :SB_TQ], acc_scr[s, SB_TQ:]).astype(BF16)
        return 0

    lax.fori_loop(0, SB_QBLK // (SB_TQ * SB_GROUP), subtile_group, 0)


def _sb_attention(q, k, v):
    b, s, w = q.shape
    assert s % SB_QBLK == 0 and s >= SB_TK
    qspec = pl.BlockSpec((1, SB_QBLK, LANES), lambda bi, hp, i: (bi, i, hp))
    kvspec = pl.BlockSpec((1, s, LANES), lambda bi, hp, i: (bi, 0, hp))
    return pl.pallas_call(
        _sb_body,
        grid=(b, w // LANES, s // SB_QBLK),
        in_specs=[qspec, kvspec, kvspec],
        out_specs=qspec,
        out_shape=jax.ShapeDtypeStruct((b, s, w), BF16),
        scratch_shapes=[pltpu.VMEM((SB_GROUP, 2 * SB_TQ, LANES), F32),
                        pltpu.VMEM((SB_GROUP, 2 * SB_TQ, LANES), F32)],
        compiler_params=_cparams(("arbitrary", "arbitrary", "arbitrary")),
        name="sb_attention",
    )(q, k, v)


def _out_odd_body(u_ref, vn_ref, yd_ref, h_ref, ws_ref, bs_ref, wo_ref, fg_ref, wr_ref,
                  o_ref, xn_ref, route_ref, yc_scr):
    tm = u_ref.shape[0]
    r = lax.broadcasted_iota(jnp.int32, (SGU_BLOCK, SGU_BLOCK), 0)
    c = lax.broadcasted_iota(jnp.int32, (SGU_BLOCK, SGU_BLOCK), 1)
    for g in range(MIX_WIDTH // LANES):
        ls = slice(g * LANES, (g + 1) * LANES)
        ws = jnp.where(c <= r, ws_ref[g], jnp.zeros_like(ws_ref[g]))
        for blk in range(tm // SGU_BLOCK):
            rs = slice(blk * SGU_BLOCK, (blk + 1) * SGU_BLOCK)
            mixed = jnp.dot(ws, vn_ref[rs, ls], preferred_element_type=F32) + bs_ref[g]
            yc_scr[rs, ls] = (u_ref[rs, ls].astype(F32) * mixed).astype(BF16)
    mixed = jnp.concatenate([yc_scr[...], yd_ref[...]], axis=1)
    _project_and_route(h_ref, mixed, wo_ref, fg_ref, wr_ref, o_ref, xn_ref, route_ref)


def _out_odd(u, vn, yd, h, sgu_w, sgu_b, w_out, ffn_gain, w_router, tm):
    n, d = h.shape
    w = MIX_WIDTH
    row = lambda i: (i, 0)
    const2 = lambda i: (0, 0)
    const3 = lambda i: (0, 0, 0)
    return pl.pallas_call(
        _out_odd_body,
        grid=(n // tm,),
        in_specs=[pl.BlockSpec((tm, w), row), pl.BlockSpec((tm, w), row), pl.BlockSpec((tm, w), row),
                  pl.BlockSpec((tm, d), row), pl.BlockSpec(sgu_w.shape, const3),
                  pl.BlockSpec(sgu_b.shape, const3), pl.BlockSpec(w_out.shape, const2),
                  pl.BlockSpec((1, d), const2), pl.BlockSpec(w_router.shape, const2)],
        out_specs=[pl.BlockSpec((tm, d), row), pl.BlockSpec((tm * SUBLANES, LANES), row),
                   pl.BlockSpec((tm, ROUTER_LANES), row)],
        out_shape=[jax.ShapeDtypeStruct((n, d), F32), jax.ShapeDtypeStruct((n * SUBLANES, LANES), F32),
                   jax.ShapeDtypeStruct((n, ROUTER_LANES), F32)],
        scratch_shapes=[pltpu.VMEM((tm, w), BF16)],
        compiler_params=_cparams(("arbitrary",)),
        name="out_odd",
    )(u, vn, yd, h, sgu_w, sgu_b, w_out, ffn_gain, w_router)


def _moe_layer(h, xn, route, w_gate, w_up, w_down, layer):
    pos1, pos2, tile_map, tile_valid = _slots(route, tm=SLOTS_TM)
    pos = jnp.concatenate([pos1[0], pos2[0]]).astype(jnp.int32) * SUBLANES
    tile_map = tile_map[0].astype(jnp.int32)
    xs = _dispatch(pos, tile_valid[0].astype(jnp.int32), xn, tm=DISPATCH_TM)
    ys = _experts(tile_map, xs, w_gate, w_up, w_down, layer)
    return _combine(pos, h, route, ys, tm=COMBINE_TM)


def kernel(x, mix_norm_even, w_in_even, att_q_norm, att_k_norm, att_rel_bias, pool_w, pool_scale,
           w_out_even, mix_norm_odd, w_in_odd, sgu_v_norm, sgu_w, sgu_b, w_out_odd, ffn_norm,
           w_router_group, w_router_expert, w_exp_gate, w_exp_up, w_exp_down):
    b, s, d = x.shape
    n = b * s
    depth = ffn_norm.shape[0]
    heads = MIX_WIDTH // HEAD_DIM
    h = x.reshape(n, d)
    w_gate = w_exp_gate.reshape(depth * N_EXPERTS, d, EXPERT_FF)
    w_up = w_exp_up.reshape(depth * N_EXPERTS, d, EXPERT_FF)
    w_down = w_exp_down.reshape(depth * N_EXPERTS, EXPERT_FF, d)
    for layer in range(depth):
        i = layer // 2
        ffn_gain = ffn_norm[layer][None, :]
        w_router = _router_weights(w_router_group[layer], w_router_expert[layer])
        if layer % 2 == 0:
            q, k_pad, v_pad, p = _in_even(
                h.reshape(b, s, d), mix_norm_even[i][None, :], w_in_even[i].astype(BF16),
                jnp.tile(att_q_norm[i], heads)[None, :], jnp.tile(att_k_norm[i], heads)[None, :],
                tm=ATT_LEFT)
            ya = _band_attention(q, k_pad, v_pad, _band_bias(att_rel_bias[i]))
            h, xn, route = _out_even(ya.reshape(n, MIX_WIDTH), p.reshape(n, MIX_WIDTH), h,
                                     pool_w[i].astype(BF16), pool_scale[i][None, :],
                                     w_out_even[i].astype(BF16), ffn_gain, w_router, seq=s, tm=PROJ_TM)
        else:
            u, vn, q, k, v = _in_odd(h, mix_norm_odd[i][None, :], w_in_odd[i].astype(BF16),
                                     sgu_v_norm[i][None, :], tm=PROJ_TM)
            to3 = lambda t: t.reshape(b, s, MIX_WIDTH)
            yd = _sb_attention(to3(q), to3(k), to3(v))
            bias = jnp.broadcast_to(sgu_b[i][:, :, None], (N_GROUPS, SGU_BLOCK, LANES))
            h, xn, route = _out_odd(u, vn, yd.reshape(n, MIX_WIDTH), h, sgu_w[i].astype(BF16), bias,
                                    w_out_odd[i].astype(BF16), ffn_gain, w_router, tm=PROJ_TM)
        h = _moe_layer(h, xn, route, w_gate, w_up, w_down, layer)
    return h.reshape(b, s, d)
```

```python
import functools
import math

import jax
import jax.numpy as jnp
from jax import lax
from jax.experimental import pallas as pl
from jax.experimental.pallas import tpu as pltpu

F32 = jnp.float32
BF16 = jnp.bfloat16

CHUNK = 64
EPS = 1e-6
HEAD_DIM = 64
MIX_WIDTH = 512
LANES = 128
SUBLANES = 8
ATT_LEFT = 8 * CHUNK
ATT_MAX_REL = 128
POOL_WINDOWS = (2, 4, 8, 16)
POOL_HALO = 16
SGU_BLOCK = 128
N_GROUPS = 4
N_EXP_PER_GROUP = 8
N_EXPERTS = N_GROUPS * N_EXP_PER_GROUP
EXPERT_FF = 256
ROUTER_LANES = 128
ROUTER_ROWS = 40
NEG_BIG = -1e30
VMEM_LIMIT = 56 * 1024 * 1024

PROJ_TM = 512
SLOTS_TM = 1024
DISPATCH_TM = 1024
COMBINE_TM = 256

NT_DIMS = (((1,), (1,)), ((), ()))
LOG2E = math.log2(math.e)
Q_SCALE = LOG2E / math.sqrt(HEAD_DIM)


def _cparams(sem):
    return pltpu.CompilerParams(dimension_semantics=sem, vmem_limit_bytes=VMEM_LIMIT)


def _store_token_major(ref, x, first_token=0):
    rows = x.shape[0]
    for s in range(SUBLANES):
        ref[pl.ds(first_token * SUBLANES + s, rows, stride=SUBLANES), :] = x[:, s * LANES:(s + 1) * LANES]


def _load_token_major(ref, rows, lead=()):
    return jnp.concatenate(
        [ref[lead + (pl.ds(s, rows, stride=SUBLANES), slice(None))] for s in range(SUBLANES)], axis=1)


def _rms(x, gain):
    return x * lax.rsqrt(jnp.mean(x * x, axis=-1, keepdims=True) + EPS) * gain


def _split_dot(x, m):
    hi = x.astype(BF16)
    lo = (x - hi.astype(F32)).astype(BF16)
    return (jnp.dot(hi, m, preferred_element_type=F32)
            + jnp.dot(lo, m, preferred_element_type=F32))


def _head_rms(t, gain):
    n = t.shape[-1]
    r = lax.broadcasted_iota(jnp.int32, (n, n), 0) // HEAD_DIM
    c = lax.broadcasted_iota(jnp.int32, (n, n), 1) // HEAD_DIM
    bd = jnp.where(r == c, 1.0, 0.0).astype(BF16)
    ms = _split_dot(t * t, bd) * (1.0 / HEAD_DIM)
    return t * lax.rsqrt(ms + EPS) * gain


def _in_even_body(h_ref, g_ref, w_ref, qg_ref, kg_ref, q_ref, k_ref, v_ref, p_ref, w_bf):
    j = pl.program_id(1)

    @pl.when(j == 0)
    def _():
        k_ref[...] = jnp.zeros_like(k_ref)
        v_ref[...] = jnp.zeros_like(v_ref)

        @pl.when(pl.program_id(0) == 0)
        def _():
            w_bf[...] = w_ref[...].astype(BF16)

    @pl.when(j > 0)
    def _():
        xn = _rms(h_ref[0], g_ref[...])
        proj = jnp.dot(xn.astype(BF16), w_bf[...], preferred_element_type=F32)
        w = MIX_WIDTH
        q_ref[0] = (_head_rms(proj[:, :w], qg_ref[...]) * Q_SCALE).astype(BF16)
        k_ref[0] = _head_rms(proj[:, w:2 * w], kg_ref[...]).astype(BF16)
        v_ref[0] = proj[:, 2 * w:3 * w].astype(BF16)
        p_ref[0] = proj[:, 3 * w:].astype(BF16)


def _in_even(h, gain, w, q_gain, k_gain, tm):
    b, s, d = h.shape
    assert tm == ATT_LEFT and s % tm == 0
    nt = s // tm
    cur = lambda bi, j: (bi, jnp.maximum(j - 1, 0), 0)
    const = lambda bi, j: (0, 0)
    out_sds = lambda rows: jax.ShapeDtypeStruct((b, rows, MIX_WIDTH), BF16)
    return pl.pallas_call(
        _in_even_body,
        grid=(b, nt + 1),
        in_specs=[
            pl.BlockSpec((1, tm, d), cur),
            pl.BlockSpec((1, d), const),
            pl.BlockSpec(w.shape, const),
            pl.BlockSpec((1, MIX_WIDTH), const),
            pl.BlockSpec((1, MIX_WIDTH), const),
        ],
        out_specs=[
            pl.BlockSpec((1, tm, MIX_WIDTH), cur),
            pl.BlockSpec((1, tm, MIX_WIDTH), lambda bi, j: (bi, j, 0)),
            pl.BlockSpec((1, tm, MIX_WIDTH), lambda bi, j: (bi, j, 0)),
            pl.BlockSpec((1, tm, MIX_WIDTH), cur),
        ],
        out_shape=[out_sds(s), out_sds(s + ATT_LEFT), out_sds(s + ATT_LEFT), out_sds(s)],
        scratch_shapes=[pltpu.VMEM(w.shape, BF16)],
        compiler_params=_cparams(("arbitrary", "arbitrary")),
        name="in_even",
    )(h, gain, w, q_gain, k_gain)


BAND_TQ = 2 * CHUNK
BAND_TK = BAND_TQ + ATT_LEFT


BAND_STEP_TILES = 4
BAND_STAGED = 2


def _band_body(q_ref, k_ref, v_ref, bias_ref, o_ref):
    @pl.loop(0, BAND_STEP_TILES)
    def _(t):
        rows = pl.ds(pl.multiple_of(t * BAND_TQ, BAND_TQ), BAND_TQ)
        _band_tile(pl.program_id(1) * BAND_STEP_TILES + t, q_ref.at[0, rows, :], k_ref, v_ref, bias_ref,
                   o_ref.at[0, rows, :])


def _band_tile(i, q_ref, k_ref, v_ref, bias_ref, o_ref):
    start = pl.multiple_of(i * BAND_TQ, BAND_TQ)
    lane = lax.broadcasted_iota(jnp.int32, (BAND_TQ, LANES), 1)
    col = lax.broadcasted_iota(jnp.int32, (2 * BAND_TQ, BAND_TK), 1)
    is_pad = (col + start) < ATT_LEFT
    n_pairs = MIX_WIDTH // LANES
    lanes = [slice(hp * LANES, (hp + 1) * LANES) for hp in range(n_pairs)]
    for first in range(0, n_pairs, BAND_STAGED):
        pairs = range(first, first + BAND_STAGED)
        scores, probs, denoms = {}, {}, {}
        for hp in pairs:
            q = q_ref[:, lanes[hp]]
            kb = k_ref[0, pl.ds(start, BAND_TK), lanes[hp]]
            zero = jnp.zeros_like(q)
            q2 = jnp.concatenate([jnp.where(lane < HEAD_DIM, q, zero), jnp.where(lane < HEAD_DIM, zero, q)],
                                 axis=0)
            scores[hp] = lax.dot_general(q2, kb, NT_DIMS, preferred_element_type=F32)
        for hp in pairs:
            bias = bias_ref[2 * hp:2 * hp + 2].reshape(2 * BAND_TQ, BAND_TK)
            s = jnp.where(is_pad, NEG_BIG, scores[hp] + bias)
            p = jnp.exp2(s - jnp.max(s, axis=-1, keepdims=True))
            denoms[hp] = jnp.sum(p, axis=-1, keepdims=True)
            probs[hp] = p.astype(BF16)
        for hp in pairs:
            vb = v_ref[0, pl.ds(start, BAND_TK), lanes[hp]]
            o = jnp.dot(probs[hp], vb, preferred_element_type=F32) / denoms[hp]
            o_ref[:, lanes[hp]] = jnp.where(lane < HEAD_DIM, o[:BAND_TQ], o[BAND_TQ:]).astype(BF16)


def _band_bias(rel_bias):
    heads = rel_bias.shape[0]
    r = jnp.arange(BAND_TQ)[:, None]
    j = jnp.arange(BAND_TK)[None, :]
    jb = j - CHUNK * (r // CHUNK)
    in_band = (jb >= 0) & (jb < ATT_LEFT + CHUNK)
    period = BAND_TK + BAND_TQ
    far = jnp.broadcast_to(rel_bias[:, 2 * ATT_MAX_REL:], (heads, ATT_LEFT - ATT_MAX_REL + 1))
    near = rel_bias[:, 2 * ATT_MAX_REL - 1:0:-1]
    wrap = jnp.broadcast_to(rel_bias[:, 2 * ATT_MAX_REL:], (heads, period - BAND_TK))
    g = jnp.concatenate([far, near, wrap], axis=1).astype(F32)
    assert g.shape[1] == period
    toep = jnp.tile(g, (1, BAND_TQ))[:, :BAND_TQ * (period - 1)].reshape(heads, BAND_TQ, period - 1)
    return jnp.where(in_band[None], toep[:, :, :BAND_TK] * LOG2E, NEG_BIG)


def _band_attention(q, k_pad, v_pad, bias):
    b, s, w = q.shape
    sp = k_pad.shape[1]
    step_rows = BAND_STEP_TILES * BAND_TQ
    return pl.pallas_call(
        _band_body,
        grid=(b, s // step_rows),
        in_specs=[
            pl.BlockSpec((1, step_rows, w), lambda bi, i: (bi, i, 0)),
            pl.BlockSpec((1, sp, w), lambda bi, i: (bi, 0, 0)),
            pl.BlockSpec((1, sp, w), lambda bi, i: (bi, 0, 0)),
            pl.BlockSpec(bias.shape, lambda bi, i: (0, 0, 0)),
        ],
        out_specs=pl.BlockSpec((1, step_rows, w), lambda bi, i: (bi, i, 0)),
        out_shape=jax.ShapeDtypeStruct((b, s, w), BF16),
        compiler_params=_cparams(("arbitrary", "arbitrary")),
        name="band_attention",
    )(q, k_pad, v_pad, bias)


def _route_tokens(h, gain, w_router):
    xn = _rms(h, gain)
    x_hi = xn.astype(BF16)
    x_lo = (xn - x_hi.astype(F32)).astype(BF16)
    w_hi = w_router.astype(BF16)
    w_lo = (w_router - w_hi.astype(F32)).astype(BF16)
    logits = (jnp.dot(x_hi, w_hi, preferred_element_type=F32)
              + jnp.dot(x_lo, w_hi, preferred_element_type=F32)
              + jnp.dot(x_hi, w_lo, preferred_element_type=F32))
    lt = logits.T[:ROUTER_ROWS]
    sub = lax.broadcasted_iota(jnp.int32, lt.shape, 0).astype(F32)
    ninf = -jnp.inf

    def top(vals):
        m = jnp.max(vals, axis=0, keepdims=True)
        idx = jnp.min(jnp.where(vals == m, sub, float(ROUTER_LANES)), axis=0, keepdims=True)
        return m, idx

    is_group = sub < N_GROUPS
    g_max, g_sel = top(jnp.where(is_group, lt, ninf))
    g_den = jnp.sum(jnp.where(is_group, jnp.exp(lt - g_max), 0.0), axis=0, keepdims=True)
    g_weight = 1.0 / g_den
    lo = N_GROUPS + N_EXP_PER_GROUP * g_sel
    e_logits = jnp.where((sub >= lo) & (sub < lo + N_EXP_PER_GROUP), lt, ninf)
    e1, i1 = top(e_logits)
    e2, i2 = top(jnp.where(sub == i1, ninf, e_logits))
    t = jnp.exp(e2 - e1)
    w1 = g_weight / (1.0 + t)
    w2 = g_weight * t / (1.0 + t)
    rows = lax.broadcasted_iota(jnp.int32, (ROUTER_LANES, lt.shape[1]), 0)
    route_t = jnp.where(rows == 0, i1 - N_GROUPS, jnp.where(rows == 1, i2 - N_GROUPS, 0.0))
    route_t = jnp.where(rows == 2, w1, jnp.where(rows == 3, w2, route_t))
    return xn, route_t.T


def _router_weights(w_rg, w_re):
    pad = jnp.zeros((w_rg.shape[0], ROUTER_LANES - N_GROUPS - N_EXPERTS), F32)
    return jnp.concatenate([w_rg, w_re, pad], axis=1)


def _project_and_route(h_ref, mixed, wo_ref, fg_ref, wr_ref, o_ref, xn_ref, route_ref):
    rows = h_ref.shape[0]
    halves = [slice(0, rows // 2), slice(rows // 2, rows)]
    h_new = [h_ref[rs, :] + jnp.dot(mixed[rs], wo_ref[...], preferred_element_type=F32) for rs in halves]
    for rs, hn in zip(halves, h_new):
        o_ref[rs, :] = hn
    for rs, hn in zip(halves, h_new):
        xn, route_ref[rs, :] = _route_tokens(hn, fg_ref[...], wr_ref[...])
        _store_token_major(xn_ref, xn, first_token=rs.start)


def _out_even_body(tiles_per_seq, ya_ref, p_ref, halo_ref, h_ref, pw_ref, ps_ref, wo_ref, fg_ref, wr_ref,
                   o_ref, xn_ref, route_ref, p_scr, yb_scr):
    tm = p_ref.shape[0]
    it = pl.program_id(0) % tiles_per_seq
    halo = halo_ref[...].astype(F32)
    p_scr[0:POOL_HALO, :] = jnp.where(it == 0, jnp.zeros_like(halo), halo)
    p_scr[POOL_HALO:, :] = p_ref[...].astype(F32)
    t = it * tm + lax.broadcasted_iota(jnp.int32, (tm, 1), 0)
    for g, win in enumerate(POOL_WINDOWS):
        ls = slice(g * LANES, (g + 1) * LANES)
        cur = p_scr[POOL_HALO:POOL_HALO + tm, ls]
        acc = cur
        for dlt in range(1, win):
            acc = acc + p_scr[POOL_HALO - dlt:POOL_HALO - dlt + tm, ls]
        cnt = jnp.minimum(t + 1, win).astype(F32)
        mixed = acc / cnt - cur
        yb = jnp.dot(mixed.astype(BF16), pw_ref[g], preferred_element_type=F32) * ps_ref[:, ls]
        yb_scr[:, ls] = yb.astype(BF16)
    mixed = jnp.concatenate([ya_ref[...], yb_scr[...]], axis=1)
    _project_and_route(h_ref, mixed, wo_ref, fg_ref, wr_ref, o_ref, xn_ref, route_ref)


def _out_even(ya, p, h, pool_w, pool_scale, w_out, ffn_gain, w_router, seq, tm):
    n, d = h.shape
    w = MIX_WIDTH
    row = lambda i: (i, 0)
    const2 = lambda i: (0, 0)
    halo_blocks = tm // POOL_HALO
    return pl.pallas_call(
        functools.partial(_out_even_body, seq // tm),
        grid=(n // tm,),
        in_specs=[
            pl.BlockSpec((tm, w), row),
            pl.BlockSpec((tm, w), row),
            pl.BlockSpec((POOL_HALO, w), lambda i: (jnp.maximum(i * halo_blocks - 1, 0), 0)),
            pl.BlockSpec((tm, d), row),
            pl.BlockSpec(pool_w.shape, lambda i: (0, 0, 0)),
            pl.BlockSpec((1, w), const2),
            pl.BlockSpec(w_out.shape, const2),
            pl.BlockSpec((1, d), const2),
            pl.BlockSpec(w_router.shape, const2),
        ],
        out_specs=[pl.BlockSpec((tm, d), row), pl.BlockSpec((tm * SUBLANES, LANES), row),
                   pl.BlockSpec((tm, ROUTER_LANES), row)],
        out_shape=[jax.ShapeDtypeStruct((n, d), F32), jax.ShapeDtypeStruct((n * SUBLANES, LANES), F32),
                   jax.ShapeDtypeStruct((n, ROUTER_LANES), F32)],
        scratch_shapes=[pltpu.VMEM((tm + POOL_HALO, w), F32), pltpu.VMEM((tm, w), BF16)],
        compiler_params=_cparams(("arbitrary",)),
        name="out_even",
    )(ya, p, p, h, pool_w, pool_scale, w_out, ffn_gain, w_router)


MOE_TM = 512
MOE_TILE_LANES = 256


def _moe_tiles(n):
    return (2 * n) // MOE_TM + N_EXPERTS


def _exact_dot_nt(ones, x):
    out = None
    for _ in range(3):
        part = x.astype(BF16)
        x = x - part.astype(F32)
        term = lax.dot_general(ones, part, NT_DIMS, preferred_element_type=F32)
        out = term if out is None else out + term
    return out


def _slots_body(route_ref, pos1_ref, pos2_ref, tile_ref, valid_ref, run_scr, start_scr, earlier_scr):
    phase = pl.program_id(0)
    i = pl.program_id(1)
    tm = route_ref.shape[0]
    route = route_ref[...]
    lane = lax.broadcasted_iota(jnp.int32, (tm, ROUTER_LANES), 1).astype(F32)
    pick1 = jnp.where(lane == route[:, 0:1], 1.0, 0.0)
    pick2 = jnp.where(lane == route[:, 1:2], 1.0, 0.0)
    occ = (pick1 + pick2).astype(BF16)
    ones_rows = jnp.ones((SUBLANES, tm), BF16)
    ones_lanes = jnp.ones((SUBLANES, ROUTER_LANES), BF16)

    @pl.when(i == 0)
    def _():
        run_scr[...] = jnp.zeros_like(run_scr)

    @pl.when(phase == 0)
    def _():
        run_scr[...] += jnp.dot(ones_rows, occ, preferred_element_type=F32)

        @pl.when(i == pl.num_programs(1) - 1)
        def _():
            padded = jnp.floor((run_scr[...] + (MOE_TM - 1)) * (1.0 / MOE_TM)) * MOE_TM
            r = lax.broadcasted_iota(jnp.int32, (ROUTER_LANES, ROUTER_LANES), 0)
            c = lax.broadcasted_iota(jnp.int32, (ROUTER_LANES, ROUTER_LANES), 1)
            before = jnp.where(r < c, 1.0, 0.0).astype(BF16)
            hi = padded.astype(BF16)
            mid = (padded - hi.astype(F32)).astype(BF16)
            low = (padded - hi.astype(F32) - mid.astype(F32)).astype(BF16)
            start = (jnp.dot(hi, before, preferred_element_type=F32)
                     + jnp.dot(mid, before, preferred_element_type=F32)
                     + jnp.dot(low, before, preferred_element_type=F32))
            start_scr[...] = start
            seg_end = start[0:1, :] + padded[0:1, :]
            tile_lo = (lax.broadcasted_iota(jnp.int32, (MOE_TILE_LANES, ROUTER_LANES), 0) * MOE_TM).astype(F32)
            e_lane = lax.broadcasted_iota(jnp.int32, (MOE_TILE_LANES, ROUTER_LANES), 1)
            ended = jnp.where((seg_end <= tile_lo) & (e_lane < N_EXPERTS), 1.0, 0.0).astype(BF16)
            tile_ref[...] = lax.dot_general(ones_lanes, ended, NT_DIMS, preferred_element_type=F32)
            in_segment = (start[0:1, :] <= tile_lo) & (tile_lo < seg_end) & (e_lane < N_EXPERTS)
            real_end = start[0:1, :] + run_scr[0:1, :]
            valid = jnp.where(in_segment, jnp.clip(real_end - tile_lo, 0.0, float(MOE_TM)), 0.0)
            valid_ref[...] = _exact_dot_nt(ones_lanes, valid)

    @pl.when(phase == 1)
    def _():
        @pl.when(i == 0)
        def _():
            r = lax.broadcasted_iota(jnp.int32, (tm, tm), 0)
            c = lax.broadcasted_iota(jnp.int32, (tm, tm), 1)
            earlier_scr[...] = jnp.where(c < r, 1.0, 0.0).astype(BF16)

        base = (jnp.dot(earlier_scr[...], occ, preferred_element_type=F32)
                + run_scr[0:1, :] + start_scr[0:1, :])
        pos1_ref[...] = _exact_dot_nt(ones_lanes, pick1 * base)
        pos2_ref[...] = _exact_dot_nt(ones_lanes, pick2 * base)
        run_scr[...] += jnp.dot(ones_rows, occ, preferred_element_type=F32)


def _slots(route, tm):
    n = route.shape[0]
    assert _moe_tiles(n) <= MOE_TILE_LANES and 2 * n + N_EXPERTS * MOE_TM < 2 ** 24
    row_out = pl.BlockSpec((SUBLANES, tm), lambda ph, i: (0, i * ph))
    sds = jax.ShapeDtypeStruct((SUBLANES, n), F32)
    return pl.pallas_call(
        _slots_body,
        grid=(2, n // tm),
        in_specs=[pl.BlockSpec((tm, ROUTER_LANES), lambda ph, i: (i, 0))],
        out_specs=[row_out, row_out] + [pl.BlockSpec((SUBLANES, MOE_TILE_LANES), lambda ph, i: (0, 0))] * 2,
        out_shape=[sds, sds] + [jax.ShapeDtypeStruct((SUBLANES, MOE_TILE_LANES), F32)] * 2,
        scratch_shapes=[pltpu.VMEM((SUBLANES, ROUTER_LANES), F32), pltpu.VMEM((SUBLANES, ROUTER_LANES), F32),
                        pltpu.VMEM((tm, tm), BF16)],
        compiler_params=_cparams(("arbitrary", "arbitrary")),
        name="moe_slots",
    )(route)


MOE_FILL = 64


def _dispatch_body(n_tiles, pos_ref, valid_ref, xn_ref, xs_hbm, zero_scr, zero_sem, row_sem):
    tm = xn_ref.shape[0] // SUBLANES
    n = pl.num_programs(0) * tm
    base = pl.program_id(0) * tm

    @pl.when(pl.program_id(0) == 0)
    def _():
        zero_scr[...] = jnp.zeros_like(zero_scr)
        chunk_rows = MOE_FILL * SUBLANES

        def for_each_fill(action):
            @pl.loop(0, n_tiles)
            def _(t):
                def chunk(c, carry):
                    rows = pl.ds(pl.multiple_of((t * MOE_TM + c * MOE_FILL) * SUBLANES, chunk_rows), chunk_rows)
                    action(pltpu.make_async_copy(zero_scr, xs_hbm.at[rows, :], zero_sem))
                    return carry

                lax.fori_loop(valid_ref[t] // MOE_FILL, MOE_TM // MOE_FILL, chunk, 0)

        for_each_fill(lambda copy: copy.start())
        for_each_fill(lambda copy: copy.wait())

    def issue(j, carry):
        src = xn_ref.at[pl.ds(pl.multiple_of(j * SUBLANES, SUBLANES), SUBLANES), :]
        for pick in range(2):
            dst = pl.multiple_of(pos_ref[pick * n + base + j], SUBLANES)
            pltpu.make_async_copy(src, xs_hbm.at[pl.ds(dst, SUBLANES), :], row_sem).start(priority=pick)
        return carry

    lax.fori_loop(0, tm, issue, 0, unroll=8)
    for _ in range(2):
        pltpu.make_async_copy(xn_ref, xs_hbm.at[pl.ds(0, tm * SUBLANES), :], row_sem).wait()


def _dispatch(pos, tile_valid, xn, tm):
    n = xn.shape[0] // SUBLANES
    n_tiles = _moe_tiles(n)
    return pl.pallas_call(
        functools.partial(_dispatch_body, n_tiles),
        grid_spec=pltpu.PrefetchScalarGridSpec(
            num_scalar_prefetch=2,
            grid=(n // tm,),
            in_specs=[pl.BlockSpec((tm * SUBLANES, LANES), lambda i, p, t: (i, 0))],
            out_specs=pl.BlockSpec(memory_space=pl.ANY),
            scratch_shapes=[pltpu.VMEM((MOE_FILL * SUBLANES, LANES), F32), pltpu.SemaphoreType.DMA(()),
                            pltpu.SemaphoreType.DMA(())],
        ),
        out_shape=jax.ShapeDtypeStruct((n_tiles * MOE_TM * SUBLANES, LANES), F32),
        compiler_params=_cparams(("arbitrary",)),
        name="moe_dispatch",
    )(pos, tile_valid, xn)


EXPERT_SLOTS = 3


def _experts_body(tile_ref, xs_hbm, wg_ref, wu_ref, wd_ref, ys_ref, xbuf, sems, wg_bf, wu_bf, wd_bf):
    i = pl.program_id(0)
    tile_rows = MOE_TM * SUBLANES
    used = tile_ref[i] < N_EXPERTS

    def tile_copy(t, slot):
        rows = pl.ds(pl.multiple_of(t * tile_rows, tile_rows), tile_rows)
        return pltpu.make_async_copy(xs_hbm.at[rows, :], xbuf.at[slot], sems.at[slot])

    def prefetch(t):
        @pl.when(tile_ref[t] < N_EXPERTS)
        def _():
            tile_copy(t, t % EXPERT_SLOTS).start()

    @pl.when(i == 0)
    def _():
        prefetch(0)
        prefetch(1)

    prefetch(i + 2)

    @pl.when(used & ((i == 0) | (tile_ref[i] != tile_ref[jnp.maximum(i - 1, 0)])))
    def _():
        wg_bf[...] = wg_ref[0].astype(BF16)
        wu_bf[...] = wu_ref[0].astype(BF16)
        wd_bf[...] = wd_ref[0].astype(BF16)

    @pl.when(used)
    def _():
        slot = i % EXPERT_SLOTS
        tile_copy(i, slot).wait()
        x = _load_token_major(xbuf, MOE_TM, (slot,)).astype(BF16)
        gate = jnp.dot(x, wg_bf[...], preferred_element_type=F32)
        up = jnp.dot(x, wu_bf[...], preferred_element_type=F32)
        hid = gate * jax.nn.sigmoid(gate) * up
        _store_token_major(ys_ref, jnp.dot(hid.astype(BF16), wd_bf[...], preferred_element_type=F32))

    @pl.when(jnp.logical_not(used))
    def _():
        ys_ref[...] = jnp.zeros_like(ys_ref)


def _experts(tile_map, xs, w_gate, w_up, w_down, layer):
    d, f = w_gate.shape[1:]
    tile_rows = MOE_TM * SUBLANES
    n_tiles = xs.shape[0] // tile_rows
    y_map = lambda i, tm_ref: (i, 0)
    w_map = lambda i, tm_ref: (layer * N_EXPERTS + jnp.minimum(tm_ref[i], N_EXPERTS - 1), 0, 0)
    return pl.pallas_call(
        _experts_body,
        grid_spec=pltpu.PrefetchScalarGridSpec(
            num_scalar_prefetch=1,
            grid=(n_tiles,),
            in_specs=[pl.BlockSpec(memory_space=pl.ANY), pl.BlockSpec((1, d, f), w_map),
                      pl.BlockSpec((1, d, f), w_map), pl.BlockSpec((1, f, d), w_map)],
            out_specs=pl.BlockSpec((tile_rows, LANES), y_map),
            scratch_shapes=[pltpu.VMEM((EXPERT_SLOTS, tile_rows, LANES), F32),
                            pltpu.SemaphoreType.DMA((EXPERT_SLOTS,)),
                            pltpu.VMEM((d, f), BF16), pltpu.VMEM((d, f), BF16), pltpu.VMEM((f, d), BF16)],
        ),
        out_shape=jax.ShapeDtypeStruct(xs.shape, F32),
        compiler_params=_cparams(("arbitrary",)),
        name="moe_experts",
    )(tile_map, xs, w_gate, w_up, w_down)


def _combine_body(pos_ref, h_ref, route_ref, ys_hbm, o_ref, buf, sems):
    tm = h_ref.shape[0]
    steps = pl.num_programs(0)
    n = steps * tm
    i = pl.program_id(0)

    def start_gather(step, slot):
        base = step * tm

        def issue(g, carry):
            for u in range(SUBLANES):
                for pick in range(2):
                    src = pl.multiple_of(pos_ref[pick * n + base + g * SUBLANES + u], SUBLANES)
                    dst = pl.multiple_of(g * SUBLANES * SUBLANES, SUBLANES) + u * SUBLANES
                    pltpu.make_async_copy(ys_hbm.at[pl.ds(src, SUBLANES), :],
                                          buf.at[slot, pick, pl.ds(dst, SUBLANES), :],
                                          sems.at[slot]).start(priority=pick)
            return carry

        lax.fori_loop(0, tm // SUBLANES, issue, 0)

    @pl.when(i == 0)
    def _():
        start_gather(0, 0)

    @pl.when(i + 1 < steps)
    def _():
        start_gather(i + 1, (i + 1) % 2)

    slot = i % 2
    pltpu.make_async_copy(buf.at[slot], buf.at[slot], sems.at[slot]).wait()
    route = route_ref[...]
    y1 = _load_token_major(buf, tm, (slot, 0))
    y2 = _load_token_major(buf, tm, (slot, 1))
    o_ref[...] = h_ref[...] + route[:, 2:3] * y1 + route[:, 3:4] * y2


def _combine(pos, h, route, ys, tm):
    n, d = h.shape
    row = lambda i, p: (i, 0)
    return pl.pallas_call(
        _combine_body,
        grid_spec=pltpu.PrefetchScalarGridSpec(
            num_scalar_prefetch=1,
            grid=(n // tm,),
            in_specs=[pl.BlockSpec((tm, d), row), pl.BlockSpec((tm, ROUTER_LANES), row),
                      pl.BlockSpec(memory_space=pl.ANY)],
            out_specs=pl.BlockSpec((tm, d), row),
            scratch_shapes=[pltpu.VMEM((2, 2, tm * SUBLANES, LANES), F32), pltpu.SemaphoreType.DMA((2,))],
        ),
        out_shape=jax.ShapeDtypeStruct((n, d), F32),
        compiler_params=_cparams(("arbitrary",)),
        name="moe_combine",
    )(pos, h, route, ys)


def _gelu(x):
    return 0.5 * x * (1.0 + lax.erf(x * (1.0 / math.sqrt(2.0))))


def _in_odd_body(h_ref, g_ref, w_ref, vg_ref, u_ref, vn_ref, q_ref, k_ref, v_ref, w_bf):
    @pl.when(pl.program_id(0) == 0)
    def _():
        w_bf[...] = w_ref[...].astype(BF16)

    xn = _rms(h_ref[...], g_ref[...])
    proj = jnp.dot(xn.astype(BF16), w_bf[...], preferred_element_type=F32)
    w = MIX_WIDTH
    u_ref[...] = _gelu(proj[:, :w]).astype(BF16)
    vn_ref[...] = _rms(_gelu(proj[:, w:2 * w]), vg_ref[...]).astype(BF16)
    q_ref[...] = (proj[:, 2 * w:3 * w] * Q_SCALE).astype(BF16)
    k_ref[...] = proj[:, 3 * w:4 * w].astype(BF16)
    v_ref[...] = proj[:, 4 * w:].astype(BF16)


def _in_odd(h, gain, w, v_gain, tm):
    n, d = h.shape
    row = lambda i: (i, 0)
    const2 = lambda i: (0, 0)
    sds = jax.ShapeDtypeStruct((n, MIX_WIDTH), BF16)
    return pl.pallas_call(
        _in_odd_body,
        grid=(n // tm,),
        in_specs=[pl.BlockSpec((tm, d), row), pl.BlockSpec((1, d), const2),
                  pl.BlockSpec(w.shape, const2), pl.BlockSpec((1, MIX_WIDTH), const2)],
        out_specs=[pl.BlockSpec((tm, MIX_WIDTH), row)] * 5,
        out_shape=[sds] * 5,
        scratch_shapes=[pltpu.VMEM(w.shape, BF16)],
        compiler_params=_cparams(("arbitrary",)),
        name="in_odd",
    )(h, gain, w, v_gain)


SB_TQ = 64
SB_KB = 128
SB_NB = 2
SB_TK = SB_NB * SB_KB
SB_QBLK = 1024
SB_GROUP = 16
SB_UNDERFLOW = -150.0


def _sb_body(q_ref, k_ref, v_ref, o_ref, acc_scr, run_scr):
    qi = pl.program_id(2)
    lane = lax.broadcasted_iota(jnp.int32, (SB_TQ, LANES), 1)
    row = lax.broadcasted_iota(jnp.int32, (2 * SB_TQ, 1), 0) % SB_TQ
    col = lax.broadcasted_iota(jnp.int32, (2 * SB_TQ, SB_TK), 1)
    rr = lax.broadcasted_iota(jnp.int32, (2 * SB_KB, 2 * SB_KB), 0) % SB_KB
    cc = lax.broadcasted_iota(jnp.int32, (2 * SB_KB, 2 * SB_KB), 1)
    suffix = jnp.where((cc >= SB_KB) | (rr > cc), 1.0, 0.0).astype(BF16)

    def suffix_sums(x):
        hi = x.astype(BF16)
        lo = (x - hi.astype(F32)).astype(BF16)
        return jnp.dot(jnp.concatenate([hi, lo], axis=1), suffix, preferred_element_type=F32)

    def subtile_group(grp, _):
        q_los = [pl.multiple_of((grp * SB_GROUP + s) * SB_TQ, SB_TQ) for s in range(SB_GROUP)]
        q_starts = [qi * SB_QBLK + q_lo for q_lo in q_los]
        q_heads = []
        for q_lo in q_los:
            q = q_ref[0, pl.ds(q_lo, SB_TQ), :]
            zero = jnp.zeros_like(q)
            q_heads.append(jnp.concatenate(
                [jnp.where(lane < HEAD_DIM, q, zero), jnp.where(lane < HEAD_DIM, zero, q)], axis=0))
        acc_scr[...] = jnp.zeros_like(acc_scr)
        run_scr[...] = jnp.zeros_like(run_scr)

        def cond(carry):
            his, dones = carry
            active = [(hi > 0) & (done == 0) for hi, done in zip(his, dones)]
            return functools.reduce(jnp.logical_or, active)

        def body(carry):
            his, _ = carry
            group = range(SB_GROUP)
            kss = [pl.multiple_of(jnp.maximum(his[s] - SB_TK, 0), SB_TQ) for s in group]
            valid = [col < (jnp.minimum(row + q_starts[s], his[s]) - kss[s]) for s in group]
            zs = [lax.dot_general(q_heads[s], k_ref[0, pl.ds(kss[s], SB_TK), :], NT_DIMS,
                                  preferred_element_type=F32) for s in group]
            log_beta, log_rest = [], []
            for s in group:
                z = jnp.where(valid[s], zs[s], NEG_BIG)
                sp = jnp.log2(1.0 + jnp.exp2(-jnp.abs(z)))
                log_beta.append(jnp.minimum(z, 0.0) - sp)
                log_rest.append(log_beta[s] - z)
            sums = [[suffix_sums(log_rest[s][:, blk * SB_KB:(blk + 1) * SB_KB]) for blk in range(SB_NB)]
                    for s in group]
            dones = []
            for s in group:
                run = run_scr[s]
                pieces = [None] * SB_NB
                for blk in reversed(range(SB_NB)):
                    tt = sums[s][blk]
                    pieces[blk] = jnp.exp2(log_beta[s][:, blk * SB_KB:(blk + 1) * SB_KB] + tt[:, :SB_KB] + run)
                    run = run + tt[:, SB_KB:]
                att = jnp.concatenate(pieces, axis=1)
                acc_scr[s] += jnp.dot(att.astype(BF16), v_ref[0, pl.ds(kss[s], SB_TK), :],
                                      preferred_element_type=F32)
                run_scr[s] = run
                dones.append((jnp.max(run) <= SB_UNDERFLOW).astype(jnp.int32))
            return tuple(kss), tuple(dones)

        lax.while_loop(cond, body, (tuple(qs + SB_TQ for qs in q_starts),
                                    tuple(jnp.int32(0) for _ in range(SB_GROUP))))
        for s in range(SB_GROUP):
            o_ref[0, pl.ds(q_los[s], SB_TQ), :] = jnp.where(
                lane < HEAD_DIM, acc_scr[s, :SB_TQ], acc_scr[s, SB_TQ:]).astype(BF16)
        return 0

    lax.fori_loop(0, SB_QBLK // (SB_TQ * SB_GROUP), subtile_group, 0)


def _sb_attention(q, k, v):
    b, s, w = q.shape
    assert s % SB_QBLK == 0 and s >= SB_TK
    qspec = pl.BlockSpec((1, SB_QBLK, LANES), lambda bi, hp, i: (bi, i, hp))
    kvspec = pl.BlockSpec((1, s, LANES), lambda bi, hp, i: (bi, 0, hp))
    return pl.pallas_call(
        _sb_body,
        grid=(b, w // LANES, s // SB_QBLK),
        in_specs=[qspec, kvspec, kvspec],
        out_specs=qspec,
        out_shape=jax.ShapeDtypeStruct((b, s, w), BF16),
        scratch_shapes=[pltpu.VMEM((SB_GROUP, 2 * SB_TQ, LANES), F32),
                        pltpu.VMEM((SB_GROUP, 2 * SB_TQ, LANES), F32)],
        compiler_params=_cparams(("arbitrary", "arbitrary", "arbitrary")),
        name="sb_attention",
    )(q, k, v)


def _out_odd_body(u_ref, vn_ref, yd_ref, h_ref, ws_ref, bs_ref, wo_ref, fg_ref, wr_ref,
                  o_ref, xn_ref, route_ref, yc_scr):
    tm = u_ref.shape[0]
    r = lax.broadcasted_iota(jnp.int32, (SGU_BLOCK, SGU_BLOCK), 0)
    c = lax.broadcasted_iota(jnp.int32, (SGU_BLOCK, SGU_BLOCK), 1)
    for g in range(MIX_WIDTH // LANES):
        ls = slice(g * LANES, (g + 1) * LANES)
        ws = jnp.where(c <= r, ws_ref[g], jnp.zeros_like(ws_ref[g]))
        for blk in range(tm // SGU_BLOCK):
            rs = slice(blk * SGU_BLOCK, (blk + 1) * SGU_BLOCK)
            mixed = jnp.dot(ws, vn_ref[rs, ls], preferred_element_type=F32) + bs_ref[g]
            yc_scr[rs, ls] = (u_ref[rs, ls].astype(F32) * mixed).astype(BF16)
    mixed = jnp.concatenate([yc_scr[...], yd_ref[...]], axis=1)
    _project_and_route(h_ref, mixed, wo_ref, fg_ref, wr_ref, o_ref, xn_ref, route_ref)


def _out_odd(u, vn, yd, h, sgu_w, sgu_b, w_out, ffn_gain, w_router, tm):
    n, d = h.shape
    w = MIX_WIDTH
    row = lambda i: (i, 0)
    const2 = lambda i: (0, 0)
    const3 = lambda i: (0, 0, 0)
    return pl.pallas_call(
        _out_odd_body,
        grid=(n // tm,),
        in_specs=[pl.BlockSpec((tm, w), row), pl.BlockSpec((tm, w), row), pl.BlockSpec((tm, w), row),
                  pl.BlockSpec((tm, d), row), pl.BlockSpec(sgu_w.shape, const3),
                  pl.BlockSpec(sgu_b.shape, const3), pl.BlockSpec(w_out.shape, const2),
                  pl.BlockSpec((1, d), const2), pl.BlockSpec(w_router.shape, const2)],
        out_specs=[pl.BlockSpec((tm, d), row), pl.BlockSpec((tm * SUBLANES, LANES), row),
                   pl.BlockSpec((tm, ROUTER_LANES), row)],
        out_shape=[jax.ShapeDtypeStruct((n, d), F32), jax.ShapeDtypeStruct((n * SUBLANES, LANES), F32),
                   jax.ShapeDtypeStruct((n, ROUTER_LANES), F32)],
        scratch_shapes=[pltpu.VMEM((tm, w), BF16)],
        compiler_params=_cparams(("arbitrary",)),
        name="out_odd",
    )(u, vn, yd, h, sgu_w, sgu_b, w_out, ffn_gain, w_router)


def _moe_layer(h, xn, route, w_gate, w_up, w_down, layer):
    pos1, pos2, tile_map, tile_valid = _slots(route, tm=SLOTS_TM)
    pos = jnp.concatenate([pos1[0], pos2[0]]).astype(jnp.int32) * SUBLANES
    tile_map = tile_map[0].astype(jnp.int32)
    xs = _dispatch(pos, tile_valid[0].astype(jnp.int32), xn, tm=DISPATCH_TM)
    ys = _experts(tile_map, xs, w_gate, w_up, w_down, layer)
    return _combine(pos, h, route, ys, tm=COMBINE_TM)


def kernel(x, mix_norm_even, w_in_even, att_q_norm, att_k_norm, att_rel_bias, pool_w, pool_scale,
           w_out_even, mix_norm_odd, w_in_odd, sgu_v_norm, sgu_w, sgu_b, w_out_odd, ffn_norm,
           w_router_group, w_router_expert, w_exp_gate, w_exp_up, w_exp_down):
    b, s, d = x.shape
    n = b * s
    depth = ffn_norm.shape[0]
    heads = MIX_WIDTH // HEAD_DIM
    h = x.reshape(n, d)
    w_gate = w_exp_gate.reshape(depth * N_EXPERTS, d, EXPERT_FF)
    w_up = w_exp_up.reshape(depth * N_EXPERTS, d, EXPERT_FF)
    w_down = w_exp_down.reshape(depth * N_EXPERTS, EXPERT_FF, d)
    for layer in range(depth):
        i = layer // 2
        ffn_gain = ffn_norm[layer][None, :]
        w_router = _router_weights(w_router_group[layer], w_router_expert[layer])
        if layer % 2 == 0:
            q, k_pad, v_pad, p = _in_even(
                h.reshape(b, s, d), mix_norm_even[i][None, :], w_in_even[i],
                jnp.tile(att_q_norm[i], heads)[None, :], jnp.tile(att_k_norm[i], heads)[None, :],
                tm=ATT_LEFT)
            ya = _band_attention(q, k_pad, v_pad, _band_bias(att_rel_bias[i]))
            h, xn, route = _out_even(ya.reshape(n, MIX_WIDTH), p.reshape(n, MIX_WIDTH), h,
                                     pool_w[i].astype(BF16), pool_scale[i][None, :],
                                     w_out_even[i].astype(BF16), ffn_gain, w_router, seq=s, tm=PROJ_TM)
        else:
            u, vn, q, k, v = _in_odd(h, mix_norm_odd[i][None, :], w_in_odd[i],
                                     sgu_v_norm[i][None, :], tm=PROJ_TM)
            to3 = lambda t: t.reshape(b, s, MIX_WIDTH)
            yd = _sb_attention(to3(q), to3(k), to3(v))
            bias = jnp.broadcast_to(sgu_b[i][:, :, None], (N_GROUPS, SGU_BLOCK, LANES))
            h, xn, route = _out_odd(u, vn, yd.reshape(n, MIX_WIDTH), h, sgu_w[i].astype(BF16), bias,
                                    w_out_odd[i].astype(BF16), ffn_gain, w_router, tm=PROJ_TM)
        h = _moe_layer(h, xn, route, w_gate, w_up, w_down, layer)
    return h.reshape(b, s, d)
```

```python
import functools
import math

import jax
import jax.numpy as jnp
from jax import lax
from jax.experimental import pallas as pl
from jax.experimental.pallas import tpu as pltpu

F32 = jnp.float32
BF16 = jnp.bfloat16

CHUNK = 64
EPS = 1e-6
HEAD_DIM = 64
MIX_WIDTH = 512
LANES = 128
SUBLANES = 8
ATT_LEFT = 8 * CHUNK
ATT_MAX_REL = 128
POOL_WINDOWS = (2, 4, 8, 16)
POOL_HALO = 16
SGU_BLOCK = 128
N_GROUPS = 4
N_EXP_PER_GROUP = 8
N_EXPERTS = N_GROUPS * N_EXP_PER_GROUP
EXPERT_FF = 256
ROUTER_LANES = 128
ROUTER_ROWS = 40
NEG_BIG = -1e30
VMEM_LIMIT = 56 * 1024 * 1024

PROJ_TM = 512
SLOTS_TM = 2048
DISPATCH_TM = 2048
COMBINE_TM = 256

NT_DIMS = (((1,), (1,)), ((), ()))
LOG2E = math.log2(math.e)
Q_SCALE = LOG2E / math.sqrt(HEAD_DIM)


def _cparams(sem):
    return pltpu.CompilerParams(dimension_semantics=sem, vmem_limit_bytes=VMEM_LIMIT)


def _store_token_major(ref, x, first_token=0):
    rows = x.shape[0]
    for s in range(SUBLANES):
        ref[pl.ds(first_token * SUBLANES + s, rows, stride=SUBLANES), :] = x[:, s * LANES:(s + 1) * LANES]


def _load_token_major(ref, rows, lead=()):
    return jnp.concatenate(
        [ref[lead + (pl.ds(s, rows, stride=SUBLANES), slice(None))] for s in range(SUBLANES)], axis=1)


def _rms(x, gain):
    return x * lax.rsqrt(jnp.mean(x * x, axis=-1, keepdims=True) + EPS) * gain


def _split_dot(x, m):
    hi = x.astype(BF16)
    lo = (x - hi.astype(F32)).astype(BF16)
    return (jnp.dot(hi, m, preferred_element_type=F32)
            + jnp.dot(lo, m, preferred_element_type=F32))


def _head_rms(t, gain):
    n = t.shape[-1]
    r = lax.broadcasted_iota(jnp.int32, (n, n), 0) // HEAD_DIM
    c = lax.broadcasted_iota(jnp.int32, (n, n), 1) // HEAD_DIM
    bd = jnp.where(r == c, 1.0, 0.0).astype(BF16)
    ms = _split_dot(t * t, bd) * (1.0 / HEAD_DIM)
    return t * lax.rsqrt(ms + EPS) * gain


def _in_even_body(h_ref, g_ref, w_ref, qg_ref, kg_ref, q_ref, k_ref, v_ref, p_ref, w_bf):
    j = pl.program_id(1)

    @pl.when(j == 0)
    def _():
        k_ref[...] = jnp.zeros_like(k_ref)
        v_ref[...] = jnp.zeros_like(v_ref)

        @pl.when(pl.program_id(0) == 0)
        def _():
            w_bf[...] = w_ref[...].astype(BF16)

    @pl.when(j > 0)
    def _():
        xn = _rms(h_ref[0], g_ref[...])
        proj = jnp.dot(xn.astype(BF16), w_bf[...], preferred_element_type=F32)
        w = MIX_WIDTH
        q_ref[0] = (_head_rms(proj[:, :w], qg_ref[...]) * Q_SCALE).astype(BF16)
        k_ref[0] = _head_rms(proj[:, w:2 * w], kg_ref[...]).astype(BF16)
        v_ref[0] = proj[:, 2 * w:3 * w].astype(BF16)
        p_ref[0] = proj[:, 3 * w:].astype(BF16)


def _in_even(h, gain, w, q_gain, k_gain, tm):
    b, s, d = h.shape
    assert tm == ATT_LEFT and s % tm == 0
    nt = s // tm
    cur = lambda bi, j: (bi, jnp.maximum(j - 1, 0), 0)
    const = lambda bi, j: (0, 0)
    out_sds = lambda rows: jax.ShapeDtypeStruct((b, rows, MIX_WIDTH), BF16)
    return pl.pallas_call(
        _in_even_body,
        grid=(b, nt + 1),
        in_specs=[
            pl.BlockSpec((1, tm, d), cur),
            pl.BlockSpec((1, d), const),
            pl.BlockSpec(w.shape, const),
            pl.BlockSpec((1, MIX_WIDTH), const),
            pl.BlockSpec((1, MIX_WIDTH), const),
        ],
        out_specs=[
            pl.BlockSpec((1, tm, MIX_WIDTH), cur),
            pl.BlockSpec((1, tm, MIX_WIDTH), lambda bi, j: (bi, j, 0)),
            pl.BlockSpec((1, tm, MIX_WIDTH), lambda bi, j: (bi, j, 0)),
            pl.BlockSpec((1, tm, MIX_WIDTH), cur),
        ],
        out_shape=[out_sds(s), out_sds(s + ATT_LEFT), out_sds(s + ATT_LEFT), out_sds(s)],
        scratch_shapes=[pltpu.VMEM(w.shape, BF16)],
        compiler_params=_cparams(("arbitrary", "arbitrary")),
        name="in_even",
    )(h, gain, w, q_gain, k_gain)


BAND_TQ = 2 * CHUNK
BAND_TK = BAND_TQ + ATT_LEFT


BAND_STEP_TILES = 4
BAND_STAGED = 2


def _band_body(q_ref, k_ref, v_ref, bias_ref, o_ref):
    @pl.loop(0, BAND_STEP_TILES)
    def _(t):
        rows = pl.ds(pl.multiple_of(t * BAND_TQ, BAND_TQ), BAND_TQ)
        _band_tile(pl.program_id(1) * BAND_STEP_TILES + t, q_ref.at[0, rows, :], k_ref, v_ref, bias_ref,
                   o_ref.at[0, rows, :])


def _band_tile(i, q_ref, k_ref, v_ref, bias_ref, o_ref):
    start = pl.multiple_of(i * BAND_TQ, BAND_TQ)
    lane = lax.broadcasted_iota(jnp.int32, (BAND_TQ, LANES), 1)
    col = lax.broadcasted_iota(jnp.int32, (2 * BAND_TQ, BAND_TK), 1)
    is_pad = (col + start) < ATT_LEFT
    n_pairs = MIX_WIDTH // LANES
    lanes = [slice(hp * LANES, (hp + 1) * LANES) for hp in range(n_pairs)]
    for first in range(0, n_pairs, BAND_STAGED):
        pairs = range(first, first + BAND_STAGED)
        scores, probs, denoms = {}, {}, {}
        for hp in pairs:
            q = q_ref[:, lanes[hp]]
            kb = k_ref[0, pl.ds(start, BAND_TK), lanes[hp]]
            zero = jnp.zeros_like(q)
            q2 = jnp.concatenate([jnp.where(lane < HEAD_DIM, q, zero), jnp.where(lane < HEAD_DIM, zero, q)],
                                 axis=0)
            scores[hp] = lax.dot_general(q2, kb, NT_DIMS, preferred_element_type=F32)
        for hp in pairs:
            bias = bias_ref[2 * hp:2 * hp + 2].reshape(2 * BAND_TQ, BAND_TK)
            s = jnp.where(is_pad, NEG_BIG, scores[hp] + bias)
            p = jnp.exp2(s - jnp.max(s, axis=-1, keepdims=True))
            denoms[hp] = jnp.sum(p, axis=-1, keepdims=True)
            probs[hp] = p.astype(BF16)
        for hp in pairs:
            vb = v_ref[0, pl.ds(start, BAND_TK), lanes[hp]]
            o = jnp.dot(probs[hp], vb, preferred_element_type=F32) / denoms[hp]
            o_ref[:, lanes[hp]] = jnp.where(lane < HEAD_DIM, o[:BAND_TQ], o[BAND_TQ:]).astype(BF16)


def _band_bias(rel_bias):
    heads = rel_bias.shape[0]
    r = jnp.arange(BAND_TQ)[:, None]
    j = jnp.arange(BAND_TK)[None, :]
    jb = j - CHUNK * (r // CHUNK)
    in_band = (jb >= 0) & (jb < ATT_LEFT + CHUNK)
    period = BAND_TK + BAND_TQ
    far = jnp.broadcast_to(rel_bias[:, 2 * ATT_MAX_REL:], (heads, ATT_LEFT - ATT_MAX_REL + 1))
    near = rel_bias[:, 2 * ATT_MAX_REL - 1:0:-1]
    wrap = jnp.broadcast_to(rel_bias[:, 2 * ATT_MAX_REL:], (heads, period - BAND_TK))
    g = jnp.concatenate([far, near, wrap], axis=1).astype(F32)
    assert g.shape[1] == period
    toep = jnp.tile(g, (1, BAND_TQ))[:, :BAND_TQ * (period - 1)].reshape(heads, BAND_TQ, period - 1)
    return jnp.where(in_band[None], toep[:, :, :BAND_TK] * LOG2E, NEG_BIG)


def _band_attention(q, k_pad, v_pad, bias):
    b, s, w = q.shape
    sp = k_pad.shape[1]
    step_rows = BAND_STEP_TILES * BAND_TQ
    return pl.pallas_call(
        _band_body,
        grid=(b, s // step_rows),
        in_specs=[
            pl.BlockSpec((1, step_rows, w), lambda bi, i: (bi, i, 0)),
            pl.BlockSpec((1, sp, w), lambda bi, i: (bi, 0, 0)),
            pl.BlockSpec((1, sp, w), lambda bi, i: (bi, 0, 0)),
            pl.BlockSpec(bias.shape, lambda bi, i: (0, 0, 0)),
        ],
        out_specs=pl.BlockSpec((1, step_rows, w), lambda bi, i: (bi, i, 0)),
        out_shape=jax.ShapeDtypeStruct((b, s, w), BF16),
        compiler_params=_cparams(("arbitrary", "arbitrary")),
        name="band_attention",
    )(q, k_pad, v_pad, bias)


def _route_tokens(h, gain, w_router):
    xn = _rms(h, gain)
    x_hi = xn.astype(BF16)
    x_lo = (xn - x_hi.astype(F32)).astype(BF16)
    w_hi = w_router.astype(BF16)
    w_lo = (w_router - w_hi.astype(F32)).astype(BF16)
    logits = (jnp.dot(x_hi, w_hi, preferred_element_type=F32)
              + jnp.dot(x_lo, w_hi, preferred_element_type=F32)
              + jnp.dot(x_hi, w_lo, preferred_element_type=F32))
    lt = logits.T[:ROUTER_ROWS]
    sub = lax.broadcasted_iota(jnp.int32, lt.shape, 0).astype(F32)
    ninf = -jnp.inf

    def top(vals):
        m = jnp.max(vals, axis=0, keepdims=True)
        idx = jnp.min(jnp.where(vals == m, sub, float(ROUTER_LANES)), axis=0, keepdims=True)
        return m, idx

    is_group = sub < N_GROUPS
    g_max, g_sel = top(jnp.where(is_group, lt, ninf))
    g_den = jnp.sum(jnp.where(is_group, jnp.exp(lt - g_max), 0.0), axis=0, keepdims=True)
    g_weight = 1.0 / g_den
    lo = N_GROUPS + N_EXP_PER_GROUP * g_sel
    e_logits = jnp.where((sub >= lo) & (sub < lo + N_EXP_PER_GROUP), lt, ninf)
    e1, i1 = top(e_logits)
    e2, i2 = top(jnp.where(sub == i1, ninf, e_logits))
    t = jnp.exp(e2 - e1)
    w1 = g_weight / (1.0 + t)
    w2 = g_weight * t / (1.0 + t)
    rows = lax.broadcasted_iota(jnp.int32, (ROUTER_LANES, lt.shape[1]), 0)
    route_t = jnp.where(rows == 0, i1 - N_GROUPS, jnp.where(rows == 1, i2 - N_GROUPS, 0.0))
    route_t = jnp.where(rows == 2, w1, jnp.where(rows == 3, w2, route_t))
    return xn, route_t.T


def _router_weights(w_rg, w_re):
    pad = jnp.zeros((w_rg.shape[0], ROUTER_LANES - N_GROUPS - N_EXPERTS), F32)
    return jnp.concatenate([w_rg, w_re, pad], axis=1)


def _project_and_route(h_ref, mixed, wo_ref, fg_ref, wr_ref, o_ref, xn_ref, route_ref):
    rows = h_ref.shape[0]
    halves = [slice(0, rows // 2), slice(rows // 2, rows)]
    h_new = [h_ref[rs, :] + jnp.dot(mixed[rs], wo_ref[...], preferred_element_type=F32) for rs in halves]
    for rs, hn in zip(halves, h_new):
        o_ref[rs, :] = hn
    for rs, hn in zip(halves, h_new):
        xn, route_ref[rs, :] = _route_tokens(hn, fg_ref[...], wr_ref[...])
        _store_token_major(xn_ref, xn, first_token=rs.start)


def _out_even_body(tiles_per_seq, ya_ref, p_ref, halo_ref, h_ref, pw_ref, ps_ref, wo_ref, fg_ref, wr_ref,
                   o_ref, xn_ref, route_ref, p_scr, yb_scr):
    tm = p_ref.shape[0]
    it = pl.program_id(0) % tiles_per_seq
    halo = halo_ref[...].astype(F32)
    p_scr[0:POOL_HALO, :] = jnp.where(it == 0, jnp.zeros_like(halo), halo)
    p_scr[POOL_HALO:, :] = p_ref[...].astype(F32)
    t = it * tm + lax.broadcasted_iota(jnp.int32, (tm, 1), 0)
    for g, win in enumerate(POOL_WINDOWS):
        ls = slice(g * LANES, (g + 1) * LANES)
        cur = p_scr[POOL_HALO:POOL_HALO + tm, ls]
        acc = cur
        for dlt in range(1, win):
            acc = acc + p_scr[POOL_HALO - dlt:POOL_HALO - dlt + tm, ls]
        cnt = jnp.minimum(t + 1, win).astype(F32)
        mixed = acc / cnt - cur
        yb = jnp.dot(mixed.astype(BF16), pw_ref[g], preferred_element_type=F32) * ps_ref[:, ls]
        yb_scr[:, ls] = yb.astype(BF16)
    mixed = jnp.concatenate([ya_ref[...], yb_scr[...]], axis=1)
    _project_and_route(h_ref, mixed, wo_ref, fg_ref, wr_ref, o_ref, xn_ref, route_ref)


def _out_even(ya, p, h, pool_w, pool_scale, w_out, ffn_gain, w_router, seq, tm):
    n, d = h.shape
    w = MIX_WIDTH
    row = lambda i: (i, 0)
    const2 = lambda i: (0, 0)
    halo_blocks = tm // POOL_HALO
    return pl.pallas_call(
        functools.partial(_out_even_body, seq // tm),
        grid=(n // tm,),
        in_specs=[
            pl.BlockSpec((tm, w), row),
            pl.BlockSpec((tm, w), row),
            pl.BlockSpec((POOL_HALO, w), lambda i: (jnp.maximum(i * halo_blocks - 1, 0), 0)),
            pl.BlockSpec((tm, d), row),
            pl.BlockSpec(pool_w.shape, lambda i: (0, 0, 0)),
            pl.BlockSpec((1, w), const2),
            pl.BlockSpec(w_out.shape, const2),
            pl.BlockSpec((1, d), const2),
            pl.BlockSpec(w_router.shape, const2),
        ],
        out_specs=[pl.BlockSpec((tm, d), row), pl.BlockSpec((tm * SUBLANES, LANES), row),
                   pl.BlockSpec((tm, ROUTER_LANES), row)],
        out_shape=[jax.ShapeDtypeStruct((n, d), F32), jax.ShapeDtypeStruct((n * SUBLANES, LANES), F32),
                   jax.ShapeDtypeStruct((n, ROUTER_LANES), F32)],
        scratch_shapes=[pltpu.VMEM((tm + POOL_HALO, w), F32), pltpu.VMEM((tm, w), BF16)],
        compiler_params=_cparams(("arbitrary",)),
        name="out_even",
    )(ya, p, p, h, pool_w, pool_scale, w_out, ffn_gain, w_router)


MOE_TM = 512
MOE_TILE_LANES = 256


def _moe_tiles(n):
    return (2 * n) // MOE_TM + N_EXPERTS


def _exact_dot_nt(ones, x):
    out = None
    for _ in range(3):
        part = x.astype(BF16)
        x = x - part.astype(F32)
        term = lax.dot_general(ones, part, NT_DIMS, preferred_element_type=F32)
        out = term if out is None else out + term
    return out


def _slots_body(route_ref, pos1_ref, pos2_ref, tile_ref, valid_ref, run_scr, start_scr, earlier_scr):
    phase = pl.program_id(0)
    i = pl.program_id(1)
    tm = route_ref.shape[0]
    route = route_ref[...]
    lane = lax.broadcasted_iota(jnp.int32, (tm, ROUTER_LANES), 1).astype(F32)
    pick1 = jnp.where(lane == route[:, 0:1], 1.0, 0.0)
    pick2 = jnp.where(lane == route[:, 1:2], 1.0, 0.0)
    occ = (pick1 + pick2).astype(BF16)
    ones_rows = jnp.ones((SUBLANES, tm), BF16)
    ones_lanes = jnp.ones((SUBLANES, ROUTER_LANES), BF16)

    @pl.when(i == 0)
    def _():
        run_scr[...] = jnp.zeros_like(run_scr)

    @pl.when(phase == 0)
    def _():
        run_scr[...] += jnp.dot(ones_rows, occ, preferred_element_type=F32)

        @pl.when(i == pl.num_programs(1) - 1)
        def _():
            padded = jnp.floor((run_scr[...] + (MOE_TM - 1)) * (1.0 / MOE_TM)) * MOE_TM
            r = lax.broadcasted_iota(jnp.int32, (ROUTER_LANES, ROUTER_LANES), 0)
            c = lax.broadcasted_iota(jnp.int32, (ROUTER_LANES, ROUTER_LANES), 1)
            before = jnp.where(r < c, 1.0, 0.0).astype(BF16)
            hi = padded.astype(BF16)
            mid = (padded - hi.astype(F32)).astype(BF16)
            low = (padded - hi.astype(F32) - mid.astype(F32)).astype(BF16)
            start = (jnp.dot(hi, before, preferred_element_type=F32)
                     + jnp.dot(mid, before, preferred_element_type=F32)
                     + jnp.dot(low, before, preferred_element_type=F32))
            start_scr[...] = start
            seg_end = start[0:1, :] + padded[0:1, :]
            tile_lo = (lax.broadcasted_iota(jnp.int32, (MOE_TILE_LANES, ROUTER_LANES), 0) * MOE_TM).astype(F32)
            e_lane = lax.broadcasted_iota(jnp.int32, (MOE_TILE_LANES, ROUTER_LANES), 1)
            ended = jnp.where((seg_end <= tile_lo) & (e_lane < N_EXPERTS), 1.0, 0.0).astype(BF16)
            tile_ref[...] = lax.dot_general(ones_lanes, ended, NT_DIMS, preferred_element_type=F32)
            in_segment = (start[0:1, :] <= tile_lo) & (tile_lo < seg_end) & (e_lane < N_EXPERTS)
            real_end = start[0:1, :] + run_scr[0:1, :]
            valid = jnp.where(in_segment, jnp.clip(real_end - tile_lo, 0.0, float(MOE_TM)), 0.0)
            valid_ref[...] = _exact_dot_nt(ones_lanes, valid)

    @pl.when(phase == 1)
    def _():
        @pl.when(i == 0)
        def _():
            r = lax.broadcasted_iota(jnp.int32, (tm, tm), 0)
            c = lax.broadcasted_iota(jnp.int32, (tm, tm), 1)
            earlier_scr[...] = jnp.where(c < r, 1.0, 0.0).astype(BF16)

        base = (jnp.dot(earlier_scr[...], occ, preferred_element_type=F32)
                + run_scr[0:1, :] + start_scr[0:1, :])
        pos1_ref[...] = _exact_dot_nt(ones_lanes, pick1 * base)
        pos2_ref[...] = _exact_dot_nt(ones_lanes, pick2 * base)
        run_scr[...] += jnp.dot(ones_rows, occ, preferred_element_type=F32)


def _slots(route, tm):
    n = route.shape[0]
    assert _moe_tiles(n) <= MOE_TILE_LANES and 2 * n + N_EXPERTS * MOE_TM < 2 ** 24
    row_out = pl.BlockSpec((SUBLANES, tm), lambda ph, i: (0, i * ph))
    sds = jax.ShapeDtypeStruct((SUBLANES, n), F32)
    return pl.pallas_call(
        _slots_body,
        grid=(2, n // tm),
        in_specs=[pl.BlockSpec((tm, ROUTER_LANES), lambda ph, i: (i, 0))],
        out_specs=[row_out, row_out] + [pl.BlockSpec((SUBLANES, MOE_TILE_LANES), lambda ph, i: (0, 0))] * 2,
        out_shape=[sds, sds] + [jax.ShapeDtypeStruct((SUBLANES, MOE_TILE_LANES), F32)] * 2,
        scratch_shapes=[pltpu.VMEM((SUBLANES, ROUTER_LANES), F32), pltpu.VMEM((SUBLANES, ROUTER_LANES), F32),
                        pltpu.VMEM((tm, tm), BF16)],
        compiler_params=_cparams(("arbitrary", "arbitrary")),
        name="moe_slots",
    )(route)


MOE_FILL = 64


def _dispatch_body(n_tiles, pos_ref, valid_ref, xn_ref, xs_hbm, zero_scr, zero_sem, row_sem):
    tm = xn_ref.shape[0] // SUBLANES
    n = pl.num_programs(0) * tm
    base = pl.program_id(0) * tm

    @pl.when(pl.program_id(0) == 0)
    def _():
        zero_scr[...] = jnp.zeros_like(zero_scr)
        chunk_rows = MOE_FILL * SUBLANES

        def for_each_fill(action):
            @pl.loop(0, n_tiles)
            def _(t):
                def chunk(c, carry):
                    rows = pl.ds(pl.multiple_of((t * MOE_TM + c * MOE_FILL) * SUBLANES, chunk_rows), chunk_rows)
                    action(pltpu.make_async_copy(zero_scr, xs_hbm.at[rows, :], zero_sem))
                    return carry

                lax.fori_loop(valid_ref[t] // MOE_FILL, MOE_TM // MOE_FILL, chunk, 0)

        for_each_fill(lambda copy: copy.start())
        for_each_fill(lambda copy: copy.wait())

    def issue(j, carry):
        src = xn_ref.at[pl.ds(pl.multiple_of(j * SUBLANES, SUBLANES), SUBLANES), :]
        for pick in range(2):
            dst = pl.multiple_of(pos_ref[pick * n + base + j], SUBLANES)
            pltpu.make_async_copy(src, xs_hbm.at[pl.ds(dst, SUBLANES), :], row_sem).start(priority=pick)
        return carry

    lax.fori_loop(0, tm, issue, 0, unroll=8)
    for _ in range(2):
        pltpu.make_async_copy(xn_ref, xs_hbm.at[pl.ds(0, tm * SUBLANES), :], row_sem).wait()


def _dispatch(pos, tile_valid, xn, tm):
    n = xn.shape[0] // SUBLANES
    n_tiles = _moe_tiles(n)
    return pl.pallas_call(
        functools.partial(_dispatch_body, n_tiles),
        grid_spec=pltpu.PrefetchScalarGridSpec(
            num_scalar_prefetch=2,
            grid=(n // tm,),
            in_specs=[pl.BlockSpec((tm * SUBLANES, LANES), lambda i, p, t: (i, 0))],
            out_specs=pl.BlockSpec(memory_space=pl.ANY),
            scratch_shapes=[pltpu.VMEM((MOE_FILL * SUBLANES, LANES), F32), pltpu.SemaphoreType.DMA(()),
                            pltpu.SemaphoreType.DMA(())],
        ),
        out_shape=jax.ShapeDtypeStruct((n_tiles * MOE_TM * SUBLANES, LANES), F32),
        compiler_params=_cparams(("arbitrary",)),
        name="moe_dispatch",
    )(pos, tile_valid, xn)


EXPERT_SLOTS = 3


def _experts_body(tile_ref, xs_hbm, wg_ref, wu_ref, wd_ref, ys_ref, xbuf, sems, wg_bf, wu_bf, wd_bf):
    i = pl.program_id(0)
    tile_rows = MOE_TM * SUBLANES
    used = tile_ref[i] < N_EXPERTS

    def tile_copy(t, slot):
        rows = pl.ds(pl.multiple_of(t * tile_rows, tile_rows), tile_rows)
        return pltpu.make_async_copy(xs_hbm.at[rows, :], xbuf.at[slot], sems.at[slot])

    def prefetch(t):
        @pl.when(tile_ref[t] < N_EXPERTS)
        def _():
            tile_copy(t, t % EXPERT_SLOTS).start()

    @pl.when(i == 0)
    def _():
        prefetch(0)
        prefetch(1)

    prefetch(i + 2)

    @pl.when(used & ((i == 0) | (tile_ref[i] != tile_ref[jnp.maximum(i - 1, 0)])))
    def _():
        wg_bf[...] = wg_ref[0].astype(BF16)
        wu_bf[...] = wu_ref[0].astype(BF16)
        wd_bf[...] = wd_ref[0].astype(BF16)

    @pl.when(used)
    def _():
        slot = i % EXPERT_SLOTS
        tile_copy(i, slot).wait()
        x = _load_token_major(xbuf, MOE_TM, (slot,)).astype(BF16)
        gate = jnp.dot(x, wg_bf[...], preferred_element_type=F32)
        up = jnp.dot(x, wu_bf[...], preferred_element_type=F32)
        hid = gate * jax.nn.sigmoid(gate) * up
        _store_token_major(ys_ref, jnp.dot(hid.astype(BF16), wd_bf[...], preferred_element_type=F32))

    @pl.when(jnp.logical_not(used))
    def _():
        ys_ref[...] = jnp.zeros_like(ys_ref)


def _experts(tile_map, xs, w_gate, w_up, w_down, layer):
    d, f = w_gate.shape[1:]
    tile_rows = MOE_TM * SUBLANES
    n_tiles = xs.shape[0] // tile_rows
    y_map = lambda i, tm_ref: (i, 0)
    w_map = lambda i, tm_ref: (layer * N_EXPERTS + jnp.minimum(tm_ref[i], N_EXPERTS - 1), 0, 0)
    return pl.pallas_call(
        _experts_body,
        grid_spec=pltpu.PrefetchScalarGridSpec(
            num_scalar_prefetch=1,
            grid=(n_tiles,),
            in_specs=[pl.BlockSpec(memory_space=pl.ANY), pl.BlockSpec((1, d, f), w_map),
                      pl.BlockSpec((1, d, f), w_map), pl.BlockSpec((1, f, d), w_map)],
            out_specs=pl.BlockSpec((tile_rows, LANES), y_map),
            scratch_shapes=[pltpu.VMEM((EXPERT_SLOTS, tile_rows, LANES), F32),
                            pltpu.SemaphoreType.DMA((EXPERT_SLOTS,)),
                            pltpu.VMEM((d, f), BF16), pltpu.VMEM((d, f), BF16), pltpu.VMEM((f, d), BF16)],
        ),
        out_shape=jax.ShapeDtypeStruct(xs.shape, F32),
        compiler_params=_cparams(("arbitrary",)),
        name="moe_experts",
    )(tile_map, xs, w_gate, w_up, w_down)


def _combine_body(pos_ref, h_ref, route_ref, ys_hbm, o_ref, buf, sems):
    tm = h_ref.shape[0]
    steps = pl.num_programs(0)
    n = steps * tm
    i = pl.program_id(0)

    def start_gather(step, slot):
        base = step * tm

        def issue(g, carry):
            for u in range(SUBLANES):
                for pick in range(2):
                    src = pl.multiple_of(pos_ref[pick * n + base + g * SUBLANES + u], SUBLANES)
                    dst = pl.multiple_of(g * SUBLANES * SUBLANES, SUBLANES) + u * SUBLANES
                    pltpu.make_async_copy(ys_hbm.at[pl.ds(src, SUBLANES), :],
                                          buf.at[slot, pick, pl.ds(dst, SUBLANES), :],
                                          sems.at[slot]).start(priority=pick)
            return carry

        lax.fori_loop(0, tm // SUBLANES, issue, 0)

    @pl.when(i == 0)
    def _():
        start_gather(0, 0)

    @pl.when(i + 1 < steps)
    def _():
        start_gather(i + 1, (i + 1) % 2)

    slot = i % 2
    pltpu.make_async_copy(buf.at[slot], buf.at[slot], sems.at[slot]).wait()
    route = route_ref[...]
    y1 = _load_token_major(buf, tm, (slot, 0))
    y2 = _load_token_major(buf, tm, (slot, 1))
    o_ref[...] = h_ref[...] + route[:, 2:3] * y1 + route[:, 3:4] * y2


def _combine(pos, h, route, ys, tm):
    n, d = h.shape
    row = lambda i, p: (i, 0)
    return pl.pallas_call(
        _combine_body,
        grid_spec=pltpu.PrefetchScalarGridSpec(
            num_scalar_prefetch=1,
            grid=(n // tm,),
            in_specs=[pl.BlockSpec((tm, d), row), pl.BlockSpec((tm, ROUTER_LANES), row),
                      pl.BlockSpec(memory_space=pl.ANY)],
            out_specs=pl.BlockSpec((tm, d), row),
            scratch_shapes=[pltpu.VMEM((2, 2, tm * SUBLANES, LANES), F32), pltpu.SemaphoreType.DMA((2,))],
        ),
        out_shape=jax.ShapeDtypeStruct((n, d), F32),
        compiler_params=_cparams(("arbitrary",)),
        name="moe_combine",
    )(pos, h, route, ys)


def _gelu(x):
    return 0.5 * x * (1.0 + lax.erf(x * (1.0 / math.sqrt(2.0))))


def _in_odd_body(h_ref, g_ref, w_ref, vg_ref, u_ref, vn_ref, q_ref, k_ref, v_ref, w_bf):
    @pl.when(pl.program_id(0) == 0)
    def _():
        w_bf[...] = w_ref[...].astype(BF16)

    xn = _rms(h_ref[...], g_ref[...])
    proj = jnp.dot(xn.astype(BF16), w_bf[...], preferred_element_type=F32)
    w = MIX_WIDTH
    u_ref[...] = _gelu(proj[:, :w]).astype(BF16)
    vn_ref[...] = _rms(_gelu(proj[:, w:2 * w]), vg_ref[...]).astype(BF16)
    q_ref[...] = (proj[:, 2 * w:3 * w] * Q_SCALE).astype(BF16)
    k_ref[...] = proj[:, 3 * w:4 * w].astype(BF16)
    v_ref[...] = proj[:, 4 * w:].astype(BF16)


def _in_odd(h, gain, w, v_gain, tm):
    n, d = h.shape
    row = lambda i: (i, 0)
    const2 = lambda i: (0, 0)
    sds = jax.ShapeDtypeStruct((n, MIX_WIDTH), BF16)
    return pl.pallas_call(
        _in_odd_body,
        grid=(n // tm,),
        in_specs=[pl.BlockSpec((tm, d), row), pl.BlockSpec((1, d), const2),
                  pl.BlockSpec(w.shape, const2), pl.BlockSpec((1, MIX_WIDTH), const2)],
        out_specs=[pl.BlockSpec((tm, MIX_WIDTH), row)] * 5,
        out_shape=[sds] * 5,
        scratch_shapes=[pltpu.VMEM(w.shape, BF16)],
        compiler_params=_cparams(("arbitrary",)),
        name="in_odd",
    )(h, gain, w, v_gain)


SB_TQ = 64
SB_KB = 128
SB_NB = 2
SB_TK = SB_NB * SB_KB
SB_QBLK = 1024
SB_GROUP = 16
SB_UNDERFLOW = -150.0


def _sb_body(q_ref, k_ref, v_ref, o_ref, acc_scr, run_scr):
    qi = pl.program_id(2)
    lane = lax.broadcasted_iota(jnp.int32, (SB_TQ, LANES), 1)
    row = lax.broadcasted_iota(jnp.int32, (2 * SB_TQ, 1), 0) % SB_TQ
    col = lax.broadcasted_iota(jnp.int32, (2 * SB_TQ, SB_TK), 1)
    rr = lax.broadcasted_iota(jnp.int32, (2 * SB_KB, 2 * SB_KB), 0) % SB_KB
    cc = lax.broadcasted_iota(jnp.int32, (2 * SB_KB, 2 * SB_KB), 1)
    suffix = jnp.where((cc >= SB_KB) | (rr > cc), 1.0, 0.0).astype(BF16)

    def suffix_sums(x):
        hi = x.astype(BF16)
        lo = (x - hi.astype(F32)).astype(BF16)
        return jnp.dot(jnp.concatenate([hi, lo], axis=1), suffix, preferred_element_type=F32)

    def subtile_group(grp, _):
        q_los = [pl.multiple_of((grp * SB_GROUP + s) * SB_TQ, SB_TQ) for s in range(SB_GROUP)]
        q_starts = [qi * SB_QBLK + q_lo for q_lo in q_los]
        q_heads = []
        for q_lo in q_los:
            q = q_ref[0, pl.ds(q_lo, SB_TQ), :]
            zero = jnp.zeros_like(q)
            q_heads.append(jnp.concatenate(
                [jnp.where(lane < HEAD_DIM, q, zero), jnp.where(lane < HEAD_DIM, zero, q)], axis=0))
        acc_scr[...] = jnp.zeros_like(acc_scr)
        run_scr[...] = jnp.zeros_like(run_scr)

        def cond(carry):
            his, dones = carry
            active = [(hi > 0) & (done == 0) for hi, done in zip(his, dones)]
            return functools.reduce(jnp.logical_or, active)

        def body(carry):
            his, _ = carry
            group = range(SB_GROUP)
            kss = [pl.multiple_of(jnp.maximum(his[s] - SB_TK, 0), SB_TQ) for s in group]
            valid = [col < (jnp.minimum(row + q_starts[s], his[s]) - kss[s]) for s in group]
            zs = [lax.dot_general(q_heads[s], k_ref[0, pl.ds(kss[s], SB_TK), :], NT_DIMS,
                                  preferred_element_type=F32) for s in group]
            log_beta, log_rest = [], []
            for s in group:
                z = jnp.where(valid[s], zs[s], NEG_BIG)
                sp = jnp.log2(1.0 + jnp.exp2(-jnp.abs(z)))
                log_beta.append(jnp.minimum(z, 0.0) - sp)
                log_rest.append(log_beta[s] - z)
            sums = [[suffix_sums(log_rest[s][:, blk * SB_KB:(blk + 1) * SB_KB]) for blk in range(SB_NB)]
                    for s in group]
            dones = []
            for s in group:
                run = run_scr[s]
                pieces = [None] * SB_NB
                for blk in reversed(range(SB_NB)):
                    tt = sums[s][blk]
                    pieces[blk] = jnp.exp2(log_beta[s][:, blk * SB_KB:(blk + 1) * SB_KB] + tt[:, :SB_KB] + run)
                    run = run + tt[:, SB_KB:]
                att = jnp.concatenate(pieces, axis=1)
                acc_scr[s] += jnp.dot(att.astype(BF16), v_ref[0, pl.ds(kss[s], SB_TK), :],
                                      preferred_element_type=F32)
                run_scr[s] = run
                dones.append((jnp.max(run) <= SB_UNDERFLOW).astype(jnp.int32))
            return tuple(kss), tuple(dones)

        lax.while_loop(cond, body, (tuple(qs + SB_TQ for qs in q_starts),
                                    tuple(jnp.int32(0) for _ in range(SB_GROUP))))
        for s in range(SB_GROUP):
            o_ref[0, pl.ds(q_los[s], SB_TQ), :] = jnp.where(
                lane < HEAD_DIM, acc_scr[s, :SB_TQ], acc_scr[s, SB_TQ:]).astype(BF16)
        return 0

    lax.fori_loop(0, SB_QBLK // (SB_TQ * SB_GROUP), subtile_group, 0)


def _sb_attention(q, k, v):
    b, s, w = q.shape
    assert s % SB_QBLK == 0 and s >= SB_TK
    qspec = pl.BlockSpec((1, SB_QBLK, LANES), lambda bi, hp, i: (bi, i, hp))
    kvspec = pl.BlockSpec((1, s, LANES), lambda bi, hp, i: (bi, 0, hp))
    return pl.pallas_call(
        _sb_body,
        grid=(b, w // LANES, s // SB_QBLK),
        in_specs=[qspec, kvspec, kvspec],
        out_specs=qspec,
        out_shape=jax.ShapeDtypeStruct((b, s, w), BF16),
        scratch_shapes=[pltpu.VMEM((SB_GROUP, 2 * SB_TQ, LANES), F32),
                        pltpu.VMEM((SB_GROUP, 2 * SB_TQ, LANES), F32)],
        compiler_params=_cparams(("arbitrary", "arbitrary", "arbitrary")),
        name="sb_attention",
    )(q, k, v)


def _out_odd_body(u_ref, vn_ref, yd_ref, h_ref, ws_ref, bs_ref, wo_ref, fg_ref, wr_ref,
                  o_ref, xn_ref, route_ref, yc_scr):
    tm = u_ref.shape[0]
    r = lax.broadcasted_iota(jnp.int32, (SGU_BLOCK, SGU_BLOCK), 0)
    c = lax.broadcasted_iota(jnp.int32, (SGU_BLOCK, SGU_BLOCK), 1)
    for g in range(MIX_WIDTH // LANES):
        ls = slice(g * LANES, (g + 1) * LANES)
        ws = jnp.where(c <= r, ws_ref[g], jnp.zeros_like(ws_ref[g]))
        for blk in range(tm // SGU_BLOCK):
            rs = slice(blk * SGU_BLOCK, (blk + 1) * SGU_BLOCK)
            mixed = jnp.dot(ws, vn_ref[rs, ls], preferred_element_type=F32) + bs_ref[g]
            yc_scr[rs, ls] = (u_ref[rs, ls].astype(F32) * mixed).astype(BF16)
    mixed = jnp.concatenate([yc_scr[...], yd_ref[...]], axis=1)
    _project_and_route(h_ref, mixed, wo_ref, fg_ref, wr_ref, o_ref, xn_ref, route_ref)


def _out_odd(u, vn, yd, h, sgu_w, sgu_b, w_out, ffn_gain, w_router, tm):
    n, d = h.shape
    w = MIX_WIDTH
    row = lambda i: (i, 0)
    const2 = lambda i: (0, 0)
    const3 = lambda i: (0, 0, 0)
    return pl.pallas_call(
        _out_odd_body,
        grid=(n // tm,),
        in_specs=[pl.BlockSpec((tm, w), row), pl.BlockSpec((tm, w), row), pl.BlockSpec((tm, w), row),
                  pl.BlockSpec((tm, d), row), pl.BlockSpec(sgu_w.shape, const3),
                  pl.BlockSpec(sgu_b.shape, const3), pl.BlockSpec(w_out.shape, const2),
                  pl.BlockSpec((1, d), const2), pl.BlockSpec(w_router.shape, const2)],
        out_specs=[pl.BlockSpec((tm, d), row), pl.BlockSpec((tm * SUBLANES, LANES), row),
                   pl.BlockSpec((tm, ROUTER_LANES), row)],
        out_shape=[jax.ShapeDtypeStruct((n, d), F32), jax.ShapeDtypeStruct((n * SUBLANES, LANES), F32),
                   jax.ShapeDtypeStruct((n, ROUTER_LANES), F32)],
        scratch_shapes=[pltpu.VMEM((tm, w), BF16)],
        compiler_params=_cparams(("arbitrary",)),
        name="out_odd",
    )(u, vn, yd, h, sgu_w, sgu_b, w_out, ffn_gain, w_router)


def _moe_layer(h, xn, route, w_gate, w_up, w_down, layer):
    pos1, pos2, tile_map, tile_valid = _slots(route, tm=SLOTS_TM)
    pos = jnp.concatenate([pos1[0], pos2[0]]).astype(jnp.int32) * SUBLANES
    tile_map = tile_map[0].astype(jnp.int32)
    xs = _dispatch(pos, tile_valid[0].astype(jnp.int32), xn, tm=DISPATCH_TM)
    ys = _experts(tile_map, xs, w_gate, w_up, w_down, layer)
    return _combine(pos, h, route, ys, tm=COMBINE_TM)


def kernel(x, mix_norm_even, w_in_even, att_q_norm, att_k_norm, att_rel_bias, pool_w, pool_scale,
           w_out_even, mix_norm_odd, w_in_odd, sgu_v_norm, sgu_w, sgu_b, w_out_odd, ffn_norm,
           w_router_group, w_router_expert, w_exp_gate, w_exp_up, w_exp_down):
    b, s, d = x.shape
    n = b * s
    depth = ffn_norm.shape[0]
    heads = MIX_WIDTH // HEAD_DIM
    h = x.reshape(n, d)
    w_gate = w_exp_gate.reshape(depth * N_EXPERTS, d, EXPERT_FF)
    w_up = w_exp_up.reshape(depth * N_EXPERTS, d, EXPERT_FF)
    w_down = w_exp_down.reshape(depth * N_EXPERTS, EXPERT_FF, d)
    for layer in range(depth):
        i = layer // 2
        ffn_gain = ffn_norm[layer][None, :]
        w_router = _router_weights(w_router_group[layer], w_router_expert[layer])
        if layer % 2 == 0:
            q, k_pad, v_pad, p = _in_even(
                h.reshape(b, s, d), mix_norm_even[i][None, :], w_in_even[i],
                jnp.tile(att_q_norm[i], heads)[None, :], jnp.tile(att_k_norm[i], heads)[None, :],
                tm=ATT_LEFT)
            ya = _band_attention(q, k_pad, v_pad, _band_bias(att_rel_bias[i]))
            h, xn, route = _out_even(ya.reshape(n, MIX_WIDTH), p.reshape(n, MIX_WIDTH), h,
                                     pool_w[i].astype(BF16), pool_scale[i][None, :],
                                     w_out_even[i].astype(BF16), ffn_gain, w_router, seq=s, tm=PROJ_TM)
        else:
            u, vn, q, k, v = _in_odd(h, mix_norm_odd[i][None, :], w_in_odd[i],
                                     sgu_v_norm[i][None, :], tm=PROJ_TM)
            to3 = lambda t: t.reshape(b, s, MIX_WIDTH)
            yd = _sb_attention(to3(q), to3(k), to3(v))
            bias = jnp.broadcast_to(sgu_b[i][:, :, None], (N_GROUPS, SGU_BLOCK, LANES))
            h, xn, route = _out_odd(u, vn, yd.reshape(n, MIX_WIDTH), h, sgu_w[i].astype(BF16), bias,
                                    w_out_odd[i].astype(BF16), ffn_gain, w_router, tm=PROJ_TM)
        h = _moe_layer(h, xn, route, w_gate, w_up, w_down, layer)
    return h.reshape(b, s, d)
```
